```python
import math
import jax, jax.numpy as jnp
from jax import lax
import numpy as np

D_MODEL = 1024
BATCH = 8
SEQ = 2048
DEPTH = 1

D_MIX = D_MODEL
A_WIDTH = D_MIX // 2
B_WIDTH = D_MIX - A_WIDTH
A_HEADS = 4
A_HEAD_DIM = A_WIDTH // A_HEADS
CHUNK = 128
B_HEADS = 4
B_HEAD_DIM = B_WIDTH // (2 * B_HEADS)
B_V_DIM = 2 * B_HEAD_DIM
Q_BLOCK = 128
ROPE_THETA = 10000.0
NORM_EPS = 1e-6
SUBLN_EPS = 1e-5
IN_COLS = 3 * A_WIDTH + 4 * B_WIDTH

kernel_name = "hybrid_gmlp_diffattn_adaln_layer"


def rms_norm(x, w, eps):
    xf = x.astype(jnp.float32)
    y = xf * lax.rsqrt(jnp.mean(xf * xf, axis=-1, keepdims=True) + eps)
    return (y * w.astype(jnp.float32)).astype(x.dtype)


def apply_rope(t, positions):
    d = t.shape[-1]
    inv_freq = ROPE_THETA ** (-jnp.arange(0, d, 2, dtype=jnp.float32) / d)
    ang = positions.astype(jnp.float32)[..., None] * inv_freq
    ang = jnp.concatenate([ang, ang], axis=-1)[:, :, None, :]
    cos, sin = jnp.cos(ang), jnp.sin(ang)
    tf = t.astype(jnp.float32)
    t1, t2 = tf[..., : d // 2], tf[..., d // 2:]
    rot = jnp.concatenate([-t2, t1], axis=-1)
    return (tf * cos + rot * sin).astype(t.dtype)


def lambda_init_fn(layer_idx):
    return 0.8 - 0.6 * math.exp(-0.3 * layer_idx)


def gmlp_spatial_gating(u, v, sgu_norm_w, w_s, b_s):
    bsz, seq = u.shape[0], u.shape[1]
    n_chunks = seq // CHUNK
    vn = rms_norm(v, sgu_norm_w, NORM_EPS).reshape(bsz, n_chunks, CHUNK, A_HEADS, A_HEAD_DIM)
    ws_causal = jnp.tril(w_s)
    mix = jnp.einsum('hts,bnshc->bnthc', ws_causal.astype(vn.dtype), vn)
    mix = mix + jnp.transpose(b_s)[None, None, :, :, None].astype(vn.dtype)
    return u * mix.reshape(bsz, seq, A_HEADS, A_HEAD_DIM)


def diff_attention(q, k, v, lam, lambda_init, subln_w):
    bsz, seq = q.shape[0], q.shape[1]
    n_blocks = seq // Q_BLOCK
    scale = B_HEAD_DIM ** -0.5
    qh = jnp.transpose(q, (0, 2, 1, 3))
    kh = jnp.transpose(k, (0, 2, 1, 3))
    vh = jnp.transpose(v, (0, 2, 1, 3))
    q_blocks = jnp.transpose(qh.reshape(bsz, 2 * B_HEADS, n_blocks, Q_BLOCK, B_HEAD_DIM), (2, 0, 1, 3, 4))
    key_idx = jnp.arange(seq)

    def one_block(args):
        qb, bi = args
        s = jnp.einsum('bhqd,bhkd->bhqk', qb, kh).astype(jnp.float32) * scale
        q_idx = bi * Q_BLOCK + jnp.arange(Q_BLOCK)
        causal = q_idx[:, None] >= key_idx[None, :]
        s = jnp.where(causal[None, None], s, -jnp.inf)
        p = jax.nn.softmax(s, axis=-1).reshape(bsz, B_HEADS, 2, Q_BLOCK, seq)
        a = p[:, :, 0] - lam * p[:, :, 1]
        return jnp.einsum('bhqk,bhkd->bhqd', a.astype(vh.dtype), vh)

    o = lax.map(one_block, (q_blocks, jnp.arange(n_blocks)))
    o = jnp.transpose(o, (1, 0, 3, 2, 4)).reshape(bsz, seq, B_HEADS, B_V_DIM)
    o = rms_norm(o, subln_w, SUBLN_EPS)
    return o * (1.0 - lambda_init)


def setup_inputs(seed: int = 0) -> dict:
    key = jax.random.key(seed)
    ks = jax.random.split(key, 20)
    f32 = jnp.float32
    x = jax.random.normal(ks[0], (BATCH, SEQ, D_MODEL), f32)
    c = jax.random.normal(ks[1], (BATCH, D_MODEL), f32)
    offset = jax.random.randint(ks[2], (BATCH, 1), 0, 1024, dtype=jnp.int32)
    positions = (jnp.arange(SEQ, dtype=jnp.int32)[None, :] + offset).astype(jnp.int32)
    norm_w = 1.0 + 0.02 * jax.random.normal(ks[3], (DEPTH, D_MODEL), f32)
    w_ada = jax.random.normal(ks[4], (DEPTH, D_MODEL, 3 * D_MODEL), f32) * (D_MODEL ** -0.5)
    b_ada = 0.02 * jax.random.normal(ks[5], (DEPTH, 3 * D_MODEL), f32)
    w_in = jax.random.normal(ks[6], (DEPTH, D_MODEL, IN_COLS), f32) * (D_MODEL ** -0.5)
    sgu_norm_w = 1.0 + 0.02 * jax.random.normal(ks[7], (DEPTH, A_HEADS, A_HEAD_DIM), f32)
    w_s = jax.random.normal(ks[8], (DEPTH, A_HEADS, CHUNK, CHUNK), f32) * (CHUNK ** -0.5)
    b_s = 1.0 + 0.02 * jax.random.normal(ks[9], (DEPTH, A_HEADS, CHUNK), f32)
    q_norm_w = 1.0 + 0.02 * jax.random.normal(ks[10], (DEPTH, B_HEAD_DIM), f32)
    k_norm_w = 1.0 + 0.02 * jax.random.normal(ks[11], (DEPTH, B_HEAD_DIM), f32)
    lambda_q1 = 0.1 * jax.random.normal(ks[12], (DEPTH, B_HEAD_DIM), f32)
    lambda_k1 = 0.1 * jax.random.normal(ks[13], (DEPTH, B_HEAD_DIM), f32)
    lambda_q2 = 0.1 * jax.random.normal(ks[14], (DEPTH, B_HEAD_DIM), f32)
    lambda_k2 = 0.1 * jax.random.normal(ks[15], (DEPTH, B_HEAD_DIM), f32)
    subln_w = 1.0 + 0.02 * jax.random.normal(ks[16], (DEPTH, B_V_DIM), f32)
    w_out = jax.random.normal(ks[17], (DEPTH, D_MIX, D_MODEL), f32) * (D_MIX ** -0.5)
    return {"x": x, "c": c, "positions": positions, "norm_w": norm_w, "w_ada": w_ada,
            "b_ada": b_ada, "w_in": w_in, "sgu_norm_w": sgu_norm_w, "w_s": w_s, "b_s": b_s,
            "q_norm_w": q_norm_w, "k_norm_w": k_norm_w, "lambda_q1": lambda_q1,
            "lambda_k1": lambda_k1, "lambda_q2": lambda_q2, "lambda_k2": lambda_k2,
            "subln_w": subln_w, "w_out": w_out}


def reference(x, c, positions, norm_w, w_ada, b_ada, w_in, sgu_norm_w, w_s, b_s,
              q_norm_w, k_norm_w, lambda_q1, lambda_k1, lambda_q2, lambda_k2, subln_w, w_out):
    bsz, seq = x.shape[0], x.shape[1]
    c_act = jax.nn.silu(c)
    for l in range(DEPTH):
        mod = c_act @ w_ada[l] + b_ada[l]
        shift, scale, gate = jnp.split(mod, 3, axis=-1)
        h = rms_norm(x, norm_w[l], NORM_EPS) * (1.0 + scale[:, None, :]) + shift[:, None, :]

        proj = jnp.einsum('bsd,dn->bsn', h, w_in[l])
        u_a, v_a, z_a, q_b, k_b, v_b, z_b = jnp.split(
            proj, np.cumsum([A_WIDTH, A_WIDTH, A_WIDTH, B_WIDTH, B_WIDTH, B_WIDTH]).tolist(), axis=-1)

        a_out = gmlp_spatial_gating(u_a.reshape(bsz, seq, A_HEADS, A_HEAD_DIM),
                                    v_a.reshape(bsz, seq, A_HEADS, A_HEAD_DIM),
                                    sgu_norm_w[l], w_s[l], b_s[l]).reshape(bsz, seq, A_WIDTH)
        a_out = a_out * jax.nn.silu(z_a)

        q = rms_norm(q_b.reshape(bsz, seq, 2 * B_HEADS, B_HEAD_DIM), q_norm_w[l], NORM_EPS)
        k = rms_norm(k_b.reshape(bsz, seq, 2 * B_HEADS, B_HEAD_DIM), k_norm_w[l], NORM_EPS)
        q = apply_rope(q, positions)
        k = apply_rope(k, positions)
        lam_init = lambda_init_fn(l)
        lam = (jnp.exp(jnp.sum(lambda_q1[l].astype(jnp.float32) * lambda_k1[l].astype(jnp.float32)))
               - jnp.exp(jnp.sum(lambda_q2[l].astype(jnp.float32) * lambda_k2[l].astype(jnp.float32)))
               + lam_init)
        b_out = diff_attention(q, k, v_b.reshape(bsz, seq, B_HEADS, B_V_DIM), lam, lam_init,
                               subln_w[l]).reshape(bsz, seq, B_WIDTH)
        b_out = b_out * jax.nn.silu(z_b)

        mixed = jnp.concatenate([a_out, b_out], axis=-1)
        out = jnp.einsum('bsm,md->bsd', mixed, w_out[l])
        x = x + gate[:, None, :] * out
    return x
```

```python
import functools
import math

import numpy as np
import jax
import jax.numpy as jnp
from jax import lax
from jax.experimental import pallas as pl
from jax.experimental.pallas import tpu as pltpu

D_MODEL = 1024
A_WIDTH = 512
B_WIDTH = 512
A_HEADS = 4
A_HEAD_DIM = 128
CHUNK = 128
B_HEADS = 4
B_HEAD_DIM = 64
B_V_DIM = 128
ROPE_THETA = 10000.0
NORM_EPS = 1e-6
SUBLN_EPS = 1e-5
IN_COLS = 3 * A_WIDTH + 4 * B_WIDTH
LAMBDA_INIT = 0.8 - 0.6 * math.exp(-0.3 * 0)

LANES = 128
MXU_DIM = 256
SEQ_TILE = 256
VMEM_LIMIT_BYTES = 56 * 1024 * 1024

F32 = jnp.float32
BF16 = jnp.bfloat16


def _adaln_kernel(c_ref, w_ref, b_ref, mod_ref):
    c = c_ref[...]
    c_act = c * jax.nn.sigmoid(c)
    mod_ref[...] = jnp.dot(c_act.astype(BF16), w_ref[...].astype(BF16),
                           preferred_element_type=F32) + b_ref[...]


def _silu(z):
    return z * jax.nn.sigmoid(z)


def _head_mean_sq(t, bd_ref):
    sq = (t * t).astype(BF16)
    halves = [jnp.dot(sq[:, h * MXU_DIM:(h + 1) * MXU_DIM], bd_ref[...],
                      preferred_element_type=F32) for h in range(B_WIDTH // MXU_DIM)]
    return jnp.concatenate(halves, axis=-1) * (1.0 / B_HEAD_DIM)


def _rope(t, cos, sin_signed, first_half):
    cols = []
    for cb in range(B_WIDTH // LANES):
        tc = t[:, cb * LANES:(cb + 1) * LANES]
        rot = jnp.where(first_half, pltpu.roll(tc, LANES - B_HEAD_DIM // 2, 1),
                        pltpu.roll(tc, B_HEAD_DIM // 2, 1))
        cols.append(tc * cos + rot * sin_signed)
    return jnp.concatenate(cols, axis=-1)


def _layer_kernel(x_ref, mod_ref, pos_ref, invf_ref, normw_ref, win_ref, sgu_ref, ws_ref,
                  bst_ref, qnw_ref, knw_ref, lq1_ref, lk1_ref, lq2_ref, lk2_ref, subln_ref,
                  wout_ref, bd_ref, out_ref, kt_scr, v_scr, mixed_scr):
    i = pl.program_id(1)
    tq = SEQ_TILE

    x = x_ref[0]
    shift = mod_ref[0, 0:1, :]
    scale = mod_ref[0, 1:2, :]
    gate = mod_ref[0, 2:3, :]
    ms = jnp.mean(x * x, axis=-1, keepdims=True)
    h = (x * lax.rsqrt(ms + NORM_EPS) * normw_ref[...]) * (1.0 + scale) + shift
    hb = h.astype(BF16)

    def proj(c0, width):
        return jnp.dot(hb, win_ref[:, c0:c0 + width], preferred_element_type=F32)

    ua = proj(0, A_WIDTH)
    va = proj(A_WIDTH, A_WIDTH)
    za = proj(2 * A_WIDTH, A_WIDTH)
    row_c = lax.broadcasted_iota(jnp.int32, (CHUNK, CHUNK), 0)
    col_c = lax.broadcasted_iota(jnp.int32, (CHUNK, CHUNK), 1)
    tril = row_c >= col_c
    for hh in range(A_HEADS):
        sl = slice(hh * A_HEAD_DIM, (hh + 1) * A_HEAD_DIM)
        v = va[:, sl]
        vn = v * lax.rsqrt(jnp.mean(v * v, axis=-1, keepdims=True) + NORM_EPS) * sgu_ref[hh:hh + 1, :]
        vnb = vn.astype(BF16)
        ws = jnp.where(tril, ws_ref[hh], 0.0).astype(BF16)
        bias = bst_ref[:, hh:hh + 1]
        mixes = [jnp.dot(ws, vnb[c * CHUNK:(c + 1) * CHUNK, :], preferred_element_type=F32) + bias
                 for c in range(tq // CHUNK)]
        mix = jnp.concatenate(mixes, axis=0)
        a_out = ua[:, sl] * mix * _silu(za[:, sl])
        mixed_scr[:, sl] = a_out.astype(BF16)

    qb = proj(3 * A_WIDTH, B_WIDTH)
    kb = proj(3 * A_WIDTH + B_WIDTH, B_WIDTH)
    vb = proj(3 * A_WIDTH + 2 * B_WIDTH, B_WIDTH)
    zb = proj(3 * A_WIDTH + 3 * B_WIDTH, B_WIDTH)

    ang = pos_ref[0].astype(F32) * invf_ref[...]
    lane = lax.broadcasted_iota(jnp.int32, (1, LANES), 1)
    first_half = (lane % B_HEAD_DIM) < (B_HEAD_DIM // 2)
    cos = jnp.cos(ang)
    sin = jnp.sin(ang)
    sin_signed = jnp.where(first_half, -sin, sin)

    q = qb * lax.rsqrt(_head_mean_sq(qb, bd_ref) + NORM_EPS) * qnw_ref[...]
    k = kb * lax.rsqrt(_head_mean_sq(kb, bd_ref) + NORM_EPS) * knw_ref[...]
    q = _rope(q, cos, sin_signed, first_half) * (B_HEAD_DIM ** -0.5)
    k = _rope(k, cos, sin_signed, first_half)
    qbf = q.astype(BF16)
    kt_scr[i] = k.T.astype(BF16)
    v_scr[i] = vb.astype(BF16)

    lam = (jnp.exp(jnp.sum(lq1_ref[...] * lk1_ref[...], axis=-1, keepdims=True))
           - jnp.exp(jnp.sum(lq2_ref[...] * lk2_ref[...], axis=-1, keepdims=True))
           + LAMBDA_INIT)

    row_t = lax.broadcasted_iota(jnp.int32, (tq, tq), 0)
    col_t = lax.broadcasted_iota(jnp.int32, (tq, tq), 1)
    causal = row_t >= col_t

    def online_update(s, carry, vt):
        m, l, acc = carry
        m_new = jnp.maximum(m, jnp.max(s, axis=-1, keepdims=True))
        alpha = jnp.exp(m - m_new)
        p = jnp.exp(s - m_new)
        l = alpha * l + jnp.sum(p, axis=-1, keepdims=True)
        acc = alpha * acc + jnp.dot(p.astype(BF16), vt, preferred_element_type=F32)
        return m_new, l, acc

    for g in range(B_HEADS):
        q1 = qbf[:, (2 * g) * B_HEAD_DIM:(2 * g + 1) * B_HEAD_DIM]
        q2 = qbf[:, (2 * g + 1) * B_HEAD_DIM:(2 * g + 2) * B_HEAD_DIM]
        r1 = slice((2 * g) * B_HEAD_DIM, (2 * g + 1) * B_HEAD_DIM)
        r2 = slice((2 * g + 1) * B_HEAD_DIM, (2 * g + 2) * B_HEAD_DIM)
        vsl = slice(g * B_V_DIM, (g + 1) * B_V_DIM)

        def kv_step(j, carry, masked):
            c1, c2 = carry
            vt = v_scr[j, :, vsl]
            s1 = jnp.dot(q1, kt_scr[j, r1, :], preferred_element_type=F32)
            s2 = jnp.dot(q2, kt_scr[j, r2, :], preferred_element_type=F32)
            if masked:
                s1 = jnp.where(causal, s1, -jnp.inf)
                s2 = jnp.where(causal, s2, -jnp.inf)
            return online_update(s1, c1, vt), online_update(s2, c2, vt)

        init = (jnp.full((tq, 1), -jnp.inf, F32), jnp.zeros((tq, 1), F32),
                jnp.zeros((tq, B_V_DIM), F32))
        carry = lax.fori_loop(0, i, functools.partial(kv_step, masked=False), (init, init))
        (_, l1, acc1), (_, l2, acc2) = kv_step(i, carry, masked=True)

        o = acc1 / l1 - lam * (acc2 / l2)
        o = o * lax.rsqrt(jnp.mean(o * o, axis=-1, keepdims=True) + SUBLN_EPS) * subln_ref[...]
        o = o * (1.0 - LAMBDA_INIT)
        b_out = o * _silu(zb[:, vsl])
        mixed_scr[:, A_WIDTH + g * B_V_DIM:A_WIDTH + (g + 1) * B_V_DIM] = b_out.astype(BF16)

    out = jnp.dot(mixed_scr[...], wout_ref[...], preferred_element_type=F32)
    out_ref[0] = x + gate * out


def kernel(x, c, positions, norm_w, w_ada, b_ada, w_in, sgu_norm_w, w_s, b_s, q_norm_w, k_norm_w,
           lambda_q1, lambda_k1, lambda_q2, lambda_k2, subln_w, w_out):
    bsz, seq, d = x.shape
    assert d == D_MODEL and seq % SEQ_TILE == 0 and norm_w.shape[0] == 1
    nt = seq // SEQ_TILE

    mod = pl.pallas_call(
        _adaln_kernel,
        grid=(3,),
        in_specs=[pl.BlockSpec((bsz, d), lambda n: (0, 0)),
                  pl.BlockSpec((d, d), lambda n: (0, n)),
                  pl.BlockSpec((1, d), lambda n: (0, n))],
        out_specs=pl.BlockSpec((bsz, d), lambda n: (0, n)),
        out_shape=jax.ShapeDtypeStruct((bsz, 3 * d), F32),
        name="adaln_mod",
    )(c, w_ada[0], b_ada[0][None, :])
    mod = mod.reshape(bsz, 3, d)

    inv_freq = ROPE_THETA ** (-jnp.arange(0, B_HEAD_DIM, 2, dtype=F32) / B_HEAD_DIM)
    invf = jnp.tile(inv_freq, LANES // (B_HEAD_DIM // 2))[None, :]
    head_of_lane = np.arange(MXU_DIM) // B_HEAD_DIM
    bd = jnp.asarray(head_of_lane[:, None] == head_of_lane[None, :], dtype=BF16)

    const2 = lambda b, i: (0, 0)
    const3 = lambda b, i: (0, 0, 0)
    out = pl.pallas_call(
        _layer_kernel,
        grid=(bsz, nt),
        in_specs=[
            pl.BlockSpec((1, SEQ_TILE, d), lambda b, i: (b, i, 0)),
            pl.BlockSpec((1, 3, d), lambda b, i: (b, 0, 0)),
            pl.BlockSpec((1, SEQ_TILE, 1), lambda b, i: (b, i, 0)),
            pl.BlockSpec((1, LANES), const2),
            pl.BlockSpec((1, d), const2),
            pl.BlockSpec((d, IN_COLS), const2),
            pl.BlockSpec((A_HEADS, A_HEAD_DIM), const2),
            pl.BlockSpec((A_HEADS, CHUNK, CHUNK), const3),
            pl.BlockSpec((CHUNK, A_HEADS), const2),
            pl.BlockSpec((1, B_WIDTH), const2),
            pl.BlockSpec((1, B_WIDTH), const2),
            pl.BlockSpec((1, B_HEAD_DIM), const2),
            pl.BlockSpec((1, B_HEAD_DIM), const2),
            pl.BlockSpec((1, B_HEAD_DIM), const2),
            pl.BlockSpec((1, B_HEAD_DIM), const2),
            pl.BlockSpec((1, B_V_DIM), const2),
            pl.BlockSpec((A_WIDTH + B_WIDTH, d), const2),
            pl.BlockSpec((MXU_DIM, MXU_DIM), const2),
        ],
        out_specs=pl.BlockSpec((1, SEQ_TILE, d), lambda b, i: (b, i, 0)),
        out_shape=jax.ShapeDtypeStruct((bsz, seq, d), F32),
        scratch_shapes=[
            pltpu.VMEM((nt, B_WIDTH, SEQ_TILE), BF16),
            pltpu.VMEM((nt, SEQ_TILE, B_WIDTH), BF16),
            pltpu.VMEM((SEQ_TILE, A_WIDTH + B_WIDTH), BF16),
        ],
        compiler_params=pltpu.CompilerParams(
            dimension_semantics=("arbitrary", "arbitrary"),
            vmem_limit_bytes=VMEM_LIMIT_BYTES),
        name="fused_layer",
    )(x, mod, positions.reshape(bsz, seq, 1), invf, norm_w, w_in[0].astype(BF16),
      sgu_norm_w[0], w_s[0], jnp.transpose(b_s[0]),
      jnp.tile(q_norm_w[0], 2 * B_HEADS)[None, :], jnp.tile(k_norm_w[0], 2 * B_HEADS)[None, :],
      lambda_q1, lambda_k1, lambda_q2, lambda_k2, subln_w, w_out[0].astype(BF16), bd)
    return out
```

```python
import math

import numpy as np
import jax
import jax.numpy as jnp
from jax import lax
from jax.experimental import pallas as pl
from jax.experimental.pallas import tpu as pltpu

D_MODEL = 1024
A_WIDTH = 512
B_WIDTH = 512
A_HEADS = 4
A_HEAD_DIM = 128
CHUNK = 128
B_HEADS = 4
B_HEAD_DIM = 64
B_V_DIM = 128
ROPE_THETA = 10000.0
NORM_EPS = 1e-6
SUBLN_EPS = 1e-5
IN_COLS = 3 * A_WIDTH + 4 * B_WIDTH
LAMBDA_INIT = 0.8 - 0.6 * math.exp(-0.3 * 0)
LOG2E = math.log2(math.e)

LANES = 128
SUBLANES = 8
MXU_DIM = 256
SEQ_TILE = 256
VMEM_LIMIT_BYTES = 56 * 1024 * 1024

F32 = jnp.float32
BF16 = jnp.bfloat16


def _adaln_kernel(c_ref, w_ref, b_ref, mod_ref):
    c = c_ref[...]
    c_act = c * jax.nn.sigmoid(c)
    mod_ref[...] = jnp.dot(c_act.astype(BF16), w_ref[...].astype(BF16),
                           preferred_element_type=F32) + b_ref[...]


def _silu(z):
    return z * jax.nn.sigmoid(z)


def _head_mean_sq(t, bd_ref):
    sq = (t * t).astype(BF16)
    halves = [jnp.dot(sq[:, h * MXU_DIM:(h + 1) * MXU_DIM], bd_ref[...],
                      preferred_element_type=F32) for h in range(B_WIDTH // MXU_DIM)]
    return jnp.concatenate(halves, axis=-1) * (1.0 / B_HEAD_DIM)


def _rope(t, cos, sin_signed, first_half):
    cols = []
    for cb in range(B_WIDTH // LANES):
        tc = t[:, cb * LANES:(cb + 1) * LANES]
        rot = jnp.where(first_half, pltpu.roll(tc, LANES - B_HEAD_DIM // 2, 1),
                        pltpu.roll(tc, B_HEAD_DIM // 2, 1))
        cols.append(tc * cos + rot * sin_signed)
    return jnp.concatenate(cols, axis=-1)


def _across_sublanes(x, op):
    for shift in (4, 2, 1):
        x = op(x, pltpu.roll(x, shift, 0))
    return x


def _layer_kernel(x_ref, mod_ref, pos_ref, invf_ref, normw_ref, win_ref, sgu_ref, ws_ref,
                  bst_ref, qnw_ref, knw_ref, lq1_ref, lk1_ref, lq2_ref, lk2_ref, subln_ref,
                  wout_ref, bd_ref, out_ref,
                  k_scr, vt_scr, qblk_scr, m_scr, l_scr, acc_scr, mixed_scr):
    i = pl.program_id(1)
    tq = SEQ_TILE
    tk = SEQ_TILE

    x = x_ref[0]
    shift = mod_ref[0, 0:1, :]
    scale = mod_ref[0, 1:2, :]
    gate = mod_ref[0, 2:3, :]
    ms = jnp.mean(x * x, axis=-1, keepdims=True)
    h = (x * lax.rsqrt(ms + NORM_EPS) * normw_ref[...]) * (1.0 + scale) + shift
    hb = h.astype(BF16)

    def proj(c0, width):
        return jnp.dot(hb, win_ref[:, c0:c0 + width], preferred_element_type=F32)

    ua = proj(0, A_WIDTH)
    va = proj(A_WIDTH, A_WIDTH)
    za = proj(2 * A_WIDTH, A_WIDTH)
    row_c = lax.broadcasted_iota(jnp.int32, (CHUNK, CHUNK), 0)
    col_c = lax.broadcasted_iota(jnp.int32, (CHUNK, CHUNK), 1)
    tril = row_c >= col_c
    for hh in range(A_HEADS):
        sl = slice(hh * A_HEAD_DIM, (hh + 1) * A_HEAD_DIM)
        v = va[:, sl]
        vn = v * lax.rsqrt(jnp.mean(v * v, axis=-1, keepdims=True) + NORM_EPS) * sgu_ref[hh:hh + 1, :]
        vnb = vn.astype(BF16)
        ws = jnp.where(tril, ws_ref[hh], 0.0).astype(BF16)
        bias = bst_ref[:, hh:hh + 1]
        mixes = [jnp.dot(ws, vnb[c * CHUNK:(c + 1) * CHUNK, :], preferred_element_type=F32) + bias
                 for c in range(tq // CHUNK)]
        mix = jnp.concatenate(mixes, axis=0)
        a_out = ua[:, sl] * mix * _silu(za[:, sl])
        mixed_scr[:, sl] = a_out.astype(BF16)

    qb = proj(3 * A_WIDTH, B_WIDTH)
    kb = proj(3 * A_WIDTH + B_WIDTH, B_WIDTH)
    vb = proj(3 * A_WIDTH + 2 * B_WIDTH, B_WIDTH)
    zb = proj(3 * A_WIDTH + 3 * B_WIDTH, B_WIDTH)

    ang = pos_ref[0].astype(F32) * invf_ref[...]
    lane = lax.broadcasted_iota(jnp.int32, (1, LANES), 1)
    first_half = (lane % B_HEAD_DIM) < (B_HEAD_DIM // 2)
    cos = jnp.cos(ang)
    sin = jnp.sin(ang)
    sin_signed = jnp.where(first_half, -sin, sin)

    q = qb * lax.rsqrt(_head_mean_sq(qb, bd_ref) + NORM_EPS) * qnw_ref[...]
    k = kb * lax.rsqrt(_head_mean_sq(kb, bd_ref) + NORM_EPS) * knw_ref[...]
    q = _rope(q, cos, sin_signed, first_half) * (B_HEAD_DIM ** -0.5 * LOG2E)
    k = _rope(k, cos, sin_signed, first_half)
    k_scr[i] = k.astype(BF16)
    vt_scr[i] = vb.T.astype(BF16)

    qt = q.T
    head_row = lax.broadcasted_iota(jnp.int32, (2 * B_HEAD_DIM, tq), 0)
    top = head_row < B_HEAD_DIM
    for g in range(B_HEADS):
        qp = qt[g * 2 * B_HEAD_DIM:(g + 1) * 2 * B_HEAD_DIM, :]
        qblk_scr[g] = jnp.concatenate([jnp.where(top, qp, 0.0), jnp.where(top, 0.0, qp)],
                                      axis=1).astype(BF16)

    m_scr[...] = jnp.full(m_scr.shape, -jnp.inf, F32)
    l_scr[...] = jnp.zeros(l_scr.shape, F32)
    acc_scr[...] = jnp.zeros(acc_scr.shape, F32)

    kv_row = lax.broadcasted_iota(jnp.int32, (tk, 2 * tq), 0)
    q_col = lax.broadcasted_iota(jnp.int32, (tk, 2 * tq), 1) % tq
    causal = kv_row <= q_col

    def tile_update(g, j, masked):
        kp = k_scr[j, :, g * LANES:(g + 1) * LANES]
        s = jnp.dot(kp, qblk_scr[g], preferred_element_type=F32)
        if masked:
            s = jnp.where(causal, s, -jnp.inf)
        s3 = s.reshape(tk // SUBLANES, SUBLANES, 2 * tq)
        m_old = m_scr[g]
        m_new = jnp.maximum(m_old, _across_sublanes(jnp.max(s3, axis=0), jnp.maximum))
        alpha = jnp.exp2(m_old - m_new)
        p3 = jnp.exp2(s3 - m_new[None])
        m_scr[g] = m_new
        l_scr[g] = alpha * l_scr[g] + jnp.sum(p3, axis=0)
        pb = p3.reshape(tk, 2 * tq).astype(BF16)
        vt = vt_scr[j, g * B_V_DIM:(g + 1) * B_V_DIM, :]
        for st in range(2):
            pv = jnp.dot(vt, pb[:, st * tq:(st + 1) * tq], preferred_element_type=F32)
            a = alpha[:, st * tq:(st + 1) * tq]
            acc = acc_scr[2 * g + st].reshape(B_V_DIM // SUBLANES, SUBLANES, tq)
            acc_scr[2 * g + st] = (acc * a[None]).reshape(B_V_DIM, tq) + pv

    def kv_body(j, carry):
        for g in range(B_HEADS):
            tile_update(g, j, masked=False)
        return carry

    lax.fori_loop(0, i, kv_body, 0)
    for g in range(B_HEADS):
        tile_update(g, i, masked=True)

    lam = (jnp.exp(jnp.sum(lq1_ref[...] * lk1_ref[...], axis=-1, keepdims=True))
           - jnp.exp(jnp.sum(lq2_ref[...] * lk2_ref[...], axis=-1, keepdims=True))
           + LAMBDA_INIT)

    for g in range(B_HEADS):
        inv_l = 1.0 / _across_sublanes(l_scr[g], jnp.add)
        acc1 = acc_scr[2 * g].reshape(B_V_DIM // SUBLANES, SUBLANES, tq)
        acc2 = acc_scr[2 * g + 1].reshape(B_V_DIM // SUBLANES, SUBLANES, tq)
        ot = acc1 * inv_l[None, :, :tq] - lam * (acc2 * inv_l[None, :, tq:])
        o = ot.reshape(B_V_DIM, tq).T
        o = o * lax.rsqrt(jnp.mean(o * o, axis=-1, keepdims=True) + SUBLN_EPS) * subln_ref[...]
        o = o * (1.0 - LAMBDA_INIT)
        vsl = slice(g * B_V_DIM, (g + 1) * B_V_DIM)
        b_out = o * _silu(zb[:, vsl])
        mixed_scr[:, A_WIDTH + g * B_V_DIM:A_WIDTH + (g + 1) * B_V_DIM] = b_out.astype(BF16)

    out = jnp.dot(mixed_scr[...], wout_ref[...], preferred_element_type=F32)
    out_ref[0] = x + gate * out


def kernel(x, c, positions, norm_w, w_ada, b_ada, w_in, sgu_norm_w, w_s, b_s, q_norm_w, k_norm_w,
           lambda_q1, lambda_k1, lambda_q2, lambda_k2, subln_w, w_out):
    bsz, seq, d = x.shape
    assert d == D_MODEL and seq % SEQ_TILE == 0 and norm_w.shape[0] == 1
    nt = seq // SEQ_TILE

    mod = pl.pallas_call(
        _adaln_kernel,
        grid=(3,),
        in_specs=[pl.BlockSpec((bsz, d), lambda n: (0, 0)),
                  pl.BlockSpec((d, d), lambda n: (0, n)),
                  pl.BlockSpec((1, d), lambda n: (0, n))],
        out_specs=pl.BlockSpec((bsz, d), lambda n: (0, n)),
        out_shape=jax.ShapeDtypeStruct((bsz, 3 * d), F32),
        name="adaln_mod",
    )(c, w_ada[0], b_ada[0][None, :])
    mod = mod.reshape(bsz, 3, d)

    inv_freq = ROPE_THETA ** (-jnp.arange(0, B_HEAD_DIM, 2, dtype=F32) / B_HEAD_DIM)
    invf = jnp.tile(inv_freq, LANES // (B_HEAD_DIM // 2))[None, :]
    head_of_lane = np.arange(MXU_DIM) // B_HEAD_DIM
    bd = jnp.asarray(head_of_lane[:, None] == head_of_lane[None, :], dtype=BF16)

    const2 = lambda b, i: (0, 0)
    const3 = lambda b, i: (0, 0, 0)
    out = pl.pallas_call(
        _layer_kernel,
        grid=(bsz, nt),
        in_specs=[
            pl.BlockSpec((1, SEQ_TILE, d), lambda b, i: (b, i, 0)),
            pl.BlockSpec((1, 3, d), lambda b, i: (b, 0, 0)),
            pl.BlockSpec((1, SEQ_TILE, 1), lambda b, i: (b, i, 0)),
            pl.BlockSpec((1, LANES), const2),
            pl.BlockSpec((1, d), const2),
            pl.BlockSpec((d, IN_COLS), const2),
            pl.BlockSpec((A_HEADS, A_HEAD_DIM), const2),
            pl.BlockSpec((A_HEADS, CHUNK, CHUNK), const3),
            pl.BlockSpec((CHUNK, A_HEADS), const2),
            pl.BlockSpec((1, B_WIDTH), const2),
            pl.BlockSpec((1, B_WIDTH), const2),
            pl.BlockSpec((1, B_HEAD_DIM), const2),
            pl.BlockSpec((1, B_HEAD_DIM), const2),
            pl.BlockSpec((1, B_HEAD_DIM), const2),
            pl.BlockSpec((1, B_HEAD_DIM), const2),
            pl.BlockSpec((1, B_V_DIM), const2),
            pl.BlockSpec((A_WIDTH + B_WIDTH, d), const2),
            pl.BlockSpec((MXU_DIM, MXU_DIM), const2),
        ],
        out_specs=pl.BlockSpec((1, SEQ_TILE, d), lambda b, i: (b, i, 0)),
        out_shape=jax.ShapeDtypeStruct((bsz, seq, d), F32),
        scratch_shapes=[
            pltpu.VMEM((nt, SEQ_TILE, B_WIDTH), BF16),
            pltpu.VMEM((nt, B_WIDTH, SEQ_TILE), BF16),
            pltpu.VMEM((B_HEADS, 2 * B_HEAD_DIM, 2 * SEQ_TILE), BF16),
            pltpu.VMEM((B_HEADS, SUBLANES, 2 * SEQ_TILE), F32),
            pltpu.VMEM((B_HEADS, SUBLANES, 2 * SEQ_TILE), F32),
            pltpu.VMEM((2 * B_HEADS, B_V_DIM, SEQ_TILE), F32),
            pltpu.VMEM((SEQ_TILE, A_WIDTH + B_WIDTH), BF16),
        ],
        compiler_params=pltpu.CompilerParams(
            dimension_semantics=("arbitrary", "arbitrary"),
            vmem_limit_bytes=VMEM_LIMIT_BYTES),
        name="fused_layer",
    )(x, mod, positions.reshape(bsz, seq, 1), invf, norm_w, w_in[0].astype(BF16),
      sgu_norm_w[0], w_s[0], jnp.transpose(b_s[0]),
      jnp.tile(q_norm_w[0], 2 * B_HEADS)[None, :], jnp.tile(k_norm_w[0], 2 * B_HEADS)[None, :],
      lambda_q1, lambda_k1, lambda_q2, lambda_k2, subln_w, w_out[0].astype(BF16), bd)
    return out
```

```python
import math

import numpy as np
import jax
import jax.numpy as jnp
from jax import lax
from jax.experimental import pallas as pl
from jax.experimental.pallas import tpu as pltpu

D_MODEL = 1024
A_WIDTH = 512
B_WIDTH = 512
A_HEADS = 4
A_HEAD_DIM = 128
CHUNK = 128
B_HEADS = 4
B_HEAD_DIM = 64
B_V_DIM = 128
ROPE_THETA = 10000.0
NORM_EPS = 1e-6
SUBLN_EPS = 1e-5
IN_COLS = 3 * A_WIDTH + 4 * B_WIDTH
LAMBDA_INIT = 0.8 - 0.6 * math.exp(-0.3 * 0)
LOG2E = math.log2(math.e)

LANES = 128
SUBLANES = 8
MXU_DIM = 256
SEQ_TILE = 256
VMEM_LIMIT_BYTES = 56 * 1024 * 1024

F32 = jnp.float32
BF16 = jnp.bfloat16


def _adaln_kernel(c_ref, w_ref, b_ref, mod_ref):
    c = c_ref[...]
    c_act = c * jax.nn.sigmoid(c)
    mod_ref[...] = jnp.dot(c_act.astype(BF16), w_ref[...].astype(BF16),
                           preferred_element_type=F32) + b_ref[...]


def _silu(z):
    return z * jax.nn.sigmoid(z)


def _head_mean_sq(t, bd_ref):
    sq = (t * t).astype(BF16)
    halves = [jnp.dot(sq[:, h * MXU_DIM:(h + 1) * MXU_DIM], bd_ref[...],
                      preferred_element_type=F32) for h in range(B_WIDTH // MXU_DIM)]
    return jnp.concatenate(halves, axis=-1) * (1.0 / B_HEAD_DIM)


def _rope(t, cos, sin_signed, first_half):
    cols = []
    for cb in range(B_WIDTH // LANES):
        tc = t[:, cb * LANES:(cb + 1) * LANES]
        rot = jnp.where(first_half, pltpu.roll(tc, LANES - B_HEAD_DIM // 2, 1),
                        pltpu.roll(tc, B_HEAD_DIM // 2, 1))
        cols.append(tc * cos + rot * sin_signed)
    return jnp.concatenate(cols, axis=-1)


def _across_sublanes(x, op):
    for shift in (4, 2, 1):
        x = op(x, pltpu.roll(x, shift, 0))
    return x


def _layer_kernel(x_ref, mod_ref, pos_ref, invf_ref, normw_ref, win_ref, sgu_ref, ws_ref,
                  bst_ref, qnw_ref, knw_ref, lq1_ref, lk1_ref, lq2_ref, lk2_ref, subln_ref,
                  wout_ref, bd_ref, out_ref,
                  k_scr, vt_scr, qblk_scr, m_scr, l_scr, acc_scr, mixed_scr, s_scr):
    i = pl.program_id(1)
    tq = SEQ_TILE
    tk = SEQ_TILE

    x = x_ref[0]
    shift = mod_ref[0, 0:1, :]
    scale = mod_ref[0, 1:2, :]
    gate = mod_ref[0, 2:3, :]
    ms = jnp.mean(x * x, axis=-1, keepdims=True)
    h = (x * lax.rsqrt(ms + NORM_EPS) * normw_ref[...]) * (1.0 + scale) + shift
    hb = h.astype(BF16)

    def proj(c0, width):
        return jnp.dot(hb, win_ref[:, c0:c0 + width], preferred_element_type=F32)

    ua = proj(0, A_WIDTH)
    va = proj(A_WIDTH, A_WIDTH)
    za = proj(2 * A_WIDTH, A_WIDTH)
    row_c = lax.broadcasted_iota(jnp.int32, (CHUNK, CHUNK), 0)
    col_c = lax.broadcasted_iota(jnp.int32, (CHUNK, CHUNK), 1)
    tril = row_c >= col_c
    for hh in range(A_HEADS):
        sl = slice(hh * A_HEAD_DIM, (hh + 1) * A_HEAD_DIM)
        v = va[:, sl]
        vn = v * lax.rsqrt(jnp.mean(v * v, axis=-1, keepdims=True) + NORM_EPS) * sgu_ref[hh:hh + 1, :]
        vnb = vn.astype(BF16)
        ws = jnp.where(tril, ws_ref[hh], 0.0).astype(BF16)
        bias = bst_ref[:, hh:hh + 1]
        mixes = [jnp.dot(ws, vnb[c * CHUNK:(c + 1) * CHUNK, :], preferred_element_type=F32) + bias
                 for c in range(tq // CHUNK)]
        mix = jnp.concatenate(mixes, axis=0)
        a_out = ua[:, sl] * mix * _silu(za[:, sl])
        mixed_scr[:, sl] = a_out.astype(BF16)

    qb = proj(3 * A_WIDTH, B_WIDTH)
    kb = proj(3 * A_WIDTH + B_WIDTH, B_WIDTH)
    vb = proj(3 * A_WIDTH + 2 * B_WIDTH, B_WIDTH)
    zb = proj(3 * A_WIDTH + 3 * B_WIDTH, B_WIDTH)

    ang = pos_ref[0].astype(F32) * invf_ref[...]
    lane = lax.broadcasted_iota(jnp.int32, (1, LANES), 1)
    first_half = (lane % B_HEAD_DIM) < (B_HEAD_DIM // 2)
    cos = jnp.cos(ang)
    sin = jnp.sin(ang)
    sin_signed = jnp.where(first_half, -sin, sin)

    q = qb * lax.rsqrt(_head_mean_sq(qb, bd_ref) + NORM_EPS) * qnw_ref[...]
    k = kb * lax.rsqrt(_head_mean_sq(kb, bd_ref) + NORM_EPS) * knw_ref[...]
    q = _rope(q, cos, sin_signed, first_half) * (B_HEAD_DIM ** -0.5 * LOG2E)
    k = _rope(k, cos, sin_signed, first_half)
    k_scr[i] = k.astype(BF16)
    vt_scr[i] = vb.T.astype(BF16)

    qt = q.T
    head_row = lax.broadcasted_iota(jnp.int32, (2 * B_HEAD_DIM, tq), 0)
    top = head_row < B_HEAD_DIM
    for g in range(B_HEADS):
        qp = qt[g * 2 * B_HEAD_DIM:(g + 1) * 2 * B_HEAD_DIM, :]
        qblk_scr[g] = jnp.concatenate([jnp.where(top, qp, 0.0), jnp.where(top, 0.0, qp)],
                                      axis=1).astype(BF16)

    m_scr[...] = jnp.full(m_scr.shape, -jnp.inf, F32)
    l_scr[...] = jnp.zeros(l_scr.shape, F32)
    acc_scr[...] = jnp.zeros(acc_scr.shape, F32)

    kv_row = lax.broadcasted_iota(jnp.int32, (tk, 2 * tq), 0)
    q_col = lax.broadcasted_iota(jnp.int32, (tk, 2 * tq), 1) % tq
    causal = kv_row <= q_col

    def scores(g, j, slot):
        kp = k_scr[j, :, g * LANES:(g + 1) * LANES]
        s_scr[slot] = jnp.dot(kp, qblk_scr[g], preferred_element_type=F32)

    def tile_update(g, j, slot, masked):
        s = s_scr[slot]
        if masked:
            s = jnp.where(causal, s, -jnp.inf)
        s3 = s.reshape(tk // SUBLANES, SUBLANES, 2 * tq)
        m_old = m_scr[g]
        m_new = jnp.maximum(m_old, _across_sublanes(jnp.max(s3, axis=0), jnp.maximum))
        alpha = jnp.exp2(m_old - m_new)
        p3 = jnp.exp2(s3 - m_new[None])
        m_scr[g] = m_new
        l_scr[g] = alpha * l_scr[g] + jnp.sum(p3, axis=0)
        pb = p3.reshape(tk, 2 * tq).astype(BF16)
        vt = vt_scr[j, g * B_V_DIM:(g + 1) * B_V_DIM, :]
        for st in range(2):
            pv = jnp.dot(vt, pb[:, st * tq:(st + 1) * tq], preferred_element_type=F32)
            a = alpha[:, st * tq:(st + 1) * tq]
            acc = acc_scr[2 * g + st].reshape(B_V_DIM // SUBLANES, SUBLANES, tq)
            acc_scr[2 * g + st] = (acc * a[None]).reshape(B_V_DIM, tq) + pv

    def kv_body(j, carry):
        for g in range(B_HEADS):
            if g + 1 < B_HEADS:
                scores(g + 1, j, (g + 1) % 2)
            else:
                scores(0, j + 1, 0)
            tile_update(g, j, g % 2, masked=False)
        return carry

    scores(0, 0, 0)
    lax.fori_loop(0, i, kv_body, 0)
    for g in range(B_HEADS):
        if g + 1 < B_HEADS:
            scores(g + 1, i, (g + 1) % 2)
        tile_update(g, i, g % 2, masked=True)

    lam = (jnp.exp(jnp.sum(lq1_ref[...] * lk1_ref[...], axis=-1, keepdims=True))
           - jnp.exp(jnp.sum(lq2_ref[...] * lk2_ref[...], axis=-1, keepdims=True))
           + LAMBDA_INIT)

    for g in range(B_HEADS):
        inv_l = 1.0 / _across_sublanes(l_scr[g], jnp.add)
        acc1 = acc_scr[2 * g].reshape(B_V_DIM // SUBLANES, SUBLANES, tq)
        acc2 = acc_scr[2 * g + 1].reshape(B_V_DIM // SUBLANES, SUBLANES, tq)
        ot = acc1 * inv_l[None, :, :tq] - lam * (acc2 * inv_l[None, :, tq:])
        o = ot.reshape(B_V_DIM, tq).T
        o = o * lax.rsqrt(jnp.mean(o * o, axis=-1, keepdims=True) + SUBLN_EPS) * subln_ref[...]
        o = o * (1.0 - LAMBDA_INIT)
        vsl = slice(g * B_V_DIM, (g + 1) * B_V_DIM)
        b_out = o * _silu(zb[:, vsl])
        mixed_scr[:, A_WIDTH + g * B_V_DIM:A_WIDTH + (g + 1) * B_V_DIM] = b_out.astype(BF16)

    out = jnp.dot(mixed_scr[...], wout_ref[...], preferred_element_type=F32)
    out_ref[0] = x + gate * out


def kernel(x, c, positions, norm_w, w_ada, b_ada, w_in, sgu_norm_w, w_s, b_s, q_norm_w, k_norm_w,
           lambda_q1, lambda_k1, lambda_q2, lambda_k2, subln_w, w_out):
    bsz, seq, d = x.shape
    assert d == D_MODEL and seq % SEQ_TILE == 0 and norm_w.shape[0] == 1
    nt = seq // SEQ_TILE

    mod = pl.pallas_call(
        _adaln_kernel,
        grid=(3,),
        in_specs=[pl.BlockSpec((bsz, d), lambda n: (0, 0)),
                  pl.BlockSpec((d, d), lambda n: (0, n)),
                  pl.BlockSpec((1, d), lambda n: (0, n))],
        out_specs=pl.BlockSpec((bsz, d), lambda n: (0, n)),
        out_shape=jax.ShapeDtypeStruct((bsz, 3 * d), F32),
        name="adaln_mod",
    )(c, w_ada[0], b_ada[0][None, :])
    mod = mod.reshape(bsz, 3, d)

    inv_freq = ROPE_THETA ** (-jnp.arange(0, B_HEAD_DIM, 2, dtype=F32) / B_HEAD_DIM)
    invf = jnp.tile(inv_freq, LANES // (B_HEAD_DIM // 2))[None, :]
    head_of_lane = np.arange(MXU_DIM) // B_HEAD_DIM
    bd = jnp.asarray(head_of_lane[:, None] == head_of_lane[None, :], dtype=BF16)

    const2 = lambda b, i: (0, 0)
    const3 = lambda b, i: (0, 0, 0)
    out = pl.pallas_call(
        _layer_kernel,
        grid=(bsz, nt),
        in_specs=[
            pl.BlockSpec((1, SEQ_TILE, d), lambda b, i: (b, i, 0)),
            pl.BlockSpec((1, 3, d), lambda b, i: (b, 0, 0)),
            pl.BlockSpec((1, SEQ_TILE, 1), lambda b, i: (b, i, 0)),
            pl.BlockSpec((1, LANES), const2),
            pl.BlockSpec((1, d), const2),
            pl.BlockSpec((d, IN_COLS), const2),
            pl.BlockSpec((A_HEADS, A_HEAD_DIM), const2),
            pl.BlockSpec((A_HEADS, CHUNK, CHUNK), const3),
            pl.BlockSpec((CHUNK, A_HEADS), const2),
            pl.BlockSpec((1, B_WIDTH), const2),
            pl.BlockSpec((1, B_WIDTH), const2),
            pl.BlockSpec((1, B_HEAD_DIM), const2),
            pl.BlockSpec((1, B_HEAD_DIM), const2),
            pl.BlockSpec((1, B_HEAD_DIM), const2),
            pl.BlockSpec((1, B_HEAD_DIM), const2),
            pl.BlockSpec((1, B_V_DIM), const2),
            pl.BlockSpec((A_WIDTH + B_WIDTH, d), const2),
            pl.BlockSpec((MXU_DIM, MXU_DIM), const2),
        ],
        out_specs=pl.BlockSpec((1, SEQ_TILE, d), lambda b, i: (b, i, 0)),
        out_shape=jax.ShapeDtypeStruct((bsz, seq, d), F32),
        scratch_shapes=[
            pltpu.VMEM((nt, SEQ_TILE, B_WIDTH), BF16),
            pltpu.VMEM((nt, B_WIDTH, SEQ_TILE), BF16),
            pltpu.VMEM((B_HEADS, 2 * B_HEAD_DIM, 2 * SEQ_TILE), BF16),
            pltpu.VMEM((B_HEADS, SUBLANES, 2 * SEQ_TILE), F32),
            pltpu.VMEM((B_HEADS, SUBLANES, 2 * SEQ_TILE), F32),
            pltpu.VMEM((2 * B_HEADS, B_V_DIM, SEQ_TILE), F32),
            pltpu.VMEM((SEQ_TILE, A_WIDTH + B_WIDTH), BF16),
            pltpu.VMEM((2, SEQ_TILE, 2 * SEQ_TILE), F32),
        ],
        compiler_params=pltpu.CompilerParams(
            dimension_semantics=("arbitrary", "arbitrary"),
            vmem_limit_bytes=VMEM_LIMIT_BYTES),
        name="fused_layer",
    )(x, mod, positions.reshape(bsz, seq, 1), invf, norm_w, w_in[0].astype(BF16),
      sgu_norm_w[0], w_s[0], jnp.transpose(b_s[0]),
      jnp.tile(q_norm_w[0], 2 * B_HEADS)[None, :], jnp.tile(k_norm_w[0], 2 * B_HEADS)[None, :],
      lambda_q1, lambda_k1, lambda_q2, lambda_k2, subln_w, w_out[0].astype(BF16), bd)
    return out
```

```python
import math

import numpy as np
import jax
import jax.numpy as jnp
from jax import lax
from jax.experimental import pallas as pl
from jax.experimental.pallas import tpu as pltpu

D_MODEL = 1024
A_WIDTH = 512
B_WIDTH = 512
A_HEADS = 4
A_HEAD_DIM = 128
CHUNK = 128
B_HEADS = 4
B_HEAD_DIM = 64
B_V_DIM = 128
ROPE_THETA = 10000.0
NORM_EPS = 1e-6
SUBLN_EPS = 1e-5
IN_COLS = 3 * A_WIDTH + 4 * B_WIDTH
LAMBDA_INIT = 0.8 - 0.6 * math.exp(-0.3 * 0)
LOG2E = math.log2(math.e)

LANES = 128
SUBLANES = 8
MXU_DIM = 256
SEQ_TILE = 256
VMEM_LIMIT_BYTES = 56 * 1024 * 1024
HALF = B_HEAD_DIM // 2

F32 = jnp.float32
BF16 = jnp.bfloat16


def _pair_layout(t):
    lead = t.shape[:-1]
    t = t.reshape(lead + (B_HEADS, 2, 2, HALF))
    t = jnp.swapaxes(t, -3, -2)
    return t.reshape(lead + (B_WIDTH,))


def _adaln_kernel(c_ref, w_ref, b_ref, mod_ref):
    c = c_ref[...]
    c_act = c * jax.nn.sigmoid(c)
    mod_ref[...] = jnp.dot(c_act.astype(BF16), w_ref[...].astype(BF16),
                           preferred_element_type=F32) + b_ref[...]


def _silu(z):
    return z * jax.nn.sigmoid(z)


def _head_mean_sq(t, bd_ref):
    sq = (t * t).astype(BF16)
    halves = [jnp.dot(sq[:, h * MXU_DIM:(h + 1) * MXU_DIM], bd_ref[...],
                      preferred_element_type=F32) for h in range(B_WIDTH // MXU_DIM)]
    return jnp.concatenate(halves, axis=-1) * (1.0 / B_HEAD_DIM)


def _rope(t, cos, sin_signed):
    cols = []
    for cb in range(B_WIDTH // LANES):
        tc = t[:, cb * LANES:(cb + 1) * LANES]
        cols.append(tc * cos + pltpu.roll(tc, LANES // 2, 1) * sin_signed)
    return jnp.concatenate(cols, axis=-1)


def _rope_tables(posf, invf, lane):
    n_grp = LANES // HALF
    rows = posf.shape[0]
    qrows = rows // n_grp
    grp = lane // HALF
    packed = posf[(n_grp - 1) * qrows:, :]
    for gi in range(n_grp - 2, -1, -1):
        packed = jnp.where(grp == gi, posf[gi * qrows:(gi + 1) * qrows, :], packed)
    ang = packed * invf
    tables = []
    for packed_tab in (jnp.cos(ang), jnp.sin(ang)):
        parts = []
        for gi in range(n_grp):
            t = jnp.where(grp == gi, packed_tab, 0.0)
            t = t + pltpu.roll(t, 2 * HALF, 1)
            parts.append(t + pltpu.roll(t, HALF, 1))
        tables.append(jnp.concatenate(parts, axis=0))
    return tables


def _across_sublanes(x, op):
    for shift in (4, 2, 1):
        x = op(x, pltpu.roll(x, shift, 0))
    return x


def _layer_kernel(x_ref, mod_ref, pos_ref, invf_ref, normw_ref, win_ref, sgu_ref, ws_ref,
                  bst_ref, qnw_ref, knw_ref, lq1_ref, lk1_ref, lq2_ref, lk2_ref, subln_ref,
                  wout_ref, bd_ref, out_ref,
                  k_scr, vt_scr, qblk_scr, m_scr, l_scr, acc_scr, mixed_scr, s_scr, p_scr,
                  alpha_scr):
    i = pl.program_id(1)
    tq = SEQ_TILE
    tk = SEQ_TILE

    x = x_ref[0]
    shift = mod_ref[0, 0:1, :]
    scale = mod_ref[0, 1:2, :]
    gate = mod_ref[0, 2:3, :]
    ms = jnp.mean(x * x, axis=-1, keepdims=True)
    h = (x * lax.rsqrt(ms + NORM_EPS) * normw_ref[...]) * (1.0 + scale) + shift
    hb = h.astype(BF16)

    lane = lax.broadcasted_iota(jnp.int32, (1, LANES), 1)
    cos, sin = _rope_tables(pos_ref[0].astype(F32), invf_ref[...], lane)
    sin_signed = jnp.where(lane < LANES // 2, -sin, sin)

    def proj(c0, width):
        return jnp.dot(hb, win_ref[:, c0:c0 + width], preferred_element_type=F32)

    qb = proj(3 * A_WIDTH, B_WIDTH)
    kb = proj(3 * A_WIDTH + B_WIDTH, B_WIDTH)
    vb = proj(3 * A_WIDTH + 2 * B_WIDTH, B_WIDTH)
    zb = proj(3 * A_WIDTH + 3 * B_WIDTH, B_WIDTH)

    q = qb * lax.rsqrt(_head_mean_sq(qb, bd_ref) + NORM_EPS) * qnw_ref[...]
    k = kb * lax.rsqrt(_head_mean_sq(kb, bd_ref) + NORM_EPS) * knw_ref[...]
    q = _rope(q, cos, sin_signed) * (B_HEAD_DIM ** -0.5 * LOG2E)
    k = _rope(k, cos, sin_signed)
    k_scr[i] = k.astype(BF16)
    vt_scr[i] = vb.T.astype(BF16)

    qt = q.T
    pair_row = lax.broadcasted_iota(jnp.int32, (LANES, tq), 0)
    is_a = (pair_row // HALF) % 2 == 0
    for g in range(B_HEADS):
        qp = qt[g * LANES:(g + 1) * LANES, :]
        qblk_scr[g] = jnp.concatenate([jnp.where(is_a, qp, 0.0), jnp.where(is_a, 0.0, qp)],
                                      axis=1).astype(BF16)

    ua = proj(0, A_WIDTH)
    va = proj(A_WIDTH, A_WIDTH)
    za = proj(2 * A_WIDTH, A_WIDTH)
    row_c = lax.broadcasted_iota(jnp.int32, (CHUNK, CHUNK), 0)
    col_c = lax.broadcasted_iota(jnp.int32, (CHUNK, CHUNK), 1)
    tril = row_c >= col_c
    for hh in range(A_HEADS):
        sl = slice(hh * A_HEAD_DIM, (hh + 1) * A_HEAD_DIM)
        v = va[:, sl]
        vn = v * lax.rsqrt(jnp.mean(v * v, axis=-1, keepdims=True) + NORM_EPS) * sgu_ref[hh:hh + 1, :]
        vnb = vn.astype(BF16)
        ws = jnp.where(tril, ws_ref[hh], 0.0).astype(BF16)
        bias = bst_ref[:, hh:hh + 1]
        mixes = [jnp.dot(ws, vnb[c * CHUNK:(c + 1) * CHUNK, :], preferred_element_type=F32) + bias
                 for c in range(tq // CHUNK)]
        mix = jnp.concatenate(mixes, axis=0)
        a_out = ua[:, sl] * mix * _silu(za[:, sl])
        mixed_scr[:, sl] = a_out.astype(BF16)

    m_scr[...] = jnp.full(m_scr.shape, -jnp.inf, F32)
    l_scr[...] = jnp.zeros(l_scr.shape, F32)
    acc_scr[...] = jnp.zeros(acc_scr.shape, F32)

    kv_row = lax.broadcasted_iota(jnp.int32, (tk, 2 * tq), 0)
    q_col = lax.broadcasted_iota(jnp.int32, (tk, 2 * tq), 1) % tq
    causal = kv_row <= q_col

    def scores(g, j, slot):
        kp = k_scr[j, :, g * LANES:(g + 1) * LANES]
        s_scr[slot] = jnp.dot(kp, qblk_scr[g], preferred_element_type=F32)

    def softmax(g, slot, masked):
        s = s_scr[slot]
        if masked:
            s = jnp.where(causal, s, -jnp.inf)
        s3 = s.reshape(tk // SUBLANES, SUBLANES, 2 * tq)
        m_old = m_scr[g]
        m_new = jnp.maximum(m_old, _across_sublanes(jnp.max(s3, axis=0), jnp.maximum))
        alpha = jnp.exp2(m_old - m_new)
        p3 = jnp.exp2(s3 - m_new[None])
        m_scr[g] = m_new
        l_scr[g] = alpha * l_scr[g] + jnp.sum(p3, axis=0)
        alpha_scr[slot] = alpha
        p_scr[slot] = p3.reshape(tk, 2 * tq).astype(BF16)

    def value_update(g, j, slot):
        vt = vt_scr[j, g * B_V_DIM:(g + 1) * B_V_DIM, :]
        alpha = alpha_scr[slot]
        for st in range(2):
            pv = jnp.dot(vt, p_scr[slot, :, st * tq:(st + 1) * tq], preferred_element_type=F32)
            a = alpha[:, st * tq:(st + 1) * tq]
            acc = acc_scr[2 * g + st].reshape(B_V_DIM // SUBLANES, SUBLANES, tq)
            acc_scr[2 * g + st] = (acc * a[None]).reshape(B_V_DIM, tq) + pv

    last = B_HEADS - 1
    p_scr[last % 2] = jnp.zeros((tk, 2 * tq), BF16)
    alpha_scr[last % 2] = jnp.ones((SUBLANES, 2 * tq), F32)

    def unit(g, j, masked):
        if g < last:
            scores(g + 1, j, (g + 1) % 2)
        elif not masked:
            scores(0, j + 1, 0)
        if g > 0:
            value_update(g - 1, j, (g - 1) % 2)
        else:
            value_update(last, jnp.maximum(j - 1, 0), last % 2)
        softmax(g, g % 2, masked)

    def kv_body(j, carry):
        for g in range(B_HEADS):
            unit(g, j, masked=False)
        return carry

    scores(0, 0, 0)
    lax.fori_loop(0, i, kv_body, 0)
    for g in range(B_HEADS):
        unit(g, i, masked=True)
    value_update(last, i, last % 2)

    lam = (jnp.exp(jnp.sum(lq1_ref[...] * lk1_ref[...], axis=-1, keepdims=True))
           - jnp.exp(jnp.sum(lq2_ref[...] * lk2_ref[...], axis=-1, keepdims=True))
           + LAMBDA_INIT)

    for g in range(B_HEADS):
        inv_l = 1.0 / _across_sublanes(l_scr[g], jnp.add)
        acc1 = acc_scr[2 * g].reshape(B_V_DIM // SUBLANES, SUBLANES, tq)
        acc2 = acc_scr[2 * g + 1].reshape(B_V_DIM // SUBLANES, SUBLANES, tq)
        ot = acc1 * inv_l[None, :, :tq] - lam * (acc2 * inv_l[None, :, tq:])
        o = ot.reshape(B_V_DIM, tq).T
        o = o * lax.rsqrt(jnp.mean(o * o, axis=-1, keepdims=True) + SUBLN_EPS) * subln_ref[...]
        o = o * (1.0 - LAMBDA_INIT)
        vsl = slice(g * B_V_DIM, (g + 1) * B_V_DIM)
        b_out = o * _silu(zb[:, vsl])
        mixed_scr[:, A_WIDTH + g * B_V_DIM:A_WIDTH + (g + 1) * B_V_DIM] = b_out.astype(BF16)

    out = jnp.dot(mixed_scr[...], wout_ref[...], preferred_element_type=F32)
    out_ref[0] = x + gate * out


def kernel(x, c, positions, norm_w, w_ada, b_ada, w_in, sgu_norm_w, w_s, b_s, q_norm_w, k_norm_w,
           lambda_q1, lambda_k1, lambda_q2, lambda_k2, subln_w, w_out):
    bsz, seq, d = x.shape
    assert d == D_MODEL and seq % SEQ_TILE == 0 and norm_w.shape[0] == 1
    nt = seq // SEQ_TILE

    mod = pl.pallas_call(
        _adaln_kernel,
        grid=(3,),
        in_specs=[pl.BlockSpec((bsz, d), lambda n: (0, 0)),
                  pl.BlockSpec((d, d), lambda n: (0, n)),
                  pl.BlockSpec((1, d), lambda n: (0, n))],
        out_specs=pl.BlockSpec((bsz, d), lambda n: (0, n)),
        out_shape=jax.ShapeDtypeStruct((bsz, 3 * d), F32),
        name="adaln_mod",
    )(c, w_ada[0], b_ada[0][None, :])
    mod = mod.reshape(bsz, 3, d)

    inv_freq = ROPE_THETA ** (-jnp.arange(0, B_HEAD_DIM, 2, dtype=F32) / B_HEAD_DIM)
    invf = jnp.tile(inv_freq, LANES // HALF)[None, :]
    lane_head = (np.arange(MXU_DIM) // LANES) * 2 + (np.arange(MXU_DIM) // HALF) % 2
    bd = jnp.asarray(lane_head[:, None] == lane_head[None, :], dtype=BF16)

    q0, k0, v0 = 3 * A_WIDTH, 3 * A_WIDTH + B_WIDTH, 3 * A_WIDTH + 2 * B_WIDTH
    w_in_l = w_in[0]
    w_in_b = jnp.concatenate(
        [w_in_l[:, :q0], _pair_layout(w_in_l[:, q0:k0]), _pair_layout(w_in_l[:, k0:v0]),
         w_in_l[:, v0:]], axis=1).astype(BF16)
    qnw = _pair_layout(jnp.tile(q_norm_w[0], 2 * B_HEADS))[None, :]
    knw = _pair_layout(jnp.tile(k_norm_w[0], 2 * B_HEADS))[None, :]

    const2 = lambda b, i: (0, 0)
    const3 = lambda b, i: (0, 0, 0)
    out = pl.pallas_call(
        _layer_kernel,
        grid=(bsz, nt),
        in_specs=[
            pl.BlockSpec((1, SEQ_TILE, d), lambda b, i: (b, i, 0)),
            pl.BlockSpec((1, 3, d), lambda b, i: (b, 0, 0)),
            pl.BlockSpec((1, SEQ_TILE, 1), lambda b, i: (b, i, 0)),
            pl.BlockSpec((1, LANES), const2),
            pl.BlockSpec((1, d), const2),
            pl.BlockSpec((d, IN_COLS), const2),
            pl.BlockSpec((A_HEADS, A_HEAD_DIM), const2),
            pl.BlockSpec((A_HEADS, CHUNK, CHUNK), const3),
            pl.BlockSpec((CHUNK, A_HEADS), const2),
            pl.BlockSpec((1, B_WIDTH), const2),
            pl.BlockSpec((1, B_WIDTH), const2),
            pl.BlockSpec((1, B_HEAD_DIM), const2),
            pl.BlockSpec((1, B_HEAD_DIM), const2),
            pl.BlockSpec((1, B_HEAD_DIM), const2),
            pl.BlockSpec((1, B_HEAD_DIM), const2),
            pl.BlockSpec((1, B_V_DIM), const2),
            pl.BlockSpec((A_WIDTH + B_WIDTH, d), const2),
            pl.BlockSpec((MXU_DIM, MXU_DIM), const2),
        ],
        out_specs=pl.BlockSpec((1, SEQ_TILE, d), lambda b, i: (b, i, 0)),
        out_shape=jax.ShapeDtypeStruct((bsz, seq, d), F32),
        scratch_shapes=[
            pltpu.VMEM((nt, SEQ_TILE, B_WIDTH), BF16),
            pltpu.VMEM((nt, B_WIDTH, SEQ_TILE), BF16),
            pltpu.VMEM((B_HEADS, LANES, 2 * SEQ_TILE), BF16),
            pltpu.VMEM((B_HEADS, SUBLANES, 2 * SEQ_TILE), F32),
            pltpu.VMEM((B_HEADS, SUBLANES, 2 * SEQ_TILE), F32),
            pltpu.VMEM((2 * B_HEADS, B_V_DIM, SEQ_TILE), F32),
            pltpu.VMEM((SEQ_TILE, A_WIDTH + B_WIDTH), BF16),
            pltpu.VMEM((2, SEQ_TILE, 2 * SEQ_TILE), F32),
            pltpu.VMEM((2, SEQ_TILE, 2 * SEQ_TILE), BF16),
            pltpu.VMEM((2, SUBLANES, 2 * SEQ_TILE), F32),
        ],
        compiler_params=pltpu.CompilerParams(
            dimension_semantics=("arbitrary", "arbitrary"),
            vmem_limit_bytes=VMEM_LIMIT_BYTES),
        name="fused_layer",
    )(x, mod, positions.reshape(bsz, seq, 1), invf, norm_w, w_in_b,
      sgu_norm_w[0], w_s[0], jnp.transpose(b_s[0]), qnw, knw,
      lambda_q1, lambda_k1, lambda_q2, lambda_k2, subln_w, w_out[0].astype(BF16), bd)
    return out
```

```python
import math

import numpy as np
import jax
import jax.numpy as jnp
from jax import lax
from jax.experimental import pallas as pl
from jax.experimental.pallas import tpu as pltpu

D_MODEL = 1024
A_WIDTH = 512
B_WIDTH = 512
A_HEADS = 4
A_HEAD_DIM = 128
CHUNK = 128
B_HEADS = 4
B_HEAD_DIM = 64
B_V_DIM = 128
ROPE_THETA = 10000.0
NORM_EPS = 1e-6
SUBLN_EPS = 1e-5
IN_COLS = 3 * A_WIDTH + 4 * B_WIDTH
LAMBDA_INIT = 0.8 - 0.6 * math.exp(-0.3 * 0)
LOG2E = math.log2(math.e)

LANES = 128
SUBLANES = 8
MXU_DIM = 256
SEQ_TILE = 256
VMEM_LIMIT_BYTES = 56 * 1024 * 1024
HALF = B_HEAD_DIM // 2

F32 = jnp.float32
BF16 = jnp.bfloat16


def _pair_layout(t):
    lead = t.shape[:-1]
    t = t.reshape(lead + (B_HEADS, 2, 2, HALF))
    t = jnp.swapaxes(t, -3, -2)
    return t.reshape(lead + (B_WIDTH,))


def _prep_w_in_kernel(w_ref, o_ref):
    grp = lax.broadcasted_iota(jnp.int32, (1, LANES), 1) // HALF
    q0, v0 = 3 * A_WIDTH, 3 * A_WIDTH + 2 * B_WIDTH
    for cb in range(IN_COLS // LANES):
        t = w_ref[:, cb * LANES:(cb + 1) * LANES]
        if q0 <= cb * LANES < v0:
            t = jnp.where(grp == 1, pltpu.roll(t, LANES - HALF, 1),
                          jnp.where(grp == 2, pltpu.roll(t, HALF, 1), t))
        o_ref[:, cb * LANES:(cb + 1) * LANES] = t.astype(BF16)


def _adaln_kernel(c_ref, w_ref, b_ref, mod_ref):
    c = c_ref[...]
    c_act = c * jax.nn.sigmoid(c)
    mod_ref[...] = jnp.dot(c_act.astype(BF16), w_ref[...].astype(BF16),
                           preferred_element_type=F32) + b_ref[...]


def _silu(z):
    return z * jax.nn.sigmoid(z)


def _head_mean_sq(t, bd_ref):
    sq = (t * t).astype(BF16)
    halves = [jnp.dot(sq[:, h * MXU_DIM:(h + 1) * MXU_DIM], bd_ref[...],
                      preferred_element_type=F32) for h in range(B_WIDTH // MXU_DIM)]
    return jnp.concatenate(halves, axis=-1) * (1.0 / B_HEAD_DIM)


def _rope(t, cos, sin_signed):
    cols = []
    for cb in range(B_WIDTH // LANES):
        tc = t[:, cb * LANES:(cb + 1) * LANES]
        cols.append(tc * cos + pltpu.roll(tc, LANES // 2, 1) * sin_signed)
    return jnp.concatenate(cols, axis=-1)


def _rope_tables(pos_quarters, invf, lane):
    n_grp = LANES // HALF
    qrows = SEQ_TILE // n_grp
    grp = lane // HALF
    pos_t = pos_quarters.T
    packed = pos_t[:qrows, n_grp - 1:n_grp]
    for gi in range(n_grp - 2, -1, -1):
        packed = jnp.where(grp == gi, pos_t[:qrows, gi:gi + 1], packed)
    ang = packed * invf
    tables = []
    for packed_tab in (jnp.cos(ang), jnp.sin(ang)):
        parts = []
        for gi in range(n_grp):
            t = jnp.where(grp == gi, packed_tab, 0.0)
            t = t + pltpu.roll(t, 2 * HALF, 1)
            parts.append(t + pltpu.roll(t, HALF, 1))
        tables.append(jnp.concatenate(parts, axis=0))
    return tables


def _across_sublanes(x, op):
    for shift in (4, 2, 1):
        x = op(x, pltpu.roll(x, shift, 0))
    return x


def _layer_kernel(x_ref, mod_ref, pos_ref, invf_ref, normw_ref, win_ref, sgu_ref, ws_ref,
                  bst_ref, qnw_ref, knw_ref, lq1_ref, lk1_ref, lq2_ref, lk2_ref, subln_ref,
                  wout_ref, bd_ref, out_ref,
                  k_scr, vt_scr, qblk_scr, m_scr, l_scr, acc_scr, mixed_scr, s_scr, p_scr,
                  alpha_scr):
    i = pl.program_id(1)
    tq = SEQ_TILE
    tk = SEQ_TILE

    x = x_ref[0]
    b = pl.program_id(0)
    shift = mod_ref[0, pl.ds(b, 1), :]
    scale = mod_ref[1, pl.ds(b, 1), :]
    gate = mod_ref[2, pl.ds(b, 1), :]
    ms = jnp.mean(x * x, axis=-1, keepdims=True)
    h = (x * lax.rsqrt(ms + NORM_EPS) * normw_ref[...]) * (1.0 + scale) + shift
    hb = h.astype(BF16)

    lane = lax.broadcasted_iota(jnp.int32, (1, LANES), 1)
    cos, sin = _rope_tables(pos_ref[...].astype(F32), invf_ref[...], lane)
    sin_signed = jnp.where(lane < LANES // 2, -sin, sin)

    def proj(c0, width):
        return jnp.dot(hb, win_ref[:, c0:c0 + width], preferred_element_type=F32)

    qb = proj(3 * A_WIDTH, B_WIDTH)
    kb = proj(3 * A_WIDTH + B_WIDTH, B_WIDTH)
    vb = proj(3 * A_WIDTH + 2 * B_WIDTH, B_WIDTH)
    zb = proj(3 * A_WIDTH + 3 * B_WIDTH, B_WIDTH)

    q = qb * lax.rsqrt(_head_mean_sq(qb, bd_ref) + NORM_EPS) * qnw_ref[...]
    k = kb * lax.rsqrt(_head_mean_sq(kb, bd_ref) + NORM_EPS) * knw_ref[...]
    q = _rope(q, cos, sin_signed) * (B_HEAD_DIM ** -0.5 * LOG2E)
    k = _rope(k, cos, sin_signed)
    k_scr[i] = k.astype(BF16)
    vt_scr[i] = vb.T.astype(BF16)

    qt = q.T
    pair_row = lax.broadcasted_iota(jnp.int32, (LANES, tq), 0)
    is_a = (pair_row // HALF) % 2 == 0
    for g in range(B_HEADS):
        qp = qt[g * LANES:(g + 1) * LANES, :]
        qblk_scr[g] = jnp.concatenate([jnp.where(is_a, qp, 0.0), jnp.where(is_a, 0.0, qp)],
                                      axis=1).astype(BF16)

    ua = proj(0, A_WIDTH)
    va = proj(A_WIDTH, A_WIDTH)
    za = proj(2 * A_WIDTH, A_WIDTH)
    row_c = lax.broadcasted_iota(jnp.int32, (CHUNK, CHUNK), 0)
    col_c = lax.broadcasted_iota(jnp.int32, (CHUNK, CHUNK), 1)
    tril = row_c >= col_c
    for hh in range(A_HEADS):
        sl = slice(hh * A_HEAD_DIM, (hh + 1) * A_HEAD_DIM)
        v = va[:, sl]
        vn = v * lax.rsqrt(jnp.mean(v * v, axis=-1, keepdims=True) + NORM_EPS) * sgu_ref[hh:hh + 1, :]
        vnb = vn.astype(BF16)
        ws = jnp.where(tril, ws_ref[hh], 0.0).astype(BF16)
        bias = bst_ref[:, hh:hh + 1]
        mixes = [jnp.dot(ws, vnb[c * CHUNK:(c + 1) * CHUNK, :], preferred_element_type=F32) + bias
                 for c in range(tq // CHUNK)]
        mix = jnp.concatenate(mixes, axis=0)
        a_out = ua[:, sl] * mix * _silu(za[:, sl])
        mixed_scr[:, sl] = a_out.astype(BF16)

    m_scr[...] = jnp.full(m_scr.shape, -jnp.inf, F32)
    l_scr[...] = jnp.zeros(l_scr.shape, F32)
    acc_scr[...] = jnp.zeros(acc_scr.shape, F32)

    kv_row = lax.broadcasted_iota(jnp.int32, (tk, 2 * tq), 0)
    q_col = lax.broadcasted_iota(jnp.int32, (tk, 2 * tq), 1) % tq
    causal = kv_row <= q_col

    def scores(g, j, slot):
        kp = k_scr[j, :, g * LANES:(g + 1) * LANES]
        s_scr[slot] = jnp.dot(kp, qblk_scr[g], preferred_element_type=F32)

    def softmax(g, slot, masked):
        s = s_scr[slot]
        if masked:
            s = jnp.where(causal, s, -jnp.inf)
        s3 = s.reshape(tk // SUBLANES, SUBLANES, 2 * tq)
        m_old = m_scr[g]
        m_new = jnp.maximum(m_old, _across_sublanes(jnp.max(s3, axis=0), jnp.maximum))
        alpha = jnp.exp2(m_old - m_new)
        p3 = jnp.exp2(s3 - m_new[None])
        m_scr[g] = m_new
        l_scr[g] = alpha * l_scr[g] + jnp.sum(p3, axis=0)
        alpha_scr[slot] = alpha
        p_scr[slot] = p3.reshape(tk, 2 * tq).astype(BF16)

    def value_update(g, j, slot):
        vt = vt_scr[j, g * B_V_DIM:(g + 1) * B_V_DIM, :]
        alpha = alpha_scr[slot]
        for st in range(2):
            pv = jnp.dot(vt, p_scr[slot, :, st * tq:(st + 1) * tq], preferred_element_type=F32)
            a = alpha[:, st * tq:(st + 1) * tq]
            acc = acc_scr[2 * g + st].reshape(B_V_DIM // SUBLANES, SUBLANES, tq)
            acc_scr[2 * g + st] = (acc * a[None]).reshape(B_V_DIM, tq) + pv

    last = B_HEADS - 1
    p_scr[last % 2] = jnp.zeros((tk, 2 * tq), BF16)
    alpha_scr[last % 2] = jnp.ones((SUBLANES, 2 * tq), F32)

    def unit(g, j, masked):
        if g < last:
            scores(g + 1, j, (g + 1) % 2)
        elif not masked:
            scores(0, j + 1, 0)
        if g > 0:
            value_update(g - 1, j, (g - 1) % 2)
        else:
            value_update(last, jnp.maximum(j - 1, 0), last % 2)
        softmax(g, g % 2, masked)

    def kv_body(j, carry):
        for g in range(B_HEADS):
            unit(g, j, masked=False)
        return carry

    scores(0, 0, 0)
    lax.fori_loop(0, i, kv_body, 0)
    for g in range(B_HEADS):
        unit(g, i, masked=True)
    value_update(last, i, last % 2)

    lam = (jnp.exp(jnp.sum(lq1_ref[...] * lk1_ref[...], axis=-1, keepdims=True))
           - jnp.exp(jnp.sum(lq2_ref[...] * lk2_ref[...], axis=-1, keepdims=True))
           + LAMBDA_INIT)

    for g in range(B_HEADS):
        inv_l = 1.0 / _across_sublanes(l_scr[g], jnp.add)
        acc1 = acc_scr[2 * g].reshape(B_V_DIM // SUBLANES, SUBLANES, tq)
        acc2 = acc_scr[2 * g + 1].reshape(B_V_DIM // SUBLANES, SUBLANES, tq)
        ot = acc1 * inv_l[None, :, :tq] - lam * (acc2 * inv_l[None, :, tq:])
        o = ot.reshape(B_V_DIM, tq).T
        o = o * lax.rsqrt(jnp.mean(o * o, axis=-1, keepdims=True) + SUBLN_EPS) * subln_ref[...]
        o = o * (1.0 - LAMBDA_INIT)
        vsl = slice(g * B_V_DIM, (g + 1) * B_V_DIM)
        b_out = o * _silu(zb[:, vsl])
        mixed_scr[:, A_WIDTH + g * B_V_DIM:A_WIDTH + (g + 1) * B_V_DIM] = b_out.astype(BF16)

    out = jnp.dot(mixed_scr[...], wout_ref[...], preferred_element_type=F32)
    out_ref[0] = x + gate * out


def kernel(x, c, positions, norm_w, w_ada, b_ada, w_in, sgu_norm_w, w_s, b_s, q_norm_w, k_norm_w,
           lambda_q1, lambda_k1, lambda_q2, lambda_k2, subln_w, w_out):
    bsz, seq, d = x.shape
    assert d == D_MODEL and seq % SEQ_TILE == 0 and norm_w.shape[0] == 1
    nt = seq // SEQ_TILE

    mod = pl.pallas_call(
        _adaln_kernel,
        grid=(3,),
        in_specs=[pl.BlockSpec((bsz, d), lambda n: (0, 0)),
                  pl.BlockSpec((None, d, d), lambda n: (0, 0, n)),
                  pl.BlockSpec((1, d), lambda n: (0, n))],
        out_specs=pl.BlockSpec((None, bsz, d), lambda n: (n, 0, 0)),
        out_shape=jax.ShapeDtypeStruct((3, bsz, d), F32),
        name="adaln_mod",
    )(c, w_ada, b_ada)

    n_grp = LANES // HALF
    pos_q = jnp.pad(positions.reshape(bsz, nt, n_grp, SEQ_TILE // n_grp),
                    ((0, 0), (0, 0), (0, SUBLANES - n_grp), (0, LANES - SEQ_TILE // n_grp)))

    inv_freq = ROPE_THETA ** (-jnp.arange(0, B_HEAD_DIM, 2, dtype=F32) / B_HEAD_DIM)
    invf = jnp.tile(inv_freq, LANES // HALF)[None, :]
    lane_head = (np.arange(MXU_DIM) // LANES) * 2 + (np.arange(MXU_DIM) // HALF) % 2
    bd = jnp.asarray(lane_head[:, None] == lane_head[None, :], dtype=BF16)

    prep_rows = d // 8
    w_in_b = pl.pallas_call(
        _prep_w_in_kernel,
        grid=(d // prep_rows,),
        in_specs=[pl.BlockSpec((None, prep_rows, IN_COLS), lambda r: (0, r, 0))],
        out_specs=pl.BlockSpec((prep_rows, IN_COLS), lambda r: (r, 0)),
        out_shape=jax.ShapeDtypeStruct((d, IN_COLS), BF16),
        name="prep_w_in",
    )(w_in)
    qnw = _pair_layout(jnp.tile(q_norm_w[0], 2 * B_HEADS))[None, :]
    knw = _pair_layout(jnp.tile(k_norm_w[0], 2 * B_HEADS))[None, :]

    const2 = lambda b, i: (0, 0)
    const3 = lambda b, i: (0, 0, 0)
    out = pl.pallas_call(
        _layer_kernel,
        grid=(bsz, nt),
        in_specs=[
            pl.BlockSpec((1, SEQ_TILE, d), lambda b, i: (b, i, 0)),
            pl.BlockSpec((3, bsz, d), const3),
            pl.BlockSpec((None, None, SUBLANES, LANES), lambda b, i: (b, i, 0, 0)),
            pl.BlockSpec((1, LANES), const2),
            pl.BlockSpec((1, d), const2),
            pl.BlockSpec((d, IN_COLS), const2),
            pl.BlockSpec((A_HEADS, A_HEAD_DIM), const2),
            pl.BlockSpec((A_HEADS, CHUNK, CHUNK), const3),
            pl.BlockSpec((CHUNK, A_HEADS), const2),
            pl.BlockSpec((1, B_WIDTH), const2),
            pl.BlockSpec((1, B_WIDTH), const2),
            pl.BlockSpec((1, B_HEAD_DIM), const2),
            pl.BlockSpec((1, B_HEAD_DIM), const2),
            pl.BlockSpec((1, B_HEAD_DIM), const2),
            pl.BlockSpec((1, B_HEAD_DIM), const2),
            pl.BlockSpec((1, B_V_DIM), const2),
            pl.BlockSpec((A_WIDTH + B_WIDTH, d), const2),
            pl.BlockSpec((MXU_DIM, MXU_DIM), const2),
        ],
        out_specs=pl.BlockSpec((1, SEQ_TILE, d), lambda b, i: (b, i, 0)),
        out_shape=jax.ShapeDtypeStruct((bsz, seq, d), F32),
        scratch_shapes=[
            pltpu.VMEM((nt, SEQ_TILE, B_WIDTH), BF16),
            pltpu.VMEM((nt, B_WIDTH, SEQ_TILE), BF16),
            pltpu.VMEM((B_HEADS, LANES, 2 * SEQ_TILE), BF16),
            pltpu.VMEM((B_HEADS, SUBLANES, 2 * SEQ_TILE), F32),
            pltpu.VMEM((B_HEADS, SUBLANES, 2 * SEQ_TILE), F32),
            pltpu.VMEM((2 * B_HEADS, B_V_DIM, SEQ_TILE), F32),
            pltpu.VMEM((SEQ_TILE, A_WIDTH + B_WIDTH), BF16),
            pltpu.VMEM((2, SEQ_TILE, 2 * SEQ_TILE), F32),
            pltpu.VMEM((2, SEQ_TILE, 2 * SEQ_TILE), BF16),
            pltpu.VMEM((2, SUBLANES, 2 * SEQ_TILE), F32),
        ],
        compiler_params=pltpu.CompilerParams(
            dimension_semantics=("arbitrary", "arbitrary"),
            vmem_limit_bytes=VMEM_LIMIT_BYTES),
        name="fused_layer",
    )(x, mod, pos_q, invf, norm_w, w_in_b,
      sgu_norm_w[0], w_s[0], jnp.transpose(b_s[0]), qnw, knw,
      lambda_q1, lambda_k1, lambda_q2, lambda_k2, subln_w, w_out[0].astype(BF16), bd)
    return out
```

```python
import math

import numpy as np
import jax
import jax.numpy as jnp
from jax import lax
from jax.experimental import pallas as pl
from jax.experimental.pallas import tpu as pltpu

D_MODEL = 1024
A_WIDTH = 512
B_WIDTH = 512
A_HEADS = 4
A_HEAD_DIM = 128
CHUNK = 128
B_HEADS = 4
B_HEAD_DIM = 64
B_V_DIM = 128
ROPE_THETA = 10000.0
NORM_EPS = 1e-6
SUBLN_EPS = 1e-5
IN_COLS = 3 * A_WIDTH + 4 * B_WIDTH
LAMBDA_INIT = 0.8 - 0.6 * math.exp(-0.3 * 0)
LOG2E = math.log2(math.e)

LANES = 128
SUBLANES = 8
MXU_DIM = 256
SEQ_TILE = 256
VMEM_LIMIT_BYTES = 56 * 1024 * 1024
HALF = B_HEAD_DIM // 2

F32 = jnp.float32
BF16 = jnp.bfloat16


def _pair_layout(t):
    lead = t.shape[:-1]
    t = t.reshape(lead + (B_HEADS, 2, 2, HALF))
    t = jnp.swapaxes(t, -3, -2)
    return t.reshape(lead + (B_WIDTH,))


def _prep_w_in_kernel(w_ref, o_ref):
    grp = lax.broadcasted_iota(jnp.int32, (1, LANES), 1) // HALF
    q0, v0 = 3 * A_WIDTH, 3 * A_WIDTH + 2 * B_WIDTH
    for cb in range(IN_COLS // LANES):
        t = w_ref[:, cb * LANES:(cb + 1) * LANES]
        if q0 <= cb * LANES < v0:
            t = jnp.where(grp == 1, pltpu.roll(t, LANES - HALF, 1),
                          jnp.where(grp == 2, pltpu.roll(t, HALF, 1), t))
        o_ref[:, cb * LANES:(cb + 1) * LANES] = t.astype(BF16)


def _adaln_kernel(c_ref, w_ref, b_ref, mod_ref):
    c = c_ref[...]
    c_act = c * jax.nn.sigmoid(c)
    mod_ref[...] = jnp.dot(c_act.astype(BF16), w_ref[...].astype(BF16),
                           preferred_element_type=F32) + b_ref[...]


def _silu(z):
    return z * jax.nn.sigmoid(z)


def _head_mean_sq(t, bd_ref):
    sq = (t * t).astype(BF16)
    halves = [jnp.dot(sq[:, h * MXU_DIM:(h + 1) * MXU_DIM], bd_ref[...],
                      preferred_element_type=F32) for h in range(B_WIDTH // MXU_DIM)]
    return jnp.concatenate(halves, axis=-1) * (1.0 / B_HEAD_DIM)


def _rope(t, cos, sin_signed):
    cols = []
    for cb in range(B_WIDTH // LANES):
        tc = t[:, cb * LANES:(cb + 1) * LANES]
        cols.append(tc * cos + pltpu.roll(tc, LANES // 2, 1) * sin_signed)
    return jnp.concatenate(cols, axis=-1)


def _rope_tables(pos_quarters, invf, lane):
    n_grp = LANES // HALF
    qrows = SEQ_TILE // n_grp
    grp = lane // HALF
    pos_t = pos_quarters.T
    packed = pos_t[:qrows, n_grp - 1:n_grp]
    for gi in range(n_grp - 2, -1, -1):
        packed = jnp.where(grp == gi, pos_t[:qrows, gi:gi + 1], packed)
    ang = packed * invf
    tables = []
    for packed_tab in (jnp.cos(ang), jnp.sin(ang)):
        parts = []
        for gi in range(n_grp):
            t = jnp.where(grp == gi, packed_tab, 0.0)
            t = t + pltpu.roll(t, 2 * HALF, 1)
            parts.append(t + pltpu.roll(t, HALF, 1))
        tables.append(jnp.concatenate(parts, axis=0))
    return tables


def _across_sublanes(x, op):
    for shift in (4, 2, 1):
        x = op(x, pltpu.roll(x, shift, 0))
    return x


def _layer_kernel(x_ref, mod_ref, pos_ref, invf_ref, normw_ref, win_ref, sgu_ref, ws_ref,
                  bst_ref, qnw_ref, knw_ref, lq1_ref, lk1_ref, lq2_ref, lk2_ref, subln_ref,
                  wout_ref, bd_ref, out_ref,
                  k_scr, vt_scr, qblk_scr, m_scr, l_scr, acc_scr, mixed_scr, s_scr, p_scr,
                  alpha_scr, smax_scr):
    i = pl.program_id(1)
    tq = SEQ_TILE
    tk = SEQ_TILE

    x = x_ref[0]
    b = pl.program_id(0)
    shift = mod_ref[0, pl.ds(b, 1), :]
    scale = mod_ref[1, pl.ds(b, 1), :]
    gate = mod_ref[2, pl.ds(b, 1), :]
    ms = jnp.mean(x * x, axis=-1, keepdims=True)
    h = (x * lax.rsqrt(ms + NORM_EPS) * normw_ref[...]) * (1.0 + scale) + shift
    hb = h.astype(BF16)

    lane = lax.broadcasted_iota(jnp.int32, (1, LANES), 1)
    cos, sin = _rope_tables(pos_ref[...].astype(F32), invf_ref[...], lane)
    sin_signed = jnp.where(lane < LANES // 2, -sin, sin)

    def proj(c0, width):
        return jnp.dot(hb, win_ref[:, c0:c0 + width], preferred_element_type=F32)

    qb = proj(3 * A_WIDTH, B_WIDTH)
    kb = proj(3 * A_WIDTH + B_WIDTH, B_WIDTH)
    q = qb * lax.rsqrt(_head_mean_sq(qb, bd_ref) + NORM_EPS) * qnw_ref[...]
    k = kb * lax.rsqrt(_head_mean_sq(kb, bd_ref) + NORM_EPS) * knw_ref[...]
    q = _rope(q, cos, sin_signed) * (B_HEAD_DIM ** -0.5 * LOG2E)
    k = _rope(k, cos, sin_signed)
    k_scr[i] = k.astype(BF16)

    qt = q.T
    pair_row = lax.broadcasted_iota(jnp.int32, (LANES, tq), 0)
    is_a = (pair_row // HALF) % 2 == 0
    for g in range(B_HEADS):
        qp = qt[g * LANES:(g + 1) * LANES, :]
        qblk_scr[g] = jnp.concatenate([jnp.where(is_a, qp, 0.0), jnp.where(is_a, 0.0, qp)],
                                      axis=1).astype(BF16)

    m_scr[...] = jnp.full(m_scr.shape, -jnp.inf, F32)
    l_scr[...] = jnp.zeros(l_scr.shape, F32)
    acc_scr[...] = jnp.zeros(acc_scr.shape, F32)

    kv_row = lax.broadcasted_iota(jnp.int32, (tk, 2 * tq), 0)
    q_col = lax.broadcasted_iota(jnp.int32, (tk, 2 * tq), 1) % tq
    causal = kv_row <= q_col

    def scores(g, j, slot):
        kp = k_scr[j, :, g * LANES:(g + 1) * LANES]
        s = jnp.dot(kp, qblk_scr[g], preferred_element_type=F32)
        s_scr[slot] = s
        smax_scr[slot] = jnp.max(s.reshape(tk // SUBLANES, SUBLANES, 2 * tq), axis=0)

    def softmax(g, slot, masked):
        rows = 2 * SUBLANES

        def chunk(c):
            sc = s_scr[slot, c * rows:(c + 1) * rows, :]
            if masked:
                sc = jnp.where(causal[c * rows:(c + 1) * rows, :], sc, -jnp.inf)
            return sc.reshape(2, SUBLANES, 2 * tq)

        if masked:
            col_max = jnp.max(chunk(0), axis=0)
            for c in range(1, tk // rows):
                col_max = jnp.maximum(col_max, jnp.max(chunk(c), axis=0))
        else:
            col_max = smax_scr[slot]
        m_old = m_scr[g]
        m_new = jnp.maximum(m_old, _across_sublanes(col_max, jnp.maximum))
        alpha = jnp.exp2(m_old - m_new)
        m_scr[g] = m_new
        alpha_scr[slot] = alpha
        l_new = alpha * l_scr[g]
        for c in range(tk // rows):
            pc = jnp.exp2(chunk(c) - m_new[None])
            l_new = l_new + (pc[0] + pc[1])
            p_scr[slot, c * rows:(c + 1) * rows, :] = pc.reshape(rows, 2 * tq).astype(BF16)
        l_scr[g] = l_new

    def value_update(g, j, slot):
        vt = vt_scr[j, g * B_V_DIM:(g + 1) * B_V_DIM, :]
        alpha = alpha_scr[slot]
        for st in range(2):
            pv = jnp.dot(vt, p_scr[slot, :, st * tq:(st + 1) * tq], preferred_element_type=F32)
            a = alpha[:, st * tq:(st + 1) * tq]
            acc = acc_scr[2 * g + st].reshape(B_V_DIM // SUBLANES, SUBLANES, tq)
            acc_scr[2 * g + st] = (acc * a[None]).reshape(B_V_DIM, tq) + pv

    last = B_HEADS - 1

    def unit(g, masked, next_tile, prev_tile):
        scores((g + 1) % B_HEADS, next_tile, (g + 1) % 2)
        if prev_tile is not None:
            value_update((g - 1) % B_HEADS, prev_tile, (g - 1) % 2)
        softmax(g, g % 2, masked)

    vb = proj(3 * A_WIDTH + 2 * B_WIDTH, B_WIDTH)
    vt_scr[i] = vb.T.astype(BF16)
    va = proj(A_WIDTH, A_WIDTH)
    scores(0, i, 0)
    ua = proj(0, A_WIDTH)
    unit(0, True, i, None)
    za = proj(2 * A_WIDTH, A_WIDTH)
    unit(1, True, i, i)
    zb = proj(3 * A_WIDTH + 3 * B_WIDTH, B_WIDTH)
    unit(2, True, i, i)

    row_c = lax.broadcasted_iota(jnp.int32, (CHUNK, CHUNK), 0)
    col_c = lax.broadcasted_iota(jnp.int32, (CHUNK, CHUNK), 1)
    tril = row_c >= col_c
    for hh in range(A_HEADS):
        sl = slice(hh * A_HEAD_DIM, (hh + 1) * A_HEAD_DIM)
        v = va[:, sl]
        vn = v * lax.rsqrt(jnp.mean(v * v, axis=-1, keepdims=True) + NORM_EPS) * sgu_ref[hh:hh + 1, :]
        vnb = vn.astype(BF16)
        ws = jnp.where(tril, ws_ref[hh], 0.0).astype(BF16)
        bias = bst_ref[:, hh:hh + 1]
        n_chunks = tq // CHUNK
        chunks = jnp.concatenate([vnb[c * CHUNK:(c + 1) * CHUNK, :] for c in range(n_chunks)],
                                 axis=1)
        mixed_chunks = jnp.dot(ws, chunks, preferred_element_type=F32)
        mix = jnp.concatenate([mixed_chunks[:, c * A_HEAD_DIM:(c + 1) * A_HEAD_DIM] + bias
                               for c in range(n_chunks)], axis=0)
        a_out = ua[:, sl] * mix * _silu(za[:, sl])
        mixed_scr[:, sl] = a_out.astype(BF16)
    zb_gate = _silu(zb)
    unit(3, True, 0, i)

    def kv_body(j, carry):
        unit(0, False, j, jnp.where(j == 0, i, j - 1))
        unit(1, False, j, j)
        unit(2, False, j, j)
        unit(3, False, j + 1, j)
        return carry

    lax.fori_loop(0, i, kv_body, 0)
    value_update(last, jnp.maximum(i - 1, 0), last % 2)
    out_ref[0] = x + gate * jnp.dot(mixed_scr[:, :A_WIDTH], wout_ref[:A_WIDTH, :],
                                    preferred_element_type=F32)

    lam = (jnp.exp(jnp.sum(lq1_ref[...] * lk1_ref[...], axis=-1, keepdims=True))
           - jnp.exp(jnp.sum(lq2_ref[...] * lk2_ref[...], axis=-1, keepdims=True))
           + LAMBDA_INIT)

    for g in range(B_HEADS):
        inv_l = 1.0 / _across_sublanes(l_scr[g], jnp.add)
        acc1 = acc_scr[2 * g].reshape(B_V_DIM // SUBLANES, SUBLANES, tq)
        acc2 = acc_scr[2 * g + 1].reshape(B_V_DIM // SUBLANES, SUBLANES, tq)
        ot = acc1 * inv_l[None, :, :tq] - lam * (acc2 * inv_l[None, :, tq:])
        o = ot.reshape(B_V_DIM, tq).T
        o = o * lax.rsqrt(jnp.mean(o * o, axis=-1, keepdims=True) + SUBLN_EPS) * subln_ref[...]
        o = o * (1.0 - LAMBDA_INIT)
        vsl = slice(g * B_V_DIM, (g + 1) * B_V_DIM)
        b_out = o * zb_gate[:, vsl]
        mixed_scr[:, A_WIDTH + g * B_V_DIM:A_WIDTH + (g + 1) * B_V_DIM] = b_out.astype(BF16)

    out_ref[0] += gate * jnp.dot(mixed_scr[:, A_WIDTH:], wout_ref[A_WIDTH:, :],
                                 preferred_element_type=F32)


def kernel(x, c, positions, norm_w, w_ada, b_ada, w_in, sgu_norm_w, w_s, b_s, q_norm_w, k_norm_w,
           lambda_q1, lambda_k1, lambda_q2, lambda_k2, subln_w, w_out):
    bsz, seq, d = x.shape
    assert d == D_MODEL and seq % SEQ_TILE == 0 and norm_w.shape[0] == 1
    nt = seq // SEQ_TILE

    mod = pl.pallas_call(
        _adaln_kernel,
        grid=(3,),
        in_specs=[pl.BlockSpec((bsz, d), lambda n: (0, 0)),
                  pl.BlockSpec((None, d, d), lambda n: (0, 0, n)),
                  pl.BlockSpec((1, d), lambda n: (0, n))],
        out_specs=pl.BlockSpec((None, bsz, d), lambda n: (n, 0, 0)),
        out_shape=jax.ShapeDtypeStruct((3, bsz, d), F32),
        name="adaln_mod",
    )(c, w_ada, b_ada)

    n_grp = LANES // HALF
    pos_q = jnp.pad(positions.reshape(bsz, nt, n_grp, SEQ_TILE // n_grp),
                    ((0, 0), (0, 0), (0, SUBLANES - n_grp), (0, LANES - SEQ_TILE // n_grp)))

    inv_freq = ROPE_THETA ** (-jnp.arange(0, B_HEAD_DIM, 2, dtype=F32) / B_HEAD_DIM)
    invf = jnp.tile(inv_freq, LANES // HALF)[None, :]
    lane_head = (np.arange(MXU_DIM) // LANES) * 2 + (np.arange(MXU_DIM) // HALF) % 2
    bd = jnp.asarray(lane_head[:, None] == lane_head[None, :], dtype=BF16)

    prep_rows = d // 8
    w_in_b = pl.pallas_call(
        _prep_w_in_kernel,
        grid=(d // prep_rows,),
        in_specs=[pl.BlockSpec((None, prep_rows, IN_COLS), lambda r: (0, r, 0))],
        out_specs=pl.BlockSpec((prep_rows, IN_COLS), lambda r: (r, 0)),
        out_shape=jax.ShapeDtypeStruct((d, IN_COLS), BF16),
        name="prep_w_in",
    )(w_in)
    qnw = _pair_layout(jnp.tile(q_norm_w[0], 2 * B_HEADS))[None, :]
    knw = _pair_layout(jnp.tile(k_norm_w[0], 2 * B_HEADS))[None, :]

    const2 = lambda b, i: (0, 0)
    const3 = lambda b, i: (0, 0, 0)
    out = pl.pallas_call(
        _layer_kernel,
        grid=(bsz, nt),
        in_specs=[
            pl.BlockSpec((1, SEQ_TILE, d), lambda b, i: (b, i, 0)),
            pl.BlockSpec((3, bsz, d), const3),
            pl.BlockSpec((None, None, SUBLANES, LANES), lambda b, i: (b, i, 0, 0)),
            pl.BlockSpec((1, LANES), const2),
            pl.BlockSpec((1, d), const2),
            pl.BlockSpec((d, IN_COLS), const2),
            pl.BlockSpec((A_HEADS, A_HEAD_DIM), const2),
            pl.BlockSpec((A_HEADS, CHUNK, CHUNK), const3),
            pl.BlockSpec((CHUNK, A_HEADS), const2),
            pl.BlockSpec((1, B_WIDTH), const2),
            pl.BlockSpec((1, B_WIDTH), const2),
            pl.BlockSpec((1, B_HEAD_DIM), const2),
            pl.BlockSpec((1, B_HEAD_DIM), const2),
            pl.BlockSpec((1, B_HEAD_DIM), const2),
            pl.BlockSpec((1, B_HEAD_DIM), const2),
            pl.BlockSpec((1, B_V_DIM), const2),
            pl.BlockSpec((A_WIDTH + B_WIDTH, d), const2),
            pl.BlockSpec((MXU_DIM, MXU_DIM), const2),
        ],
        out_specs=pl.BlockSpec((1, SEQ_TILE, d), lambda b, i: (b, i, 0)),
        out_shape=jax.ShapeDtypeStruct((bsz, seq, d), F32),
        scratch_shapes=[
            pltpu.VMEM((nt, SEQ_TILE, B_WIDTH), BF16),
            pltpu.VMEM((nt, B_WIDTH, SEQ_TILE), BF16),
            pltpu.VMEM((B_HEADS, LANES, 2 * SEQ_TILE), BF16),
            pltpu.VMEM((B_HEADS, SUBLANES, 2 * SEQ_TILE), F32),
            pltpu.VMEM((B_HEADS, SUBLANES, 2 * SEQ_TILE), F32),
            pltpu.VMEM((2 * B_HEADS, B_V_DIM, SEQ_TILE), F32),
            pltpu.VMEM((SEQ_TILE, A_WIDTH + B_WIDTH), BF16),
            pltpu.VMEM((2, SEQ_TILE, 2 * SEQ_TILE), F32),
            pltpu.VMEM((2, SEQ_TILE, 2 * SEQ_TILE), BF16),
            pltpu.VMEM((2, SUBLANES, 2 * SEQ_TILE), F32),
            pltpu.VMEM((2, SUBLANES, 2 * SEQ_TILE), F32),
        ],
        compiler_params=pltpu.CompilerParams(
            dimension_semantics=("arbitrary", "arbitrary"),
            vmem_limit_bytes=VMEM_LIMIT_BYTES),
        name="fused_layer",
    )(x, mod, pos_q, invf, norm_w, w_in_b,
      sgu_norm_w[0], w_s[0], jnp.transpose(b_s[0]), qnw, knw,
      lambda_q1, lambda_k1, lambda_q2, lambda_k2, subln_w, w_out[0].astype(BF16), bd)
    return out
```

```python
import math

import numpy as np
import jax
import jax.numpy as jnp
from jax import lax
from jax.experimental import pallas as pl
from jax.experimental.pallas import tpu as pltpu

D_MODEL = 1024
A_WIDTH = 512
B_WIDTH = 512
A_HEADS = 4
A_HEAD_DIM = 128
CHUNK = 128
B_HEADS = 4
B_HEAD_DIM = 64
B_V_DIM = 128
ROPE_THETA = 10000.0
NORM_EPS = 1e-6
SUBLN_EPS = 1e-5
IN_COLS = 3 * A_WIDTH + 4 * B_WIDTH
LAMBDA_INIT = 0.8 - 0.6 * math.exp(-0.3 * 0)
LOG2E = math.log2(math.e)

LANES = 128
SUBLANES = 8
MXU_DIM = 256
SEQ_TILE = 256
VMEM_LIMIT_BYTES = 56 * 1024 * 1024
HALF = B_HEAD_DIM // 2
BF16_ROWS = 16
V_EXT = B_V_DIM + BF16_ROWS

F32 = jnp.float32
BF16 = jnp.bfloat16


def _pair_layout(t):
    lead = t.shape[:-1]
    t = t.reshape(lead + (B_HEADS, 2, 2, HALF))
    t = jnp.swapaxes(t, -3, -2)
    return t.reshape(lead + (B_WIDTH,))


def _prep_w_in_kernel(w_ref, o_ref):
    grp = lax.broadcasted_iota(jnp.int32, (1, LANES), 1) // HALF
    q0, v0 = 3 * A_WIDTH, 3 * A_WIDTH + 2 * B_WIDTH
    for cb in range(IN_COLS // LANES):
        t = w_ref[:, cb * LANES:(cb + 1) * LANES]
        if q0 <= cb * LANES < v0:
            t = jnp.where(grp == 1, pltpu.roll(t, LANES - HALF, 1),
                          jnp.where(grp == 2, pltpu.roll(t, HALF, 1), t))
        o_ref[:, cb * LANES:(cb + 1) * LANES] = t.astype(BF16)


def _adaln_kernel(c_ref, w_ref, b_ref, mod_ref):
    c = c_ref[...]
    c_act = c * jax.nn.sigmoid(c)
    mod_ref[...] = jnp.dot(c_act.astype(BF16), w_ref[...].astype(BF16),
                           preferred_element_type=F32) + b_ref[...]


def _silu(z):
    return z * jax.nn.sigmoid(z)


def _head_mean_sq(t, bd_ref):
    sq = (t * t).astype(BF16)
    halves = [jnp.dot(sq[:, h * MXU_DIM:(h + 1) * MXU_DIM], bd_ref[...],
                      preferred_element_type=F32) for h in range(B_WIDTH // MXU_DIM)]
    return jnp.concatenate(halves, axis=-1) * (1.0 / B_HEAD_DIM)


def _rope(t, cos, sin_signed):
    cols = []
    for cb in range(B_WIDTH // LANES):
        tc = t[:, cb * LANES:(cb + 1) * LANES]
        cols.append(tc * cos + pltpu.roll(tc, LANES // 2, 1) * sin_signed)
    return jnp.concatenate(cols, axis=-1)


def _rope_tables(pos_quarters, invf, lane):
    n_grp = LANES // HALF
    qrows = SEQ_TILE // n_grp
    grp = lane // HALF
    pos_t = pos_quarters.T
    packed = pos_t[:qrows, n_grp - 1:n_grp]
    for gi in range(n_grp - 2, -1, -1):
        packed = jnp.where(grp == gi, pos_t[:qrows, gi:gi + 1], packed)
    ang = packed * invf
    tables = []
    for packed_tab in (jnp.cos(ang), jnp.sin(ang)):
        parts = []
        for gi in range(n_grp):
            t = jnp.where(grp == gi, packed_tab, 0.0)
            t = t + pltpu.roll(t, 2 * HALF, 1)
            parts.append(t + pltpu.roll(t, HALF, 1))
        tables.append(jnp.concatenate(parts, axis=0))
    return tables


def _across_sublanes(x, op):
    for shift in (4, 2, 1):
        x = op(x, pltpu.roll(x, shift, 0))
    return x


def _layer_kernel(x_ref, mod_ref, pos_ref, invf_ref, normw_ref, win_ref, sgu_ref, ws_ref,
                  bst_ref, qnw_ref, knw_ref, lq1_ref, lk1_ref, lq2_ref, lk2_ref, subln_ref,
                  wout_ref, bd_ref, out_ref,
                  k_scr, vt_scr, qblk_scr, m_scr, acc_scr, mixed_scr, s_scr, p_scr,
                  alpha_scr, smax_scr):
    i = pl.program_id(1)
    tq = SEQ_TILE
    tk = SEQ_TILE

    x = x_ref[0]
    b = pl.program_id(0)
    shift = mod_ref[0, pl.ds(b, 1), :]
    scale = mod_ref[1, pl.ds(b, 1), :]
    gate = mod_ref[2, pl.ds(b, 1), :]
    ms = jnp.mean(x * x, axis=-1, keepdims=True)
    h = (x * lax.rsqrt(ms + NORM_EPS) * normw_ref[...]) * (1.0 + scale) + shift
    hb = h.astype(BF16)

    lane = lax.broadcasted_iota(jnp.int32, (1, LANES), 1)
    cos, sin = _rope_tables(pos_ref[...].astype(F32), invf_ref[...], lane)
    sin_signed = jnp.where(lane < LANES // 2, -sin, sin)

    def proj(c0, width):
        return jnp.dot(hb, win_ref[:, c0:c0 + width], preferred_element_type=F32)

    qb = proj(3 * A_WIDTH, B_WIDTH)
    kb = proj(3 * A_WIDTH + B_WIDTH, B_WIDTH)
    q = qb * lax.rsqrt(_head_mean_sq(qb, bd_ref) + NORM_EPS) * qnw_ref[...]
    k = kb * lax.rsqrt(_head_mean_sq(kb, bd_ref) + NORM_EPS) * knw_ref[...]
    q = _rope(q, cos, sin_signed) * (B_HEAD_DIM ** -0.5 * LOG2E)
    k = _rope(k, cos, sin_signed)
    k_scr[i] = k.astype(BF16)

    qt = q.T
    pair_row = lax.broadcasted_iota(jnp.int32, (LANES, tq), 0)
    is_a = (pair_row // HALF) % 2 == 0
    for g in range(B_HEADS):
        qp = qt[g * LANES:(g + 1) * LANES, :]
        qblk_scr[g] = jnp.concatenate([jnp.where(is_a, qp, 0.0), jnp.where(is_a, 0.0, qp)],
                                      axis=1).astype(BF16)

    m_scr[...] = jnp.full(m_scr.shape, -jnp.inf, F32)
    acc_scr[...] = jnp.zeros(acc_scr.shape, F32)

    kv_row = lax.broadcasted_iota(jnp.int32, (tk, 2 * tq), 0)
    q_col = lax.broadcasted_iota(jnp.int32, (tk, 2 * tq), 1) % tq
    causal = kv_row <= q_col

    def scores(g, j):
        kp = k_scr[j, :, g * LANES:(g + 1) * LANES]
        s = jnp.dot(kp, qblk_scr[g], preferred_element_type=F32)
        s_scr[g] = s
        smax_scr[g] = jnp.max(s.reshape(tk // SUBLANES, SUBLANES, 2 * tq), axis=0)

    def softmax(g, slot, masked):
        rows = 2 * SUBLANES

        def chunk(c):
            sc = s_scr[g, c * rows:(c + 1) * rows, :]
            if masked:
                sc = jnp.where(causal[c * rows:(c + 1) * rows, :], sc, -jnp.inf)
            return sc.reshape(2, SUBLANES, 2 * tq)

        if masked:
            col_max = jnp.max(chunk(0), axis=0)
            for c in range(1, tk // rows):
                col_max = jnp.maximum(col_max, jnp.max(chunk(c), axis=0))
        else:
            col_max = smax_scr[g]
        m_old = m_scr[g]
        m_new = jnp.maximum(m_old, _across_sublanes(col_max, jnp.maximum))
        alpha = jnp.exp2(m_old - m_new)
        m_scr[g] = m_new
        alpha_scr[slot] = alpha
        for c in range(tk // rows):
            pc = jnp.exp2(chunk(c) - m_new[None])
            p_scr[slot, c * rows:(c + 1) * rows, :] = pc.reshape(rows, 2 * tq).astype(BF16)

    def value_update(g, j, slot):
        vt = vt_scr[j, g]
        alpha = alpha_scr[slot]
        for st in range(2):
            pv = jnp.dot(vt, p_scr[slot, :, st * tq:(st + 1) * tq], preferred_element_type=F32)
            a = alpha[:, st * tq:(st + 1) * tq]
            acc = acc_scr[2 * g + st].reshape(V_EXT // SUBLANES, SUBLANES, tq)
            acc_scr[2 * g + st] = (acc * a[None]).reshape(V_EXT, tq) + pv

    last = B_HEADS - 1

    def unit(g, masked, tile, succ_tile, prev_tile):
        ahead = (g + 2) % B_HEADS
        scores(ahead, tile if g + 2 < B_HEADS else succ_tile)
        if prev_tile is not None:
            value_update((g - 1) % B_HEADS, prev_tile, (g - 1) % 2)
        softmax(g, g % 2, masked)

    vb = proj(3 * A_WIDTH + 2 * B_WIDTH, B_WIDTH)
    vbt = vb.T.astype(BF16)
    for g in range(B_HEADS):
        vt_scr[i, g, :B_V_DIM, :] = vbt[g * B_V_DIM:(g + 1) * B_V_DIM, :]
        vt_scr[i, g, B_V_DIM:, :] = jnp.ones((V_EXT - B_V_DIM, tk), BF16)
    va = proj(A_WIDTH, A_WIDTH)
    scores(0, i)
    scores(1, i)
    ua = proj(0, A_WIDTH)
    unit(0, True, i, 0, None)
    za = proj(2 * A_WIDTH, A_WIDTH)
    unit(1, True, i, 0, i)
    zb = proj(3 * A_WIDTH + 3 * B_WIDTH, B_WIDTH)
    unit(2, True, i, 0, i)

    row_c = lax.broadcasted_iota(jnp.int32, (CHUNK, CHUNK), 0)
    col_c = lax.broadcasted_iota(jnp.int32, (CHUNK, CHUNK), 1)
    tril = row_c >= col_c
    for hh in range(A_HEADS):
        sl = slice(hh * A_HEAD_DIM, (hh + 1) * A_HEAD_DIM)
        v = va[:, sl]
        vn = v * lax.rsqrt(jnp.mean(v * v, axis=-1, keepdims=True) + NORM_EPS) * sgu_ref[hh:hh + 1, :]
        vnb = vn.astype(BF16)
        ws = jnp.where(tril, ws_ref[hh], 0.0).astype(BF16)
        bias = bst_ref[:, hh:hh + 1]
        n_chunks = tq // CHUNK
        chunks = jnp.concatenate([vnb[c * CHUNK:(c + 1) * CHUNK, :] for c in range(n_chunks)],
                                 axis=1)
        mixed_chunks = jnp.dot(ws, chunks, preferred_element_type=F32)
        mix = jnp.concatenate([mixed_chunks[:, c * A_HEAD_DIM:(c + 1) * A_HEAD_DIM] + bias
                               for c in range(n_chunks)], axis=0)
        a_out = ua[:, sl] * mix * _silu(za[:, sl])
        mixed_scr[:, sl] = a_out.astype(BF16)
    zb_gate = _silu(zb)
    unit(3, True, i, 0, i)

    def tile_units(j):
        unit(0, False, j, j + 1, jnp.where(j == 0, i, j - 1))
        for g in range(1, B_HEADS):
            unit(g, False, j, j + 1, j)

    def pair_body(jj, carry):
        tile_units(2 * jj)
        tile_units(2 * jj + 1)
        return carry

    lax.fori_loop(0, i // 2, pair_body, 0)

    @pl.when(i % 2 == 1)
    def _():
        tile_units(i - 1)

    value_update(last, jnp.maximum(i - 1, 0), last % 2)
    out_ref[0] = x + gate * jnp.dot(mixed_scr[:, :A_WIDTH], wout_ref[:A_WIDTH, :],
                                    preferred_element_type=F32)

    lam = (jnp.exp(jnp.sum(lq1_ref[...] * lk1_ref[...], axis=-1, keepdims=True))
           - jnp.exp(jnp.sum(lq2_ref[...] * lk2_ref[...], axis=-1, keepdims=True))
           + LAMBDA_INIT)

    for g in range(B_HEADS):
        inv_l1 = 1.0 / acc_scr[2 * g, B_V_DIM:B_V_DIM + SUBLANES, :]
        inv_l2 = 1.0 / acc_scr[2 * g + 1, B_V_DIM:B_V_DIM + SUBLANES, :]
        acc1 = acc_scr[2 * g, :B_V_DIM, :].reshape(B_V_DIM // SUBLANES, SUBLANES, tq)
        acc2 = acc_scr[2 * g + 1, :B_V_DIM, :].reshape(B_V_DIM // SUBLANES, SUBLANES, tq)
        ot = acc1 * inv_l1[None] - lam * (acc2 * inv_l2[None])
        o = ot.reshape(B_V_DIM, tq).T
        o = o * lax.rsqrt(jnp.mean(o * o, axis=-1, keepdims=True) + SUBLN_EPS) * subln_ref[...]
        o = o * (1.0 - LAMBDA_INIT)
        vsl = slice(g * B_V_DIM, (g + 1) * B_V_DIM)
        b_out = o * zb_gate[:, vsl]
        mixed_scr[:, A_WIDTH + g * B_V_DIM:A_WIDTH + (g + 1) * B_V_DIM] = b_out.astype(BF16)

    out_ref[0] += gate * jnp.dot(mixed_scr[:, A_WIDTH:], wout_ref[A_WIDTH:, :],
                                 preferred_element_type=F32)


def kernel(x, c, positions, norm_w, w_ada, b_ada, w_in, sgu_norm_w, w_s, b_s, q_norm_w, k_norm_w,
           lambda_q1, lambda_k1, lambda_q2, lambda_k2, subln_w, w_out):
    bsz, seq, d = x.shape
    assert d == D_MODEL and seq % SEQ_TILE == 0 and norm_w.shape[0] == 1
    nt = seq // SEQ_TILE

    mod = pl.pallas_call(
        _adaln_kernel,
        grid=(3,),
        in_specs=[pl.BlockSpec((bsz, d), lambda n: (0, 0)),
                  pl.BlockSpec((None, d, d), lambda n: (0, 0, n)),
                  pl.BlockSpec((1, d), lambda n: (0, n))],
        out_specs=pl.BlockSpec((None, bsz, d), lambda n: (n, 0, 0)),
        out_shape=jax.ShapeDtypeStruct((3, bsz, d), F32),
        name="adaln_mod",
    )(c, w_ada, b_ada)

    n_grp = LANES // HALF
    pos_q = jnp.pad(positions.reshape(bsz, nt, n_grp, SEQ_TILE // n_grp),
                    ((0, 0), (0, 0), (0, SUBLANES - n_grp), (0, LANES - SEQ_TILE // n_grp)))

    inv_freq = ROPE_THETA ** (-jnp.arange(0, B_HEAD_DIM, 2, dtype=F32) / B_HEAD_DIM)
    invf = jnp.tile(inv_freq, LANES // HALF)[None, :]
    lane_head = (np.arange(MXU_DIM) // LANES) * 2 + (np.arange(MXU_DIM) // HALF) % 2
    bd = jnp.asarray(lane_head[:, None] == lane_head[None, :], dtype=BF16)

    prep_rows = d // 8
    w_in_b = pl.pallas_call(
        _prep_w_in_kernel,
        grid=(d // prep_rows,),
        in_specs=[pl.BlockSpec((None, prep_rows, IN_COLS), lambda r: (0, r, 0))],
        out_specs=pl.BlockSpec((prep_rows, IN_COLS), lambda r: (r, 0)),
        out_shape=jax.ShapeDtypeStruct((d, IN_COLS), BF16),
        name="prep_w_in",
    )(w_in)
    qnw = _pair_layout(jnp.tile(q_norm_w[0], 2 * B_HEADS))[None, :]
    knw = _pair_layout(jnp.tile(k_norm_w[0], 2 * B_HEADS))[None, :]

    const2 = lambda b, i: (0, 0)
    const3 = lambda b, i: (0, 0, 0)
    out = pl.pallas_call(
        _layer_kernel,
        grid=(bsz, nt),
        in_specs=[
            pl.BlockSpec((1, SEQ_TILE, d), lambda b, i: (b, i, 0)),
            pl.BlockSpec((3, bsz, d), const3),
            pl.BlockSpec((None, None, SUBLANES, LANES), lambda b, i: (b, i, 0, 0)),
            pl.BlockSpec((1, LANES), const2),
            pl.BlockSpec((1, d), const2),
            pl.BlockSpec((d, IN_COLS), const2),
            pl.BlockSpec((A_HEADS, A_HEAD_DIM), const2),
            pl.BlockSpec((A_HEADS, CHUNK, CHUNK), const3),
            pl.BlockSpec((CHUNK, A_HEADS), const2),
            pl.BlockSpec((1, B_WIDTH), const2),
            pl.BlockSpec((1, B_WIDTH), const2),
            pl.BlockSpec((1, B_HEAD_DIM), const2),
            pl.BlockSpec((1, B_HEAD_DIM), const2),
            pl.BlockSpec((1, B_HEAD_DIM), const2),
            pl.BlockSpec((1, B_HEAD_DIM), const2),
            pl.BlockSpec((1, B_V_DIM), const2),
            pl.BlockSpec((A_WIDTH + B_WIDTH, d), const2),
            pl.BlockSpec((MXU_DIM, MXU_DIM), const2),
        ],
        out_specs=pl.BlockSpec((1, SEQ_TILE, d), lambda b, i: (b, i, 0)),
        out_shape=jax.ShapeDtypeStruct((bsz, seq, d), F32),
        scratch_shapes=[
            pltpu.VMEM((nt, SEQ_TILE, B_WIDTH), BF16),
            pltpu.VMEM((nt, B_HEADS, V_EXT, SEQ_TILE), BF16),
            pltpu.VMEM((B_HEADS, LANES, 2 * SEQ_TILE), BF16),
            pltpu.VMEM((B_HEADS, SUBLANES, 2 * SEQ_TILE), F32),
            pltpu.VMEM((2 * B_HEADS, V_EXT, SEQ_TILE), F32),
            pltpu.VMEM((SEQ_TILE, A_WIDTH + B_WIDTH), BF16),
            pltpu.VMEM((B_HEADS, SEQ_TILE, 2 * SEQ_TILE), F32),
            pltpu.VMEM((2, SEQ_TILE, 2 * SEQ_TILE), BF16),
            pltpu.VMEM((2, SUBLANES, 2 * SEQ_TILE), F32),
            pltpu.VMEM((B_HEADS, SUBLANES, 2 * SEQ_TILE), F32),
        ],
        compiler_params=pltpu.CompilerParams(
            dimension_semantics=("arbitrary", "arbitrary"),
            vmem_limit_bytes=VMEM_LIMIT_BYTES),
        name="fused_layer",
    )(x, mod, pos_q, invf, norm_w, w_in_b,
      sgu_norm_w[0], w_s[0], jnp.transpose(b_s[0]), qnw, knw,
      lambda_q1, lambda_k1, lambda_q2, lambda_k2, subln_w, w_out[0].astype(BF16), bd)
    return out
```

```python
import math

import numpy as np
import jax
import jax.numpy as jnp
from jax import lax
from jax.experimental import pallas as pl
from jax.experimental.pallas import tpu as pltpu

D_MODEL = 1024
A_WIDTH = 512
B_WIDTH = 512
A_HEADS = 4
A_HEAD_DIM = 128
CHUNK = 128
B_HEADS = 4
B_HEAD_DIM = 64
B_V_DIM = 128
ROPE_THETA = 10000.0
NORM_EPS = 1e-6
SUBLN_EPS = 1e-5
IN_COLS = 3 * A_WIDTH + 4 * B_WIDTH
LAMBDA_INIT = 0.8 - 0.6 * math.exp(-0.3 * 0)
LOG2E = math.log2(math.e)

LANES = 128
SUBLANES = 8
MXU_DIM = 256
SEQ_TILE = 256
VMEM_LIMIT_BYTES = 56 * 1024 * 1024
HALF = B_HEAD_DIM // 2
BF16_ROWS = 16
V_EXT = B_V_DIM + BF16_ROWS

F32 = jnp.float32
BF16 = jnp.bfloat16


def _pair_layout(t):
    lead = t.shape[:-1]
    t = t.reshape(lead + (B_HEADS, 2, 2, HALF))
    t = jnp.swapaxes(t, -3, -2)
    return t.reshape(lead + (B_WIDTH,))


def _prep_weights_kernel(w_ref, wout_ref, o_ref, oout_ref):
    oout_ref[...] = wout_ref[...].astype(BF16)
    grp = lax.broadcasted_iota(jnp.int32, (1, LANES), 1) // HALF
    q0, v0 = 3 * A_WIDTH, 3 * A_WIDTH + 2 * B_WIDTH
    for cb in range(IN_COLS // LANES):
        t = w_ref[:, cb * LANES:(cb + 1) * LANES]
        if q0 <= cb * LANES < v0:
            t = jnp.where(grp == 1, pltpu.roll(t, LANES - HALF, 1),
                          jnp.where(grp == 2, pltpu.roll(t, HALF, 1), t))
        o_ref[:, cb * LANES:(cb + 1) * LANES] = t.astype(BF16)


def _adaln_kernel(c_ref, w_ref, b_ref, mod_ref):
    c = c_ref[...]
    c_act = c * jax.nn.sigmoid(c)
    mod_ref[...] = jnp.dot(c_act.astype(BF16), w_ref[...].astype(BF16),
                           preferred_element_type=F32) + b_ref[...]


def _silu(z):
    return z * jax.nn.sigmoid(z)


def _head_mean_sq(t, bd_ref):
    sq = (t * t).astype(BF16)
    halves = [jnp.dot(sq[:, h * MXU_DIM:(h + 1) * MXU_DIM], bd_ref[...],
                      preferred_element_type=F32) for h in range(B_WIDTH // MXU_DIM)]
    return jnp.concatenate(halves, axis=-1) * (1.0 / B_HEAD_DIM)


def _rope(t, cos, sin_signed):
    cols = []
    for cb in range(B_WIDTH // LANES):
        tc = t[:, cb * LANES:(cb + 1) * LANES]
        cols.append(tc * cos + pltpu.roll(tc, LANES // 2, 1) * sin_signed)
    return jnp.concatenate(cols, axis=-1)


def _rope_tables(pos_quarters, invf, lane):
    n_grp = LANES // HALF
    qrows = SEQ_TILE // n_grp
    grp = lane // HALF
    pos_t = pos_quarters.T
    packed = pos_t[:qrows, n_grp - 1:n_grp]
    for gi in range(n_grp - 2, -1, -1):
        packed = jnp.where(grp == gi, pos_t[:qrows, gi:gi + 1], packed)
    ang = packed * invf
    tables = []
    for packed_tab in (jnp.cos(ang), jnp.sin(ang)):
        parts = []
        for gi in range(n_grp):
            t = jnp.where(grp == gi, packed_tab, 0.0)
            t = t + pltpu.roll(t, 2 * HALF, 1)
            parts.append(t + pltpu.roll(t, HALF, 1))
        tables.append(jnp.concatenate(parts, axis=0))
    return tables


def _across_sublanes(x, op):
    for shift in (4, 2, 1):
        x = op(x, pltpu.roll(x, shift, 0))
    return x


def _layer_kernel(x_ref, mod_ref, pos_ref, invf_ref, normw_ref, win_ref, sgu_ref, ws_ref,
                  bst_ref, qnw_ref, knw_ref, lq1_ref, lk1_ref, lq2_ref, lk2_ref, subln_ref,
                  wout_ref, bd_ref, out_ref,
                  k_scr, vt_scr, qblk_scr, m_scr, acc_scr, mixed_scr, s_scr, p_scr,
                  alpha_scr, smax_scr):
    i = pl.program_id(1)
    tq = SEQ_TILE
    tk = SEQ_TILE

    x = x_ref[0]
    b = pl.program_id(0)
    shift = mod_ref[0, pl.ds(b, 1), :]
    scale = mod_ref[1, pl.ds(b, 1), :]
    gate = mod_ref[2, pl.ds(b, 1), :]
    inv_rms = lax.rsqrt(jnp.mean(x * x, axis=-1, keepdims=True) + NORM_EPS)
    hb_cols = []
    for c0 in range(0, D_MODEL, MXU_DIM):
        cs = slice(c0, c0 + MXU_DIM)
        h_c = (x[:, cs] * inv_rms * normw_ref[:, cs]) * (1.0 + scale[:, cs]) + shift[:, cs]
        hb_cols.append(h_c.astype(BF16))
    hb = jnp.concatenate(hb_cols, axis=1)

    lane = lax.broadcasted_iota(jnp.int32, (1, LANES), 1)
    cos, sin = _rope_tables(pos_ref[...].astype(F32), invf_ref[...], lane)
    sin_signed = jnp.where(lane < LANES // 2, -sin, sin)

    def proj(c0, width):
        return jnp.dot(hb, win_ref[:, c0:c0 + width], preferred_element_type=F32)

    qb = proj(3 * A_WIDTH, B_WIDTH)
    kb = proj(3 * A_WIDTH + B_WIDTH, B_WIDTH)
    q = qb * lax.rsqrt(_head_mean_sq(qb, bd_ref) + NORM_EPS) * qnw_ref[...]
    k = kb * lax.rsqrt(_head_mean_sq(kb, bd_ref) + NORM_EPS) * knw_ref[...]
    q = _rope(q, cos, sin_signed) * (B_HEAD_DIM ** -0.5 * LOG2E)
    k = _rope(k, cos, sin_signed)
    k_scr[i] = k.astype(BF16)

    qt = q.T
    pair_row = lax.broadcasted_iota(jnp.int32, (LANES, tq), 0)
    is_a = (pair_row // HALF) % 2 == 0
    for g in range(B_HEADS):
        qp = qt[g * LANES:(g + 1) * LANES, :]
        qblk_scr[g] = jnp.concatenate([jnp.where(is_a, qp, 0.0), jnp.where(is_a, 0.0, qp)],
                                      axis=1).astype(BF16)

    m_scr[...] = jnp.full(m_scr.shape, -jnp.inf, F32)
    acc_scr[...] = jnp.zeros(acc_scr.shape, F32)

    kv_row = lax.broadcasted_iota(jnp.int32, (tk, 2 * tq), 0)
    q_col = lax.broadcasted_iota(jnp.int32, (tk, 2 * tq), 1) % tq
    causal = kv_row <= q_col

    def scores(g, j):
        kp = k_scr[j, :, g * LANES:(g + 1) * LANES]
        s = jnp.dot(kp, qblk_scr[g], preferred_element_type=F32)
        s_scr[g] = s
        smax_scr[g] = jnp.max(s.reshape(tk // SUBLANES, SUBLANES, 2 * tq), axis=0)

    def softmax(g, slot, masked):
        rows = 2 * SUBLANES

        def chunk(c):
            sc = s_scr[g, c * rows:(c + 1) * rows, :]
            if masked:
                sc = jnp.where(causal[c * rows:(c + 1) * rows, :], sc, -jnp.inf)
            return sc.reshape(2, SUBLANES, 2 * tq)

        if masked:
            col_max = jnp.max(chunk(0), axis=0)
            for c in range(1, tk // rows):
                col_max = jnp.maximum(col_max, jnp.max(chunk(c), axis=0))
        else:
            col_max = smax_scr[g]
        m_old = m_scr[g]
        m_new = jnp.maximum(m_old, _across_sublanes(col_max, jnp.maximum))
        alpha = jnp.exp2(m_old - m_new)
        m_scr[g] = m_new
        alpha_scr[slot] = alpha
        for c in range(tk // rows):
            pc = jnp.exp2(chunk(c) - m_new[None])
            p_scr[slot, c * rows:(c + 1) * rows, :] = pc.reshape(rows, 2 * tq).astype(BF16)

    def value_update(g, j, slot):
        vt = vt_scr[j, g]
        alpha = alpha_scr[slot]
        for st in range(2):
            pv = jnp.dot(vt, p_scr[slot, :, st * tq:(st + 1) * tq], preferred_element_type=F32)
            a = alpha[:, st * tq:(st + 1) * tq]
            acc = acc_scr[2 * g + st].reshape(V_EXT // SUBLANES, SUBLANES, tq)
            acc_scr[2 * g + st] = (acc * a[None]).reshape(V_EXT, tq) + pv

    last = B_HEADS - 1

    def unit(g, masked, tile, succ_tile, prev_tile):
        ahead = (g + 2) % B_HEADS
        scores(ahead, tile if g + 2 < B_HEADS else succ_tile)
        if prev_tile is not None:
            value_update((g - 1) % B_HEADS, prev_tile, (g - 1) % 2)
        softmax(g, g % 2, masked)

    vb = proj(3 * A_WIDTH + 2 * B_WIDTH, B_WIDTH)
    vbt = vb.T.astype(BF16)
    for g in range(B_HEADS):
        vt_scr[i, g, :B_V_DIM, :] = vbt[g * B_V_DIM:(g + 1) * B_V_DIM, :]
        vt_scr[i, g, B_V_DIM:, :] = jnp.ones((V_EXT - B_V_DIM, tk), BF16)
    va = proj(A_WIDTH, A_WIDTH)
    scores(0, i)
    scores(1, i)
    ua = proj(0, A_WIDTH)
    unit(0, True, i, 0, None)
    za = proj(2 * A_WIDTH, A_WIDTH)
    unit(1, True, i, 0, i)
    zb = proj(3 * A_WIDTH + 3 * B_WIDTH, B_WIDTH)
    unit(2, True, i, 0, i)

    row_c = lax.broadcasted_iota(jnp.int32, (CHUNK, CHUNK), 0)
    col_c = lax.broadcasted_iota(jnp.int32, (CHUNK, CHUNK), 1)
    tril = row_c >= col_c
    for hh in range(A_HEADS):
        sl = slice(hh * A_HEAD_DIM, (hh + 1) * A_HEAD_DIM)
        v = va[:, sl]
        vn = v * lax.rsqrt(jnp.mean(v * v, axis=-1, keepdims=True) + NORM_EPS) * sgu_ref[hh:hh + 1, :]
        vnb = vn.astype(BF16)
        ws = jnp.where(tril, ws_ref[hh], 0.0).astype(BF16)
        bias = bst_ref[:, hh:hh + 1]
        n_chunks = tq // CHUNK
        chunks = jnp.concatenate([vnb[c * CHUNK:(c + 1) * CHUNK, :] for c in range(n_chunks)],
                                 axis=1)
        mixed_chunks = jnp.dot(ws, chunks, preferred_element_type=F32)
        mix = jnp.concatenate([mixed_chunks[:, c * A_HEAD_DIM:(c + 1) * A_HEAD_DIM] + bias
                               for c in range(n_chunks)], axis=0)
        a_out = ua[:, sl] * mix * _silu(za[:, sl])
        mixed_scr[:, sl] = a_out.astype(BF16)
    zb_gate = _silu(zb)
    unit(3, True, i, 0, i)

    def tile_units(j):
        unit(0, False, j, j + 1, jnp.where(j == 0, i, j - 1))
        for g in range(1, B_HEADS):
            unit(g, False, j, j + 1, j)

    def pair_body(jj, carry):
        tile_units(2 * jj)
        tile_units(2 * jj + 1)
        return carry

    lax.fori_loop(0, i // 2, pair_body, 0)

    @pl.when(i % 2 == 1)
    def _():
        tile_units(i - 1)

    value_update(last, jnp.maximum(i - 1, 0), last % 2)
    out_ref[0] = x + gate * jnp.dot(mixed_scr[:, :A_WIDTH], wout_ref[:A_WIDTH, :],
                                    preferred_element_type=F32)

    lam = (jnp.exp(jnp.sum(lq1_ref[...] * lk1_ref[...], axis=-1, keepdims=True))
           - jnp.exp(jnp.sum(lq2_ref[...] * lk2_ref[...], axis=-1, keepdims=True))
           + LAMBDA_INIT)

    for g in range(B_HEADS):
        inv_l1 = 1.0 / acc_scr[2 * g, B_V_DIM:B_V_DIM + SUBLANES, :]
        inv_l2 = 1.0 / acc_scr[2 * g + 1, B_V_DIM:B_V_DIM + SUBLANES, :]
        acc1 = acc_scr[2 * g, :B_V_DIM, :].reshape(B_V_DIM // SUBLANES, SUBLANES, tq)
        acc2 = acc_scr[2 * g + 1, :B_V_DIM, :].reshape(B_V_DIM // SUBLANES, SUBLANES, tq)
        ot = acc1 * inv_l1[None] - lam * (acc2 * inv_l2[None])
        o = ot.reshape(B_V_DIM, tq).T
        o = o * lax.rsqrt(jnp.mean(o * o, axis=-1, keepdims=True) + SUBLN_EPS) * subln_ref[...]
        o = o * (1.0 - LAMBDA_INIT)
        vsl = slice(g * B_V_DIM, (g + 1) * B_V_DIM)
        b_out = o * zb_gate[:, vsl]
        mixed_scr[:, A_WIDTH + g * B_V_DIM:A_WIDTH + (g + 1) * B_V_DIM] = b_out.astype(BF16)

    out_ref[0] += gate * jnp.dot(mixed_scr[:, A_WIDTH:], wout_ref[A_WIDTH:, :],
                                 preferred_element_type=F32)


def kernel(x, c, positions, norm_w, w_ada, b_ada, w_in, sgu_norm_w, w_s, b_s, q_norm_w, k_norm_w,
           lambda_q1, lambda_k1, lambda_q2, lambda_k2, subln_w, w_out):
    bsz, seq, d = x.shape
    assert d == D_MODEL and seq % SEQ_TILE == 0 and norm_w.shape[0] == 1
    nt = seq // SEQ_TILE

    mod = pl.pallas_call(
        _adaln_kernel,
        grid=(3,),
        in_specs=[pl.BlockSpec((bsz, d), lambda n: (0, 0)),
                  pl.BlockSpec((None, d, d), lambda n: (0, 0, n)),
                  pl.BlockSpec((1, d), lambda n: (0, n))],
        out_specs=pl.BlockSpec((None, bsz, d), lambda n: (n, 0, 0)),
        out_shape=jax.ShapeDtypeStruct((3, bsz, d), F32),
        name="adaln_mod",
    )(c, w_ada, b_ada)

    n_grp = LANES // HALF
    pos_q = jnp.pad(positions.reshape(bsz, nt, n_grp, SEQ_TILE // n_grp),
                    ((0, 0), (0, 0), (0, SUBLANES - n_grp), (0, LANES - SEQ_TILE // n_grp)))

    inv_freq = ROPE_THETA ** (-jnp.arange(0, B_HEAD_DIM, 2, dtype=F32) / B_HEAD_DIM)
    invf = jnp.tile(inv_freq, LANES // HALF)[None, :]
    lane_head = (np.arange(MXU_DIM) // LANES) * 2 + (np.arange(MXU_DIM) // HALF) % 2
    bd = jnp.asarray(lane_head[:, None] == lane_head[None, :], dtype=BF16)

    prep_rows = d // 8
    w_in_b, w_out_b = pl.pallas_call(
        _prep_weights_kernel,
        grid=(d // prep_rows,),
        in_specs=[pl.BlockSpec((None, prep_rows, IN_COLS), lambda r: (0, r, 0)),
                  pl.BlockSpec((None, prep_rows, d), lambda r: (0, r, 0))],
        out_specs=[pl.BlockSpec((prep_rows, IN_COLS), lambda r: (r, 0)),
                   pl.BlockSpec((prep_rows, d), lambda r: (r, 0))],
        out_shape=[jax.ShapeDtypeStruct((d, IN_COLS), BF16),
                   jax.ShapeDtypeStruct((A_WIDTH + B_WIDTH, d), BF16)],
        name="prep_weights",
    )(w_in, w_out)
    qnw = _pair_layout(jnp.tile(q_norm_w.reshape(B_HEAD_DIM), 2 * B_HEADS))[None, :]
    knw = _pair_layout(jnp.tile(k_norm_w.reshape(B_HEAD_DIM), 2 * B_HEADS))[None, :]

    const2 = lambda b, i: (0, 0)
    const3 = lambda b, i: (0, 0, 0)
    out = pl.pallas_call(
        _layer_kernel,
        grid=(bsz, nt),
        in_specs=[
            pl.BlockSpec((1, SEQ_TILE, d), lambda b, i: (b, i, 0)),
            pl.BlockSpec((3, bsz, d), const3),
            pl.BlockSpec((None, None, SUBLANES, LANES), lambda b, i: (b, i, 0, 0)),
            pl.BlockSpec((1, LANES), const2),
            pl.BlockSpec((1, d), const2),
            pl.BlockSpec((d, IN_COLS), const2),
            pl.BlockSpec((A_HEADS, A_HEAD_DIM), const2),
            pl.BlockSpec((A_HEADS, CHUNK, CHUNK), const3),
            pl.BlockSpec((CHUNK, A_HEADS), const2),
            pl.BlockSpec((1, B_WIDTH), const2),
            pl.BlockSpec((1, B_WIDTH), const2),
            pl.BlockSpec((1, B_HEAD_DIM), const2),
            pl.BlockSpec((1, B_HEAD_DIM), const2),
            pl.BlockSpec((1, B_HEAD_DIM), const2),
            pl.BlockSpec((1, B_HEAD_DIM), const2),
            pl.BlockSpec((1, B_V_DIM), const2),
            pl.BlockSpec((A_WIDTH + B_WIDTH, d), const2),
            pl.BlockSpec((MXU_DIM, MXU_DIM), const2),
        ],
        out_specs=pl.BlockSpec((1, SEQ_TILE, d), lambda b, i: (b, i, 0)),
        out_shape=jax.ShapeDtypeStruct((bsz, seq, d), F32),
        scratch_shapes=[
            pltpu.VMEM((nt, SEQ_TILE, B_WIDTH), BF16),
            pltpu.VMEM((nt, B_HEADS, V_EXT, SEQ_TILE), BF16),
            pltpu.VMEM((B_HEADS, LANES, 2 * SEQ_TILE), BF16),
            pltpu.VMEM((B_HEADS, SUBLANES, 2 * SEQ_TILE), F32),
            pltpu.VMEM((2 * B_HEADS, V_EXT, SEQ_TILE), F32),
            pltpu.VMEM((SEQ_TILE, A_WIDTH + B_WIDTH), BF16),
            pltpu.VMEM((B_HEADS, SEQ_TILE, 2 * SEQ_TILE), F32),
            pltpu.VMEM((2, SEQ_TILE, 2 * SEQ_TILE), BF16),
            pltpu.VMEM((2, SUBLANES, 2 * SEQ_TILE), F32),
            pltpu.VMEM((B_HEADS, SUBLANES, 2 * SEQ_TILE), F32),
        ],
        compiler_params=pltpu.CompilerParams(
            dimension_semantics=("arbitrary", "arbitrary"),
            vmem_limit_bytes=VMEM_LIMIT_BYTES),
        name="fused_layer",
    )(x, mod, pos_q, invf, norm_w, w_in_b,
      sgu_norm_w.reshape(A_HEADS, A_HEAD_DIM), w_s.reshape(A_HEADS, CHUNK, CHUNK),
      jnp.transpose(b_s.reshape(A_HEADS, CHUNK)), qnw, knw,
      lambda_q1, lambda_k1, lambda_q2, lambda_k2, subln_w, w_out_b, bd)
    return out
```

```python
import math

import numpy as np
import jax
import jax.numpy as jnp
from jax import lax
from jax.experimental import pallas as pl
from jax.experimental.pallas import tpu as pltpu

D_MODEL = 1024
A_WIDTH = 512
B_WIDTH = 512
A_HEADS = 4
A_HEAD_DIM = 128
CHUNK = 128
B_HEADS = 4
B_HEAD_DIM = 64
B_V_DIM = 128
ROPE_THETA = 10000.0
NORM_EPS = 1e-6
SUBLN_EPS = 1e-5
IN_COLS = 3 * A_WIDTH + 4 * B_WIDTH
LAMBDA_INIT = 0.8 - 0.6 * math.exp(-0.3 * 0)
LOG2E = math.log2(math.e)

LANES = 128
SUBLANES = 8
MXU_DIM = 256
SEQ_TILE = 256
TILES_PER_STEP = 4
VMEM_LIMIT_BYTES = 56 * 1024 * 1024
HALF = B_HEAD_DIM // 2
BF16_ROWS = 16
V_EXT = B_V_DIM + BF16_ROWS

F32 = jnp.float32
BF16 = jnp.bfloat16


def _pair_layout(t):
    lead = t.shape[:-1]
    t = t.reshape(lead + (B_HEADS, 2, 2, HALF))
    t = jnp.swapaxes(t, -3, -2)
    return t.reshape(lead + (B_WIDTH,))


def _prep_weights_kernel(w_ref, wout_ref, o_ref, oout_ref):
    oout_ref[...] = wout_ref[...].astype(BF16)
    grp = lax.broadcasted_iota(jnp.int32, (1, LANES), 1) // HALF
    q0, v0 = 3 * A_WIDTH, 3 * A_WIDTH + 2 * B_WIDTH
    for cb in range(IN_COLS // LANES):
        t = w_ref[:, cb * LANES:(cb + 1) * LANES]
        if q0 <= cb * LANES < v0:
            t = jnp.where(grp == 1, pltpu.roll(t, LANES - HALF, 1),
                          jnp.where(grp == 2, pltpu.roll(t, HALF, 1), t))
        o_ref[:, cb * LANES:(cb + 1) * LANES] = t.astype(BF16)


def _adaln_kernel(c_ref, w_ref, b_ref, mod_ref):
    c = c_ref[...]
    c_act = c * jax.nn.sigmoid(c)
    mod_ref[...] = jnp.dot(c_act.astype(BF16), w_ref[...].astype(BF16),
                           preferred_element_type=F32) + b_ref[...]


def _silu(z):
    return z * jax.nn.sigmoid(z)


def _head_mean_sq(t, bd_ref):
    sq = (t * t).astype(BF16)
    halves = [jnp.dot(sq[:, h * MXU_DIM:(h + 1) * MXU_DIM], bd_ref[...],
                      preferred_element_type=F32) for h in range(B_WIDTH // MXU_DIM)]
    return jnp.concatenate(halves, axis=-1) * (1.0 / B_HEAD_DIM)


def _rope(t, cos, sin_signed):
    cols = []
    for cb in range(B_WIDTH // LANES):
        tc = t[:, cb * LANES:(cb + 1) * LANES]
        cols.append(tc * cos + pltpu.roll(tc, LANES // 2, 1) * sin_signed)
    return jnp.concatenate(cols, axis=-1)


def _rope_tables(pos_quarters, invf, lane):
    n_grp = LANES // HALF
    qrows = SEQ_TILE // n_grp
    grp = lane // HALF
    pos_t = pos_quarters.T
    packed = pos_t[:qrows, n_grp - 1:n_grp]
    for gi in range(n_grp - 2, -1, -1):
        packed = jnp.where(grp == gi, pos_t[:qrows, gi:gi + 1], packed)
    ang = packed * invf
    tables = []
    for packed_tab in (jnp.cos(ang), jnp.sin(ang)):
        parts = []
        for gi in range(n_grp):
            t = jnp.where(grp == gi, packed_tab, 0.0)
            t = t + pltpu.roll(t, 2 * HALF, 1)
            parts.append(t + pltpu.roll(t, HALF, 1))
        tables.append(jnp.concatenate(parts, axis=0))
    return tables


def _across_sublanes(x, op):
    for shift in (4, 2, 1):
        x = op(x, pltpu.roll(x, shift, 0))
    return x


def _layer_kernel(*refs):
    def body(sub, carry):
        _layer_tile(sub, *refs)
        return carry

    lax.fori_loop(0, TILES_PER_STEP, body, 0)


def _layer_tile(sub, x_ref, mod_ref, pos_ref, invf_ref, normw_ref, win_ref, sgu_ref, ws_ref,
                bst_ref, qnw_ref, knw_ref, lq1_ref, lk1_ref, lq2_ref, lk2_ref, subln_ref,
                wout_ref, bd_ref, out_ref,
                k_scr, vt_scr, qblk_scr, m_scr, acc_scr, mixed_scr, s_scr, p_scr,
                alpha_scr, smax_scr):
    i = pl.program_id(1) * TILES_PER_STEP + sub
    tq = SEQ_TILE
    tk = SEQ_TILE
    tile_rows = pl.ds(pl.multiple_of(sub * SEQ_TILE, SEQ_TILE), SEQ_TILE)

    x = x_ref[0, tile_rows, :]
    b = pl.program_id(0)
    shift = mod_ref[0, pl.ds(b, 1), :]
    scale = mod_ref[1, pl.ds(b, 1), :]
    gate = mod_ref[2, pl.ds(b, 1), :]
    inv_rms = lax.rsqrt(jnp.mean(x * x, axis=-1, keepdims=True) + NORM_EPS)
    hb_cols = []
    for c0 in range(0, D_MODEL, MXU_DIM):
        cs = slice(c0, c0 + MXU_DIM)
        h_c = (x[:, cs] * inv_rms * normw_ref[:, cs]) * (1.0 + scale[:, cs]) + shift[:, cs]
        hb_cols.append(h_c.astype(BF16))
    hb = jnp.concatenate(hb_cols, axis=1)

    lane = lax.broadcasted_iota(jnp.int32, (1, LANES), 1)
    cos, sin = _rope_tables(pos_ref[sub].astype(F32), invf_ref[...], lane)
    sin_signed = jnp.where(lane < LANES // 2, -sin, sin)

    def proj(c0, width):
        return jnp.dot(hb, win_ref[:, c0:c0 + width], preferred_element_type=F32)

    qb = proj(3 * A_WIDTH, B_WIDTH)
    kb = proj(3 * A_WIDTH + B_WIDTH, B_WIDTH)
    q = qb * lax.rsqrt(_head_mean_sq(qb, bd_ref) + NORM_EPS) * qnw_ref[...]
    k = kb * lax.rsqrt(_head_mean_sq(kb, bd_ref) + NORM_EPS) * knw_ref[...]
    q = _rope(q, cos, sin_signed) * (B_HEAD_DIM ** -0.5 * LOG2E)
    k = _rope(k, cos, sin_signed)
    k_scr[i] = k.astype(BF16)

    qt = q.T
    pair_row = lax.broadcasted_iota(jnp.int32, (LANES, tq), 0)
    is_a = (pair_row // HALF) % 2 == 0
    for g in range(B_HEADS):
        qp = qt[g * LANES:(g + 1) * LANES, :]
        qblk_scr[g] = jnp.concatenate([jnp.where(is_a, qp, 0.0), jnp.where(is_a, 0.0, qp)],
                                      axis=1).astype(BF16)

    m_scr[...] = jnp.full(m_scr.shape, -jnp.inf, F32)
    acc_scr[...] = jnp.zeros(acc_scr.shape, F32)

    kv_row = lax.broadcasted_iota(jnp.int32, (tk, 2 * tq), 0)
    q_col = lax.broadcasted_iota(jnp.int32, (tk, 2 * tq), 1) % tq
    causal = kv_row <= q_col

    def scores(g, j):
        kp = k_scr[j, :, g * LANES:(g + 1) * LANES]
        s = jnp.dot(kp, qblk_scr[g], preferred_element_type=F32)
        s_scr[g] = s
        smax_scr[g] = jnp.max(s.reshape(tk // SUBLANES, SUBLANES, 2 * tq), axis=0)

    def softmax(g, slot, masked):
        rows = 2 * SUBLANES

        def chunk(c):
            sc = s_scr[g, c * rows:(c + 1) * rows, :]
            if masked:
                sc = jnp.where(causal[c * rows:(c + 1) * rows, :], sc, -jnp.inf)
            return sc.reshape(2, SUBLANES, 2 * tq)

        if masked:
            col_max = jnp.max(chunk(0), axis=0)
            for c in range(1, tk // rows):
                col_max = jnp.maximum(col_max, jnp.max(chunk(c), axis=0))
        else:
            col_max = smax_scr[g]
        m_old = m_scr[g]
        m_new = jnp.maximum(m_old, _across_sublanes(col_max, jnp.maximum))
        alpha = jnp.exp2(m_old - m_new)
        m_scr[g] = m_new
        alpha_scr[slot] = alpha
        for c in range(tk // rows):
            pc = jnp.exp2(chunk(c) - m_new[None])
            p_scr[slot, c * rows:(c + 1) * rows, :] = pc.reshape(rows, 2 * tq).astype(BF16)

    def value_update(g, j, slot):
        vt = vt_scr[j, g]
        alpha = alpha_scr[slot]
        for st in range(2):
            pv = jnp.dot(vt, p_scr[slot, :, st * tq:(st + 1) * tq], preferred_element_type=F32)
            a = alpha[:, st * tq:(st + 1) * tq]
            acc = acc_scr[2 * g + st].reshape(V_EXT // SUBLANES, SUBLANES, tq)
            acc_scr[2 * g + st] = (acc * a[None]).reshape(V_EXT, tq) + pv

    last = B_HEADS - 1

    def unit(g, masked, tile, succ_tile, prev_tile):
        ahead = (g + 2) % B_HEADS
        scores(ahead, tile if g + 2 < B_HEADS else succ_tile)
        if prev_tile is not None:
            value_update((g - 1) % B_HEADS, prev_tile, (g - 1) % 2)
        softmax(g, g % 2, masked)

    vb = proj(3 * A_WIDTH + 2 * B_WIDTH, B_WIDTH)
    vbt = vb.T.astype(BF16)
    for g in range(B_HEADS):
        vt_scr[i, g, :B_V_DIM, :] = vbt[g * B_V_DIM:(g + 1) * B_V_DIM, :]
        vt_scr[i, g, B_V_DIM:, :] = jnp.ones((V_EXT - B_V_DIM, tk), BF16)
    va = proj(A_WIDTH, A_WIDTH)
    scores(0, i)
    scores(1, i)
    ua = proj(0, A_WIDTH)
    unit(0, True, i, 0, None)
    za = proj(2 * A_WIDTH, A_WIDTH)
    unit(1, True, i, 0, i)
    zb = proj(3 * A_WIDTH + 3 * B_WIDTH, B_WIDTH)
    unit(2, True, i, 0, i)

    row_c = lax.broadcasted_iota(jnp.int32, (CHUNK, CHUNK), 0)
    col_c = lax.broadcasted_iota(jnp.int32, (CHUNK, CHUNK), 1)
    tril = row_c >= col_c
    for hh in range(A_HEADS):
        sl = slice(hh * A_HEAD_DIM, (hh + 1) * A_HEAD_DIM)
        v = va[:, sl]
        vn = v * lax.rsqrt(jnp.mean(v * v, axis=-1, keepdims=True) + NORM_EPS) * sgu_ref[hh:hh + 1, :]
        vnb = vn.astype(BF16)
        ws = jnp.where(tril, ws_ref[hh], 0.0).astype(BF16)
        bias = bst_ref[:, hh:hh + 1]
        n_chunks = tq // CHUNK
        chunks = jnp.concatenate([vnb[c * CHUNK:(c + 1) * CHUNK, :] for c in range(n_chunks)],
                                 axis=1)
        mixed_chunks = jnp.dot(ws, chunks, preferred_element_type=F32)
        mix = jnp.concatenate([mixed_chunks[:, c * A_HEAD_DIM:(c + 1) * A_HEAD_DIM] + bias
                               for c in range(n_chunks)], axis=0)
        a_out = ua[:, sl] * mix * _silu(za[:, sl])
        mixed_scr[:, sl] = a_out.astype(BF16)
    zb_gate = _silu(zb)
    unit(3, True, i, 0, i)

    def tile_units(j):
        unit(0, False, j, j + 1, jnp.where(j == 0, i, j - 1))
        for g in range(1, B_HEADS):
            unit(g, False, j, j + 1, j)

    def pair_body(jj, carry):
        tile_units(2 * jj)
        tile_units(2 * jj + 1)
        return carry

    lax.fori_loop(0, i // 2, pair_body, 0)

    @pl.when(i % 2 == 1)
    def _():
        tile_units(i - 1)

    value_update(last, jnp.maximum(i - 1, 0), last % 2)
    out_ref[0, tile_rows, :] = x + gate * jnp.dot(
        mixed_scr[:, :A_WIDTH], wout_ref[:A_WIDTH, :], preferred_element_type=F32)

    lam = (jnp.exp(jnp.sum(lq1_ref[...] * lk1_ref[...], axis=-1, keepdims=True))
           - jnp.exp(jnp.sum(lq2_ref[...] * lk2_ref[...], axis=-1, keepdims=True))
           + LAMBDA_INIT)

    for g in range(B_HEADS):
        inv_l1 = 1.0 / acc_scr[2 * g, B_V_DIM:B_V_DIM + SUBLANES, :]
        inv_l2 = 1.0 / acc_scr[2 * g + 1, B_V_DIM:B_V_DIM + SUBLANES, :]
        acc1 = acc_scr[2 * g, :B_V_DIM, :].reshape(B_V_DIM // SUBLANES, SUBLANES, tq)
        acc2 = acc_scr[2 * g + 1, :B_V_DIM, :].reshape(B_V_DIM // SUBLANES, SUBLANES, tq)
        ot = acc1 * inv_l1[None] - lam * (acc2 * inv_l2[None])
        o = ot.reshape(B_V_DIM, tq).T
        o = o * lax.rsqrt(jnp.mean(o * o, axis=-1, keepdims=True) + SUBLN_EPS) * subln_ref[...]
        o = o * (1.0 - LAMBDA_INIT)
        vsl = slice(g * B_V_DIM, (g + 1) * B_V_DIM)
        b_out = o * zb_gate[:, vsl]
        mixed_scr[:, A_WIDTH + g * B_V_DIM:A_WIDTH + (g + 1) * B_V_DIM] = b_out.astype(BF16)

    out_ref[0, tile_rows, :] += gate * jnp.dot(
        mixed_scr[:, A_WIDTH:], wout_ref[A_WIDTH:, :], preferred_element_type=F32)


def kernel(x, c, positions, norm_w, w_ada, b_ada, w_in, sgu_norm_w, w_s, b_s, q_norm_w, k_norm_w,
           lambda_q1, lambda_k1, lambda_q2, lambda_k2, subln_w, w_out):
    bsz, seq, d = x.shape
    assert d == D_MODEL and seq % (TILES_PER_STEP * SEQ_TILE) == 0 and norm_w.shape[0] == 1
    nt = seq // SEQ_TILE

    mod = pl.pallas_call(
        _adaln_kernel,
        grid=(3,),
        in_specs=[pl.BlockSpec((bsz, d), lambda n: (0, 0)),
                  pl.BlockSpec((None, d, d), lambda n: (0, 0, n)),
                  pl.BlockSpec((1, d), lambda n: (0, n))],
        out_specs=pl.BlockSpec((None, bsz, d), lambda n: (n, 0, 0)),
        out_shape=jax.ShapeDtypeStruct((3, bsz, d), F32),
        name="adaln_mod",
    )(c, w_ada, b_ada)

    n_grp = LANES // HALF
    pos_q = jnp.pad(positions.reshape(bsz, nt, n_grp, SEQ_TILE // n_grp),
                    ((0, 0), (0, 0), (0, SUBLANES - n_grp), (0, LANES - SEQ_TILE // n_grp)))

    inv_freq = ROPE_THETA ** (-jnp.arange(0, B_HEAD_DIM, 2, dtype=F32) / B_HEAD_DIM)
    invf = jnp.tile(inv_freq, LANES // HALF)[None, :]
    lane_head = (np.arange(MXU_DIM) // LANES) * 2 + (np.arange(MXU_DIM) // HALF) % 2
    bd = jnp.asarray(lane_head[:, None] == lane_head[None, :], dtype=BF16)

    prep_rows = d // 8
    w_in_b, w_out_b = pl.pallas_call(
        _prep_weights_kernel,
        grid=(d // prep_rows,),
        in_specs=[pl.BlockSpec((None, prep_rows, IN_COLS), lambda r: (0, r, 0)),
                  pl.BlockSpec((None, prep_rows, d), lambda r: (0, r, 0))],
        out_specs=[pl.BlockSpec((prep_rows, IN_COLS), lambda r: (r, 0)),
                   pl.BlockSpec((prep_rows, d), lambda r: (r, 0))],
        out_shape=[jax.ShapeDtypeStruct((d, IN_COLS), BF16),
                   jax.ShapeDtypeStruct((A_WIDTH + B_WIDTH, d), BF16)],
        name="prep_weights",
    )(w_in, w_out)
    qnw = _pair_layout(jnp.tile(q_norm_w.reshape(B_HEAD_DIM), 2 * B_HEADS))[None, :]
    knw = _pair_layout(jnp.tile(k_norm_w.reshape(B_HEAD_DIM), 2 * B_HEADS))[None, :]

    const2 = lambda b, i: (0, 0)
    const3 = lambda b, i: (0, 0, 0)
    step_rows = TILES_PER_STEP * SEQ_TILE
    out = pl.pallas_call(
        _layer_kernel,
        grid=(bsz, nt // TILES_PER_STEP),
        in_specs=[
            pl.BlockSpec((1, step_rows, d), lambda b, i: (b, i, 0)),
            pl.BlockSpec((3, bsz, d), const3),
            pl.BlockSpec((None, TILES_PER_STEP, SUBLANES, LANES),
                         lambda b, i: (b, i, 0, 0)),
            pl.BlockSpec((1, LANES), const2),
            pl.BlockSpec((1, d), const2),
            pl.BlockSpec((d, IN_COLS), const2),
            pl.BlockSpec((A_HEADS, A_HEAD_DIM), const2),
            pl.BlockSpec((A_HEADS, CHUNK, CHUNK), const3),
            pl.BlockSpec((CHUNK, A_HEADS), const2),
            pl.BlockSpec((1, B_WIDTH), const2),
            pl.BlockSpec((1, B_WIDTH), const2),
            pl.BlockSpec((1, B_HEAD_DIM), const2),
            pl.BlockSpec((1, B_HEAD_DIM), const2),
            pl.BlockSpec((1, B_HEAD_DIM), const2),
            pl.BlockSpec((1, B_HEAD_DIM), const2),
            pl.BlockSpec((1, B_V_DIM), const2),
            pl.BlockSpec((A_WIDTH + B_WIDTH, d), const2),
            pl.BlockSpec((MXU_DIM, MXU_DIM), const2),
        ],
        out_specs=pl.BlockSpec((1, step_rows, d), lambda b, i: (b, i, 0)),
        out_shape=jax.ShapeDtypeStruct((bsz, seq, d), F32),
        scratch_shapes=[
            pltpu.VMEM((nt, SEQ_TILE, B_WIDTH), BF16),
            pltpu.VMEM((nt, B_HEADS, V_EXT, SEQ_TILE), BF16),
            pltpu.VMEM((B_HEADS, LANES, 2 * SEQ_TILE), BF16),
            pltpu.VMEM((B_HEADS, SUBLANES, 2 * SEQ_TILE), F32),
            pltpu.VMEM((2 * B_HEADS, V_EXT, SEQ_TILE), F32),
            pltpu.VMEM((SEQ_TILE, A_WIDTH + B_WIDTH), BF16),
            pltpu.VMEM((B_HEADS, SEQ_TILE, 2 * SEQ_TILE), F32),
            pltpu.VMEM((2, SEQ_TILE, 2 * SEQ_TILE), BF16),
            pltpu.VMEM((2, SUBLANES, 2 * SEQ_TILE), F32),
            pltpu.VMEM((B_HEADS, SUBLANES, 2 * SEQ_TILE), F32),
        ],
        compiler_params=pltpu.CompilerParams(
            dimension_semantics=("arbitrary", "arbitrary"),
            vmem_limit_bytes=VMEM_LIMIT_BYTES),
        name="fused_layer",
    )(x, mod, pos_q, invf, norm_w, w_in_b,
      sgu_norm_w.reshape(A_HEADS, A_HEAD_DIM), w_s.reshape(A_HEADS, CHUNK, CHUNK),
      jnp.transpose(b_s.reshape(A_HEADS, CHUNK)), qnw, knw,
      lambda_q1, lambda_k1, lambda_q2, lambda_k2, subln_w, w_out_b, bd)
    return out
```

```python
import math

import numpy as np
import jax
import jax.numpy as jnp
from jax import lax
from jax.experimental import pallas as pl
from jax.experimental.pallas import tpu as pltpu

D_MODEL = 1024
A_WIDTH = 512
B_WIDTH = 512
A_HEADS = 4
A_HEAD_DIM = 128
CHUNK = 128
B_HEADS = 4
B_HEAD_DIM = 64
B_V_DIM = 128
ROPE_THETA = 10000.0
NORM_EPS = 1e-6
SUBLN_EPS = 1e-5
IN_COLS = 3 * A_WIDTH + 4 * B_WIDTH
LAMBDA_INIT = 0.8 - 0.6 * math.exp(-0.3 * 0)
LOG2E = math.log2(math.e)

LANES = 128
SUBLANES = 8
MXU_DIM = 256
SEQ_TILE = 256
TILES_PER_STEP = 1
VMEM_LIMIT_BYTES = 56 * 1024 * 1024
HALF = B_HEAD_DIM // 2
BF16_ROWS = 16
V_EXT = B_V_DIM + BF16_ROWS

F32 = jnp.float32
BF16 = jnp.bfloat16


def _pair_layout(t):
    lead = t.shape[:-1]
    t = t.reshape(lead + (B_HEADS, 2, 2, HALF))
    t = jnp.swapaxes(t, -3, -2)
    return t.reshape(lead + (B_WIDTH,))


def _prep_weights_kernel(w_ref, wout_ref, o_ref, oout_ref):
    oout_ref[...] = wout_ref[...].astype(BF16)
    grp = lax.broadcasted_iota(jnp.int32, (1, LANES), 1) // HALF
    q0, v0 = 3 * A_WIDTH, 3 * A_WIDTH + 2 * B_WIDTH
    for cb in range(IN_COLS // LANES):
        t = w_ref[:, cb * LANES:(cb + 1) * LANES]
        if q0 <= cb * LANES < v0:
            t = jnp.where(grp == 1, pltpu.roll(t, LANES - HALF, 1),
                          jnp.where(grp == 2, pltpu.roll(t, HALF, 1), t))
        o_ref[:, cb * LANES:(cb + 1) * LANES] = t.astype(BF16)


def _adaln_kernel(c_ref, w_ref, b_ref, mod_ref):
    c = c_ref[...]
    c_act = c * jax.nn.sigmoid(c)
    mod_ref[...] = jnp.dot(c_act.astype(BF16), w_ref[...].astype(BF16),
                           preferred_element_type=F32) + b_ref[...]


def _silu(z):
    return z * jax.nn.sigmoid(z)


def _head_mean_sq(t, bd_ref):
    sq = (t * t).astype(BF16)
    halves = [jnp.dot(sq[:, h * MXU_DIM:(h + 1) * MXU_DIM], bd_ref[...],
                      preferred_element_type=F32) for h in range(B_WIDTH // MXU_DIM)]
    return jnp.concatenate(halves, axis=-1) * (1.0 / B_HEAD_DIM)


def _rope(t, cos, sin_signed):
    cols = []
    for cb in range(B_WIDTH // LANES):
        tc = t[:, cb * LANES:(cb + 1) * LANES]
        cols.append(tc * cos + pltpu.roll(tc, LANES // 2, 1) * sin_signed)
    return jnp.concatenate(cols, axis=-1)


def _rope_tables(pos_quarters, invf, lane):
    n_grp = LANES // HALF
    qrows = SEQ_TILE // n_grp
    grp = lane // HALF
    pos_t = pos_quarters.T
    packed = pos_t[:qrows, n_grp - 1:n_grp]
    for gi in range(n_grp - 2, -1, -1):
        packed = jnp.where(grp == gi, pos_t[:qrows, gi:gi + 1], packed)
    ang = packed * invf
    tables = []
    for packed_tab in (jnp.cos(ang), jnp.sin(ang)):
        parts = []
        for gi in range(n_grp):
            t = jnp.where(grp == gi, packed_tab, 0.0)
            t = t + pltpu.roll(t, 2 * HALF, 1)
            parts.append(t + pltpu.roll(t, HALF, 1))
        tables.append(jnp.concatenate(parts, axis=0))
    return tables


def _across_sublanes(x, op):
    for shift in (4, 2, 1):
        x = op(x, pltpu.roll(x, shift, 0))
    return x


def _layer_kernel(*refs):
    def body(sub, carry):
        _layer_tile(sub, *refs)
        return carry

    lax.fori_loop(0, TILES_PER_STEP, body, 0)


def _layer_tile(sub, x_ref, mod_ref, pos_ref, invf_ref, normw_ref, win_ref, sgu_ref, ws_ref,
                bst_ref, qnw_ref, knw_ref, lq1_ref, lk1_ref, lq2_ref, lk2_ref, subln_ref,
                wout_ref, bd_ref, out_ref,
                k_scr, vt_scr, qblk_scr, m_scr, acc_scr, mixed_scr, s_scr, p_scr,
                alpha_scr, smax_scr):
    i = pl.program_id(1) * TILES_PER_STEP + sub
    tq = SEQ_TILE
    tk = SEQ_TILE
    tile_rows = pl.ds(pl.multiple_of(sub * SEQ_TILE, SEQ_TILE), SEQ_TILE)

    b = pl.program_id(0)
    shift = mod_ref[0, pl.ds(b, 1), :]
    scale = mod_ref[1, pl.ds(b, 1), :]
    gate = mod_ref[2, pl.ds(b, 1), :]
    col_tiles = [slice(c0, c0 + MXU_DIM) for c0 in range(0, D_MODEL, MXU_DIM)]
    sq_lanes = None
    for c0 in range(0, D_MODEL, LANES):
        x_c = x_ref[0, tile_rows, c0:c0 + LANES]
        sq_lanes = x_c * x_c if sq_lanes is None else sq_lanes + x_c * x_c
    sum_sq = jnp.sum(sq_lanes, axis=-1, keepdims=True)
    inv_rms = lax.rsqrt(sum_sq * (1.0 / D_MODEL) + NORM_EPS)
    hb_cols = []
    for cs in col_tiles:
        h_c = ((x_ref[0, tile_rows, cs] * inv_rms * normw_ref[:, cs]) * (1.0 + scale[:, cs])
               + shift[:, cs])
        hb_cols.append(h_c.astype(BF16))
    hb = jnp.concatenate(hb_cols, axis=1)

    def proj(c0, width):
        return jnp.dot(hb, win_ref[:, c0:c0 + width], preferred_element_type=F32)

    qb = proj(3 * A_WIDTH, B_WIDTH)
    kb = proj(3 * A_WIDTH + B_WIDTH, B_WIDTH)
    lane = lax.broadcasted_iota(jnp.int32, (1, LANES), 1)
    cos, sin = _rope_tables(pos_ref[sub].astype(F32), invf_ref[...], lane)
    sin_signed = jnp.where(lane < LANES // 2, -sin, sin)
    q = qb * lax.rsqrt(_head_mean_sq(qb, bd_ref) + NORM_EPS) * qnw_ref[...]
    k = kb * lax.rsqrt(_head_mean_sq(kb, bd_ref) + NORM_EPS) * knw_ref[...]
    q = _rope(q, cos, sin_signed) * (B_HEAD_DIM ** -0.5 * LOG2E)
    k = _rope(k, cos, sin_signed)
    k_scr[i] = k.astype(BF16)

    qt = q.T
    pair_row = lax.broadcasted_iota(jnp.int32, (LANES, tq), 0)
    is_a = (pair_row // HALF) % 2 == 0
    for g in range(B_HEADS):
        qp = qt[g * LANES:(g + 1) * LANES, :]
        qblk_scr[g] = jnp.concatenate([jnp.where(is_a, qp, 0.0), jnp.where(is_a, 0.0, qp)],
                                      axis=1).astype(BF16)

    m_scr[...] = jnp.full(m_scr.shape, -jnp.inf, F32)
    acc_scr[...] = jnp.zeros(acc_scr.shape, F32)

    kv_row = lax.broadcasted_iota(jnp.int32, (tk, 2 * tq), 0)
    q_col = lax.broadcasted_iota(jnp.int32, (tk, 2 * tq), 1) % tq
    causal = kv_row <= q_col

    def scores(g, j):
        kp = k_scr[j, :, g * LANES:(g + 1) * LANES]
        s = jnp.dot(kp, qblk_scr[g], preferred_element_type=F32)
        s_scr[g] = s
        smax_scr[g] = jnp.max(s.reshape(tk // SUBLANES, SUBLANES, 2 * tq), axis=0)

    def softmax(g, slot, masked):
        rows = 2 * SUBLANES

        def chunk(c):
            sc = s_scr[g, c * rows:(c + 1) * rows, :]
            if masked:
                sc = jnp.where(causal[c * rows:(c + 1) * rows, :], sc, -jnp.inf)
            return sc.reshape(2, SUBLANES, 2 * tq)

        if masked:
            col_max = jnp.max(chunk(0), axis=0)
            for c in range(1, tk // rows):
                col_max = jnp.maximum(col_max, jnp.max(chunk(c), axis=0))
        else:
            col_max = smax_scr[g]
        m_old = m_scr[g]
        m_new = jnp.maximum(m_old, _across_sublanes(col_max, jnp.maximum))
        alpha = jnp.exp2(m_old - m_new)
        m_scr[g] = m_new
        alpha_scr[slot] = alpha
        for c in range(tk // rows):
            pc = jnp.exp2(chunk(c) - m_new[None])
            p_scr[slot, c * rows:(c + 1) * rows, :] = pc.reshape(rows, 2 * tq).astype(BF16)

    def value_update(g, j, slot):
        vt = vt_scr[j, g]
        alpha = alpha_scr[slot]
        for st in range(2):
            pv = jnp.dot(vt, p_scr[slot, :, st * tq:(st + 1) * tq], preferred_element_type=F32)
            a = alpha[:, st * tq:(st + 1) * tq]
            acc = acc_scr[2 * g + st].reshape(V_EXT // SUBLANES, SUBLANES, tq)
            acc_scr[2 * g + st] = (acc * a[None]).reshape(V_EXT, tq) + pv

    last = B_HEADS - 1

    def unit(g, masked, tile, succ_tile, prev_tile):
        ahead = (g + 2) % B_HEADS
        if g + 2 < B_HEADS:
            scores(ahead, tile)
        elif succ_tile is not None:
            scores(ahead, succ_tile)
        if prev_tile is not None:
            value_update((g - 1) % B_HEADS, prev_tile, (g - 1) % 2)
        softmax(g, g % 2, masked)

    vb = proj(3 * A_WIDTH + 2 * B_WIDTH, B_WIDTH)
    vbt = vb.T.astype(BF16)
    for g in range(B_HEADS):
        vt_scr[i, g, :B_V_DIM, :] = vbt[g * B_V_DIM:(g + 1) * B_V_DIM, :]
        vt_scr[i, g, B_V_DIM:, :] = jnp.ones((V_EXT - B_V_DIM, tk), BF16)
    va = proj(A_WIDTH, A_WIDTH)
    scores(0, i)
    scores(1, i)
    ua = proj(0, A_WIDTH)
    unit(0, True, i, 0, None)
    za = proj(2 * A_WIDTH, A_WIDTH)
    unit(1, True, i, 0, i)
    zb = proj(3 * A_WIDTH + 3 * B_WIDTH, B_WIDTH)
    unit(2, True, i, 0, i)

    row_c = lax.broadcasted_iota(jnp.int32, (CHUNK, CHUNK), 0)
    col_c = lax.broadcasted_iota(jnp.int32, (CHUNK, CHUNK), 1)
    tril = row_c >= col_c
    for hh in range(A_HEADS):
        sl = slice(hh * A_HEAD_DIM, (hh + 1) * A_HEAD_DIM)
        v = va[:, sl]
        vn = v * lax.rsqrt(jnp.mean(v * v, axis=-1, keepdims=True) + NORM_EPS) * sgu_ref[hh:hh + 1, :]
        vnb = vn.astype(BF16)
        ws = jnp.where(tril, ws_ref[hh], 0.0).astype(BF16)
        bias = bst_ref[:, hh:hh + 1]
        n_chunks = tq // CHUNK
        chunks = jnp.concatenate([vnb[c * CHUNK:(c + 1) * CHUNK, :] for c in range(n_chunks)],
                                 axis=1)
        mixed_chunks = jnp.dot(ws, chunks, preferred_element_type=F32)
        mix = jnp.concatenate([mixed_chunks[:, c * A_HEAD_DIM:(c + 1) * A_HEAD_DIM] + bias
                               for c in range(n_chunks)], axis=0)
        a_out = ua[:, sl] * mix * _silu(za[:, sl])
        mixed_scr[:, sl] = a_out.astype(BF16)
    zb_gate = _silu(zb)
    unit(3, True, i, 0, i)

    def tile_units(j, is_final=False):
        succ = None if is_final else j + 1
        unit(0, False, j, succ, jnp.where(j == 0, i, j - 1))
        for g in range(1, B_HEADS):
            unit(g, False, j, succ, j)

    def pair_body(jj, carry):
        tile_units(2 * jj)
        tile_units(2 * jj + 1)
        return carry

    lax.fori_loop(0, i // 2, pair_body, 0)

    @pl.when(i % 2 == 1)
    def _():
        tile_units(i - 1, is_final=True)

    value_update(last, jnp.maximum(i - 1, 0), last % 2)
    out_ref[0, tile_rows, :] = x_ref[0, tile_rows, :] + gate * jnp.dot(
        mixed_scr[:, :A_WIDTH], wout_ref[:A_WIDTH, :], preferred_element_type=F32)

    lam = (jnp.exp(jnp.sum(lq1_ref[...] * lk1_ref[...], axis=-1, keepdims=True))
           - jnp.exp(jnp.sum(lq2_ref[...] * lk2_ref[...], axis=-1, keepdims=True))
           + LAMBDA_INIT)

    for g in range(B_HEADS):
        inv_l1 = 1.0 / acc_scr[2 * g, B_V_DIM:B_V_DIM + SUBLANES, :]
        inv_l2 = 1.0 / acc_scr[2 * g + 1, B_V_DIM:B_V_DIM + SUBLANES, :]
        acc1 = acc_scr[2 * g, :B_V_DIM, :].reshape(B_V_DIM // SUBLANES, SUBLANES, tq)
        acc2 = acc_scr[2 * g + 1, :B_V_DIM, :].reshape(B_V_DIM // SUBLANES, SUBLANES, tq)
        ot = acc1 * inv_l1[None] - lam * (acc2 * inv_l2[None])
        o = ot.reshape(B_V_DIM, tq).T
        o = o * lax.rsqrt(jnp.mean(o * o, axis=-1, keepdims=True) + SUBLN_EPS) * subln_ref[...]
        o = o * (1.0 - LAMBDA_INIT)
        vsl = slice(g * B_V_DIM, (g + 1) * B_V_DIM)
        b_out = o * zb_gate[:, vsl]
        mixed_scr[:, A_WIDTH + g * B_V_DIM:A_WIDTH + (g + 1) * B_V_DIM] = b_out.astype(BF16)

    out_ref[0, tile_rows, :] += gate * jnp.dot(
        mixed_scr[:, A_WIDTH:], wout_ref[A_WIDTH:, :], preferred_element_type=F32)


def kernel(x, c, positions, norm_w, w_ada, b_ada, w_in, sgu_norm_w, w_s, b_s, q_norm_w, k_norm_w,
           lambda_q1, lambda_k1, lambda_q2, lambda_k2, subln_w, w_out):
    bsz, seq, d = x.shape
    assert d == D_MODEL and seq % (TILES_PER_STEP * SEQ_TILE) == 0 and norm_w.shape[0] == 1
    nt = seq // SEQ_TILE

    mod = pl.pallas_call(
        _adaln_kernel,
        grid=(3,),
        in_specs=[pl.BlockSpec((bsz, d), lambda n: (0, 0)),
                  pl.BlockSpec((None, d, d), lambda n: (0, 0, n)),
                  pl.BlockSpec((1, d), lambda n: (0, n))],
        out_specs=pl.BlockSpec((None, bsz, d), lambda n: (n, 0, 0)),
        out_shape=jax.ShapeDtypeStruct((3, bsz, d), F32),
        name="adaln_mod",
    )(c, w_ada, b_ada)

    n_grp = LANES // HALF
    pos_q = jnp.pad(positions.reshape(bsz, nt, n_grp, SEQ_TILE // n_grp),
                    ((0, 0), (0, 0), (0, SUBLANES - n_grp), (0, LANES - SEQ_TILE // n_grp)))

    inv_freq = ROPE_THETA ** (-jnp.arange(0, B_HEAD_DIM, 2, dtype=F32) / B_HEAD_DIM)
    invf = jnp.tile(inv_freq, LANES // HALF)[None, :]
    lane_head = (np.arange(MXU_DIM) // LANES) * 2 + (np.arange(MXU_DIM) // HALF) % 2
    bd = jnp.asarray(lane_head[:, None] == lane_head[None, :], dtype=BF16)

    prep_rows = d // 8
    w_in_b, w_out_b = pl.pallas_call(
        _prep_weights_kernel,
        grid=(d // prep_rows,),
        in_specs=[pl.BlockSpec((None, prep_rows, IN_COLS), lambda r: (0, r, 0)),
                  pl.BlockSpec((None, prep_rows, d), lambda r: (0, r, 0))],
        out_specs=[pl.BlockSpec((prep_rows, IN_COLS), lambda r: (r, 0)),
                   pl.BlockSpec((prep_rows, d), lambda r: (r, 0))],
        out_shape=[jax.ShapeDtypeStruct((d, IN_COLS), BF16),
                   jax.ShapeDtypeStruct((A_WIDTH + B_WIDTH, d), BF16)],
        name="prep_weights",
    )(w_in, w_out)
    qnw = _pair_layout(jnp.tile(q_norm_w.reshape(B_HEAD_DIM), 2 * B_HEADS))[None, :]
    knw = _pair_layout(jnp.tile(k_norm_w.reshape(B_HEAD_DIM), 2 * B_HEADS))[None, :]

    const2 = lambda b, i: (0, 0)
    const3 = lambda b, i: (0, 0, 0)
    step_rows = TILES_PER_STEP * SEQ_TILE
    out = pl.pallas_call(
        _layer_kernel,
        grid=(bsz, nt // TILES_PER_STEP),
        in_specs=[
            pl.BlockSpec((1, step_rows, d), lambda b, i: (b, i, 0)),
            pl.BlockSpec((3, bsz, d), const3),
            pl.BlockSpec((None, TILES_PER_STEP, SUBLANES, LANES),
                         lambda b, i: (b, i, 0, 0)),
            pl.BlockSpec((1, LANES), const2),
            pl.BlockSpec((1, d), const2),
            pl.BlockSpec((d, IN_COLS), const2),
            pl.BlockSpec((A_HEADS, A_HEAD_DIM), const2),
            pl.BlockSpec((A_HEADS, CHUNK, CHUNK), const3),
            pl.BlockSpec((CHUNK, A_HEADS), const2),
            pl.BlockSpec((1, B_WIDTH), const2),
            pl.BlockSpec((1, B_WIDTH), const2),
            pl.BlockSpec((1, B_HEAD_DIM), const2),
            pl.BlockSpec((1, B_HEAD_DIM), const2),
            pl.BlockSpec((1, B_HEAD_DIM), const2),
            pl.BlockSpec((1, B_HEAD_DIM), const2),
            pl.BlockSpec((1, B_V_DIM), const2),
            pl.BlockSpec((A_WIDTH + B_WIDTH, d), const2),
            pl.BlockSpec((MXU_DIM, MXU_DIM), const2),
        ],
        out_specs=pl.BlockSpec((1, step_rows, d), lambda b, i: (b, i, 0)),
        out_shape=jax.ShapeDtypeStruct((bsz, seq, d), F32),
        scratch_shapes=[
            pltpu.VMEM((nt, SEQ_TILE, B_WIDTH), BF16),
            pltpu.VMEM((nt, B_HEADS, V_EXT, SEQ_TILE), BF16),
            pltpu.VMEM((B_HEADS, LANES, 2 * SEQ_TILE), BF16),
            pltpu.VMEM((B_HEADS, SUBLANES, 2 * SEQ_TILE), F32),
            pltpu.VMEM((2 * B_HEADS, V_EXT, SEQ_TILE), F32),
            pltpu.VMEM((SEQ_TILE, A_WIDTH + B_WIDTH), BF16),
            pltpu.VMEM((B_HEADS, SEQ_TILE, 2 * SEQ_TILE), F32),
            pltpu.VMEM((2, SEQ_TILE, 2 * SEQ_TILE), BF16),
            pltpu.VMEM((2, SUBLANES, 2 * SEQ_TILE), F32),
            pltpu.VMEM((B_HEADS, SUBLANES, 2 * SEQ_TILE), F32),
        ],
        compiler_params=pltpu.CompilerParams(
            dimension_semantics=("arbitrary", "arbitrary"),
            vmem_limit_bytes=VMEM_LIMIT_BYTES),
        name="fused_layer",
    )(x, mod, pos_q, invf, norm_w, w_in_b,
      sgu_norm_w.reshape(A_HEADS, A_HEAD_DIM), w_s.reshape(A_HEADS, CHUNK, CHUNK),
      jnp.transpose(b_s.reshape(A_HEADS, CHUNK)), qnw, knw,
      lambda_q1, lambda_k1, lambda_q2, lambda_k2, subln_w, w_out_b, bd)
    return out
```

```python
import math

import numpy as np
import jax
import jax.numpy as jnp
from jax import lax
from jax.experimental import pallas as pl
from jax.experimental.pallas import tpu as pltpu

D_MODEL = 1024
A_WIDTH = 512
B_WIDTH = 512
A_HEADS = 4
A_HEAD_DIM = 128
CHUNK = 128
B_HEADS = 4
B_HEAD_DIM = 64
B_V_DIM = 128
ROPE_THETA = 10000.0
NORM_EPS = 1e-6
SUBLN_EPS = 1e-5
IN_COLS = 3 * A_WIDTH + 4 * B_WIDTH
LAMBDA_INIT = 0.8 - 0.6 * math.exp(-0.3 * 0)
LOG2E = math.log2(math.e)

LANES = 128
SUBLANES = 8
MXU_DIM = 256
SEQ_TILE = 256
TILES_PER_STEP = 2
VMEM_LIMIT_BYTES = 56 * 1024 * 1024
HALF = B_HEAD_DIM // 2
BF16_ROWS = 16
V_EXT = B_V_DIM + BF16_ROWS

F32 = jnp.float32
BF16 = jnp.bfloat16


def _pair_layout(t):
    lead = t.shape[:-1]
    t = t.reshape(lead + (B_HEADS, 2, 2, HALF))
    t = jnp.swapaxes(t, -3, -2)
    return t.reshape(lead + (B_WIDTH,))


def _prep_weights_kernel(w_ref, wout_ref, o_ref, oout_ref):
    oout_ref[...] = wout_ref[...].astype(BF16)
    grp = lax.broadcasted_iota(jnp.int32, (1, LANES), 1) // HALF
    q0, v0 = 3 * A_WIDTH, 3 * A_WIDTH + 2 * B_WIDTH
    for cb in range(IN_COLS // LANES):
        t = w_ref[:, cb * LANES:(cb + 1) * LANES]
        if q0 <= cb * LANES < v0:
            t = jnp.where(grp == 1, pltpu.roll(t, LANES - HALF, 1),
                          jnp.where(grp == 2, pltpu.roll(t, HALF, 1), t))
        o_ref[:, cb * LANES:(cb + 1) * LANES] = t.astype(BF16)


def _adaln_kernel(c_ref, w_ref, b_ref, mod_ref):
    c = c_ref[...]
    c_act = c * jax.nn.sigmoid(c)
    mod_ref[...] = jnp.dot(c_act.astype(BF16), w_ref[...].astype(BF16),
                           preferred_element_type=F32) + b_ref[...]


def _silu(z):
    return z * jax.nn.sigmoid(z)


def _head_mean_sq(t, bd_ref):
    sq = (t * t).astype(BF16)
    halves = [jnp.dot(sq[:, h * MXU_DIM:(h + 1) * MXU_DIM], bd_ref[...],
                      preferred_element_type=F32) for h in range(B_WIDTH // MXU_DIM)]
    return jnp.concatenate(halves, axis=-1) * (1.0 / B_HEAD_DIM)


def _rope(t, cos, sin_signed):
    cols = []
    for cb in range(B_WIDTH // LANES):
        tc = t[:, cb * LANES:(cb + 1) * LANES]
        cols.append(tc * cos + pltpu.roll(tc, LANES // 2, 1) * sin_signed)
    return jnp.concatenate(cols, axis=-1)


def _rope_tables(pos_quarters, invf, lane):
    n_grp = LANES // HALF
    qrows = SEQ_TILE // n_grp
    grp = lane // HALF
    pos_t = pos_quarters.T
    packed = pos_t[:qrows, n_grp - 1:n_grp]
    for gi in range(n_grp - 2, -1, -1):
        packed = jnp.where(grp == gi, pos_t[:qrows, gi:gi + 1], packed)
    ang = packed * invf
    tables = []
    for packed_tab in (jnp.cos(ang), jnp.sin(ang)):
        parts = []
        for gi in range(n_grp):
            t = jnp.where(grp == gi, packed_tab, 0.0)
            t = t + pltpu.roll(t, 2 * HALF, 1)
            parts.append(t + pltpu.roll(t, HALF, 1))
        tables.append(jnp.concatenate(parts, axis=0))
    return tables


def _across_sublanes(x, op):
    for shift in (4, 2, 1):
        x = op(x, pltpu.roll(x, shift, 0))
    return x


def _layer_kernel(x_ref, mod_ref, pos_ref, invf_ref, normw_ref, win_ref, sgu_ref, ws_ref,
                  bst_ref, qnw_ref, knw_ref, lq1_ref, lk1_ref, lq2_ref, lk2_ref, subln_ref,
                  wout_ref, bd_ref, out_ref,
                  k_scr, vt_scr, qblk_scr, m_scr, acc_scr, mixed_scr, s_scr, p_scr,
                  alpha_scr, smax_scr):
    refs = dict(locals())
    tiles = [_TilePhases(t, refs) for t in range(TILES_PER_STEP)]
    tiles[0].activations()
    tiles[0].qk_projections()
    tiles[0].before_loop()
    for t, tile in enumerate(tiles):
        nxt = tiles[t + 1] if t + 1 < TILES_PER_STEP else None
        tile.key_loop()
        tile.last_value_product_and_out_a()
        if nxt is not None:
            nxt.activations()
            nxt.qk_projections()
        tile.finalise()
        tile.out_b()
        if nxt is not None:
            nxt.before_loop()


class _TilePhases:
    def __init__(self, t, refs):
        self.__dict__.update(refs)
        self.t = t
        self.par = t % 2
        self.i = pl.program_id(1) * TILES_PER_STEP + t
        self.tile_rows = slice(t * SEQ_TILE, (t + 1) * SEQ_TILE)

    def activations(self):
        x_ref, tile_rows = self.x_ref, self.tile_rows
        b = pl.program_id(0)
        shift = self.mod_ref[0, pl.ds(b, 1), :]
        scale = self.mod_ref[1, pl.ds(b, 1), :]
        self.gate = self.mod_ref[2, pl.ds(b, 1), :]
        sq_lanes = None
        for c0 in range(0, D_MODEL, LANES):
            x_c = x_ref[0, tile_rows, c0:c0 + LANES]
            sq_lanes = x_c * x_c if sq_lanes is None else sq_lanes + x_c * x_c
        sum_sq = jnp.sum(sq_lanes, axis=-1, keepdims=True)
        inv_rms = lax.rsqrt(sum_sq * (1.0 / D_MODEL) + NORM_EPS)
        hb_cols = []
        for c0 in range(0, D_MODEL, MXU_DIM):
            cs = slice(c0, c0 + MXU_DIM)
            h_c = ((x_ref[0, tile_rows, cs] * inv_rms * self.normw_ref[:, cs])
                   * (1.0 + scale[:, cs]) + shift[:, cs])
            hb_cols.append(h_c.astype(BF16))
        self.hb = jnp.concatenate(hb_cols, axis=1)

    def proj(self, c0, width):
        return jnp.dot(self.hb, self.win_ref[:, c0:c0 + width], preferred_element_type=F32)

    def qk_projections(self):
        self.qb = self.proj(3 * A_WIDTH, B_WIDTH)
        self.kb = self.proj(3 * A_WIDTH + B_WIDTH, B_WIDTH)

    def scores(self, g, j):
        tk, tq = SEQ_TILE, SEQ_TILE
        kp = self.k_scr[j, :, g * LANES:(g + 1) * LANES]
        s = jnp.dot(kp, self.qblk_scr[g], preferred_element_type=F32)
        self.s_scr[g] = s
        self.smax_scr[g] = jnp.max(s.reshape(tk // SUBLANES, SUBLANES, 2 * tq), axis=0)

    def softmax(self, g, slot, masked):
        tk, tq = SEQ_TILE, SEQ_TILE
        rows = 2 * SUBLANES
        s_scr, m_scr, par = self.s_scr, self.m_scr, self.par

        def chunk(c):
            sc = s_scr[g, c * rows:(c + 1) * rows, :]
            if masked:
                kv_row = lax.broadcasted_iota(jnp.int32, (rows, 2 * tq), 0) + c * rows
                q_col = lax.broadcasted_iota(jnp.int32, (rows, 2 * tq), 1) % tq
                sc = jnp.where(kv_row <= q_col, sc, -jnp.inf)
            return sc.reshape(2, SUBLANES, 2 * tq)

        if masked:
            col_max = jnp.max(chunk(0), axis=0)
            for c in range(1, tk // rows):
                col_max = jnp.maximum(col_max, jnp.max(chunk(c), axis=0))
        else:
            col_max = self.smax_scr[g]
        m_old = m_scr[par, g]
        m_new = jnp.maximum(m_old, _across_sublanes(col_max, jnp.maximum))
        alpha = jnp.exp2(m_old - m_new)
        m_scr[par, g] = m_new
        self.alpha_scr[slot] = alpha
        for c in range(tk // rows):
            pc = jnp.exp2(chunk(c) - m_new[None])
            self.p_scr[slot, c * rows:(c + 1) * rows, :] = (
                pc.reshape(rows, 2 * tq).astype(BF16))

    def value_update(self, g, j, slot):
        tq = SEQ_TILE
        acc_scr, par = self.acc_scr, self.par
        vt = self.vt_scr[j, g]
        alpha = self.alpha_scr[slot]
        for st in range(2):
            pv = jnp.dot(vt, self.p_scr[slot, :, st * tq:(st + 1) * tq],
                         preferred_element_type=F32)
            a = alpha[:, st * tq:(st + 1) * tq]
            acc = acc_scr[par, 2 * g + st].reshape(V_EXT // SUBLANES, SUBLANES, tq)
            acc_scr[par, 2 * g + st] = (acc * a[None]).reshape(V_EXT, tq) + pv

    def unit(self, g, masked, tile, succ_tile, prev_tile):
        ahead = (g + 2) % B_HEADS
        if g + 2 < B_HEADS:
            self.scores(ahead, tile)
        elif succ_tile is not None:
            self.scores(ahead, succ_tile)
        if prev_tile is not None:
            self.value_update((g - 1) % B_HEADS, prev_tile, (g - 1) % 2)
        self.softmax(g, g % 2, masked)

    def before_loop(self):
        tq, tk, i, par = SEQ_TILE, SEQ_TILE, self.i, self.par
        qb, kb = self.qb, self.kb
        lane = lax.broadcasted_iota(jnp.int32, (1, LANES), 1)
        cos, sin = _rope_tables(self.pos_ref[self.t].astype(F32), self.invf_ref[...], lane)
        sin_signed = jnp.where(lane < LANES // 2, -sin, sin)
        q = qb * lax.rsqrt(_head_mean_sq(qb, self.bd_ref) + NORM_EPS) * self.qnw_ref[...]
        k = kb * lax.rsqrt(_head_mean_sq(kb, self.bd_ref) + NORM_EPS) * self.knw_ref[...]
        q = _rope(q, cos, sin_signed) * (B_HEAD_DIM ** -0.5 * LOG2E)
        k = _rope(k, cos, sin_signed)
        self.k_scr[i] = k.astype(BF16)

        qt = q.T
        pair_row = lax.broadcasted_iota(jnp.int32, (LANES, tq), 0)
        is_a = (pair_row // HALF) % 2 == 0
        for g in range(B_HEADS):
            qp = qt[g * LANES:(g + 1) * LANES, :]
            self.qblk_scr[g] = jnp.concatenate(
                [jnp.where(is_a, qp, 0.0), jnp.where(is_a, 0.0, qp)], axis=1).astype(BF16)

        self.m_scr[par] = jnp.full(self.m_scr.shape[1:], -jnp.inf, F32)
        self.acc_scr[par] = jnp.zeros(self.acc_scr.shape[1:], F32)

        vb = self.proj(3 * A_WIDTH + 2 * B_WIDTH, B_WIDTH)
        vbt = vb.T.astype(BF16)
        for g in range(B_HEADS):
            self.vt_scr[i, g, :B_V_DIM, :] = vbt[g * B_V_DIM:(g + 1) * B_V_DIM, :]
            self.vt_scr[i, g, B_V_DIM:, :] = jnp.ones((V_EXT - B_V_DIM, tk), BF16)
        va = self.proj(A_WIDTH, A_WIDTH)
        self.scores(0, i)
        self.scores(1, i)
        ua = self.proj(0, A_WIDTH)
        self.unit(0, True, i, 0, None)
        za = self.proj(2 * A_WIDTH, A_WIDTH)
        self.unit(1, True, i, 0, i)
        zb = self.proj(3 * A_WIDTH + 3 * B_WIDTH, B_WIDTH)
        self.unit(2, True, i, 0, i)

        row_c = lax.broadcasted_iota(jnp.int32, (CHUNK, CHUNK), 0)
        col_c = lax.broadcasted_iota(jnp.int32, (CHUNK, CHUNK), 1)
        tril = row_c >= col_c
        for hh in range(A_HEADS):
            sl = slice(hh * A_HEAD_DIM, (hh + 1) * A_HEAD_DIM)
            v = va[:, sl]
            vn = (v * lax.rsqrt(jnp.mean(v * v, axis=-1, keepdims=True) + NORM_EPS)
                  * self.sgu_ref[hh:hh + 1, :])
            vnb = vn.astype(BF16)
            ws = jnp.where(tril, self.ws_ref[hh], 0.0).astype(BF16)
            bias = self.bst_ref[:, hh:hh + 1]
            n_chunks = tq // CHUNK
            chunks = jnp.concatenate(
                [vnb[c * CHUNK:(c + 1) * CHUNK, :] for c in range(n_chunks)], axis=1)
            mixed_chunks = jnp.dot(ws, chunks, preferred_element_type=F32)
            mix = jnp.concatenate([mixed_chunks[:, c * A_HEAD_DIM:(c + 1) * A_HEAD_DIM] + bias
                                   for c in range(n_chunks)], axis=0)
            a_out = ua[:, sl] * mix * _silu(za[:, sl])
            self.mixed_scr[par, :, sl] = a_out.astype(BF16)
        self.zb_gate = _silu(zb)
        self.unit(3, True, i, 0, i)

    def key_loop(self):
        i = self.i

        def tile_units(j, is_final=False):
            succ = None if is_final else j + 1
            self.unit(0, False, j, succ, jnp.where(j == 0, i, j - 1))
            for g in range(1, B_HEADS):
                self.unit(g, False, j, succ, j)

        def pair_body(jj, carry):
            tile_units(2 * jj)
            tile_units(2 * jj + 1)
            return carry

        lax.fori_loop(0, i // 2, pair_body, 0)

        @pl.when(i % 2 == 1)
        def _():
            tile_units(i - 1, is_final=True)

    def last_value_product_and_out_a(self):
        last = B_HEADS - 1
        self.value_update(last, jnp.maximum(self.i - 1, 0), last % 2)
        rows = self.tile_rows
        self.out_ref[0, rows, :] = self.x_ref[0, rows, :] + self.gate * jnp.dot(
            self.mixed_scr[self.par, :, :A_WIDTH], self.wout_ref[:A_WIDTH, :],
            preferred_element_type=F32)

    def finalise(self):
        tq, par, acc_scr = SEQ_TILE, self.par, self.acc_scr
        lam = (jnp.exp(jnp.sum(self.lq1_ref[...] * self.lk1_ref[...], axis=-1, keepdims=True))
               - jnp.exp(jnp.sum(self.lq2_ref[...] * self.lk2_ref[...], axis=-1, keepdims=True))
               + LAMBDA_INIT)
        for g in range(B_HEADS):
            inv_l1 = 1.0 / acc_scr[par, 2 * g, B_V_DIM:B_V_DIM + SUBLANES, :]
            inv_l2 = 1.0 / acc_scr[par, 2 * g + 1, B_V_DIM:B_V_DIM + SUBLANES, :]
            acc1 = acc_scr[par, 2 * g, :B_V_DIM, :].reshape(B_V_DIM // SUBLANES, SUBLANES, tq)
            acc2 = acc_scr[par, 2 * g + 1, :B_V_DIM, :].reshape(
                B_V_DIM // SUBLANES, SUBLANES, tq)
            ot = acc1 * inv_l1[None] - lam * (acc2 * inv_l2[None])
            o = ot.reshape(B_V_DIM, tq).T
            o = (o * lax.rsqrt(jnp.mean(o * o, axis=-1, keepdims=True) + SUBLN_EPS)
                 * self.subln_ref[...])
            o = o * (1.0 - LAMBDA_INIT)
            vsl = slice(g * B_V_DIM, (g + 1) * B_V_DIM)
            b_out = o * self.zb_gate[:, vsl]
            self.mixed_scr[par, :, A_WIDTH + g * B_V_DIM:A_WIDTH + (g + 1) * B_V_DIM] = (
                b_out.astype(BF16))

    def out_b(self):
        self.out_ref[0, self.tile_rows, :] += self.gate * jnp.dot(
            self.mixed_scr[self.par, :, A_WIDTH:], self.wout_ref[A_WIDTH:, :],
            preferred_element_type=F32)


def kernel(x, c, positions, norm_w, w_ada, b_ada, w_in, sgu_norm_w, w_s, b_s, q_norm_w, k_norm_w,
           lambda_q1, lambda_k1, lambda_q2, lambda_k2, subln_w, w_out):
    bsz, seq, d = x.shape
    assert d == D_MODEL and seq % (TILES_PER_STEP * SEQ_TILE) == 0 and norm_w.shape[0] == 1
    nt = seq // SEQ_TILE

    mod = pl.pallas_call(
        _adaln_kernel,
        grid=(3,),
        in_specs=[pl.BlockSpec((bsz, d), lambda n: (0, 0)),
                  pl.BlockSpec((None, d, d), lambda n: (0, 0, n)),
                  pl.BlockSpec((1, d), lambda n: (0, n))],
        out_specs=pl.BlockSpec((None, bsz, d), lambda n: (n, 0, 0)),
        out_shape=jax.ShapeDtypeStruct((3, bsz, d), F32),
        name="adaln_mod",
    )(c, w_ada, b_ada)

    n_grp = LANES // HALF
    pos_q = jnp.pad(positions.reshape(bsz, nt, n_grp, SEQ_TILE // n_grp),
                    ((0, 0), (0, 0), (0, SUBLANES - n_grp), (0, LANES - SEQ_TILE // n_grp)))

    inv_freq = ROPE_THETA ** (-jnp.arange(0, B_HEAD_DIM, 2, dtype=F32) / B_HEAD_DIM)
    invf = jnp.tile(inv_freq, LANES // HALF)[None, :]
    lane_head = (np.arange(MXU_DIM) // LANES) * 2 + (np.arange(MXU_DIM) // HALF) % 2
    bd = jnp.asarray(lane_head[:, None] == lane_head[None, :], dtype=BF16)

    prep_rows = d // 8
    w_in_b, w_out_b = pl.pallas_call(
        _prep_weights_kernel,
        grid=(d // prep_rows,),
        in_specs=[pl.BlockSpec((None, prep_rows, IN_COLS), lambda r: (0, r, 0)),
                  pl.BlockSpec((None, prep_rows, d), lambda r: (0, r, 0))],
        out_specs=[pl.BlockSpec((prep_rows, IN_COLS), lambda r: (r, 0)),
                   pl.BlockSpec((prep_rows, d), lambda r: (r, 0))],
        out_shape=[jax.ShapeDtypeStruct((d, IN_COLS), BF16),
                   jax.ShapeDtypeStruct((A_WIDTH + B_WIDTH, d), BF16)],
        name="prep_weights",
    )(w_in, w_out)
    qnw = _pair_layout(jnp.tile(q_norm_w.reshape(B_HEAD_DIM), 2 * B_HEADS))[None, :]
    knw = _pair_layout(jnp.tile(k_norm_w.reshape(B_HEAD_DIM), 2 * B_HEADS))[None, :]

    const2 = lambda b, i: (0, 0)
    const3 = lambda b, i: (0, 0, 0)
    step_rows = TILES_PER_STEP * SEQ_TILE
    out = pl.pallas_call(
        _layer_kernel,
        grid=(bsz, nt // TILES_PER_STEP),
        in_specs=[
            pl.BlockSpec((1, step_rows, d), lambda b, i: (b, i, 0)),
            pl.BlockSpec((3, bsz, d), const3),
            pl.BlockSpec((None, TILES_PER_STEP, SUBLANES, LANES),
                         lambda b, i: (b, i, 0, 0)),
            pl.BlockSpec((1, LANES), const2),
            pl.BlockSpec((1, d), const2),
            pl.BlockSpec((d, IN_COLS), const2),
            pl.BlockSpec((A_HEADS, A_HEAD_DIM), const2),
            pl.BlockSpec((A_HEADS, CHUNK, CHUNK), const3),
            pl.BlockSpec((CHUNK, A_HEADS), const2),
            pl.BlockSpec((1, B_WIDTH), const2),
            pl.BlockSpec((1, B_WIDTH), const2),
            pl.BlockSpec((1, B_HEAD_DIM), const2),
            pl.BlockSpec((1, B_HEAD_DIM), const2),
            pl.BlockSpec((1, B_HEAD_DIM), const2),
            pl.BlockSpec((1, B_HEAD_DIM), const2),
            pl.BlockSpec((1, B_V_DIM), const2),
            pl.BlockSpec((A_WIDTH + B_WIDTH, d), const2),
            pl.BlockSpec((MXU_DIM, MXU_DIM), const2),
        ],
        out_specs=pl.BlockSpec((1, step_rows, d), lambda b, i: (b, i, 0)),
        out_shape=jax.ShapeDtypeStruct((bsz, seq, d), F32),
        scratch_shapes=[
            pltpu.VMEM((nt, SEQ_TILE, B_WIDTH), BF16),
            pltpu.VMEM((nt, B_HEADS, V_EXT, SEQ_TILE), BF16),
            pltpu.VMEM((B_HEADS, LANES, 2 * SEQ_TILE), BF16),
            pltpu.VMEM((2, B_HEADS, SUBLANES, 2 * SEQ_TILE), F32),
            pltpu.VMEM((2, 2 * B_HEADS, V_EXT, SEQ_TILE), F32),
            pltpu.VMEM((2, SEQ_TILE, A_WIDTH + B_WIDTH), BF16),
            pltpu.VMEM((B_HEADS, SEQ_TILE, 2 * SEQ_TILE), F32),
            pltpu.VMEM((2, SEQ_TILE, 2 * SEQ_TILE), BF16),
            pltpu.VMEM((2, SUBLANES, 2 * SEQ_TILE), F32),
            pltpu.VMEM((B_HEADS, SUBLANES, 2 * SEQ_TILE), F32),
        ],
        compiler_params=pltpu.CompilerParams(
            dimension_semantics=("arbitrary", "arbitrary"),
            vmem_limit_bytes=VMEM_LIMIT_BYTES),
        name="fused_layer",
    )(x, mod, pos_q, invf, norm_w, w_in_b,
      sgu_norm_w.reshape(A_HEADS, A_HEAD_DIM), w_s.reshape(A_HEADS, CHUNK, CHUNK),
      jnp.transpose(b_s.reshape(A_HEADS, CHUNK)), qnw, knw,
      lambda_q1, lambda_k1, lambda_q2, lambda_k2, subln_w, w_out_b, bd)
    return out
```

```python
import math

import numpy as np
import jax
import jax.numpy as jnp
from jax import lax
from jax.experimental import pallas as pl
from jax.experimental.pallas import tpu as pltpu

D_MODEL = 1024
A_WIDTH = 512
B_WIDTH = 512
A_HEADS = 4
A_HEAD_DIM = 128
CHUNK = 128
B_HEADS = 4
B_HEAD_DIM = 64
B_V_DIM = 128
ROPE_THETA = 10000.0
NORM_EPS = 1e-6
SUBLN_EPS = 1e-5
IN_COLS = 3 * A_WIDTH + 4 * B_WIDTH
LAMBDA_INIT = 0.8 - 0.6 * math.exp(-0.3 * 0)
LOG2E = math.log2(math.e)

LANES = 128
SUBLANES = 8
MXU_DIM = 256
SEQ_TILE = 256
TILES_PER_STEP = 4
VMEM_LIMIT_BYTES = 56 * 1024 * 1024
HALF = B_HEAD_DIM // 2
BF16_ROWS = 16
V_EXT = B_V_DIM + BF16_ROWS

F32 = jnp.float32
BF16 = jnp.bfloat16


def _pair_layout(t):
    lead = t.shape[:-1]
    t = t.reshape(lead + (B_HEADS, 2, 2, HALF))
    t = jnp.swapaxes(t, -3, -2)
    return t.reshape(lead + (B_WIDTH,))


def _prep_weights_kernel(w_ref, wout_ref, o_ref, oout_ref):
    oout_ref[...] = wout_ref[...].astype(BF16)
    grp = lax.broadcasted_iota(jnp.int32, (1, LANES), 1) // HALF
    q0, v0 = 3 * A_WIDTH, 3 * A_WIDTH + 2 * B_WIDTH
    for cb in range(IN_COLS // LANES):
        t = w_ref[:, cb * LANES:(cb + 1) * LANES]
        if q0 <= cb * LANES < v0:
            t = jnp.where(grp == 1, pltpu.roll(t, LANES - HALF, 1),
                          jnp.where(grp == 2, pltpu.roll(t, HALF, 1), t))
        o_ref[:, cb * LANES:(cb + 1) * LANES] = t.astype(BF16)


def _adaln_kernel(c_ref, w_ref, b_ref, mod_ref):
    c = c_ref[...]
    c_act = c * jax.nn.sigmoid(c)
    mod_ref[...] = jnp.dot(c_act.astype(BF16), w_ref[...].astype(BF16),
                           preferred_element_type=F32) + b_ref[...]


def _silu(z):
    return z * jax.nn.sigmoid(z)


def _head_mean_sq(t, bd_ref):
    sq = (t * t).astype(BF16)
    halves = [jnp.dot(sq[:, h * MXU_DIM:(h + 1) * MXU_DIM], bd_ref[...],
                      preferred_element_type=F32) for h in range(B_WIDTH // MXU_DIM)]
    return jnp.concatenate(halves, axis=-1) * (1.0 / B_HEAD_DIM)


def _rope(t, cos, sin_signed):
    cols = []
    for cb in range(B_WIDTH // LANES):
        tc = t[:, cb * LANES:(cb + 1) * LANES]
        cols.append(tc * cos + pltpu.roll(tc, LANES // 2, 1) * sin_signed)
    return jnp.concatenate(cols, axis=-1)


def _rope_tables(pos_quarters, invf, lane):
    n_grp = LANES // HALF
    qrows = SEQ_TILE // n_grp
    grp = lane // HALF
    pos_t = pos_quarters.T
    packed = pos_t[:qrows, n_grp - 1:n_grp]
    for gi in range(n_grp - 2, -1, -1):
        packed = jnp.where(grp == gi, pos_t[:qrows, gi:gi + 1], packed)
    ang = packed * invf
    tables = []
    for packed_tab in (jnp.cos(ang), jnp.sin(ang)):
        parts = []
        for gi in range(n_grp):
            t = jnp.where(grp == gi, packed_tab, 0.0)
            t = t + pltpu.roll(t, 2 * HALF, 1)
            parts.append(t + pltpu.roll(t, HALF, 1))
        tables.append(jnp.concatenate(parts, axis=0))
    return tables


def _across_sublanes(x, op):
    for shift in (4, 2, 1):
        x = op(x, pltpu.roll(x, shift, 0))
    return x


def _layer_kernel(x_ref, mod_ref, pos_ref, invf_ref, normw_ref, win_ref, sgu_ref, ws_ref,
                  bst_ref, qnw_ref, knw_ref, lq1_ref, lk1_ref, lq2_ref, lk2_ref, subln_ref,
                  wout_ref, bd_ref, out_ref,
                  k_scr, vt_scr, qblk_scr, m_scr, acc_scr, mixed_scr, s_scr, p_scr,
                  alpha_scr, smax_scr):
    refs = dict(locals())
    tiles = [_TilePhases(t, refs) for t in range(TILES_PER_STEP)]
    tiles[0].activations()
    tiles[0].qk_projections()
    tiles[0].before_loop()
    for t, tile in enumerate(tiles):
        nxt = tiles[t + 1] if t + 1 < TILES_PER_STEP else None
        tile.key_loop()
        tile.last_value_product_and_out_a()
        if nxt is not None:
            nxt.activations()
            nxt.qk_projections()
        tile.finalise()
        tile.out_b()
        if nxt is not None:
            nxt.before_loop()


class _TilePhases:
    def __init__(self, t, refs):
        self.__dict__.update(refs)
        self.t = t
        self.par = t % 2
        self.i = pl.program_id(1) * TILES_PER_STEP + t
        self.tile_rows = slice(t * SEQ_TILE, (t + 1) * SEQ_TILE)

    def activations(self):
        x_ref, tile_rows = self.x_ref, self.tile_rows
        b = pl.program_id(0)
        shift = self.mod_ref[0, pl.ds(b, 1), :]
        scale = self.mod_ref[1, pl.ds(b, 1), :]
        self.gate = self.mod_ref[2, pl.ds(b, 1), :]
        sq_lanes = None
        for c0 in range(0, D_MODEL, LANES):
            x_c = x_ref[0, tile_rows, c0:c0 + LANES]
            sq_lanes = x_c * x_c if sq_lanes is None else sq_lanes + x_c * x_c
        sum_sq = jnp.sum(sq_lanes, axis=-1, keepdims=True)
        inv_rms = lax.rsqrt(sum_sq * (1.0 / D_MODEL) + NORM_EPS)
        hb_cols = []
        for c0 in range(0, D_MODEL, MXU_DIM):
            cs = slice(c0, c0 + MXU_DIM)
            h_c = ((x_ref[0, tile_rows, cs] * inv_rms * self.normw_ref[:, cs])
                   * (1.0 + scale[:, cs]) + shift[:, cs])
            hb_cols.append(h_c.astype(BF16))
        self.hb = jnp.concatenate(hb_cols, axis=1)

    def proj(self, c0, width):
        return jnp.dot(self.hb, self.win_ref[:, c0:c0 + width], preferred_element_type=F32)

    def qk_projections(self):
        self.qb = self.proj(3 * A_WIDTH, B_WIDTH)
        self.kb = self.proj(3 * A_WIDTH + B_WIDTH, B_WIDTH)

    def scores(self, g, j):
        tk, tq = SEQ_TILE, SEQ_TILE
        kp = self.k_scr[j, :, g * LANES:(g + 1) * LANES]
        s = jnp.dot(kp, self.qblk_scr[g], preferred_element_type=F32)
        self.s_scr[g] = s
        self.smax_scr[g] = jnp.max(s.reshape(tk // SUBLANES, SUBLANES, 2 * tq), axis=0)

    def softmax(self, g, slot, masked):
        tk, tq = SEQ_TILE, SEQ_TILE
        rows = 2 * SUBLANES
        s_scr, m_scr, par = self.s_scr, self.m_scr, self.par

        def chunk(c):
            sc = s_scr[g, c * rows:(c + 1) * rows, :]
            if masked:
                kv_row = lax.broadcasted_iota(jnp.int32, (rows, 2 * tq), 0) + c * rows
                q_col = lax.broadcasted_iota(jnp.int32, (rows, 2 * tq), 1) % tq
                sc = jnp.where(kv_row <= q_col, sc, -jnp.inf)
            return sc.reshape(2, SUBLANES, 2 * tq)

        if masked:
            col_max = jnp.max(chunk(0), axis=0)
            for c in range(1, tk // rows):
                col_max = jnp.maximum(col_max, jnp.max(chunk(c), axis=0))
        else:
            col_max = self.smax_scr[g]
        m_old = m_scr[par, g]
        m_new = jnp.maximum(m_old, _across_sublanes(col_max, jnp.maximum))
        alpha = jnp.exp2(m_old - m_new)
        m_scr[par, g] = m_new
        self.alpha_scr[slot] = alpha
        for c in range(tk // rows):
            pc = jnp.exp2(chunk(c) - m_new[None])
            self.p_scr[slot, c * rows:(c + 1) * rows, :] = (
                pc.reshape(rows, 2 * tq).astype(BF16))

    def value_update(self, g, j, slot):
        tq = SEQ_TILE
        acc_scr, par = self.acc_scr, self.par
        vt = self.vt_scr[j, g]
        alpha = self.alpha_scr[slot]
        for st in range(2):
            pv = jnp.dot(vt, self.p_scr[slot, :, st * tq:(st + 1) * tq],
                         preferred_element_type=F32)
            a = alpha[:, st * tq:(st + 1) * tq]
            acc = acc_scr[par, 2 * g + st].reshape(V_EXT // SUBLANES, SUBLANES, tq)
            acc_scr[par, 2 * g + st] = (acc * a[None]).reshape(V_EXT, tq) + pv

    def unit(self, g, masked, tile, succ_tile, prev_tile):
        ahead = (g + 2) % B_HEADS
        if g + 2 < B_HEADS:
            self.scores(ahead, tile)
        elif succ_tile is not None:
            self.scores(ahead, succ_tile)
        if prev_tile is not None:
            self.value_update((g - 1) % B_HEADS, prev_tile, (g - 1) % 2)
        self.softmax(g, g % 2, masked)

    def before_loop(self):
        tq, tk, i, par = SEQ_TILE, SEQ_TILE, self.i, self.par
        qb, kb = self.qb, self.kb
        lane = lax.broadcasted_iota(jnp.int32, (1, LANES), 1)
        cos, sin = _rope_tables(self.pos_ref[self.t].astype(F32), self.invf_ref[...], lane)
        sin_signed = jnp.where(lane < LANES // 2, -sin, sin)
        q = qb * lax.rsqrt(_head_mean_sq(qb, self.bd_ref) + NORM_EPS) * self.qnw_ref[...]
        k = kb * lax.rsqrt(_head_mean_sq(kb, self.bd_ref) + NORM_EPS) * self.knw_ref[...]
        q = _rope(q, cos, sin_signed) * (B_HEAD_DIM ** -0.5 * LOG2E)
        k = _rope(k, cos, sin_signed)
        self.k_scr[i] = k.astype(BF16)

        qt = q.T
        pair_row = lax.broadcasted_iota(jnp.int32, (LANES, tq), 0)
        is_a = (pair_row // HALF) % 2 == 0
        for g in range(B_HEADS):
            qp = qt[g * LANES:(g + 1) * LANES, :]
            self.qblk_scr[g] = jnp.concatenate(
                [jnp.where(is_a, qp, 0.0), jnp.where(is_a, 0.0, qp)], axis=1).astype(BF16)

        self.m_scr[par] = jnp.full(self.m_scr.shape[1:], -jnp.inf, F32)
        self.acc_scr[par] = jnp.zeros(self.acc_scr.shape[1:], F32)

        vb = self.proj(3 * A_WIDTH + 2 * B_WIDTH, B_WIDTH)
        vbt = vb.T.astype(BF16)
        for g in range(B_HEADS):
            self.vt_scr[i, g, :B_V_DIM, :] = vbt[g * B_V_DIM:(g + 1) * B_V_DIM, :]
            self.vt_scr[i, g, B_V_DIM:, :] = jnp.ones((V_EXT - B_V_DIM, tk), BF16)
        va = self.proj(A_WIDTH, A_WIDTH)
        self.scores(0, i)
        self.scores(1, i)
        ua = self.proj(0, A_WIDTH)
        self.unit(0, True, i, 0, None)
        za = self.proj(2 * A_WIDTH, A_WIDTH)
        self.unit(1, True, i, 0, i)
        zb = self.proj(3 * A_WIDTH + 3 * B_WIDTH, B_WIDTH)
        self.unit(2, True, i, 0, i)

        row_c = lax.broadcasted_iota(jnp.int32, (CHUNK, CHUNK), 0)
        col_c = lax.broadcasted_iota(jnp.int32, (CHUNK, CHUNK), 1)
        tril = row_c >= col_c
        for hh in range(A_HEADS):
            sl = slice(hh * A_HEAD_DIM, (hh + 1) * A_HEAD_DIM)
            v = va[:, sl]
            vn = (v * lax.rsqrt(jnp.mean(v * v, axis=-1, keepdims=True) + NORM_EPS)
                  * self.sgu_ref[hh:hh + 1, :])
            vnb = vn.astype(BF16)
            ws = jnp.where(tril, self.ws_ref[hh], 0.0).astype(BF16)
            bias = self.bst_ref[:, hh:hh + 1]
            n_chunks = tq // CHUNK
            chunks = jnp.concatenate(
                [vnb[c * CHUNK:(c + 1) * CHUNK, :] for c in range(n_chunks)], axis=1)
            mixed_chunks = jnp.dot(ws, chunks, preferred_element_type=F32)
            mix = jnp.concatenate([mixed_chunks[:, c * A_HEAD_DIM:(c + 1) * A_HEAD_DIM] + bias
                                   for c in range(n_chunks)], axis=0)
            a_out = ua[:, sl] * mix * _silu(za[:, sl])
            self.mixed_scr[par, :, sl] = a_out.astype(BF16)
        self.zb_gate = _silu(zb)
        self.unit(3, True, i, 0, i)

    def key_loop(self):
        i = self.i

        def tile_units(j, is_final=False):
            succ = None if is_final else j + 1
            self.unit(0, False, j, succ, jnp.where(j == 0, i, j - 1))
            for g in range(1, B_HEADS):
                self.unit(g, False, j, succ, j)

        def pair_body(jj, carry):
            tile_units(2 * jj)
            tile_units(2 * jj + 1)
            return carry

        lax.fori_loop(0, i // 2, pair_body, 0)

        @pl.when(i % 2 == 1)
        def _():
            tile_units(i - 1, is_final=True)

    def last_value_product_and_out_a(self):
        last = B_HEADS - 1
        self.value_update(last, jnp.maximum(self.i - 1, 0), last % 2)
        rows = self.tile_rows
        self.out_ref[0, rows, :] = self.x_ref[0, rows, :] + self.gate * jnp.dot(
            self.mixed_scr[self.par, :, :A_WIDTH], self.wout_ref[:A_WIDTH, :],
            preferred_element_type=F32)

    def finalise(self):
        tq, par, acc_scr = SEQ_TILE, self.par, self.acc_scr
        lam = (jnp.exp(jnp.sum(self.lq1_ref[...] * self.lk1_ref[...], axis=-1, keepdims=True))
               - jnp.exp(jnp.sum(self.lq2_ref[...] * self.lk2_ref[...], axis=-1, keepdims=True))
               + LAMBDA_INIT)
        for g in range(B_HEADS):
            inv_l1 = 1.0 / acc_scr[par, 2 * g, B_V_DIM:B_V_DIM + SUBLANES, :]
            inv_l2 = 1.0 / acc_scr[par, 2 * g + 1, B_V_DIM:B_V_DIM + SUBLANES, :]
            acc1 = acc_scr[par, 2 * g, :B_V_DIM, :].reshape(B_V_DIM // SUBLANES, SUBLANES, tq)
            acc2 = acc_scr[par, 2 * g + 1, :B_V_DIM, :].reshape(
                B_V_DIM // SUBLANES, SUBLANES, tq)
            ot = acc1 * inv_l1[None] - lam * (acc2 * inv_l2[None])
            o = ot.reshape(B_V_DIM, tq).T
            o = (o * lax.rsqrt(jnp.mean(o * o, axis=-1, keepdims=True) + SUBLN_EPS)
                 * self.subln_ref[...])
            o = o * (1.0 - LAMBDA_INIT)
            vsl = slice(g * B_V_DIM, (g + 1) * B_V_DIM)
            b_out = o * self.zb_gate[:, vsl]
            self.mixed_scr[par, :, A_WIDTH + g * B_V_DIM:A_WIDTH + (g + 1) * B_V_DIM] = (
                b_out.astype(BF16))

    def out_b(self):
        self.out_ref[0, self.tile_rows, :] += self.gate * jnp.dot(
            self.mixed_scr[self.par, :, A_WIDTH:], self.wout_ref[A_WIDTH:, :],
            preferred_element_type=F32)


def kernel(x, c, positions, norm_w, w_ada, b_ada, w_in, sgu_norm_w, w_s, b_s, q_norm_w, k_norm_w,
           lambda_q1, lambda_k1, lambda_q2, lambda_k2, subln_w, w_out):
    bsz, seq, d = x.shape
    assert d == D_MODEL and seq % (TILES_PER_STEP * SEQ_TILE) == 0 and norm_w.shape[0] == 1
    nt = seq // SEQ_TILE

    mod = pl.pallas_call(
        _adaln_kernel,
        grid=(3,),
        in_specs=[pl.BlockSpec((bsz, d), lambda n: (0, 0)),
                  pl.BlockSpec((None, d, d), lambda n: (0, 0, n)),
                  pl.BlockSpec((1, d), lambda n: (0, n))],
        out_specs=pl.BlockSpec((None, bsz, d), lambda n: (n, 0, 0)),
        out_shape=jax.ShapeDtypeStruct((3, bsz, d), F32),
        name="adaln_mod",
    )(c, w_ada, b_ada)

    n_grp = LANES // HALF
    pos_q = jnp.pad(positions.reshape(bsz, nt, n_grp, SEQ_TILE // n_grp),
                    ((0, 0), (0, 0), (0, SUBLANES - n_grp), (0, LANES - SEQ_TILE // n_grp)))

    inv_freq = ROPE_THETA ** (-jnp.arange(0, B_HEAD_DIM, 2, dtype=F32) / B_HEAD_DIM)
    invf = jnp.tile(inv_freq, LANES // HALF)[None, :]
    lane_head = (np.arange(MXU_DIM) // LANES) * 2 + (np.arange(MXU_DIM) // HALF) % 2
    bd = jnp.asarray(lane_head[:, None] == lane_head[None, :], dtype=BF16)

    prep_rows = d // 8
    w_in_b, w_out_b = pl.pallas_call(
        _prep_weights_kernel,
        grid=(d // prep_rows,),
        in_specs=[pl.BlockSpec((None, prep_rows, IN_COLS), lambda r: (0, r, 0)),
                  pl.BlockSpec((None, prep_rows, d), lambda r: (0, r, 0))],
        out_specs=[pl.BlockSpec((prep_rows, IN_COLS), lambda r: (r, 0)),
                   pl.BlockSpec((prep_rows, d), lambda r: (r, 0))],
        out_shape=[jax.ShapeDtypeStruct((d, IN_COLS), BF16),
                   jax.ShapeDtypeStruct((A_WIDTH + B_WIDTH, d), BF16)],
        name="prep_weights",
    )(w_in, w_out)
    qnw = _pair_layout(jnp.tile(q_norm_w.reshape(B_HEAD_DIM), 2 * B_HEADS))[None, :]
    knw = _pair_layout(jnp.tile(k_norm_w.reshape(B_HEAD_DIM), 2 * B_HEADS))[None, :]

    const2 = lambda b, i: (0, 0)
    const3 = lambda b, i: (0, 0, 0)
    step_rows = TILES_PER_STEP * SEQ_TILE
    out = pl.pallas_call(
        _layer_kernel,
        grid=(bsz, nt // TILES_PER_STEP),
        in_specs=[
            pl.BlockSpec((1, step_rows, d), lambda b, i: (b, i, 0)),
            pl.BlockSpec((3, bsz, d), const3),
            pl.BlockSpec((None, TILES_PER_STEP, SUBLANES, LANES),
                         lambda b, i: (b, i, 0, 0)),
            pl.BlockSpec((1, LANES), const2),
            pl.BlockSpec((1, d), const2),
            pl.BlockSpec((d, IN_COLS), const2),
            pl.BlockSpec((A_HEADS, A_HEAD_DIM), const2),
            pl.BlockSpec((A_HEADS, CHUNK, CHUNK), const3),
            pl.BlockSpec((CHUNK, A_HEADS), const2),
            pl.BlockSpec((1, B_WIDTH), const2),
            pl.BlockSpec((1, B_WIDTH), const2),
            pl.BlockSpec((1, B_HEAD_DIM), const2),
            pl.BlockSpec((1, B_HEAD_DIM), const2),
            pl.BlockSpec((1, B_HEAD_DIM), const2),
            pl.BlockSpec((1, B_HEAD_DIM), const2),
            pl.BlockSpec((1, B_V_DIM), const2),
            pl.BlockSpec((A_WIDTH + B_WIDTH, d), const2),
            pl.BlockSpec((MXU_DIM, MXU_DIM), const2),
        ],
        out_specs=pl.BlockSpec((1, step_rows, d), lambda b, i: (b, i, 0)),
        out_shape=jax.ShapeDtypeStruct((bsz, seq, d), F32),
        scratch_shapes=[
            pltpu.VMEM((nt, SEQ_TILE, B_WIDTH), BF16),
            pltpu.VMEM((nt, B_HEADS, V_EXT, SEQ_TILE), BF16),
            pltpu.VMEM((B_HEADS, LANES, 2 * SEQ_TILE), BF16),
            pltpu.VMEM((2, B_HEADS, SUBLANES, 2 * SEQ_TILE), F32),
            pltpu.VMEM((2, 2 * B_HEADS, V_EXT, SEQ_TILE), F32),
            pltpu.VMEM((2, SEQ_TILE, A_WIDTH + B_WIDTH), BF16),
            pltpu.VMEM((B_HEADS, SEQ_TILE, 2 * SEQ_TILE), F32),
            pltpu.VMEM((2, SEQ_TILE, 2 * SEQ_TILE), BF16),
            pltpu.VMEM((2, SUBLANES, 2 * SEQ_TILE), F32),
            pltpu.VMEM((B_HEADS, SUBLANES, 2 * SEQ_TILE), F32),
        ],
        compiler_params=pltpu.CompilerParams(
            dimension_semantics=("arbitrary", "arbitrary"),
            vmem_limit_bytes=VMEM_LIMIT_BYTES),
        name="fused_layer",
    )(x, mod, pos_q, invf, norm_w, w_in_b,
      sgu_norm_w.reshape(A_HEADS, A_HEAD_DIM), w_s.reshape(A_HEADS, CHUNK, CHUNK),
      jnp.transpose(b_s.reshape(A_HEADS, CHUNK)), qnw, knw,
      lambda_q1, lambda_k1, lambda_q2, lambda_k2, subln_w, w_out_b, bd)
    return out
```

```python
import math

import numpy as np
import jax
import jax.numpy as jnp
from jax import lax
from jax.experimental import pallas as pl
from jax.experimental.pallas import tpu as pltpu

D_MODEL = 1024
A_WIDTH = 512
B_WIDTH = 512
A_HEADS = 4
A_HEAD_DIM = 128
CHUNK = 128
B_HEADS = 4
B_HEAD_DIM = 64
B_V_DIM = 128
ROPE_THETA = 10000.0
NORM_EPS = 1e-6
SUBLN_EPS = 1e-5
IN_COLS = 3 * A_WIDTH + 4 * B_WIDTH
LAMBDA_INIT = 0.8 - 0.6 * math.exp(-0.3 * 0)
LOG2E = math.log2(math.e)

LANES = 128
SUBLANES = 8
MXU_DIM = 256
SEQ_TILE = 256
TILES_PER_STEP = 4
VMEM_LIMIT_BYTES = 56 * 1024 * 1024
HALF = B_HEAD_DIM // 2
BF16_ROWS = 16
V_EXT = B_V_DIM + BF16_ROWS
WEIGHT_CHUNK_ROWS = 64

F32 = jnp.float32
BF16 = jnp.bfloat16


def _pair_layout(t):
    lead = t.shape[:-1]
    t = t.reshape(lead + (B_HEADS, 2, 2, HALF))
    t = jnp.swapaxes(t, -3, -2)
    return t.reshape(lead + (B_WIDTH,))


def _load_weights(w_in_hbm, w_out_hbm, win_scr, wout_scr, stage, sem):
    ch = stage.shape[1]
    grp = lax.broadcasted_iota(jnp.int32, (1, LANES), 1) // HALF
    q0, v0 = 3 * A_WIDTH, 3 * A_WIDTH + 2 * B_WIDTH

    def stream(src_hbm, dst_scr, n_cols, reorder):
        n_chunks = src_hbm.shape[1] // ch

        def chunk_copy(c, slot):
            return pltpu.make_async_copy(src_hbm.at[0, pl.ds(c * ch, ch), :],
                                         stage.at[slot, :, pl.ds(0, n_cols)], sem.at[slot])

        chunk_copy(0, 0).start()

        def body(c, carry):
            slot = c % 2

            @pl.when(c + 1 < n_chunks)
            def _():
                chunk_copy(c + 1, 1 - slot).start()

            chunk_copy(c, slot).wait()
            rows = pl.ds(pl.multiple_of(c * ch, ch), ch)
            for cb in range(n_cols // LANES):
                t = stage[slot, :, cb * LANES:(cb + 1) * LANES]
                if reorder and q0 <= cb * LANES < v0:
                    t = jnp.where(grp == 1, pltpu.roll(t, LANES - HALF, 1),
                                  jnp.where(grp == 2, pltpu.roll(t, HALF, 1), t))
                dst_scr[rows, cb * LANES:(cb + 1) * LANES] = t.astype(BF16)
            return carry

        lax.fori_loop(0, n_chunks, body, 0)

    stream(w_in_hbm, win_scr, IN_COLS, True)
    stream(w_out_hbm, wout_scr, D_MODEL, False)


def _adaln_kernel(c_ref, w_ref, b_ref, mod_ref):
    c = c_ref[...]
    c_act = c * jax.nn.sigmoid(c)
    mod_ref[...] = jnp.dot(c_act.astype(BF16), w_ref[...].astype(BF16),
                           preferred_element_type=F32) + b_ref[...]


def _silu(z):
    return z * jax.nn.sigmoid(z)


def _head_mean_sq(t, bd_ref):
    sq = (t * t).astype(BF16)
    halves = [jnp.dot(sq[:, h * MXU_DIM:(h + 1) * MXU_DIM], bd_ref[...],
                      preferred_element_type=F32) for h in range(B_WIDTH // MXU_DIM)]
    return jnp.concatenate(halves, axis=-1) * (1.0 / B_HEAD_DIM)


def _rope(t, cos, sin_signed):
    cols = []
    for cb in range(B_WIDTH // LANES):
        tc = t[:, cb * LANES:(cb + 1) * LANES]
        cols.append(tc * cos + pltpu.roll(tc, LANES // 2, 1) * sin_signed)
    return jnp.concatenate(cols, axis=-1)


def _rope_tables(pos_quarters, invf, lane):
    n_grp = LANES // HALF
    qrows = SEQ_TILE // n_grp
    grp = lane // HALF
    pos_t = pos_quarters.T
    packed = pos_t[:qrows, n_grp - 1:n_grp]
    for gi in range(n_grp - 2, -1, -1):
        packed = jnp.where(grp == gi, pos_t[:qrows, gi:gi + 1], packed)
    ang = packed * invf
    tables = []
    for packed_tab in (jnp.cos(ang), jnp.sin(ang)):
        parts = []
        for gi in range(n_grp):
            t = jnp.where(grp == gi, packed_tab, 0.0)
            t = t + pltpu.roll(t, 2 * HALF, 1)
            parts.append(t + pltpu.roll(t, HALF, 1))
        tables.append(jnp.concatenate(parts, axis=0))
    return tables


def _across_sublanes(x, op):
    for shift in (4, 2, 1):
        x = op(x, pltpu.roll(x, shift, 0))
    return x


def _layer_kernel(x_ref, mod_ref, pos_ref, invf_ref, normw_ref, w_in_hbm, sgu_ref, ws_ref,
                  bst_ref, qnw_ref, knw_ref, lq1_ref, lk1_ref, lq2_ref, lk2_ref, subln_ref,
                  w_out_hbm, bd_ref, out_ref,
                  k_scr, vt_scr, qblk_scr, m_scr, acc_scr, mixed_scr, s_scr, p_scr,
                  alpha_scr, smax_scr, win_ref, wout_ref, stage_scr, stage_sem):
    @pl.when((pl.program_id(0) == 0) & (pl.program_id(1) == 0))
    def _():
        _load_weights(w_in_hbm, w_out_hbm, win_ref, wout_ref, stage_scr, stage_sem)

    refs = dict(locals())
    tiles = [_TilePhases(t, refs) for t in range(TILES_PER_STEP)]
    tiles[0].activations()
    tiles[0].qk_projections()
    tiles[0].before_loop()
    for t, tile in enumerate(tiles):
        nxt = tiles[t + 1] if t + 1 < TILES_PER_STEP else None
        tile.key_loop()
        tile.last_value_product_and_out_a()
        if nxt is not None:
            nxt.activations()
            nxt.qk_projections()
        tile.finalise()
        tile.out_b()
        if nxt is not None:
            nxt.before_loop()


class _TilePhases:
    def __init__(self, t, refs):
        self.__dict__.update(refs)
        self.t = t
        self.par = t % 2
        self.i = pl.program_id(1) * TILES_PER_STEP + t
        self.tile_rows = slice(t * SEQ_TILE, (t + 1) * SEQ_TILE)

    def activations(self):
        x_ref, tile_rows = self.x_ref, self.tile_rows
        b = pl.program_id(0)
        shift = self.mod_ref[0, pl.ds(b, 1), :]
        scale = self.mod_ref[1, pl.ds(b, 1), :]
        self.gate = self.mod_ref[2, pl.ds(b, 1), :]
        sq_lanes = None
        for c0 in range(0, D_MODEL, LANES):
            x_c = x_ref[0, tile_rows, c0:c0 + LANES]
            sq_lanes = x_c * x_c if sq_lanes is None else sq_lanes + x_c * x_c
        sum_sq = jnp.sum(sq_lanes, axis=-1, keepdims=True)
        inv_rms = lax.rsqrt(sum_sq * (1.0 / D_MODEL) + NORM_EPS)
        hb_cols = []
        for c0 in range(0, D_MODEL, MXU_DIM):
            cs = slice(c0, c0 + MXU_DIM)
            h_c = ((x_ref[0, tile_rows, cs] * inv_rms * self.normw_ref[:, cs])
                   * (1.0 + scale[:, cs]) + shift[:, cs])
            hb_cols.append(h_c.astype(BF16))
        self.hb = jnp.concatenate(hb_cols, axis=1)

    def proj(self, c0, width):
        return jnp.dot(self.hb, self.win_ref[:, c0:c0 + width], preferred_element_type=F32)

    def qk_projections(self):
        self.qb = self.proj(3 * A_WIDTH, B_WIDTH)
        self.kb = self.proj(3 * A_WIDTH + B_WIDTH, B_WIDTH)

    def scores(self, g, j):
        tk, tq = SEQ_TILE, SEQ_TILE
        kp = self.k_scr[j, :, g * LANES:(g + 1) * LANES]
        s = jnp.dot(kp, self.qblk_scr[g], preferred_element_type=F32)
        self.s_scr[g] = s
        self.smax_scr[g] = jnp.max(s.reshape(tk // SUBLANES, SUBLANES, 2 * tq), axis=0)

    def softmax(self, g, slot, masked):
        tk, tq = SEQ_TILE, SEQ_TILE
        rows = 2 * SUBLANES
        s_scr, m_scr, par = self.s_scr, self.m_scr, self.par

        def chunk(c):
            sc = s_scr[g, c * rows:(c + 1) * rows, :]
            if masked:
                kv_row = lax.broadcasted_iota(jnp.int32, (rows, 2 * tq), 0) + c * rows
                q_col = lax.broadcasted_iota(jnp.int32, (rows, 2 * tq), 1) % tq
                sc = jnp.where(kv_row <= q_col, sc, -jnp.inf)
            return sc.reshape(2, SUBLANES, 2 * tq)

        if masked:
            col_max = jnp.max(chunk(0), axis=0)
            for c in range(1, tk // rows):
                col_max = jnp.maximum(col_max, jnp.max(chunk(c), axis=0))
        else:
            col_max = self.smax_scr[g]
        m_old = m_scr[par, g]
        m_new = jnp.maximum(m_old, _across_sublanes(col_max, jnp.maximum))
        alpha = jnp.exp2(m_old - m_new)
        m_scr[par, g] = m_new
        self.alpha_scr[slot] = alpha
        for c in range(tk // rows):
            pc = jnp.exp2(chunk(c) - m_new[None])
            self.p_scr[slot, c * rows:(c + 1) * rows, :] = (
                pc.reshape(rows, 2 * tq).astype(BF16))

    def value_update(self, g, j, slot):
        tq = SEQ_TILE
        acc_scr, par = self.acc_scr, self.par
        vt = self.vt_scr[j, g]
        alpha = self.alpha_scr[slot]
        for st in range(2):
            pv = jnp.dot(vt, self.p_scr[slot, :, st * tq:(st + 1) * tq],
                         preferred_element_type=F32)
            a = alpha[:, st * tq:(st + 1) * tq]
            acc = acc_scr[par, 2 * g + st].reshape(V_EXT // SUBLANES, SUBLANES, tq)
            acc_scr[par, 2 * g + st] = (acc * a[None]).reshape(V_EXT, tq) + pv

    def unit(self, g, masked, tile, succ_tile, prev_tile):
        ahead = (g + 2) % B_HEADS
        if g + 2 < B_HEADS:
            self.scores(ahead, tile)
        elif succ_tile is not None:
            self.scores(ahead, succ_tile)
        if prev_tile is not None:
            self.value_update((g - 1) % B_HEADS, prev_tile, (g - 1) % 2)
        self.softmax(g, g % 2, masked)

    def before_loop(self):
        tq, tk, i, par = SEQ_TILE, SEQ_TILE, self.i, self.par
        qb, kb = self.qb, self.kb
        lane = lax.broadcasted_iota(jnp.int32, (1, LANES), 1)
        cos, sin = _rope_tables(self.pos_ref[self.t].astype(F32), self.invf_ref[...], lane)
        sin_signed = jnp.where(lane < LANES // 2, -sin, sin)
        q = qb * lax.rsqrt(_head_mean_sq(qb, self.bd_ref) + NORM_EPS) * self.qnw_ref[...]
        k = kb * lax.rsqrt(_head_mean_sq(kb, self.bd_ref) + NORM_EPS) * self.knw_ref[...]
        q = _rope(q, cos, sin_signed) * (B_HEAD_DIM ** -0.5 * LOG2E)
        k = _rope(k, cos, sin_signed)
        self.k_scr[i] = k.astype(BF16)

        qt = q.T
        pair_row = lax.broadcasted_iota(jnp.int32, (LANES, tq), 0)
        is_a = (pair_row // HALF) % 2 == 0
        for g in range(B_HEADS):
            qp = qt[g * LANES:(g + 1) * LANES, :]
            self.qblk_scr[g] = jnp.concatenate(
                [jnp.where(is_a, qp, 0.0), jnp.where(is_a, 0.0, qp)], axis=1).astype(BF16)

        self.m_scr[par] = jnp.full(self.m_scr.shape[1:], -jnp.inf, F32)
        self.acc_scr[par] = jnp.zeros(self.acc_scr.shape[1:], F32)

        vb = self.proj(3 * A_WIDTH + 2 * B_WIDTH, B_WIDTH)
        vbt = vb.T.astype(BF16)
        for g in range(B_HEADS):
            self.vt_scr[i, g, :B_V_DIM, :] = vbt[g * B_V_DIM:(g + 1) * B_V_DIM, :]
            self.vt_scr[i, g, B_V_DIM:, :] = jnp.ones((V_EXT - B_V_DIM, tk), BF16)
        va = self.proj(A_WIDTH, A_WIDTH)
        self.scores(0, i)
        self.scores(1, i)
        ua = self.proj(0, A_WIDTH)
        self.unit(0, True, i, 0, None)
        za = self.proj(2 * A_WIDTH, A_WIDTH)
        self.unit(1, True, i, 0, i)
        zb = self.proj(3 * A_WIDTH + 3 * B_WIDTH, B_WIDTH)
        self.unit(2, True, i, 0, i)

        row_c = lax.broadcasted_iota(jnp.int32, (CHUNK, CHUNK), 0)
        col_c = lax.broadcasted_iota(jnp.int32, (CHUNK, CHUNK), 1)
        tril = row_c >= col_c
        for hh in range(A_HEADS):
            sl = slice(hh * A_HEAD_DIM, (hh + 1) * A_HEAD_DIM)
            v = va[:, sl]
            vn = (v * lax.rsqrt(jnp.mean(v * v, axis=-1, keepdims=True) + NORM_EPS)
                  * self.sgu_ref[hh:hh + 1, :])
            vnb = vn.astype(BF16)
            ws = jnp.where(tril, self.ws_ref[hh], 0.0).astype(BF16)
            bias = self.bst_ref[:, hh:hh + 1]
            n_chunks = tq // CHUNK
            chunks = jnp.concatenate(
                [vnb[c * CHUNK:(c + 1) * CHUNK, :] for c in range(n_chunks)], axis=1)
            mixed_chunks = jnp.dot(ws, chunks, preferred_element_type=F32)
            mix = jnp.concatenate([mixed_chunks[:, c * A_HEAD_DIM:(c + 1) * A_HEAD_DIM] + bias
                                   for c in range(n_chunks)], axis=0)
            a_out = ua[:, sl] * mix * _silu(za[:, sl])
            self.mixed_scr[par, :, sl] = a_out.astype(BF16)
        self.zb_gate = _silu(zb)
        self.unit(3, True, i, 0, i)

    def key_loop(self):
        i = self.i

        def tile_units(j, is_final=False):
            succ = None if is_final else j + 1
            self.unit(0, False, j, succ, jnp.where(j == 0, i, j - 1))
            for g in range(1, B_HEADS):
                self.unit(g, False, j, succ, j)

        def pair_body(jj, carry):
            tile_units(2 * jj)
            tile_units(2 * jj + 1)
            return carry

        lax.fori_loop(0, i // 2, pair_body, 0)

        @pl.when(i % 2 == 1)
        def _():
            tile_units(i - 1, is_final=True)

    def last_value_product_and_out_a(self):
        last = B_HEADS - 1
        self.value_update(last, jnp.maximum(self.i - 1, 0), last % 2)
        rows = self.tile_rows
        self.out_ref[0, rows, :] = self.x_ref[0, rows, :] + self.gate * jnp.dot(
            self.mixed_scr[self.par, :, :A_WIDTH], self.wout_ref[:A_WIDTH, :],
            preferred_element_type=F32)

    def finalise(self):
        tq, par, acc_scr = SEQ_TILE, self.par, self.acc_scr
        lam = (jnp.exp(jnp.sum(self.lq1_ref[...] * self.lk1_ref[...], axis=-1, keepdims=True))
               - jnp.exp(jnp.sum(self.lq2_ref[...] * self.lk2_ref[...], axis=-1, keepdims=True))
               + LAMBDA_INIT)
        for g in range(B_HEADS):
            inv_l1 = 1.0 / acc_scr[par, 2 * g, B_V_DIM:B_V_DIM + SUBLANES, :]
            inv_l2 = 1.0 / acc_scr[par, 2 * g + 1, B_V_DIM:B_V_DIM + SUBLANES, :]
            acc1 = acc_scr[par, 2 * g, :B_V_DIM, :].reshape(B_V_DIM // SUBLANES, SUBLANES, tq)
            acc2 = acc_scr[par, 2 * g + 1, :B_V_DIM, :].reshape(
                B_V_DIM // SUBLANES, SUBLANES, tq)
            ot = acc1 * inv_l1[None] - lam * (acc2 * inv_l2[None])
            o = ot.reshape(B_V_DIM, tq).T
            o = (o * lax.rsqrt(jnp.mean(o * o, axis=-1, keepdims=True) + SUBLN_EPS)
                 * self.subln_ref[...])
            o = o * (1.0 - LAMBDA_INIT)
            vsl = slice(g * B_V_DIM, (g + 1) * B_V_DIM)
            b_out = o * self.zb_gate[:, vsl]
            self.mixed_scr[par, :, A_WIDTH + g * B_V_DIM:A_WIDTH + (g + 1) * B_V_DIM] = (
                b_out.astype(BF16))

    def out_b(self):
        self.out_ref[0, self.tile_rows, :] += self.gate * jnp.dot(
            self.mixed_scr[self.par, :, A_WIDTH:], self.wout_ref[A_WIDTH:, :],
            preferred_element_type=F32)


def kernel(x, c, positions, norm_w, w_ada, b_ada, w_in, sgu_norm_w, w_s, b_s, q_norm_w, k_norm_w,
           lambda_q1, lambda_k1, lambda_q2, lambda_k2, subln_w, w_out):
    bsz, seq, d = x.shape
    assert d == D_MODEL and seq % (TILES_PER_STEP * SEQ_TILE) == 0 and norm_w.shape[0] == 1
    nt = seq // SEQ_TILE

    mod = pl.pallas_call(
        _adaln_kernel,
        grid=(3,),
        in_specs=[pl.BlockSpec((bsz, d), lambda n: (0, 0)),
                  pl.BlockSpec((None, d, d), lambda n: (0, 0, n)),
                  pl.BlockSpec((1, d), lambda n: (0, n))],
        out_specs=pl.BlockSpec((None, bsz, d), lambda n: (n, 0, 0)),
        out_shape=jax.ShapeDtypeStruct((3, bsz, d), F32),
        name="adaln_mod",
    )(c, w_ada, b_ada)

    n_grp = LANES // HALF
    pos_q = jnp.pad(positions.reshape(bsz, nt, n_grp, SEQ_TILE // n_grp),
                    ((0, 0), (0, 0), (0, SUBLANES - n_grp), (0, LANES - SEQ_TILE // n_grp)))

    inv_freq = ROPE_THETA ** (-jnp.arange(0, B_HEAD_DIM, 2, dtype=F32) / B_HEAD_DIM)
    invf = jnp.tile(inv_freq, LANES // HALF)[None, :]
    lane_head = (np.arange(MXU_DIM) // LANES) * 2 + (np.arange(MXU_DIM) // HALF) % 2
    bd = jnp.asarray(lane_head[:, None] == lane_head[None, :], dtype=BF16)

    qnw = _pair_layout(jnp.tile(q_norm_w.reshape(B_HEAD_DIM), 2 * B_HEADS))[None, :]
    knw = _pair_layout(jnp.tile(k_norm_w.reshape(B_HEAD_DIM), 2 * B_HEADS))[None, :]

    const2 = lambda b, i: (0, 0)
    const3 = lambda b, i: (0, 0, 0)
    step_rows = TILES_PER_STEP * SEQ_TILE
    out = pl.pallas_call(
        _layer_kernel,
        grid=(bsz, nt // TILES_PER_STEP),
        in_specs=[
            pl.BlockSpec((1, step_rows, d), lambda b, i: (b, i, 0)),
            pl.BlockSpec((3, bsz, d), const3),
            pl.BlockSpec((None, TILES_PER_STEP, SUBLANES, LANES),
                         lambda b, i: (b, i, 0, 0)),
            pl.BlockSpec((1, LANES), const2),
            pl.BlockSpec((1, d), const2),
            pl.BlockSpec(memory_space=pl.ANY),
            pl.BlockSpec((A_HEADS, A_HEAD_DIM), const2),
            pl.BlockSpec((A_HEADS, CHUNK, CHUNK), const3),
            pl.BlockSpec((CHUNK, A_HEADS), const2),
            pl.BlockSpec((1, B_WIDTH), const2),
            pl.BlockSpec((1, B_WIDTH), const2),
            pl.BlockSpec((1, B_HEAD_DIM), const2),
            pl.BlockSpec((1, B_HEAD_DIM), const2),
            pl.BlockSpec((1, B_HEAD_DIM), const2),
            pl.BlockSpec((1, B_HEAD_DIM), const2),
            pl.BlockSpec((1, B_V_DIM), const2),
            pl.BlockSpec(memory_space=pl.ANY),
            pl.BlockSpec((MXU_DIM, MXU_DIM), const2),
        ],
        out_specs=pl.BlockSpec((1, step_rows, d), lambda b, i: (b, i, 0)),
        out_shape=jax.ShapeDtypeStruct((bsz, seq, d), F32),
        scratch_shapes=[
            pltpu.VMEM((nt, SEQ_TILE, B_WIDTH), BF16),
            pltpu.VMEM((nt, B_HEADS, V_EXT, SEQ_TILE), BF16),
            pltpu.VMEM((B_HEADS, LANES, 2 * SEQ_TILE), BF16),
            pltpu.VMEM((2, B_HEADS, SUBLANES, 2 * SEQ_TILE), F32),
            pltpu.VMEM((2, 2 * B_HEADS, V_EXT, SEQ_TILE), F32),
            pltpu.VMEM((2, SEQ_TILE, A_WIDTH + B_WIDTH), BF16),
            pltpu.VMEM((B_HEADS, SEQ_TILE, 2 * SEQ_TILE), F32),
            pltpu.VMEM((2, SEQ_TILE, 2 * SEQ_TILE), BF16),
            pltpu.VMEM((2, SUBLANES, 2 * SEQ_TILE), F32),
            pltpu.VMEM((B_HEADS, SUBLANES, 2 * SEQ_TILE), F32),
            pltpu.VMEM((d, IN_COLS), BF16),
            pltpu.VMEM((A_WIDTH + B_WIDTH, d), BF16),
            pltpu.VMEM((2, WEIGHT_CHUNK_ROWS, IN_COLS), F32),
            pltpu.SemaphoreType.DMA((2,)),
        ],
        compiler_params=pltpu.CompilerParams(
            dimension_semantics=("arbitrary", "arbitrary"),
            vmem_limit_bytes=VMEM_LIMIT_BYTES),
        name="fused_layer",
    )(x, mod, pos_q, invf, norm_w, w_in,
      sgu_norm_w.reshape(A_HEADS, A_HEAD_DIM), w_s.reshape(A_HEADS, CHUNK, CHUNK),
      jnp.transpose(b_s.reshape(A_HEADS, CHUNK)), qnw, knw,
      lambda_q1, lambda_k1, lambda_q2, lambda_k2, subln_w, w_out, bd)
    return out
```

```python
import math

import numpy as np
import jax
import jax.numpy as jnp
from jax import lax
from jax.experimental import pallas as pl
from jax.experimental.pallas import tpu as pltpu

D_MODEL = 1024
A_WIDTH = 512
B_WIDTH = 512
A_HEADS = 4
A_HEAD_DIM = 128
CHUNK = 128
B_HEADS = 4
B_HEAD_DIM = 64
B_V_DIM = 128
ROPE_THETA = 10000.0
NORM_EPS = 1e-6
SUBLN_EPS = 1e-5
IN_COLS = 3 * A_WIDTH + 4 * B_WIDTH
LAMBDA_INIT = 0.8 - 0.6 * math.exp(-0.3 * 0)
LOG2E = math.log2(math.e)

LANES = 128
SUBLANES = 8
MXU_DIM = 256
SEQ_TILE = 256
TILES_PER_STEP = 4
VMEM_LIMIT_BYTES = 56 * 1024 * 1024
HALF = B_HEAD_DIM // 2
BF16_ROWS = 16
V_EXT = B_V_DIM + BF16_ROWS
WEIGHT_CHUNK_ROWS = 64
WEIGHT_STAGE_SLOTS = 4

F32 = jnp.float32
BF16 = jnp.bfloat16


def _pair_layout(t):
    lead = t.shape[:-1]
    t = t.reshape(lead + (B_HEADS, 2, 2, HALF))
    t = jnp.swapaxes(t, -3, -2)
    return t.reshape(lead + (B_WIDTH,))


def _load_weights(w_in_hbm, w_out_hbm, win_scr, wout_scr, stage, sem):
    ch = stage.shape[1]
    grp = lax.broadcasted_iota(jnp.int32, (1, LANES), 1) // HALF
    q0, v0 = 3 * A_WIDTH, 3 * A_WIDTH + 2 * B_WIDTH

    def stream(src_hbm, dst_scr, n_cols, reorder):
        n_chunks = src_hbm.shape[1] // ch

        def chunk_copy(c, slot):
            return pltpu.make_async_copy(src_hbm.at[0, pl.ds(c * ch, ch), :],
                                         stage.at[slot, :, pl.ds(0, n_cols)], sem.at[slot])

        depth = stage.shape[0]
        for c0 in range(depth - 1):
            chunk_copy(c0, c0).start()

        def body(c, carry):
            slot = c % depth
            ahead = c + depth - 1

            @pl.when(ahead < n_chunks)
            def _():
                chunk_copy(ahead, ahead % depth).start()

            chunk_copy(c, slot).wait()
            rows = pl.ds(pl.multiple_of(c * ch, ch), ch)
            for cb in range(n_cols // LANES):
                t = stage[slot, :, cb * LANES:(cb + 1) * LANES]
                if reorder and q0 <= cb * LANES < v0:
                    t = jnp.where(grp == 1, pltpu.roll(t, LANES - HALF, 1),
                                  jnp.where(grp == 2, pltpu.roll(t, HALF, 1), t))
                dst_scr[rows, cb * LANES:(cb + 1) * LANES] = t.astype(BF16)
            return carry

        lax.fori_loop(0, n_chunks, body, 0)

    stream(w_in_hbm, win_scr, IN_COLS, True)
    stream(w_out_hbm, wout_scr, D_MODEL, False)


def _adaln_kernel(c_ref, w_ref, b_ref, mod_ref):
    c = c_ref[...]
    c_act = c * jax.nn.sigmoid(c)
    mod_ref[...] = jnp.dot(c_act.astype(BF16), w_ref[...].astype(BF16),
                           preferred_element_type=F32) + b_ref[...]


def _silu(z):
    return z * jax.nn.sigmoid(z)


def _head_mean_sq(t, bd_ref):
    sq = (t * t).astype(BF16)
    halves = [jnp.dot(sq[:, h * MXU_DIM:(h + 1) * MXU_DIM], bd_ref[...],
                      preferred_element_type=F32) for h in range(B_WIDTH // MXU_DIM)]
    return jnp.concatenate(halves, axis=-1) * (1.0 / B_HEAD_DIM)


def _rope(t, cos, sin_signed):
    cols = []
    for cb in range(B_WIDTH // LANES):
        tc = t[:, cb * LANES:(cb + 1) * LANES]
        cols.append(tc * cos + pltpu.roll(tc, LANES // 2, 1) * sin_signed)
    return jnp.concatenate(cols, axis=-1)


def _rope_tables(pos_quarters, invf, lane):
    n_grp = LANES // HALF
    qrows = SEQ_TILE // n_grp
    grp = lane // HALF
    pos_t = pos_quarters.T
    packed = pos_t[:qrows, n_grp - 1:n_grp]
    for gi in range(n_grp - 2, -1, -1):
        packed = jnp.where(grp == gi, pos_t[:qrows, gi:gi + 1], packed)
    ang = packed * invf
    tables = []
    for packed_tab in (jnp.cos(ang), jnp.sin(ang)):
        parts = []
        for gi in range(n_grp):
            t = jnp.where(grp == gi, packed_tab, 0.0)
            t = t + pltpu.roll(t, 2 * HALF, 1)
            parts.append(t + pltpu.roll(t, HALF, 1))
        tables.append(jnp.concatenate(parts, axis=0))
    return tables


def _across_sublanes(x, op):
    for shift in (4, 2, 1):
        x = op(x, pltpu.roll(x, shift, 0))
    return x


def _layer_kernel(x_ref, mod_ref, pos_ref, invf_ref, normw_ref, w_in_hbm, sgu_ref, ws_ref,
                  bst_ref, qnw_ref, knw_ref, lq1_ref, lk1_ref, lq2_ref, lk2_ref, subln_ref,
                  w_out_hbm, bd_ref, out_ref,
                  k_scr, vt_scr, qblk_scr, m_scr, acc_scr, mixed_scr, s_scr, p_scr,
                  alpha_scr, smax_scr, win_ref, wout_ref, stage_scr, stage_sem):
    @pl.when((pl.program_id(0) == 0) & (pl.program_id(1) == 0))
    def _():
        _load_weights(w_in_hbm, w_out_hbm, win_ref, wout_ref, stage_scr, stage_sem)

    refs = dict(locals())
    tiles = [_TilePhases(t, refs) for t in range(TILES_PER_STEP)]
    tiles[0].activations()
    tiles[0].qk_projections()
    tiles[0].before_loop()
    for t, tile in enumerate(tiles):
        nxt = tiles[t + 1] if t + 1 < TILES_PER_STEP else None
        tile.key_loop()
        tile.last_value_product_and_out_a()
        if nxt is not None:
            nxt.activations()
            nxt.qk_projections()
        tile.finalise()
        tile.out_b()
        if nxt is not None:
            nxt.before_loop()


class _TilePhases:
    def __init__(self, t, refs):
        self.__dict__.update(refs)
        self.t = t
        self.par = t % 2
        self.i = pl.program_id(1) * TILES_PER_STEP + t
        self.tile_rows = slice(t * SEQ_TILE, (t + 1) * SEQ_TILE)

    def activations(self):
        x_ref, tile_rows = self.x_ref, self.tile_rows
        b = pl.program_id(0)
        shift = self.mod_ref[0, pl.ds(b, 1), :]
        scale = self.mod_ref[1, pl.ds(b, 1), :]
        self.gate = self.mod_ref[2, pl.ds(b, 1), :]
        sq_lanes = None
        for c0 in range(0, D_MODEL, LANES):
            x_c = x_ref[0, tile_rows, c0:c0 + LANES]
            sq_lanes = x_c * x_c if sq_lanes is None else sq_lanes + x_c * x_c
        sum_sq = jnp.sum(sq_lanes, axis=-1, keepdims=True)
        inv_rms = lax.rsqrt(sum_sq * (1.0 / D_MODEL) + NORM_EPS)
        hb_cols = []
        for c0 in range(0, D_MODEL, MXU_DIM):
            cs = slice(c0, c0 + MXU_DIM)
            h_c = ((x_ref[0, tile_rows, cs] * inv_rms * self.normw_ref[:, cs])
                   * (1.0 + scale[:, cs]) + shift[:, cs])
            hb_cols.append(h_c.astype(BF16))
        self.hb = jnp.concatenate(hb_cols, axis=1)

    def proj(self, c0, width):
        return jnp.dot(self.hb, self.win_ref[:, c0:c0 + width], preferred_element_type=F32)

    def qk_projections(self):
        self.qb = self.proj(3 * A_WIDTH, B_WIDTH)
        self.kb = self.proj(3 * A_WIDTH + B_WIDTH, B_WIDTH)

    def scores(self, g, j):
        tk, tq = SEQ_TILE, SEQ_TILE
        kp = self.k_scr[j, :, g * LANES:(g + 1) * LANES]
        s = jnp.dot(kp, self.qblk_scr[g], preferred_element_type=F32)
        self.s_scr[g] = s
        self.smax_scr[g] = jnp.max(s.reshape(tk // SUBLANES, SUBLANES, 2 * tq), axis=0)

    def softmax(self, g, slot, masked):
        tk, tq = SEQ_TILE, SEQ_TILE
        rows = 2 * SUBLANES
        s_scr, m_scr, par = self.s_scr, self.m_scr, self.par

        def chunk(c):
            sc = s_scr[g, c * rows:(c + 1) * rows, :]
            if masked:
                kv_row = lax.broadcasted_iota(jnp.int32, (rows, 2 * tq), 0) + c * rows
                q_col = lax.broadcasted_iota(jnp.int32, (rows, 2 * tq), 1) % tq
                sc = jnp.where(kv_row <= q_col, sc, -jnp.inf)
            return sc.reshape(2, SUBLANES, 2 * tq)

        if masked:
            col_max = jnp.max(chunk(0), axis=0)
            for c in range(1, tk // rows):
                col_max = jnp.maximum(col_max, jnp.max(chunk(c), axis=0))
        else:
            col_max = self.smax_scr[g]
        m_old = m_scr[par, g]
        m_new = jnp.maximum(m_old, _across_sublanes(col_max, jnp.maximum))
        alpha = jnp.exp2(m_old - m_new)
        m_scr[par, g] = m_new
        self.alpha_scr[slot] = alpha
        for c in range(tk // rows):
            pc = jnp.exp2(chunk(c) - m_new[None])
            self.p_scr[slot, c * rows:(c + 1) * rows, :] = (
                pc.reshape(rows, 2 * tq).astype(BF16))

    def value_update(self, g, j, slot):
        tq = SEQ_TILE
        acc_scr, par = self.acc_scr, self.par
        vt = self.vt_scr[j, g]
        alpha = self.alpha_scr[slot]
        for st in range(2):
            pv = jnp.dot(vt, self.p_scr[slot, :, st * tq:(st + 1) * tq],
                         preferred_element_type=F32)
            a = alpha[:, st * tq:(st + 1) * tq]
            acc = acc_scr[par, 2 * g + st].reshape(V_EXT // SUBLANES, SUBLANES, tq)
            acc_scr[par, 2 * g + st] = (acc * a[None]).reshape(V_EXT, tq) + pv

    def unit(self, g, masked, tile, succ_tile, prev_tile):
        ahead = (g + 2) % B_HEADS
        if g + 2 < B_HEADS:
            self.scores(ahead, tile)
        elif succ_tile is not None:
            self.scores(ahead, succ_tile)
        if prev_tile is not None:
            self.value_update((g - 1) % B_HEADS, prev_tile, (g - 1) % 2)
        self.softmax(g, g % 2, masked)

    def before_loop(self):
        tq, tk, i, par = SEQ_TILE, SEQ_TILE, self.i, self.par
        qb, kb = self.qb, self.kb
        lane = lax.broadcasted_iota(jnp.int32, (1, LANES), 1)
        cos, sin = _rope_tables(self.pos_ref[self.t].astype(F32), self.invf_ref[...], lane)
        sin_signed = jnp.where(lane < LANES // 2, -sin, sin)
        q = qb * lax.rsqrt(_head_mean_sq(qb, self.bd_ref) + NORM_EPS) * self.qnw_ref[...]
        k = kb * lax.rsqrt(_head_mean_sq(kb, self.bd_ref) + NORM_EPS) * self.knw_ref[...]
        q = _rope(q, cos, sin_signed) * (B_HEAD_DIM ** -0.5 * LOG2E)
        k = _rope(k, cos, sin_signed)
        self.k_scr[i] = k.astype(BF16)

        qt = q.T
        pair_row = lax.broadcasted_iota(jnp.int32, (LANES, tq), 0)
        is_a = (pair_row // HALF) % 2 == 0
        for g in range(B_HEADS):
            qp = qt[g * LANES:(g + 1) * LANES, :]
            self.qblk_scr[g] = jnp.concatenate(
                [jnp.where(is_a, qp, 0.0), jnp.where(is_a, 0.0, qp)], axis=1).astype(BF16)

        self.m_scr[par] = jnp.full(self.m_scr.shape[1:], -jnp.inf, F32)
        self.acc_scr[par] = jnp.zeros(self.acc_scr.shape[1:], F32)

        vb = self.proj(3 * A_WIDTH + 2 * B_WIDTH, B_WIDTH)
        vbt = vb.T.astype(BF16)
        for g in range(B_HEADS):
            self.vt_scr[i, g, :B_V_DIM, :] = vbt[g * B_V_DIM:(g + 1) * B_V_DIM, :]
            self.vt_scr[i, g, B_V_DIM:, :] = jnp.ones((V_EXT - B_V_DIM, tk), BF16)
        va = self.proj(A_WIDTH, A_WIDTH)
        self.scores(0, i)
        self.scores(1, i)
        ua = self.proj(0, A_WIDTH)
        self.unit(0, True, i, 0, None)
        za = self.proj(2 * A_WIDTH, A_WIDTH)
        self.unit(1, True, i, 0, i)
        zb = self.proj(3 * A_WIDTH + 3 * B_WIDTH, B_WIDTH)
        self.unit(2, True, i, 0, i)

        row_c = lax.broadcasted_iota(jnp.int32, (CHUNK, CHUNK), 0)
        col_c = lax.broadcasted_iota(jnp.int32, (CHUNK, CHUNK), 1)
        tril = row_c >= col_c
        for hh in range(A_HEADS):
            sl = slice(hh * A_HEAD_DIM, (hh + 1) * A_HEAD_DIM)
            v = va[:, sl]
            vn = (v * lax.rsqrt(jnp.mean(v * v, axis=-1, keepdims=True) + NORM_EPS)
                  * self.sgu_ref[hh:hh + 1, :])
            vnb = vn.astype(BF16)
            ws = jnp.where(tril, self.ws_ref[hh], 0.0).astype(BF16)
            bias = self.bst_ref[:, hh:hh + 1]
            n_chunks = tq // CHUNK
            chunks = jnp.concatenate(
                [vnb[c * CHUNK:(c + 1) * CHUNK, :] for c in range(n_chunks)], axis=1)
            mixed_chunks = jnp.dot(ws, chunks, preferred_element_type=F32)
            mix = jnp.concatenate([mixed_chunks[:, c * A_HEAD_DIM:(c + 1) * A_HEAD_DIM] + bias
                                   for c in range(n_chunks)], axis=0)
            a_out = ua[:, sl] * mix * _silu(za[:, sl])
            self.mixed_scr[par, :, sl] = a_out.astype(BF16)
        self.zb_gate = _silu(zb)
        self.unit(3, True, i, 0, i)

    def key_loop(self):
        i = self.i

        def tile_units(j, is_final=False):
            succ = None if is_final else j + 1
            self.unit(0, False, j, succ, jnp.where(j == 0, i, j - 1))
            for g in range(1, B_HEADS):
                self.unit(g, False, j, succ, j)

        def pair_body(jj, carry):
            tile_units(2 * jj)
            tile_units(2 * jj + 1)
            return carry

        lax.fori_loop(0, i // 2, pair_body, 0)

        @pl.when(i % 2 == 1)
        def _():
            tile_units(i - 1, is_final=True)

    def last_value_product_and_out_a(self):
        last = B_HEADS - 1
        self.value_update(last, jnp.maximum(self.i - 1, 0), last % 2)
        rows = self.tile_rows
        self.out_ref[0, rows, :] = self.x_ref[0, rows, :] + self.gate * jnp.dot(
            self.mixed_scr[self.par, :, :A_WIDTH], self.wout_ref[:A_WIDTH, :],
            preferred_element_type=F32)

    def finalise(self):
        tq, par, acc_scr = SEQ_TILE, self.par, self.acc_scr
        lam = (jnp.exp(jnp.sum(self.lq1_ref[...] * self.lk1_ref[...], axis=-1, keepdims=True))
               - jnp.exp(jnp.sum(self.lq2_ref[...] * self.lk2_ref[...], axis=-1, keepdims=True))
               + LAMBDA_INIT)
        for g in range(B_HEADS):
            inv_l1 = 1.0 / acc_scr[par, 2 * g, B_V_DIM:B_V_DIM + SUBLANES, :]
            inv_l2 = 1.0 / acc_scr[par, 2 * g + 1, B_V_DIM:B_V_DIM + SUBLANES, :]
            acc1 = acc_scr[par, 2 * g, :B_V_DIM, :].reshape(B_V_DIM // SUBLANES, SUBLANES, tq)
            acc2 = acc_scr[par, 2 * g + 1, :B_V_DIM, :].reshape(
                B_V_DIM // SUBLANES, SUBLANES, tq)
            ot = acc1 * inv_l1[None] - lam * (acc2 * inv_l2[None])
            o = ot.reshape(B_V_DIM, tq).T
            o = (o * lax.rsqrt(jnp.mean(o * o, axis=-1, keepdims=True) + SUBLN_EPS)
                 * self.subln_ref[...])
            o = o * (1.0 - LAMBDA_INIT)
            vsl = slice(g * B_V_DIM, (g + 1) * B_V_DIM)
            b_out = o * self.zb_gate[:, vsl]
            self.mixed_scr[par, :, A_WIDTH + g * B_V_DIM:A_WIDTH + (g + 1) * B_V_DIM] = (
                b_out.astype(BF16))

    def out_b(self):
        self.out_ref[0, self.tile_rows, :] += self.gate * jnp.dot(
            self.mixed_scr[self.par, :, A_WIDTH:], self.wout_ref[A_WIDTH:, :],
            preferred_element_type=F32)


def kernel(x, c, positions, norm_w, w_ada, b_ada, w_in, sgu_norm_w, w_s, b_s, q_norm_w, k_norm_w,
           lambda_q1, lambda_k1, lambda_q2, lambda_k2, subln_w, w_out):
    bsz, seq, d = x.shape
    assert d == D_MODEL and seq % (TILES_PER_STEP * SEQ_TILE) == 0 and norm_w.shape[0] == 1
    nt = seq // SEQ_TILE

    mod = pl.pallas_call(
        _adaln_kernel,
        grid=(3,),
        in_specs=[pl.BlockSpec((bsz, d), lambda n: (0, 0)),
                  pl.BlockSpec((None, d, d), lambda n: (0, 0, n)),
                  pl.BlockSpec((1, d), lambda n: (0, n))],
        out_specs=pl.BlockSpec((None, bsz, d), lambda n: (n, 0, 0)),
        out_shape=jax.ShapeDtypeStruct((3, bsz, d), F32),
        name="adaln_mod",
    )(c, w_ada, b_ada)

    n_grp = LANES // HALF
    pos_q = jnp.pad(positions.reshape(bsz, nt, n_grp, SEQ_TILE // n_grp),
                    ((0, 0), (0, 0), (0, SUBLANES - n_grp), (0, LANES - SEQ_TILE // n_grp)))

    inv_freq = ROPE_THETA ** (-jnp.arange(0, B_HEAD_DIM, 2, dtype=F32) / B_HEAD_DIM)
    invf = jnp.tile(inv_freq, LANES // HALF)[None, :]
    lane_head = (np.arange(MXU_DIM) // LANES) * 2 + (np.arange(MXU_DIM) // HALF) % 2
    bd = jnp.asarray(lane_head[:, None] == lane_head[None, :], dtype=BF16)

    qnw = _pair_layout(jnp.tile(q_norm_w.reshape(B_HEAD_DIM), 2 * B_HEADS))[None, :]
    knw = _pair_layout(jnp.tile(k_norm_w.reshape(B_HEAD_DIM), 2 * B_HEADS))[None, :]

    const2 = lambda b, i: (0, 0)
    const3 = lambda b, i: (0, 0, 0)
    step_rows = TILES_PER_STEP * SEQ_TILE
    out = pl.pallas_call(
        _layer_kernel,
        grid=(bsz, nt // TILES_PER_STEP),
        in_specs=[
            pl.BlockSpec((1, step_rows, d), lambda b, i: (b, i, 0)),
            pl.BlockSpec((3, bsz, d), const3),
            pl.BlockSpec((None, TILES_PER_STEP, SUBLANES, LANES),
                         lambda b, i: (b, i, 0, 0)),
            pl.BlockSpec((1, LANES), const2),
            pl.BlockSpec((1, d), const2),
            pl.BlockSpec(memory_space=pl.ANY),
            pl.BlockSpec((A_HEADS, A_HEAD_DIM), const2),
            pl.BlockSpec((A_HEADS, CHUNK, CHUNK), const3),
            pl.BlockSpec((CHUNK, A_HEADS), const2),
            pl.BlockSpec((1, B_WIDTH), const2),
            pl.BlockSpec((1, B_WIDTH), const2),
            pl.BlockSpec((1, B_HEAD_DIM), const2),
            pl.BlockSpec((1, B_HEAD_DIM), const2),
            pl.BlockSpec((1, B_HEAD_DIM), const2),
            pl.BlockSpec((1, B_HEAD_DIM), const2),
            pl.BlockSpec((1, B_V_DIM), const2),
            pl.BlockSpec(memory_space=pl.ANY),
            pl.BlockSpec((MXU_DIM, MXU_DIM), const2),
        ],
        out_specs=pl.BlockSpec((1, step_rows, d), lambda b, i: (b, i, 0)),
        out_shape=jax.ShapeDtypeStruct((bsz, seq, d), F32),
        scratch_shapes=[
            pltpu.VMEM((nt, SEQ_TILE, B_WIDTH), BF16),
            pltpu.VMEM((nt, B_HEADS, V_EXT, SEQ_TILE), BF16),
            pltpu.VMEM((B_HEADS, LANES, 2 * SEQ_TILE), BF16),
            pltpu.VMEM((2, B_HEADS, SUBLANES, 2 * SEQ_TILE), F32),
            pltpu.VMEM((2, 2 * B_HEADS, V_EXT, SEQ_TILE), F32),
            pltpu.VMEM((2, SEQ_TILE, A_WIDTH + B_WIDTH), BF16),
            pltpu.VMEM((B_HEADS, SEQ_TILE, 2 * SEQ_TILE), F32),
            pltpu.VMEM((2, SEQ_TILE, 2 * SEQ_TILE), BF16),
            pltpu.VMEM((2, SUBLANES, 2 * SEQ_TILE), F32),
            pltpu.VMEM((B_HEADS, SUBLANES, 2 * SEQ_TILE), F32),
            pltpu.VMEM((d, IN_COLS), BF16),
            pltpu.VMEM((A_WIDTH + B_WIDTH, d), BF16),
            pltpu.VMEM((WEIGHT_STAGE_SLOTS, WEIGHT_CHUNK_ROWS, IN_COLS), F32),
            pltpu.SemaphoreType.DMA((WEIGHT_STAGE_SLOTS,)),
        ],
        compiler_params=pltpu.CompilerParams(
            dimension_semantics=("arbitrary", "arbitrary"),
            vmem_limit_bytes=VMEM_LIMIT_BYTES),
        name="fused_layer",
    )(x, mod, pos_q, invf, norm_w, w_in,
      sgu_norm_w.reshape(A_HEADS, A_HEAD_DIM), w_s.reshape(A_HEADS, CHUNK, CHUNK),
      jnp.transpose(b_s.reshape(A_HEADS, CHUNK)), qnw, knw,
      lambda_q1, lambda_k1, lambda_q2, lambda_k2, subln_w, w_out, bd)
    return out
```

```python
import math

import numpy as np
import jax
import jax.numpy as jnp
from jax import lax
from jax.experimental import pallas as pl
from jax.experimental.pallas import tpu as pltpu

D_MODEL = 1024
A_WIDTH = 512
B_WIDTH = 512
A_HEADS = 4
A_HEAD_DIM = 128
CHUNK = 128
B_HEADS = 4
B_HEAD_DIM = 64
B_V_DIM = 128
ROPE_THETA = 10000.0
NORM_EPS = 1e-6
SUBLN_EPS = 1e-5
IN_COLS = 3 * A_WIDTH + 4 * B_WIDTH
LAMBDA_INIT = 0.8 - 0.6 * math.exp(-0.3 * 0)
LOG2E = math.log2(math.e)

LANES = 128
SUBLANES = 8
MXU_DIM = 256
SEQ_TILE = 256
TILES_PER_STEP = 4
VMEM_LIMIT_BYTES = 56 * 1024 * 1024
HALF = B_HEAD_DIM // 2
BF16_ROWS = 16
V_EXT = B_V_DIM + BF16_ROWS
WEIGHT_CHUNK_ROWS = 64
WEIGHT_STAGE_SLOTS = 4

F32 = jnp.float32
BF16 = jnp.bfloat16


def _pair_layout(t):
    lead = t.shape[:-1]
    t = t.reshape(lead + (B_HEADS, 2, 2, HALF))
    t = jnp.swapaxes(t, -3, -2)
    return t.reshape(lead + (B_WIDTH,))


def _load_weights(w_in_hbm, w_out_hbm, win_scr, wout_scr, stage, sem):
    ch = stage.shape[1]
    grp = lax.broadcasted_iota(jnp.int32, (1, LANES), 1) // HALF
    q0, v0 = 3 * A_WIDTH, 3 * A_WIDTH + 2 * B_WIDTH

    def stream(src_hbm, dst_scr, n_cols, reorder):
        n_chunks = src_hbm.shape[1] // ch

        def chunk_copy(c, slot):
            return pltpu.make_async_copy(src_hbm.at[0, pl.ds(c * ch, ch), :],
                                         stage.at[slot, :, pl.ds(0, n_cols)], sem.at[slot])

        depth = stage.shape[0]
        for c0 in range(depth - 1):
            chunk_copy(c0, c0).start()

        def body(c, carry):
            slot = c % depth
            ahead = c + depth - 1

            @pl.when(ahead < n_chunks)
            def _():
                chunk_copy(ahead, ahead % depth).start()

            chunk_copy(c, slot).wait()
            rows = pl.ds(pl.multiple_of(c * ch, ch), ch)
            for cb in range(n_cols // LANES):
                t = stage[slot, :, cb * LANES:(cb + 1) * LANES]
                if reorder and q0 <= cb * LANES < v0:
                    t = jnp.where(grp == 1, pltpu.roll(t, LANES - HALF, 1),
                                  jnp.where(grp == 2, pltpu.roll(t, HALF, 1), t))
                dst_scr[rows, cb * LANES:(cb + 1) * LANES] = t.astype(BF16)
            return carry

        lax.fori_loop(0, n_chunks, body, 0)

    stream(w_in_hbm, win_scr, IN_COLS, True)
    stream(w_out_hbm, wout_scr, D_MODEL, False)


def _adaln_kernel(c_ref, w_ref, b_ref, mod_ref):
    c = c_ref[...]
    c_act = c * jax.nn.sigmoid(c)
    mod_ref[...] = jnp.dot(c_act.astype(BF16), w_ref[...].astype(BF16),
                           preferred_element_type=F32) + b_ref[...]


def _silu(z):
    return z * jax.nn.sigmoid(z)


def _head_mean_sq(t, bd_ref):
    sq = (t * t).astype(BF16)
    halves = [jnp.dot(sq[:, h * MXU_DIM:(h + 1) * MXU_DIM], bd_ref[...],
                      preferred_element_type=F32) for h in range(B_WIDTH // MXU_DIM)]
    return jnp.concatenate(halves, axis=-1) * (1.0 / B_HEAD_DIM)


def _rope(t, cos, sin_signed):
    cols = []
    for cb in range(B_WIDTH // LANES):
        tc = t[:, cb * LANES:(cb + 1) * LANES]
        cols.append(tc * cos + pltpu.roll(tc, LANES // 2, 1) * sin_signed)
    return jnp.concatenate(cols, axis=-1)


def _rope_tables(pos_quarters, invf, lane):
    n_grp = LANES // HALF
    qrows = SEQ_TILE // n_grp
    grp = lane // HALF
    pos_t = pos_quarters.T
    packed = pos_t[:qrows, n_grp - 1:n_grp]
    for gi in range(n_grp - 2, -1, -1):
        packed = jnp.where(grp == gi, pos_t[:qrows, gi:gi + 1], packed)
    ang = packed * invf
    tables = []
    for packed_tab in (jnp.cos(ang), jnp.sin(ang)):
        parts = []
        for gi in range(n_grp):
            t = jnp.where(grp == gi, packed_tab, 0.0)
            t = t + pltpu.roll(t, 2 * HALF, 1)
            parts.append(t + pltpu.roll(t, HALF, 1))
        tables.append(jnp.concatenate(parts, axis=0))
    return tables


def _across_sublanes(x, op):
    for shift in (4, 2, 1):
        x = op(x, pltpu.roll(x, shift, 0))
    return x


def _layer_kernel(x_ref, mod_ref, pos_ref, invf_ref, normw_ref, w_in_hbm, sgu_ref, ws_ref,
                  bst_ref, qnw_ref, knw_ref, lq1_ref, lk1_ref, lq2_ref, lk2_ref, subln_ref,
                  w_out_hbm, bd_ref, out_ref,
                  k_scr, vt_scr, qblk_scr, m_scr, acc_scr, mixed_scr, s_scr, p_scr,
                  alpha_scr, smax_scr, win_ref, wout_ref, stage_scr, stage_sem):
    @pl.when((pl.program_id(0) == 0) & (pl.program_id(1) == 0))
    def _():
        _load_weights(w_in_hbm, w_out_hbm, win_ref, wout_ref, stage_scr, stage_sem)

    refs = dict(locals())
    tiles = [_TilePhases(t, refs) for t in range(TILES_PER_STEP)]
    tiles[0].activations()
    tiles[0].qk_projections()
    tiles[0].before_loop()
    for t, tile in enumerate(tiles):
        nxt = tiles[t + 1] if t + 1 < TILES_PER_STEP else None
        tile.key_loop()
        tile.last_value_product_and_out_a()
        if nxt is not None:
            nxt.activations()
            nxt.qk_projections()
        tile.finalise()
        tile.out_b()
        if nxt is not None:
            nxt.before_loop()


class _TilePhases:
    def __init__(self, t, refs):
        self.__dict__.update(refs)
        self.t = t
        self.par = t % 2
        self.base = pl.program_id(1) * TILES_PER_STEP
        self.i = self.base + t
        self.tile_rows = slice(t * SEQ_TILE, (t + 1) * SEQ_TILE)

    def activations(self):
        x_ref, tile_rows = self.x_ref, self.tile_rows
        b = pl.program_id(0)
        shift = self.mod_ref[0, pl.ds(b, 1), :]
        scale = self.mod_ref[1, pl.ds(b, 1), :]
        self.gate = self.mod_ref[2, pl.ds(b, 1), :]
        sq_lanes = None
        for c0 in range(0, D_MODEL, LANES):
            x_c = x_ref[0, tile_rows, c0:c0 + LANES]
            sq_lanes = x_c * x_c if sq_lanes is None else sq_lanes + x_c * x_c
        sum_sq = jnp.sum(sq_lanes, axis=-1, keepdims=True)
        inv_rms = lax.rsqrt(sum_sq * (1.0 / D_MODEL) + NORM_EPS)
        hb_cols = []
        for c0 in range(0, D_MODEL, MXU_DIM):
            cs = slice(c0, c0 + MXU_DIM)
            h_c = ((x_ref[0, tile_rows, cs] * inv_rms * self.normw_ref[:, cs])
                   * (1.0 + scale[:, cs]) + shift[:, cs])
            hb_cols.append(h_c.astype(BF16))
        self.hb = jnp.concatenate(hb_cols, axis=1)

    def proj(self, c0, width):
        return jnp.dot(self.hb, self.win_ref[:, c0:c0 + width], preferred_element_type=F32)

    def qk_projections(self):
        self.qb = self.proj(3 * A_WIDTH, B_WIDTH)
        self.kb = self.proj(3 * A_WIDTH + B_WIDTH, B_WIDTH)

    def scores(self, g, j):
        tk, tq = SEQ_TILE, SEQ_TILE
        kp = self.k_scr[j, :, g * LANES:(g + 1) * LANES]
        s = jnp.dot(kp, self.qblk_scr[g], preferred_element_type=F32)
        self.s_scr[g] = s
        self.smax_scr[g] = jnp.max(s.reshape(tk // SUBLANES, SUBLANES, 2 * tq), axis=0)

    def softmax(self, g, slot, masked):
        tk, tq = SEQ_TILE, SEQ_TILE
        rows = 2 * SUBLANES
        s_scr, m_scr, par = self.s_scr, self.m_scr, self.par

        def chunk(c):
            sc = s_scr[g, c * rows:(c + 1) * rows, :]
            if masked:
                kv_row = lax.broadcasted_iota(jnp.int32, (rows, 2 * tq), 0) + c * rows
                q_col = lax.broadcasted_iota(jnp.int32, (rows, 2 * tq), 1) % tq
                sc = jnp.where(kv_row <= q_col, sc, -jnp.inf)
            return sc.reshape(2, SUBLANES, 2 * tq)

        if masked:
            col_max = jnp.max(chunk(0), axis=0)
            for c in range(1, tk // rows):
                col_max = jnp.maximum(col_max, jnp.max(chunk(c), axis=0))
        else:
            col_max = self.smax_scr[g]
        m_old = m_scr[par, g]
        m_new = jnp.maximum(m_old, _across_sublanes(col_max, jnp.maximum))
        alpha = jnp.exp2(m_old - m_new)
        m_scr[par, g] = m_new
        self.alpha_scr[slot] = alpha
        for c in range(tk // rows):
            pc = jnp.exp2(chunk(c) - m_new[None])
            self.p_scr[slot, c * rows:(c + 1) * rows, :] = (
                pc.reshape(rows, 2 * tq).astype(BF16))

    def value_update(self, g, j, slot):
        tq = SEQ_TILE
        acc_scr, par = self.acc_scr, self.par
        vt = self.vt_scr[j, g]
        alpha = self.alpha_scr[slot]
        for st in range(2):
            pv = jnp.dot(vt, self.p_scr[slot, :, st * tq:(st + 1) * tq],
                         preferred_element_type=F32)
            a = alpha[:, st * tq:(st + 1) * tq]
            acc = acc_scr[par, 2 * g + st].reshape(V_EXT // SUBLANES, SUBLANES, tq)
            acc_scr[par, 2 * g + st] = (acc * a[None]).reshape(V_EXT, tq) + pv

    def unit(self, g, masked, tile, succ_tile, prev_tile):
        ahead = (g + 2) % B_HEADS
        if g + 2 < B_HEADS:
            self.scores(ahead, tile)
        elif succ_tile is not None:
            self.scores(ahead, succ_tile)
        if prev_tile is not None:
            self.value_update((g - 1) % B_HEADS, prev_tile, (g - 1) % 2)
        self.softmax(g, g % 2, masked)

    def before_loop(self):
        tq, tk, i, par = SEQ_TILE, SEQ_TILE, self.i, self.par
        qb, kb = self.qb, self.kb
        lane = lax.broadcasted_iota(jnp.int32, (1, LANES), 1)
        cos, sin = _rope_tables(self.pos_ref[self.t].astype(F32), self.invf_ref[...], lane)
        sin_signed = jnp.where(lane < LANES // 2, -sin, sin)
        q = qb * lax.rsqrt(_head_mean_sq(qb, self.bd_ref) + NORM_EPS) * self.qnw_ref[...]
        k = kb * lax.rsqrt(_head_mean_sq(kb, self.bd_ref) + NORM_EPS) * self.knw_ref[...]
        q = _rope(q, cos, sin_signed) * (B_HEAD_DIM ** -0.5 * LOG2E)
        k = _rope(k, cos, sin_signed)
        self.k_scr[i] = k.astype(BF16)

        qt = q.T
        pair_row = lax.broadcasted_iota(jnp.int32, (LANES, tq), 0)
        is_a = (pair_row // HALF) % 2 == 0
        for g in range(B_HEADS):
            qp = qt[g * LANES:(g + 1) * LANES, :]
            self.qblk_scr[g] = jnp.concatenate(
                [jnp.where(is_a, qp, 0.0), jnp.where(is_a, 0.0, qp)], axis=1).astype(BF16)

        self.m_scr[par] = jnp.full(self.m_scr.shape[1:], -jnp.inf, F32)
        self.acc_scr[par] = jnp.zeros(self.acc_scr.shape[1:], F32)

        vb = self.proj(3 * A_WIDTH + 2 * B_WIDTH, B_WIDTH)
        vbt = vb.T.astype(BF16)
        for g in range(B_HEADS):
            self.vt_scr[i, g, :B_V_DIM, :] = vbt[g * B_V_DIM:(g + 1) * B_V_DIM, :]
            self.vt_scr[i, g, B_V_DIM:, :] = jnp.ones((V_EXT - B_V_DIM, tk), BF16)
        va = self.proj(A_WIDTH, A_WIDTH)
        self.scores(0, i)
        self.scores(1, i)
        ua = self.proj(0, A_WIDTH)
        self.unit(0, True, i, 0, None)
        za = self.proj(2 * A_WIDTH, A_WIDTH)
        self.unit(1, True, i, 0, i)
        zb = self.proj(3 * A_WIDTH + 3 * B_WIDTH, B_WIDTH)
        after_diag = i - 1 if self.t > 0 else 0
        self.unit(2, True, i, after_diag, i)

        row_c = lax.broadcasted_iota(jnp.int32, (CHUNK, CHUNK), 0)
        col_c = lax.broadcasted_iota(jnp.int32, (CHUNK, CHUNK), 1)
        tril = row_c >= col_c
        for hh in range(A_HEADS):
            sl = slice(hh * A_HEAD_DIM, (hh + 1) * A_HEAD_DIM)
            v = va[:, sl]
            vn = (v * lax.rsqrt(jnp.mean(v * v, axis=-1, keepdims=True) + NORM_EPS)
                  * self.sgu_ref[hh:hh + 1, :])
            vnb = vn.astype(BF16)
            ws = jnp.where(tril, self.ws_ref[hh], 0.0).astype(BF16)
            bias = self.bst_ref[:, hh:hh + 1]
            n_chunks = tq // CHUNK
            chunks = jnp.concatenate(
                [vnb[c * CHUNK:(c + 1) * CHUNK, :] for c in range(n_chunks)], axis=1)
            mixed_chunks = jnp.dot(ws, chunks, preferred_element_type=F32)
            mix = jnp.concatenate([mixed_chunks[:, c * A_HEAD_DIM:(c + 1) * A_HEAD_DIM] + bias
                                   for c in range(n_chunks)], axis=0)
            a_out = ua[:, sl] * mix * _silu(za[:, sl])
            self.mixed_scr[par, :, sl] = a_out.astype(BF16)
        self.zb_gate = _silu(zb)
        self.unit(3, True, i, after_diag, i)
        for s in range(self.t):
            tile = i - 1 - s
            succ = tile - 1 if s + 1 < self.t else 0
            self.tile_units(tile, succ, tile + 1)

    def tile_units(self, tile, succ_tile, prev_tile):
        self.unit(0, False, tile, succ_tile, prev_tile)
        for g in range(1, B_HEADS):
            self.unit(g, False, tile, succ_tile, tile)

    def key_loop(self):
        base = self.base

        def pair_body(jj, carry):
            j = 2 * jj
            self.tile_units(j, j + 1, jnp.where(j == 0, base, j - 1))
            self.tile_units(j + 1, j + 2, j)
            return carry

        lax.fori_loop(0, base // 2, pair_body, 0)

    def last_value_product_and_out_a(self):
        last = B_HEADS - 1
        base = self.base
        self.value_update(last, jnp.where(base > 0, base - 1, base), last % 2)
        rows = self.tile_rows
        self.out_ref[0, rows, :] = self.x_ref[0, rows, :] + self.gate * jnp.dot(
            self.mixed_scr[self.par, :, :A_WIDTH], self.wout_ref[:A_WIDTH, :],
            preferred_element_type=F32)

    def finalise(self):
        tq, par, acc_scr = SEQ_TILE, self.par, self.acc_scr
        lam = (jnp.exp(jnp.sum(self.lq1_ref[...] * self.lk1_ref[...], axis=-1, keepdims=True))
               - jnp.exp(jnp.sum(self.lq2_ref[...] * self.lk2_ref[...], axis=-1, keepdims=True))
               + LAMBDA_INIT)
        for g in range(B_HEADS):
            inv_l1 = 1.0 / acc_scr[par, 2 * g, B_V_DIM:B_V_DIM + SUBLANES, :]
            inv_l2 = 1.0 / acc_scr[par, 2 * g + 1, B_V_DIM:B_V_DIM + SUBLANES, :]
            acc1 = acc_scr[par, 2 * g, :B_V_DIM, :].reshape(B_V_DIM // SUBLANES, SUBLANES, tq)
            acc2 = acc_scr[par, 2 * g + 1, :B_V_DIM, :].reshape(
                B_V_DIM // SUBLANES, SUBLANES, tq)
            ot = acc1 * inv_l1[None] - lam * (acc2 * inv_l2[None])
            o = ot.reshape(B_V_DIM, tq).T
            o = (o * lax.rsqrt(jnp.mean(o * o, axis=-1, keepdims=True) + SUBLN_EPS)
                 * self.subln_ref[...])
            o = o * (1.0 - LAMBDA_INIT)
            vsl = slice(g * B_V_DIM, (g + 1) * B_V_DIM)
            b_out = o * self.zb_gate[:, vsl]
            self.mixed_scr[par, :, A_WIDTH + g * B_V_DIM:A_WIDTH + (g + 1) * B_V_DIM] = (
                b_out.astype(BF16))

    def out_b(self):
        self.out_ref[0, self.tile_rows, :] += self.gate * jnp.dot(
            self.mixed_scr[self.par, :, A_WIDTH:], self.wout_ref[A_WIDTH:, :],
            preferred_element_type=F32)


def kernel(x, c, positions, norm_w, w_ada, b_ada, w_in, sgu_norm_w, w_s, b_s, q_norm_w, k_norm_w,
           lambda_q1, lambda_k1, lambda_q2, lambda_k2, subln_w, w_out):
    bsz, seq, d = x.shape
    assert d == D_MODEL and seq % (TILES_PER_STEP * SEQ_TILE) == 0 and norm_w.shape[0] == 1
    assert TILES_PER_STEP % 2 == 0
    nt = seq // SEQ_TILE

    mod = pl.pallas_call(
        _adaln_kernel,
        grid=(3,),
        in_specs=[pl.BlockSpec((bsz, d), lambda n: (0, 0)),
                  pl.BlockSpec((None, d, d), lambda n: (0, 0, n)),
                  pl.BlockSpec((1, d), lambda n: (0, n))],
        out_specs=pl.BlockSpec((None, bsz, d), lambda n: (n, 0, 0)),
        out_shape=jax.ShapeDtypeStruct((3, bsz, d), F32),
        name="adaln_mod",
    )(c, w_ada, b_ada)

    n_grp = LANES // HALF
    pos_q = jnp.pad(positions.reshape(bsz, nt, n_grp, SEQ_TILE // n_grp),
                    ((0, 0), (0, 0), (0, SUBLANES - n_grp), (0, LANES - SEQ_TILE // n_grp)))

    inv_freq = ROPE_THETA ** (-jnp.arange(0, B_HEAD_DIM, 2, dtype=F32) / B_HEAD_DIM)
    invf = jnp.tile(inv_freq, LANES // HALF)[None, :]
    lane_head = (np.arange(MXU_DIM) // LANES) * 2 + (np.arange(MXU_DIM) // HALF) % 2
    bd = jnp.asarray(lane_head[:, None] == lane_head[None, :], dtype=BF16)

    qnw = _pair_layout(jnp.tile(q_norm_w.reshape(B_HEAD_DIM), 2 * B_HEADS))[None, :]
    knw = _pair_layout(jnp.tile(k_norm_w.reshape(B_HEAD_DIM), 2 * B_HEADS))[None, :]

    const2 = lambda b, i: (0, 0)
    const3 = lambda b, i: (0, 0, 0)
    step_rows = TILES_PER_STEP * SEQ_TILE
    out = pl.pallas_call(
        _layer_kernel,
        grid=(bsz, nt // TILES_PER_STEP),
        in_specs=[
            pl.BlockSpec((1, step_rows, d), lambda b, i: (b, i, 0)),
            pl.BlockSpec((3, bsz, d), const3),
            pl.BlockSpec((None, TILES_PER_STEP, SUBLANES, LANES),
                         lambda b, i: (b, i, 0, 0)),
            pl.BlockSpec((1, LANES), const2),
            pl.BlockSpec((1, d), const2),
            pl.BlockSpec(memory_space=pl.ANY),
            pl.BlockSpec((A_HEADS, A_HEAD_DIM), const2),
            pl.BlockSpec((A_HEADS, CHUNK, CHUNK), const3),
            pl.BlockSpec((CHUNK, A_HEADS), const2),
            pl.BlockSpec((1, B_WIDTH), const2),
            pl.BlockSpec((1, B_WIDTH), const2),
            pl.BlockSpec((1, B_HEAD_DIM), const2),
            pl.BlockSpec((1, B_HEAD_DIM), const2),
            pl.BlockSpec((1, B_HEAD_DIM), const2),
            pl.BlockSpec((1, B_HEAD_DIM), const2),
            pl.BlockSpec((1, B_V_DIM), const2),
            pl.BlockSpec(memory_space=pl.ANY),
            pl.BlockSpec((MXU_DIM, MXU_DIM), const2),
        ],
        out_specs=pl.BlockSpec((1, step_rows, d), lambda b, i: (b, i, 0)),
        out_shape=jax.ShapeDtypeStruct((bsz, seq, d), F32),
        scratch_shapes=[
            pltpu.VMEM((nt, SEQ_TILE, B_WIDTH), BF16),
            pltpu.VMEM((nt, B_HEADS, V_EXT, SEQ_TILE), BF16),
            pltpu.VMEM((B_HEADS, LANES, 2 * SEQ_TILE), BF16),
            pltpu.VMEM((2, B_HEADS, SUBLANES, 2 * SEQ_TILE), F32),
            pltpu.VMEM((2, 2 * B_HEADS, V_EXT, SEQ_TILE), F32),
            pltpu.VMEM((2, SEQ_TILE, A_WIDTH + B_WIDTH), BF16),
            pltpu.VMEM((B_HEADS, SEQ_TILE, 2 * SEQ_TILE), F32),
            pltpu.VMEM((2, SEQ_TILE, 2 * SEQ_TILE), BF16),
            pltpu.VMEM((2, SUBLANES, 2 * SEQ_TILE), F32),
            pltpu.VMEM((B_HEADS, SUBLANES, 2 * SEQ_TILE), F32),
            pltpu.VMEM((d, IN_COLS), BF16),
            pltpu.VMEM((A_WIDTH + B_WIDTH, d), BF16),
            pltpu.VMEM((WEIGHT_STAGE_SLOTS, WEIGHT_CHUNK_ROWS, IN_COLS), F32),
            pltpu.SemaphoreType.DMA((WEIGHT_STAGE_SLOTS,)),
        ],
        compiler_params=pltpu.CompilerParams(
            dimension_semantics=("arbitrary", "arbitrary"),
            vmem_limit_bytes=VMEM_LIMIT_BYTES),
        name="fused_layer",
    )(x, mod, pos_q, invf, norm_w, w_in,
      sgu_norm_w.reshape(A_HEADS, A_HEAD_DIM), w_s.reshape(A_HEADS, CHUNK, CHUNK),
      jnp.transpose(b_s.reshape(A_HEADS, CHUNK)), qnw, knw,
      lambda_q1, lambda_k1, lambda_q2, lambda_k2, subln_w, w_out, bd)
    return out
```

```python
import math

import numpy as np
import jax
import jax.numpy as jnp
from jax import lax
from jax.experimental import pallas as pl
from jax.experimental.pallas import tpu as pltpu

D_MODEL = 1024
A_WIDTH = 512
B_WIDTH = 512
A_HEADS = 4
A_HEAD_DIM = 128
CHUNK = 128
B_HEADS = 4
B_HEAD_DIM = 64
B_V_DIM = 128
ROPE_THETA = 10000.0
NORM_EPS = 1e-6
SUBLN_EPS = 1e-5
IN_COLS = 3 * A_WIDTH + 4 * B_WIDTH
LAMBDA_INIT = 0.8 - 0.6 * math.exp(-0.3 * 0)
LOG2E = math.log2(math.e)

LANES = 128
SUBLANES = 8
MXU_DIM = 256
SEQ_TILE = 256
TILES_PER_STEP = 4
VMEM_LIMIT_BYTES = 56 * 1024 * 1024
HALF = B_HEAD_DIM // 2
BF16_ROWS = 16
V_EXT = B_V_DIM + BF16_ROWS
ADALN_SPLIT = 2
WEIGHT_CHUNK_ROWS = 32
WEIGHT_STAGE_SLOTS = 8

F32 = jnp.float32
BF16 = jnp.bfloat16


def _pair_layout(t):
    lead = t.shape[:-1]
    t = t.reshape(lead + (B_HEADS, 2, 2, HALF))
    t = jnp.swapaxes(t, -3, -2)
    return t.reshape(lead + (B_WIDTH,))


def _load_weights(w_in_hbm, w_out_hbm, win_scr, wout_scr, stage, sem):
    ch = stage.shape[1]
    grp = lax.broadcasted_iota(jnp.int32, (1, LANES), 1) // HALF
    q0, v0 = 3 * A_WIDTH, 3 * A_WIDTH + 2 * B_WIDTH

    def stream(src_hbm, dst_scr, n_cols, reorder):
        n_chunks = src_hbm.shape[1] // ch

        def chunk_copy(c, slot):
            return pltpu.make_async_copy(src_hbm.at[0, pl.ds(c * ch, ch), :],
                                         stage.at[slot, :, pl.ds(0, n_cols)], sem.at[slot])

        depth = stage.shape[0]
        for c0 in range(depth - 1):
            chunk_copy(c0, c0).start()

        def body(c, carry):
            slot = c % depth
            ahead = c + depth - 1

            @pl.when(ahead < n_chunks)
            def _():
                chunk_copy(ahead, ahead % depth).start()

            chunk_copy(c, slot).wait()
            rows = pl.ds(pl.multiple_of(c * ch, ch), ch)
            for cb in range(n_cols // LANES):
                t = stage[slot, :, cb * LANES:(cb + 1) * LANES]
                if reorder and q0 <= cb * LANES < v0:
                    t = jnp.where(grp == 1, pltpu.roll(t, LANES - HALF, 1),
                                  jnp.where(grp == 2, pltpu.roll(t, HALF, 1), t))
                dst_scr[rows, cb * LANES:(cb + 1) * LANES] = t.astype(BF16)
            return carry

        lax.fori_loop(0, n_chunks, body, 0)

    stream(w_in_hbm, win_scr, IN_COLS, True)
    stream(w_out_hbm, wout_scr, D_MODEL, False)


def _adaln_kernel(c_ref, w_ref, b_ref, mod_ref):
    c = c_ref[...]
    c_act = c * jax.nn.sigmoid(c)
    mod_ref[...] = jnp.dot(c_act.astype(BF16), w_ref[...].astype(BF16),
                           preferred_element_type=F32) + b_ref[...]


def _silu(z):
    return z * jax.nn.sigmoid(z)


def _head_mean_sq(t, bd_ref):
    sq = (t * t).astype(BF16)
    halves = [jnp.dot(sq[:, h * MXU_DIM:(h + 1) * MXU_DIM], bd_ref[...],
                      preferred_element_type=F32) for h in range(B_WIDTH // MXU_DIM)]
    return jnp.concatenate(halves, axis=-1) * (1.0 / B_HEAD_DIM)


def _rope(t, cos, sin_signed):
    cols = []
    for cb in range(B_WIDTH // LANES):
        tc = t[:, cb * LANES:(cb + 1) * LANES]
        cols.append(tc * cos + pltpu.roll(tc, LANES // 2, 1) * sin_signed)
    return jnp.concatenate(cols, axis=-1)


def _rope_tables(pos_quarters, invf, lane):
    n_grp = LANES // HALF
    qrows = SEQ_TILE // n_grp
    grp = lane // HALF
    pos_t = pos_quarters.T
    packed = pos_t[:qrows, n_grp - 1:n_grp]
    for gi in range(n_grp - 2, -1, -1):
        packed = jnp.where(grp == gi, pos_t[:qrows, gi:gi + 1], packed)
    ang = packed * invf
    tables = []
    for packed_tab in (jnp.cos(ang), jnp.sin(ang)):
        parts = []
        for gi in range(n_grp):
            t = jnp.where(grp == gi, packed_tab, 0.0)
            t = t + pltpu.roll(t, 2 * HALF, 1)
            parts.append(t + pltpu.roll(t, HALF, 1))
        tables.append(jnp.concatenate(parts, axis=0))
    return tables


def _across_sublanes(x, op):
    for shift in (4, 2, 1):
        x = op(x, pltpu.roll(x, shift, 0))
    return x


def _layer_kernel(x_ref, mod_ref, pos_ref, invf_ref, normw_ref, w_in_hbm, sgu_ref, ws_ref,
                  bst_ref, qnw_ref, knw_ref, lq1_ref, lk1_ref, lq2_ref, lk2_ref, subln_ref,
                  w_out_hbm, bd_ref, out_ref,
                  k_scr, vt_scr, qblk_scr, m_scr, acc_scr, mixed_scr, s_scr, p_scr,
                  alpha_scr, smax_scr, win_ref, wout_ref, stage_scr, stage_sem):
    @pl.when((pl.program_id(0) == 0) & (pl.program_id(1) == 0))
    def _():
        _load_weights(w_in_hbm, w_out_hbm, win_ref, wout_ref, stage_scr, stage_sem)

    refs = dict(locals())
    tiles = [_TilePhases(t, refs) for t in range(TILES_PER_STEP)]
    tiles[0].activations()
    tiles[0].qk_projections()
    tiles[0].before_loop()
    for t, tile in enumerate(tiles):
        nxt = tiles[t + 1] if t + 1 < TILES_PER_STEP else None
        tile.key_loop()
        tile.last_value_product_and_out_a()
        if nxt is not None:
            nxt.activations()
            nxt.qk_projections()
        tile.finalise()
        tile.out_b()
        if nxt is not None:
            nxt.before_loop()


class _TilePhases:
    def __init__(self, t, refs):
        self.__dict__.update(refs)
        self.t = t
        self.par = t % 2
        self.base = pl.program_id(1) * TILES_PER_STEP
        self.i = self.base + t
        self.tile_rows = slice(t * SEQ_TILE, (t + 1) * SEQ_TILE)

    def activations(self):
        x_ref, tile_rows = self.x_ref, self.tile_rows
        b = pl.program_id(0)
        shift = self.mod_ref[0, pl.ds(b, 1), :]
        scale = self.mod_ref[1, pl.ds(b, 1), :]
        self.gate = self.mod_ref[2, pl.ds(b, 1), :]
        sq_lanes = None
        for c0 in range(0, D_MODEL, LANES):
            x_c = x_ref[0, tile_rows, c0:c0 + LANES]
            sq_lanes = x_c * x_c if sq_lanes is None else sq_lanes + x_c * x_c
        sum_sq = jnp.sum(sq_lanes, axis=-1, keepdims=True)
        inv_rms = lax.rsqrt(sum_sq * (1.0 / D_MODEL) + NORM_EPS)
        hb_cols = []
        for c0 in range(0, D_MODEL, MXU_DIM):
            cs = slice(c0, c0 + MXU_DIM)
            h_c = ((x_ref[0, tile_rows, cs] * inv_rms * self.normw_ref[:, cs])
                   * (1.0 + scale[:, cs]) + shift[:, cs])
            hb_cols.append(h_c.astype(BF16))
        self.hb = jnp.concatenate(hb_cols, axis=1)

    def proj(self, c0, width):
        return jnp.dot(self.hb, self.win_ref[:, c0:c0 + width], preferred_element_type=F32)

    def qk_projections(self):
        self.qb = self.proj(3 * A_WIDTH, B_WIDTH)
        self.kb = self.proj(3 * A_WIDTH + B_WIDTH, B_WIDTH)

    def scores(self, g, j):
        tk, tq = SEQ_TILE, SEQ_TILE
        kp = self.k_scr[j, :, g * LANES:(g + 1) * LANES]
        s = jnp.dot(kp, self.qblk_scr[g], preferred_element_type=F32)
        self.s_scr[g] = s
        self.smax_scr[g] = jnp.max(s.reshape(tk // SUBLANES, SUBLANES, 2 * tq), axis=0)

    def softmax(self, g, slot, masked):
        tk, tq = SEQ_TILE, SEQ_TILE
        rows = 2 * SUBLANES
        s_scr, m_scr, par = self.s_scr, self.m_scr, self.par

        def chunk(c):
            sc = s_scr[g, c * rows:(c + 1) * rows, :]
            if masked:
                kv_row = lax.broadcasted_iota(jnp.int32, (rows, 2 * tq), 0) + c * rows
                q_col = lax.broadcasted_iota(jnp.int32, (rows, 2 * tq), 1) % tq
                sc = jnp.where(kv_row <= q_col, sc, -jnp.inf)
            return sc.reshape(2, SUBLANES, 2 * tq)

        if masked:
            col_max = jnp.max(chunk(0), axis=0)
            for c in range(1, tk // rows):
                col_max = jnp.maximum(col_max, jnp.max(chunk(c), axis=0))
        else:
            col_max = self.smax_scr[g]
        m_old = m_scr[par, g]
        m_new = jnp.maximum(m_old, _across_sublanes(col_max, jnp.maximum))
        alpha = jnp.exp2(m_old - m_new)
        m_scr[par, g] = m_new
        self.alpha_scr[slot] = alpha
        for c in range(tk // rows):
            pc = jnp.exp2(chunk(c) - m_new[None])
            self.p_scr[slot, c * rows:(c + 1) * rows, :] = (
                pc.reshape(rows, 2 * tq).astype(BF16))

    def value_update(self, g, j, slot):
        tq = SEQ_TILE
        acc_scr, par = self.acc_scr, self.par
        vt = self.vt_scr[j, g]
        alpha = self.alpha_scr[slot]
        for st in range(2):
            pv = jnp.dot(vt, self.p_scr[slot, :, st * tq:(st + 1) * tq],
                         preferred_element_type=F32)
            a = alpha[:, st * tq:(st + 1) * tq]
            acc = acc_scr[par, 2 * g + st].reshape(V_EXT // SUBLANES, SUBLANES, tq)
            acc_scr[par, 2 * g + st] = (acc * a[None]).reshape(V_EXT, tq) + pv

    def unit(self, g, masked, tile, succ_tile, prev_tile):
        ahead = (g + 2) % B_HEADS
        if g + 2 < B_HEADS:
            self.scores(ahead, tile)
        elif succ_tile is not None:
            self.scores(ahead, succ_tile)
        if prev_tile is not None:
            self.value_update((g - 1) % B_HEADS, prev_tile, (g - 1) % 2)
        self.softmax(g, g % 2, masked)

    def before_loop(self):
        tq, tk, i, par = SEQ_TILE, SEQ_TILE, self.i, self.par
        qb, kb = self.qb, self.kb
        lane = lax.broadcasted_iota(jnp.int32, (1, LANES), 1)
        cos, sin = _rope_tables(self.pos_ref[self.t].astype(F32), self.invf_ref[...], lane)
        sin_signed = jnp.where(lane < LANES // 2, -sin, sin)
        q = qb * lax.rsqrt(_head_mean_sq(qb, self.bd_ref) + NORM_EPS) * self.qnw_ref[...]
        k = kb * lax.rsqrt(_head_mean_sq(kb, self.bd_ref) + NORM_EPS) * self.knw_ref[...]
        q = _rope(q, cos, sin_signed) * (B_HEAD_DIM ** -0.5 * LOG2E)
        k = _rope(k, cos, sin_signed)
        self.k_scr[i] = k.astype(BF16)

        qt = q.T
        pair_row = lax.broadcasted_iota(jnp.int32, (LANES, tq), 0)
        is_a = (pair_row // HALF) % 2 == 0
        for g in range(B_HEADS):
            qp = qt[g * LANES:(g + 1) * LANES, :]
            self.qblk_scr[g] = jnp.concatenate(
                [jnp.where(is_a, qp, 0.0), jnp.where(is_a, 0.0, qp)], axis=1).astype(BF16)

        self.m_scr[par] = jnp.full(self.m_scr.shape[1:], -jnp.inf, F32)
        self.acc_scr[par] = jnp.zeros(self.acc_scr.shape[1:], F32)

        vb = self.proj(3 * A_WIDTH + 2 * B_WIDTH, B_WIDTH)
        vbt = vb.T.astype(BF16)
        for g in range(B_HEADS):
            self.vt_scr[i, g, :B_V_DIM, :] = vbt[g * B_V_DIM:(g + 1) * B_V_DIM, :]
            self.vt_scr[i, g, B_V_DIM:, :] = jnp.ones((V_EXT - B_V_DIM, tk), BF16)
        va = self.proj(A_WIDTH, A_WIDTH)
        self.scores(0, i)
        self.scores(1, i)
        ua = self.proj(0, A_WIDTH)
        self.unit(0, True, i, 0, None)
        za = self.proj(2 * A_WIDTH, A_WIDTH)
        self.unit(1, True, i, 0, i)
        zb = self.proj(3 * A_WIDTH + 3 * B_WIDTH, B_WIDTH)
        after_diag = i - 1 if self.t > 0 else 0
        self.unit(2, True, i, after_diag, i)

        row_c = lax.broadcasted_iota(jnp.int32, (CHUNK, CHUNK), 0)
        col_c = lax.broadcasted_iota(jnp.int32, (CHUNK, CHUNK), 1)
        tril = row_c >= col_c
        for hh in range(A_HEADS):
            sl = slice(hh * A_HEAD_DIM, (hh + 1) * A_HEAD_DIM)
            v = va[:, sl]
            vn = (v * lax.rsqrt(jnp.mean(v * v, axis=-1, keepdims=True) + NORM_EPS)
                  * self.sgu_ref[hh:hh + 1, :])
            vnb = vn.astype(BF16)
            ws = jnp.where(tril, self.ws_ref[hh], 0.0).astype(BF16)
            bias = self.bst_ref[:, hh:hh + 1]
            n_chunks = tq // CHUNK
            chunks = jnp.concatenate(
                [vnb[c * CHUNK:(c + 1) * CHUNK, :] for c in range(n_chunks)], axis=1)
            mixed_chunks = jnp.dot(ws, chunks, preferred_element_type=F32)
            mix = jnp.concatenate([mixed_chunks[:, c * A_HEAD_DIM:(c + 1) * A_HEAD_DIM] + bias
                                   for c in range(n_chunks)], axis=0)
            a_out = ua[:, sl] * mix * _silu(za[:, sl])
            self.mixed_scr[par, :, sl] = a_out.astype(BF16)
        self.zb_gate = _silu(zb)
        self.unit(3, True, i, after_diag, i)
        for s in range(self.t):
            tile = i - 1 - s
            succ = tile - 1 if s + 1 < self.t else 0
            self.tile_units(tile, succ, tile + 1)

    def tile_units(self, tile, succ_tile, prev_tile):
        self.unit(0, False, tile, succ_tile, prev_tile)
        for g in range(1, B_HEADS):
            self.unit(g, False, tile, succ_tile, tile)

    def key_loop(self):
        base = self.base

        def pair_body(jj, carry):
            j = 2 * jj
            self.tile_units(j, j + 1, jnp.where(j == 0, base, j - 1))
            self.tile_units(j + 1, j + 2, j)
            return carry

        lax.fori_loop(0, base // 2, pair_body, 0)

    def last_value_product_and_out_a(self):
        last = B_HEADS - 1
        base = self.base
        self.value_update(last, jnp.where(base > 0, base - 1, base), last % 2)
        rows = self.tile_rows
        self.out_ref[0, rows, :] = self.x_ref[0, rows, :] + self.gate * jnp.dot(
            self.mixed_scr[self.par, :, :A_WIDTH], self.wout_ref[:A_WIDTH, :],
            preferred_element_type=F32)

    def finalise(self):
        tq, par, acc_scr = SEQ_TILE, self.par, self.acc_scr
        lam = (jnp.exp(jnp.sum(self.lq1_ref[...] * self.lk1_ref[...], axis=-1, keepdims=True))
               - jnp.exp(jnp.sum(self.lq2_ref[...] * self.lk2_ref[...], axis=-1, keepdims=True))
               + LAMBDA_INIT)
        for g in range(B_HEADS):
            inv_l1 = 1.0 / acc_scr[par, 2 * g, B_V_DIM:B_V_DIM + SUBLANES, :]
            inv_l2 = 1.0 / acc_scr[par, 2 * g + 1, B_V_DIM:B_V_DIM + SUBLANES, :]
            acc1 = acc_scr[par, 2 * g, :B_V_DIM, :].reshape(B_V_DIM // SUBLANES, SUBLANES, tq)
            acc2 = acc_scr[par, 2 * g + 1, :B_V_DIM, :].reshape(
                B_V_DIM // SUBLANES, SUBLANES, tq)
            ot = acc1 * inv_l1[None] - lam * (acc2 * inv_l2[None])
            o = ot.reshape(B_V_DIM, tq).T
            o = (o * lax.rsqrt(jnp.mean(o * o, axis=-1, keepdims=True) + SUBLN_EPS)
                 * self.subln_ref[...])
            o = o * (1.0 - LAMBDA_INIT)
            vsl = slice(g * B_V_DIM, (g + 1) * B_V_DIM)
            b_out = o * self.zb_gate[:, vsl]
            self.mixed_scr[par, :, A_WIDTH + g * B_V_DIM:A_WIDTH + (g + 1) * B_V_DIM] = (
                b_out.astype(BF16))

    def out_b(self):
        self.out_ref[0, self.tile_rows, :] += self.gate * jnp.dot(
            self.mixed_scr[self.par, :, A_WIDTH:], self.wout_ref[A_WIDTH:, :],
            preferred_element_type=F32)


def kernel(x, c, positions, norm_w, w_ada, b_ada, w_in, sgu_norm_w, w_s, b_s, q_norm_w, k_norm_w,
           lambda_q1, lambda_k1, lambda_q2, lambda_k2, subln_w, w_out):
    bsz, seq, d = x.shape
    assert d == D_MODEL and seq % (TILES_PER_STEP * SEQ_TILE) == 0 and norm_w.shape[0] == 1
    assert TILES_PER_STEP % 2 == 0
    nt = seq // SEQ_TILE

    mod = pl.pallas_call(
        _adaln_kernel,
        grid=(3 * ADALN_SPLIT,),
        in_specs=[pl.BlockSpec((bsz, d), lambda n: (0, 0)),
                  pl.BlockSpec((None, d, d // ADALN_SPLIT), lambda n: (0, 0, n)),
                  pl.BlockSpec((1, d // ADALN_SPLIT), lambda n: (0, n))],
        out_specs=pl.BlockSpec((None, bsz, d // ADALN_SPLIT),
                               lambda n: (n // ADALN_SPLIT, 0, n % ADALN_SPLIT)),
        out_shape=jax.ShapeDtypeStruct((3, bsz, d), F32),
        name="adaln_mod",
    )(c, w_ada, b_ada)

    n_grp = LANES // HALF
    pos_q = jnp.pad(positions.reshape(bsz, nt, n_grp, SEQ_TILE // n_grp),
                    ((0, 0), (0, 0), (0, SUBLANES - n_grp), (0, LANES - SEQ_TILE // n_grp)))

    inv_freq = ROPE_THETA ** (-jnp.arange(0, B_HEAD_DIM, 2, dtype=F32) / B_HEAD_DIM)
    invf = jnp.tile(inv_freq, LANES // HALF)[None, :]
    lane_head = (np.arange(MXU_DIM) // LANES) * 2 + (np.arange(MXU_DIM) // HALF) % 2
    bd = jnp.asarray(lane_head[:, None] == lane_head[None, :], dtype=BF16)

    qnw = _pair_layout(jnp.tile(q_norm_w.reshape(B_HEAD_DIM), 2 * B_HEADS))[None, :]
    knw = _pair_layout(jnp.tile(k_norm_w.reshape(B_HEAD_DIM), 2 * B_HEADS))[None, :]

    const2 = lambda b, i: (0, 0)
    const3 = lambda b, i: (0, 0, 0)
    step_rows = TILES_PER_STEP * SEQ_TILE
    out = pl.pallas_call(
        _layer_kernel,
        grid=(bsz, nt // TILES_PER_STEP),
        in_specs=[
            pl.BlockSpec((1, step_rows, d), lambda b, i: (b, i, 0)),
            pl.BlockSpec((3, bsz, d), const3),
            pl.BlockSpec((None, TILES_PER_STEP, SUBLANES, LANES),
                         lambda b, i: (b, i, 0, 0)),
            pl.BlockSpec((1, LANES), const2),
            pl.BlockSpec((1, d), const2),
            pl.BlockSpec(memory_space=pl.ANY),
            pl.BlockSpec((A_HEADS, A_HEAD_DIM), const2),
            pl.BlockSpec((A_HEADS, CHUNK, CHUNK), const3),
            pl.BlockSpec((CHUNK, A_HEADS), const2),
            pl.BlockSpec((1, B_WIDTH), const2),
            pl.BlockSpec((1, B_WIDTH), const2),
            pl.BlockSpec((1, B_HEAD_DIM), const2),
            pl.BlockSpec((1, B_HEAD_DIM), const2),
            pl.BlockSpec((1, B_HEAD_DIM), const2),
            pl.BlockSpec((1, B_HEAD_DIM), const2),
            pl.BlockSpec((1, B_V_DIM), const2),
            pl.BlockSpec(memory_space=pl.ANY),
            pl.BlockSpec((MXU_DIM, MXU_DIM), const2),
        ],
        out_specs=pl.BlockSpec((1, step_rows, d), lambda b, i: (b, i, 0)),
        out_shape=jax.ShapeDtypeStruct((bsz, seq, d), F32),
        scratch_shapes=[
            pltpu.VMEM((nt, SEQ_TILE, B_WIDTH), BF16),
            pltpu.VMEM((nt, B_HEADS, V_EXT, SEQ_TILE), BF16),
            pltpu.VMEM((B_HEADS, LANES, 2 * SEQ_TILE), BF16),
            pltpu.VMEM((2, B_HEADS, SUBLANES, 2 * SEQ_TILE), F32),
            pltpu.VMEM((2, 2 * B_HEADS, V_EXT, SEQ_TILE), F32),
            pltpu.VMEM((2, SEQ_TILE, A_WIDTH + B_WIDTH), BF16),
            pltpu.VMEM((B_HEADS, SEQ_TILE, 2 * SEQ_TILE), F32),
            pltpu.VMEM((2, SEQ_TILE, 2 * SEQ_TILE), BF16),
            pltpu.VMEM((2, SUBLANES, 2 * SEQ_TILE), F32),
            pltpu.VMEM((B_HEADS, SUBLANES, 2 * SEQ_TILE), F32),
            pltpu.VMEM((d, IN_COLS), BF16),
            pltpu.VMEM((A_WIDTH + B_WIDTH, d), BF16),
            pltpu.VMEM((WEIGHT_STAGE_SLOTS, WEIGHT_CHUNK_ROWS, IN_COLS), F32),
            pltpu.SemaphoreType.DMA((WEIGHT_STAGE_SLOTS,)),
        ],
        compiler_params=pltpu.CompilerParams(
            dimension_semantics=("arbitrary", "arbitrary"),
            vmem_limit_bytes=VMEM_LIMIT_BYTES),
        name="fused_layer",
    )(x, mod, pos_q, invf, norm_w, w_in,
      sgu_norm_w.reshape(A_HEADS, A_HEAD_DIM), w_s.reshape(A_HEADS, CHUNK, CHUNK),
      jnp.transpose(b_s.reshape(A_HEADS, CHUNK)), qnw, knw,
      lambda_q1, lambda_k1, lambda_q2, lambda_k2, subln_w, w_out, bd)
    return out
```

```python
import math

import numpy as np
import jax
import jax.numpy as jnp
from jax import lax
from jax.experimental import pallas as pl
from jax.experimental.pallas import tpu as pltpu

D_MODEL = 1024
A_WIDTH = 512
B_WIDTH = 512
A_HEADS = 4
A_HEAD_DIM = 128
CHUNK = 128
B_HEADS = 4
B_HEAD_DIM = 64
B_V_DIM = 128
ROPE_THETA = 10000.0
NORM_EPS = 1e-6
SUBLN_EPS = 1e-5
IN_COLS = 3 * A_WIDTH + 4 * B_WIDTH
LAMBDA_INIT = 0.8 - 0.6 * math.exp(-0.3 * 0)
LOG2E = math.log2(math.e)

LANES = 128
SUBLANES = 8
MXU_DIM = 256
SEQ_TILE = 256
TILES_PER_STEP = 4
VMEM_LIMIT_BYTES = 56 * 1024 * 1024
HALF = B_HEAD_DIM // 2
BF16_ROWS = 16
V_EXT = B_V_DIM + BF16_ROWS
WEIGHT_CHUNK_ROWS = 64
WEIGHT_STAGE_SLOTS = 4

F32 = jnp.float32
BF16 = jnp.bfloat16


def _load_weights(w_in_hbm, w_out_hbm, win_scr, wout_scr, stage, sem):
    ch = stage.shape[1]
    grp = lax.broadcasted_iota(jnp.int32, (1, LANES), 1) // HALF
    q0, v0 = 3 * A_WIDTH, 3 * A_WIDTH + 2 * B_WIDTH

    def stream(src_hbm, dst_scr, n_cols, reorder):
        n_chunks = src_hbm.shape[1] // ch

        def chunk_copy(c, slot):
            return pltpu.make_async_copy(src_hbm.at[0, pl.ds(c * ch, ch), :],
                                         stage.at[slot, :, pl.ds(0, n_cols)], sem.at[slot])

        depth = stage.shape[0]
        for c0 in range(depth - 1):
            chunk_copy(c0, c0).start()

        def body(c, carry):
            slot = c % depth
            ahead = c + depth - 1

            @pl.when(ahead < n_chunks)
            def _():
                chunk_copy(ahead, ahead % depth).start()

            chunk_copy(c, slot).wait()
            rows = pl.ds(pl.multiple_of(c * ch, ch), ch)
            for cb in range(n_cols // LANES):
                t = stage[slot, :, cb * LANES:(cb + 1) * LANES]
                if reorder and q0 <= cb * LANES < v0:
                    t = jnp.where(grp == 1, pltpu.roll(t, LANES - HALF, 1),
                                  jnp.where(grp == 2, pltpu.roll(t, HALF, 1), t))
                dst_scr[rows, cb * LANES:(cb + 1) * LANES] = t.astype(BF16)
            return carry

        lax.fori_loop(0, n_chunks, body, 0)

    stream(w_in_hbm, win_scr, IN_COLS, True)
    stream(w_out_hbm, wout_scr, D_MODEL, False)


def _adaln_kernel(c_ref, w_ref, b_ref, mod_ref):
    c = c_ref[...]
    c_act = c * jax.nn.sigmoid(c)
    mod_ref[...] = jnp.dot(c_act.astype(BF16), w_ref[...].astype(BF16),
                           preferred_element_type=F32) + b_ref[...]


def _silu(z):
    return z * jax.nn.sigmoid(z)


def _head_mean_sq(t, bd_ref):
    sq = (t * t).astype(BF16)
    halves = [jnp.dot(sq[:, h * MXU_DIM:(h + 1) * MXU_DIM], bd_ref[...],
                      preferred_element_type=F32) for h in range(B_WIDTH // MXU_DIM)]
    return jnp.concatenate(halves, axis=-1) * (1.0 / B_HEAD_DIM)


def _rope(t, cos, sin_signed):
    cols = []
    for cb in range(B_WIDTH // LANES):
        tc = t[:, cb * LANES:(cb + 1) * LANES]
        cols.append(tc * cos + pltpu.roll(tc, LANES // 2, 1) * sin_signed)
    return jnp.concatenate(cols, axis=-1)


def _rope_tables(pos_quarters, invf, lane):
    n_grp = LANES // HALF
    qrows = SEQ_TILE // n_grp
    grp = lane // HALF
    pos_t = pos_quarters.T
    packed = pos_t[:qrows, n_grp - 1:n_grp]
    for gi in range(n_grp - 2, -1, -1):
        packed = jnp.where(grp == gi, pos_t[:qrows, gi:gi + 1], packed)
    ang = packed * invf
    tables = []
    for packed_tab in (jnp.cos(ang), jnp.sin(ang)):
        parts = []
        for gi in range(n_grp):
            t = jnp.where(grp == gi, packed_tab, 0.0)
            t = t + pltpu.roll(t, 2 * HALF, 1)
            parts.append(t + pltpu.roll(t, HALF, 1))
        tables.append(jnp.concatenate(parts, axis=0))
    return tables


def _norm_weight_lanes(w):
    lo, hi = w[:, :HALF], w[:, HALF:]
    block = jnp.concatenate([lo, lo, hi, hi], axis=1)
    return jnp.concatenate([block] * (B_WIDTH // LANES), axis=1)


def _across_sublanes(x, op):
    for shift in (4, 2, 1):
        x = op(x, pltpu.roll(x, shift, 0))
    return x


def _layer_kernel(x_ref, mod_ref, pos_ref, invf_ref, normw_ref, w_in_hbm, sgu_ref, ws_ref,
                  bs_ref, qnw_ref, knw_ref, lq1_ref, lk1_ref, lq2_ref, lk2_ref, subln_ref,
                  w_out_hbm, bd_ref, out_ref,
                  k_scr, vt_scr, qblk_scr, m_scr, acc_scr, mixed_scr, s_scr, p_scr,
                  alpha_scr, smax_scr, win_ref, wout_ref, stage_scr, stage_sem):
    @pl.when((pl.program_id(0) == 0) & (pl.program_id(1) == 0))
    def _():
        _load_weights(w_in_hbm, w_out_hbm, win_ref, wout_ref, stage_scr, stage_sem)

    refs = dict(locals())
    tiles = [_TilePhases(t, refs) for t in range(TILES_PER_STEP)]
    tiles[0].activations()
    tiles[0].qk_projections()
    tiles[0].before_loop()
    for t, tile in enumerate(tiles):
        nxt = tiles[t + 1] if t + 1 < TILES_PER_STEP else None
        tile.key_loop()
        tile.last_value_product_and_out_a()
        if nxt is not None:
            nxt.activations()
            nxt.qk_projections()
        tile.finalise()
        tile.out_b()
        if nxt is not None:
            nxt.before_loop()


class _TilePhases:
    def __init__(self, t, refs):
        self.__dict__.update(refs)
        self.t = t
        self.par = t % 2
        self.base = pl.program_id(1) * TILES_PER_STEP
        self.i = self.base + t
        self.tile_rows = slice(t * SEQ_TILE, (t + 1) * SEQ_TILE)

    def activations(self):
        x_ref, tile_rows = self.x_ref, self.tile_rows
        b = pl.program_id(0)
        shift = self.mod_ref[0, pl.ds(b, 1), :]
        scale = self.mod_ref[1, pl.ds(b, 1), :]
        self.gate = self.mod_ref[2, pl.ds(b, 1), :]
        sq_lanes = None
        for c0 in range(0, D_MODEL, LANES):
            x_c = x_ref[0, tile_rows, c0:c0 + LANES]
            sq_lanes = x_c * x_c if sq_lanes is None else sq_lanes + x_c * x_c
        sum_sq = jnp.sum(sq_lanes, axis=-1, keepdims=True)
        inv_rms = lax.rsqrt(sum_sq * (1.0 / D_MODEL) + NORM_EPS)
        hb_cols = []
        for c0 in range(0, D_MODEL, MXU_DIM):
            cs = slice(c0, c0 + MXU_DIM)
            h_c = ((x_ref[0, tile_rows, cs] * inv_rms * self.normw_ref[:, cs])
                   * (1.0 + scale[:, cs]) + shift[:, cs])
            hb_cols.append(h_c.astype(BF16))
        self.hb = jnp.concatenate(hb_cols, axis=1)

    def proj(self, c0, width):
        return jnp.dot(self.hb, self.win_ref[:, c0:c0 + width], preferred_element_type=F32)

    def qk_projections(self):
        self.qb = self.proj(3 * A_WIDTH, B_WIDTH)
        self.kb = self.proj(3 * A_WIDTH + B_WIDTH, B_WIDTH)

    def scores(self, g, j):
        tk, tq = SEQ_TILE, SEQ_TILE
        kp = self.k_scr[j, :, g * LANES:(g + 1) * LANES]
        s = jnp.dot(kp, self.qblk_scr[g], preferred_element_type=F32)
        self.s_scr[g] = s
        self.smax_scr[g] = jnp.max(s.reshape(tk // SUBLANES, SUBLANES, 2 * tq), axis=0)

    def softmax(self, g, slot, masked):
        tk, tq = SEQ_TILE, SEQ_TILE
        rows = 2 * SUBLANES
        s_scr, m_scr, par = self.s_scr, self.m_scr, self.par

        def chunk(c):
            sc = s_scr[g, c * rows:(c + 1) * rows, :]
            if masked:
                kv_row = lax.broadcasted_iota(jnp.int32, (rows, 2 * tq), 0) + c * rows
                q_col = lax.broadcasted_iota(jnp.int32, (rows, 2 * tq), 1) % tq
                sc = jnp.where(kv_row <= q_col, sc, -jnp.inf)
            return sc.reshape(2, SUBLANES, 2 * tq)

        if masked:
            col_max = jnp.max(chunk(0), axis=0)
            for c in range(1, tk // rows):
                col_max = jnp.maximum(col_max, jnp.max(chunk(c), axis=0))
        else:
            col_max = self.smax_scr[g]
        m_old = m_scr[par, g]
        m_new = jnp.maximum(m_old, _across_sublanes(col_max, jnp.maximum))
        alpha = jnp.exp2(m_old - m_new)
        m_scr[par, g] = m_new
        self.alpha_scr[slot] = alpha
        for c in range(tk // rows):
            pc = jnp.exp2(chunk(c) - m_new[None])
            self.p_scr[slot, c * rows:(c + 1) * rows, :] = (
                pc.reshape(rows, 2 * tq).astype(BF16))

    def value_update(self, g, j, slot):
        tq = SEQ_TILE
        acc_scr, par = self.acc_scr, self.par
        vt = self.vt_scr[j, g]
        alpha = self.alpha_scr[slot]
        for st in range(2):
            pv = jnp.dot(vt, self.p_scr[slot, :, st * tq:(st + 1) * tq],
                         preferred_element_type=F32)
            a = alpha[:, st * tq:(st + 1) * tq]
            acc = acc_scr[par, 2 * g + st].reshape(V_EXT // SUBLANES, SUBLANES, tq)
            acc_scr[par, 2 * g + st] = (acc * a[None]).reshape(V_EXT, tq) + pv

    def unit(self, g, masked, tile, succ_tile, prev_tile):
        ahead = (g + 2) % B_HEADS
        if g + 2 < B_HEADS:
            self.scores(ahead, tile)
        elif succ_tile is not None:
            self.scores(ahead, succ_tile)
        if prev_tile is not None:
            self.value_update((g - 1) % B_HEADS, prev_tile, (g - 1) % 2)
        self.softmax(g, g % 2, masked)

    def before_loop(self):
        tq, tk, i, par = SEQ_TILE, SEQ_TILE, self.i, self.par
        qb, kb = self.qb, self.kb
        lane = lax.broadcasted_iota(jnp.int32, (1, LANES), 1)
        cos, sin = _rope_tables(self.pos_ref[self.t].astype(F32), self.invf_ref[...], lane)
        sin_signed = jnp.where(lane < LANES // 2, -sin, sin)
        qnw = _norm_weight_lanes(self.qnw_ref[...])
        knw = _norm_weight_lanes(self.knw_ref[...])
        q = qb * lax.rsqrt(_head_mean_sq(qb, self.bd_ref) + NORM_EPS) * qnw
        k = kb * lax.rsqrt(_head_mean_sq(kb, self.bd_ref) + NORM_EPS) * knw
        q = _rope(q, cos, sin_signed) * (B_HEAD_DIM ** -0.5 * LOG2E)
        k = _rope(k, cos, sin_signed)
        self.k_scr[i] = k.astype(BF16)

        qt = q.T
        pair_row = lax.broadcasted_iota(jnp.int32, (LANES, tq), 0)
        is_a = (pair_row // HALF) % 2 == 0
        for g in range(B_HEADS):
            qp = qt[g * LANES:(g + 1) * LANES, :]
            self.qblk_scr[g] = jnp.concatenate(
                [jnp.where(is_a, qp, 0.0), jnp.where(is_a, 0.0, qp)], axis=1).astype(BF16)

        self.m_scr[par] = jnp.full(self.m_scr.shape[1:], -jnp.inf, F32)
        self.acc_scr[par] = jnp.zeros(self.acc_scr.shape[1:], F32)

        vb = self.proj(3 * A_WIDTH + 2 * B_WIDTH, B_WIDTH)
        vbt = vb.T.astype(BF16)
        for g in range(B_HEADS):
            self.vt_scr[i, g, :B_V_DIM, :] = vbt[g * B_V_DIM:(g + 1) * B_V_DIM, :]
            self.vt_scr[i, g, B_V_DIM:, :] = jnp.ones((V_EXT - B_V_DIM, tk), BF16)
        va = self.proj(A_WIDTH, A_WIDTH)
        self.scores(0, i)
        self.scores(1, i)
        ua = self.proj(0, A_WIDTH)
        self.unit(0, True, i, 0, None)
        za = self.proj(2 * A_WIDTH, A_WIDTH)
        self.unit(1, True, i, 0, i)
        zb = self.proj(3 * A_WIDTH + 3 * B_WIDTH, B_WIDTH)
        after_diag = i - 1 if self.t > 0 else 0
        self.unit(2, True, i, after_diag, i)

        row_c = lax.broadcasted_iota(jnp.int32, (CHUNK, CHUNK), 0)
        col_c = lax.broadcasted_iota(jnp.int32, (CHUNK, CHUNK), 1)
        tril = row_c >= col_c
        bs_pad = jnp.concatenate(
            [self.bs_ref[...], jnp.zeros((SUBLANES - A_HEADS, CHUNK), F32)], axis=0)
        bs_t = bs_pad.T
        for hh in range(A_HEADS):
            sl = slice(hh * A_HEAD_DIM, (hh + 1) * A_HEAD_DIM)
            v = va[:, sl]
            vn = (v * lax.rsqrt(jnp.mean(v * v, axis=-1, keepdims=True) + NORM_EPS)
                  * self.sgu_ref[hh:hh + 1, :])
            vnb = vn.astype(BF16)
            ws = jnp.where(tril, self.ws_ref[hh], 0.0).astype(BF16)
            bias = bs_t[:, hh:hh + 1]
            n_chunks = tq // CHUNK
            chunks = jnp.concatenate(
                [vnb[c * CHUNK:(c + 1) * CHUNK, :] for c in range(n_chunks)], axis=1)
            mixed_chunks = jnp.dot(ws, chunks, preferred_element_type=F32)
            mix = jnp.concatenate([mixed_chunks[:, c * A_HEAD_DIM:(c + 1) * A_HEAD_DIM] + bias
                                   for c in range(n_chunks)], axis=0)
            a_out = ua[:, sl] * mix * _silu(za[:, sl])
            self.mixed_scr[par, :, sl] = a_out.astype(BF16)
        self.zb_gate = _silu(zb)
        self.unit(3, True, i, after_diag, i)
        for s in range(self.t):
            tile = i - 1 - s
            succ = tile - 1 if s + 1 < self.t else 0
            self.tile_units(tile, succ, tile + 1)

    def tile_units(self, tile, succ_tile, prev_tile):
        self.unit(0, False, tile, succ_tile, prev_tile)
        for g in range(1, B_HEADS):
            self.unit(g, False, tile, succ_tile, tile)

    def key_loop(self):
        base = self.base

        def pair_body(jj, carry):
            j = 2 * jj
            self.tile_units(j, j + 1, jnp.where(j == 0, base, j - 1))
            self.tile_units(j + 1, j + 2, j)
            return carry

        lax.fori_loop(0, base // 2, pair_body, 0)

    def last_value_product_and_out_a(self):
        last = B_HEADS - 1
        base = self.base
        self.value_update(last, jnp.where(base > 0, base - 1, base), last % 2)
        rows = self.tile_rows
        self.out_ref[0, rows, :] = self.x_ref[0, rows, :] + self.gate * jnp.dot(
            self.mixed_scr[self.par, :, :A_WIDTH], self.wout_ref[:A_WIDTH, :],
            preferred_element_type=F32)

    def finalise(self):
        tq, par, acc_scr = SEQ_TILE, self.par, self.acc_scr
        lam = (jnp.exp(jnp.sum(self.lq1_ref[...] * self.lk1_ref[...], axis=-1, keepdims=True))
               - jnp.exp(jnp.sum(self.lq2_ref[...] * self.lk2_ref[...], axis=-1, keepdims=True))
               + LAMBDA_INIT)
        for g in range(B_HEADS):
            inv_l1 = 1.0 / acc_scr[par, 2 * g, B_V_DIM:B_V_DIM + SUBLANES, :]
            inv_l2 = 1.0 / acc_scr[par, 2 * g + 1, B_V_DIM:B_V_DIM + SUBLANES, :]
            acc1 = acc_scr[par, 2 * g, :B_V_DIM, :].reshape(B_V_DIM // SUBLANES, SUBLANES, tq)
            acc2 = acc_scr[par, 2 * g + 1, :B_V_DIM, :].reshape(
                B_V_DIM // SUBLANES, SUBLANES, tq)
            ot = acc1 * inv_l1[None] - lam * (acc2 * inv_l2[None])
            o = ot.reshape(B_V_DIM, tq).T
            o = (o * lax.rsqrt(jnp.mean(o * o, axis=-1, keepdims=True) + SUBLN_EPS)
                 * self.subln_ref[...])
            o = o * (1.0 - LAMBDA_INIT)
            vsl = slice(g * B_V_DIM, (g + 1) * B_V_DIM)
            b_out = o * self.zb_gate[:, vsl]
            self.mixed_scr[par, :, A_WIDTH + g * B_V_DIM:A_WIDTH + (g + 1) * B_V_DIM] = (
                b_out.astype(BF16))

    def out_b(self):
        self.out_ref[0, self.tile_rows, :] += self.gate * jnp.dot(
            self.mixed_scr[self.par, :, A_WIDTH:], self.wout_ref[A_WIDTH:, :],
            preferred_element_type=F32)


def kernel(x, c, positions, norm_w, w_ada, b_ada, w_in, sgu_norm_w, w_s, b_s, q_norm_w, k_norm_w,
           lambda_q1, lambda_k1, lambda_q2, lambda_k2, subln_w, w_out):
    bsz, seq, d = x.shape
    assert d == D_MODEL and seq % (TILES_PER_STEP * SEQ_TILE) == 0 and norm_w.shape[0] == 1
    assert TILES_PER_STEP % 2 == 0
    nt = seq // SEQ_TILE

    mod = pl.pallas_call(
        _adaln_kernel,
        grid=(3,),
        in_specs=[pl.BlockSpec((bsz, d), lambda n: (0, 0)),
                  pl.BlockSpec((None, d, d), lambda n: (0, 0, n)),
                  pl.BlockSpec((1, d), lambda n: (0, n))],
        out_specs=pl.BlockSpec((None, bsz, d), lambda n: (n, 0, 0)),
        out_shape=jax.ShapeDtypeStruct((3, bsz, d), F32),
        name="adaln_mod",
    )(c, w_ada, b_ada)

    n_grp = LANES // HALF
    pos_q = jnp.pad(positions.reshape(bsz, nt, n_grp, SEQ_TILE // n_grp),
                    ((0, 0), (0, 0), (0, SUBLANES - n_grp), (0, LANES - SEQ_TILE // n_grp)))

    inv_freq = ROPE_THETA ** (-jnp.arange(0, B_HEAD_DIM, 2, dtype=F32) / B_HEAD_DIM)
    invf = jnp.tile(inv_freq, LANES // HALF)[None, :]
    lane_head = (np.arange(MXU_DIM) // LANES) * 2 + (np.arange(MXU_DIM) // HALF) % 2
    bd = jnp.asarray(lane_head[:, None] == lane_head[None, :], dtype=BF16)


    const2 = lambda b, i: (0, 0)
    const3 = lambda b, i: (0, 0, 0)
    step_rows = TILES_PER_STEP * SEQ_TILE
    out = pl.pallas_call(
        _layer_kernel,
        grid=(bsz, nt // TILES_PER_STEP),
        in_specs=[
            pl.BlockSpec((1, step_rows, d), lambda b, i: (b, i, 0)),
            pl.BlockSpec((3, bsz, d), const3),
            pl.BlockSpec((None, TILES_PER_STEP, SUBLANES, LANES),
                         lambda b, i: (b, i, 0, 0)),
            pl.BlockSpec((1, LANES), const2),
            pl.BlockSpec((1, d), const2),
            pl.BlockSpec(memory_space=pl.ANY),
            pl.BlockSpec((A_HEADS, A_HEAD_DIM), const2),
            pl.BlockSpec((A_HEADS, CHUNK, CHUNK), const3),
            pl.BlockSpec((A_HEADS, CHUNK), const2),
            pl.BlockSpec((1, B_HEAD_DIM), const2),
            pl.BlockSpec((1, B_HEAD_DIM), const2),
            pl.BlockSpec((1, B_HEAD_DIM), const2),
            pl.BlockSpec((1, B_HEAD_DIM), const2),
            pl.BlockSpec((1, B_HEAD_DIM), const2),
            pl.BlockSpec((1, B_HEAD_DIM), const2),
            pl.BlockSpec((1, B_V_DIM), const2),
            pl.BlockSpec(memory_space=pl.ANY),
            pl.BlockSpec((MXU_DIM, MXU_DIM), const2),
        ],
        out_specs=pl.BlockSpec((1, step_rows, d), lambda b, i: (b, i, 0)),
        out_shape=jax.ShapeDtypeStruct((bsz, seq, d), F32),
        scratch_shapes=[
            pltpu.VMEM((nt, SEQ_TILE, B_WIDTH), BF16),
            pltpu.VMEM((nt, B_HEADS, V_EXT, SEQ_TILE), BF16),
            pltpu.VMEM((B_HEADS, LANES, 2 * SEQ_TILE), BF16),
            pltpu.VMEM((2, B_HEADS, SUBLANES, 2 * SEQ_TILE), F32),
            pltpu.VMEM((2, 2 * B_HEADS, V_EXT, SEQ_TILE), F32),
            pltpu.VMEM((2, SEQ_TILE, A_WIDTH + B_WIDTH), BF16),
            pltpu.VMEM((B_HEADS, SEQ_TILE, 2 * SEQ_TILE), F32),
            pltpu.VMEM((2, SEQ_TILE, 2 * SEQ_TILE), BF16),
            pltpu.VMEM((2, SUBLANES, 2 * SEQ_TILE), F32),
            pltpu.VMEM((B_HEADS, SUBLANES, 2 * SEQ_TILE), F32),
            pltpu.VMEM((d, IN_COLS), BF16),
            pltpu.VMEM((A_WIDTH + B_WIDTH, d), BF16),
            pltpu.VMEM((WEIGHT_STAGE_SLOTS, WEIGHT_CHUNK_ROWS, IN_COLS), F32),
            pltpu.SemaphoreType.DMA((WEIGHT_STAGE_SLOTS,)),
        ],
        compiler_params=pltpu.CompilerParams(
            dimension_semantics=("arbitrary", "arbitrary"),
            vmem_limit_bytes=VMEM_LIMIT_BYTES),
        name="fused_layer",
    )(x, mod, pos_q, invf, norm_w, w_in,
      sgu_norm_w.reshape(A_HEADS, A_HEAD_DIM), w_s.reshape(A_HEADS, CHUNK, CHUNK),
      b_s.reshape(A_HEADS, CHUNK), q_norm_w, k_norm_w,
      lambda_q1, lambda_k1, lambda_q2, lambda_k2, subln_w, w_out, bd)
    return out
```

```python
import math

import numpy as np
import jax
import jax.numpy as jnp
from jax import lax
from jax.experimental import pallas as pl
from jax.experimental.pallas import tpu as pltpu

D_MODEL = 1024
A_WIDTH = 512
B_WIDTH = 512
A_HEADS = 4
A_HEAD_DIM = 128
CHUNK = 128
B_HEADS = 4
B_HEAD_DIM = 64
B_V_DIM = 128
ROPE_THETA = 10000.0
NORM_EPS = 1e-6
SUBLN_EPS = 1e-5
IN_COLS = 3 * A_WIDTH + 4 * B_WIDTH
LAMBDA_INIT = 0.8 - 0.6 * math.exp(-0.3 * 0)
LOG2E = math.log2(math.e)

LANES = 128
SUBLANES = 8
MXU_DIM = 256
SEQ_TILE = 256
TILES_PER_STEP = 4
VMEM_LIMIT_BYTES = 56 * 1024 * 1024
HALF = B_HEAD_DIM // 2
BF16_ROWS = 16
V_EXT = B_V_DIM + BF16_ROWS
WEIGHT_CHUNK_ROWS = 64
WEIGHT_STAGE_SLOTS = 4

F32 = jnp.float32
BF16 = jnp.bfloat16


def _pair_layout(t):
    lead = t.shape[:-1]
    t = t.reshape(lead + (B_HEADS, 2, 2, HALF))
    t = jnp.swapaxes(t, -3, -2)
    return t.reshape(lead + (B_WIDTH,))


def _load_weights(w_in_hbm, w_out_hbm, win_scr, wout_scr, stage, sem):
    ch = stage.shape[1]
    grp = lax.broadcasted_iota(jnp.int32, (1, LANES), 1) // HALF
    q0, v0 = 3 * A_WIDTH, 3 * A_WIDTH + 2 * B_WIDTH

    def stream(src_hbm, dst_scr, n_cols, reorder):
        n_chunks = src_hbm.shape[1] // ch

        def chunk_copy(c, slot):
            return pltpu.make_async_copy(src_hbm.at[0, pl.ds(c * ch, ch), :],
                                         stage.at[slot, :, pl.ds(0, n_cols)], sem.at[slot])

        depth = stage.shape[0]
        for c0 in range(depth - 1):
            chunk_copy(c0, c0).start()

        def body(c, carry):
            slot = c % depth
            ahead = c + depth - 1

            @pl.when(ahead < n_chunks)
            def _():
                chunk_copy(ahead, ahead % depth).start()

            chunk_copy(c, slot).wait()
            rows = pl.ds(pl.multiple_of(c * ch, ch), ch)
            for cb in range(n_cols // LANES):
                t = stage[slot, :, cb * LANES:(cb + 1) * LANES]
                if reorder and q0 <= cb * LANES < v0:
                    t = jnp.where(grp == 1, pltpu.roll(t, LANES - HALF, 1),
                                  jnp.where(grp == 2, pltpu.roll(t, HALF, 1), t))
                dst_scr[rows, cb * LANES:(cb + 1) * LANES] = t.astype(BF16)
            return carry

        lax.fori_loop(0, n_chunks, body, 0)

    stream(w_in_hbm, win_scr, IN_COLS, True)
    stream(w_out_hbm, wout_scr, D_MODEL, False)


def _adaln_kernel(c_ref, w_ref, b_ref, mod_ref):
    c = c_ref[...]
    c_act = c * jax.nn.sigmoid(c)
    mod_ref[...] = jnp.dot(c_act.astype(BF16), w_ref[...].astype(BF16),
                           preferred_element_type=F32) + b_ref[...]


def _silu(z):
    return z * jax.nn.sigmoid(z)


def _head_mean_sq(t, bd_ref):
    sq = (t * t).astype(BF16)
    halves = [jnp.dot(sq[:, h * MXU_DIM:(h + 1) * MXU_DIM], bd_ref[...],
                      preferred_element_type=F32) for h in range(B_WIDTH // MXU_DIM)]
    return jnp.concatenate(halves, axis=-1) * (1.0 / B_HEAD_DIM)


def _rope(t, cos, sin_signed):
    cols = []
    for cb in range(B_WIDTH // LANES):
        tc = t[:, cb * LANES:(cb + 1) * LANES]
        cols.append(tc * cos + pltpu.roll(tc, LANES // 2, 1) * sin_signed)
    return jnp.concatenate(cols, axis=-1)


def _rope_tables(pos_quarters, invf, lane):
    n_grp = LANES // HALF
    qrows = SEQ_TILE // n_grp
    grp = lane // HALF
    pos_t = pos_quarters.T
    packed = pos_t[:qrows, n_grp - 1:n_grp]
    for gi in range(n_grp - 2, -1, -1):
        packed = jnp.where(grp == gi, pos_t[:qrows, gi:gi + 1], packed)
    ang = packed * invf
    tables = []
    for packed_tab in (jnp.cos(ang), jnp.sin(ang)):
        parts = []
        for gi in range(n_grp):
            t = jnp.where(grp == gi, packed_tab, 0.0)
            t = t + pltpu.roll(t, 2 * HALF, 1)
            parts.append(t + pltpu.roll(t, HALF, 1))
        tables.append(jnp.concatenate(parts, axis=0))
    return tables


def _across_sublanes(x, op):
    for shift in (4, 2, 1):
        x = op(x, pltpu.roll(x, shift, 0))
    return x


def _layer_kernel(x_ref, mod_ref, pos_ref, invf_ref, normw_ref, w_in_hbm, sgu_ref, ws_ref,
                  bst_ref, qnw_ref, knw_ref, lq1_ref, lk1_ref, lq2_ref, lk2_ref, subln_ref,
                  w_out_hbm, bd_ref, out_ref,
                  k_scr, vt_scr, qblk_scr, m_scr, acc_scr, mixed_scr, s_scr, p_scr,
                  alpha_scr, smax_scr, win_ref, wout_ref, stage_scr, stage_sem):
    @pl.when((pl.program_id(0) == 0) & (pl.program_id(1) == 0))
    def _():
        _load_weights(w_in_hbm, w_out_hbm, win_ref, wout_ref, stage_scr, stage_sem)

    refs = dict(locals())
    tiles = [_TilePhases(t, refs) for t in range(TILES_PER_STEP)]
    tiles[0].activations()
    tiles[0].qk_projections()
    tiles[0].before_loop()
    for t, tile in enumerate(tiles):
        nxt = tiles[t + 1] if t + 1 < TILES_PER_STEP else None
        tile.key_loop()
        tile.own_step_tiles()
        tile.last_value_product_and_out_a()
        if nxt is not None:
            nxt.activations()
            nxt.qk_projections()
        tile.finalise()
        tile.out_b()
        if nxt is not None:
            nxt.before_loop()


class _TilePhases:
    def __init__(self, t, refs):
        self.__dict__.update(refs)
        self.t = t
        self.par = t % 2
        self.base = pl.program_id(1) * TILES_PER_STEP
        self.i = self.base + t
        self.tile_rows = slice(t * SEQ_TILE, (t + 1) * SEQ_TILE)

    def activations(self):
        x_ref, tile_rows = self.x_ref, self.tile_rows
        b = pl.program_id(0)
        shift = self.mod_ref[0, pl.ds(b, 1), :]
        scale = self.mod_ref[1, pl.ds(b, 1), :]
        self.gate = self.mod_ref[2, pl.ds(b, 1), :]
        sq_lanes = None
        for c0 in range(0, D_MODEL, LANES):
            x_c = x_ref[0, tile_rows, c0:c0 + LANES]
            sq_lanes = x_c * x_c if sq_lanes is None else sq_lanes + x_c * x_c
        sum_sq = jnp.sum(sq_lanes, axis=-1, keepdims=True)
        inv_rms = lax.rsqrt(sum_sq * (1.0 / D_MODEL) + NORM_EPS)
        hb_cols = []
        for c0 in range(0, D_MODEL, MXU_DIM):
            cs = slice(c0, c0 + MXU_DIM)
            h_c = ((x_ref[0, tile_rows, cs] * inv_rms * self.normw_ref[:, cs])
                   * (1.0 + scale[:, cs]) + shift[:, cs])
            hb_cols.append(h_c.astype(BF16))
        self.hb = jnp.concatenate(hb_cols, axis=1)

    def proj(self, c0, width):
        return jnp.dot(self.hb, self.win_ref[:, c0:c0 + width], preferred_element_type=F32)

    def qk_projections(self):
        self.qb = self.proj(3 * A_WIDTH, B_WIDTH)
        self.kb = self.proj(3 * A_WIDTH + B_WIDTH, B_WIDTH)

    def scores(self, g, j):
        tk, tq = SEQ_TILE, SEQ_TILE
        kp = self.k_scr[j, :, g * LANES:(g + 1) * LANES]
        s = jnp.dot(kp, self.qblk_scr[g], preferred_element_type=F32)
        self.s_scr[g] = s
        self.smax_scr[g] = jnp.max(s.reshape(tk // SUBLANES, SUBLANES, 2 * tq), axis=0)

    def softmax(self, g, slot, masked):
        tk, tq = SEQ_TILE, SEQ_TILE
        rows = 2 * SUBLANES
        s_scr, m_scr, par = self.s_scr, self.m_scr, self.par

        def chunk(c):
            sc = s_scr[g, c * rows:(c + 1) * rows, :]
            if masked:
                kv_row = lax.broadcasted_iota(jnp.int32, (rows, 2 * tq), 0) + c * rows
                q_col = lax.broadcasted_iota(jnp.int32, (rows, 2 * tq), 1) % tq
                sc = jnp.where(kv_row <= q_col, sc, -jnp.inf)
            return sc.reshape(2, SUBLANES, 2 * tq)

        if masked:
            col_max = jnp.max(chunk(0), axis=0)
            for c in range(1, tk // rows):
                col_max = jnp.maximum(col_max, jnp.max(chunk(c), axis=0))
        else:
            col_max = self.smax_scr[g]
        m_old = m_scr[par, g]
        m_new = jnp.maximum(m_old, _across_sublanes(col_max, jnp.maximum))
        alpha = jnp.exp2(m_old - m_new)
        m_scr[par, g] = m_new
        self.alpha_scr[slot] = alpha
        for c in range(tk // rows):
            pc = jnp.exp2(chunk(c) - m_new[None])
            self.p_scr[slot, c * rows:(c + 1) * rows, :] = (
                pc.reshape(rows, 2 * tq).astype(BF16))

    def value_update(self, g, j, slot):
        tq = SEQ_TILE
        acc_scr, par = self.acc_scr, self.par
        vt = self.vt_scr[j, g]
        alpha = self.alpha_scr[slot]
        for st in range(2):
            pv = jnp.dot(vt, self.p_scr[slot, :, st * tq:(st + 1) * tq],
                         preferred_element_type=F32)
            a = alpha[:, st * tq:(st + 1) * tq]
            acc = acc_scr[par, 2 * g + st].reshape(V_EXT // SUBLANES, SUBLANES, tq)
            acc_scr[par, 2 * g + st] = (acc * a[None]).reshape(V_EXT, tq) + pv

    def unit(self, g, masked, tile, succ_tile, prev_tile):
        ahead = (g + 2) % B_HEADS
        if g + 2 < B_HEADS:
            self.scores(ahead, tile)
        elif succ_tile is not None:
            self.scores(ahead, succ_tile)
        if prev_tile is not None:
            self.value_update((g - 1) % B_HEADS, prev_tile, (g - 1) % 2)
        self.softmax(g, g % 2, masked)

    def before_loop(self):
        tq, tk, i, par = SEQ_TILE, SEQ_TILE, self.i, self.par
        qb, kb = self.qb, self.kb
        lane = lax.broadcasted_iota(jnp.int32, (1, LANES), 1)
        cos, sin = _rope_tables(self.pos_ref[self.t].astype(F32), self.invf_ref[...], lane)
        sin_signed = jnp.where(lane < LANES // 2, -sin, sin)
        q = qb * lax.rsqrt(_head_mean_sq(qb, self.bd_ref) + NORM_EPS) * self.qnw_ref[...]
        k = kb * lax.rsqrt(_head_mean_sq(kb, self.bd_ref) + NORM_EPS) * self.knw_ref[...]
        q = _rope(q, cos, sin_signed) * (B_HEAD_DIM ** -0.5 * LOG2E)
        k = _rope(k, cos, sin_signed)
        self.k_scr[i] = k.astype(BF16)

        qt = q.T
        pair_row = lax.broadcasted_iota(jnp.int32, (LANES, tq), 0)
        is_a = (pair_row // HALF) % 2 == 0
        for g in range(B_HEADS):
            qp = qt[g * LANES:(g + 1) * LANES, :]
            self.qblk_scr[g] = jnp.concatenate(
                [jnp.where(is_a, qp, 0.0), jnp.where(is_a, 0.0, qp)], axis=1).astype(BF16)

        self.m_scr[par] = jnp.full(self.m_scr.shape[1:], -jnp.inf, F32)
        self.acc_scr[par] = jnp.zeros(self.acc_scr.shape[1:], F32)

        vb = self.proj(3 * A_WIDTH + 2 * B_WIDTH, B_WIDTH)
        vbt = vb.T.astype(BF16)
        for g in range(B_HEADS):
            self.vt_scr[i, g, :B_V_DIM, :] = vbt[g * B_V_DIM:(g + 1) * B_V_DIM, :]
            self.vt_scr[i, g, B_V_DIM:, :] = jnp.ones((V_EXT - B_V_DIM, tk), BF16)
        va = self.proj(A_WIDTH, A_WIDTH)
        self.scores(0, i)
        self.scores(1, i)
        ua = self.proj(0, A_WIDTH)
        self.unit(0, True, i, 0, None)
        za = self.proj(2 * A_WIDTH, A_WIDTH)
        self.unit(1, True, i, 0, i)
        zb = self.proj(3 * A_WIDTH + 3 * B_WIDTH, B_WIDTH)
        self.unit(2, True, i, 0, i)

        row_c = lax.broadcasted_iota(jnp.int32, (CHUNK, CHUNK), 0)
        col_c = lax.broadcasted_iota(jnp.int32, (CHUNK, CHUNK), 1)
        tril = row_c >= col_c
        for hh in range(A_HEADS):
            sl = slice(hh * A_HEAD_DIM, (hh + 1) * A_HEAD_DIM)
            v = va[:, sl]
            vn = (v * lax.rsqrt(jnp.mean(v * v, axis=-1, keepdims=True) + NORM_EPS)
                  * self.sgu_ref[hh:hh + 1, :])
            vnb = vn.astype(BF16)
            ws = jnp.where(tril, self.ws_ref[hh], 0.0).astype(BF16)
            bias = self.bst_ref[:, hh:hh + 1]
            n_chunks = tq // CHUNK
            chunks = jnp.concatenate(
                [vnb[c * CHUNK:(c + 1) * CHUNK, :] for c in range(n_chunks)], axis=1)
            mixed_chunks = jnp.dot(ws, chunks, preferred_element_type=F32)
            mix = jnp.concatenate([mixed_chunks[:, c * A_HEAD_DIM:(c + 1) * A_HEAD_DIM] + bias
                                   for c in range(n_chunks)], axis=0)
            a_out = ua[:, sl] * mix * _silu(za[:, sl])
            self.mixed_scr[par, :, sl] = a_out.astype(BF16)
        self.zb_gate = _silu(zb)
        self.unit(3, True, i, 0, i)

    def own_step_tiles(self):
        base, i = self.base, self.i
        for s in range(self.t):
            tile = base + s
            prev = tile - 1 if s > 0 else jnp.where(base > 0, base - 1, i)
            succ = tile + 1 if s + 1 < self.t else None
            self.tile_units(tile, succ, prev)

    def tile_units(self, tile, succ_tile, prev_tile):
        self.unit(0, False, tile, succ_tile, prev_tile)
        for g in range(1, B_HEADS):
            self.unit(g, False, tile, succ_tile, tile)

    def key_loop(self):
        base, i = self.base, self.i

        def pair_body(jj, carry):
            j = 2 * jj
            self.tile_units(j, j + 1, jnp.where(j == 0, i, j - 1))
            self.tile_units(j + 1, j + 2, j)
            return carry

        lax.fori_loop(0, base // 2, pair_body, 0)

    def last_value_product_and_out_a(self):
        last = B_HEADS - 1
        base, i = self.base, self.i
        final_tile = i - 1 if self.t > 0 else jnp.where(base > 0, base - 1, i)
        self.value_update(last, final_tile, last % 2)
        rows = self.tile_rows
        self.out_ref[0, rows, :] = self.x_ref[0, rows, :] + self.gate * jnp.dot(
            self.mixed_scr[self.par, :, :A_WIDTH], self.wout_ref[:A_WIDTH, :],
            preferred_element_type=F32)

    def finalise(self):
        tq, par, acc_scr = SEQ_TILE, self.par, self.acc_scr
        lam = (jnp.exp(jnp.sum(self.lq1_ref[...] * self.lk1_ref[...], axis=-1, keepdims=True))
               - jnp.exp(jnp.sum(self.lq2_ref[...] * self.lk2_ref[...], axis=-1, keepdims=True))
               + LAMBDA_INIT)
        for g in range(B_HEADS):
            inv_l1 = 1.0 / acc_scr[par, 2 * g, B_V_DIM:B_V_DIM + SUBLANES, :]
            inv_l2 = 1.0 / acc_scr[par, 2 * g + 1, B_V_DIM:B_V_DIM + SUBLANES, :]
            acc1 = acc_scr[par, 2 * g, :B_V_DIM, :].reshape(B_V_DIM // SUBLANES, SUBLANES, tq)
            acc2 = acc_scr[par, 2 * g + 1, :B_V_DIM, :].reshape(
                B_V_DIM // SUBLANES, SUBLANES, tq)
            ot = acc1 * inv_l1[None] - lam * (acc2 * inv_l2[None])
            o = ot.reshape(B_V_DIM, tq).T
            o = (o * lax.rsqrt(jnp.mean(o * o, axis=-1, keepdims=True) + SUBLN_EPS)
                 * self.subln_ref[...])
            o = o * (1.0 - LAMBDA_INIT)
            vsl = slice(g * B_V_DIM, (g + 1) * B_V_DIM)
            b_out = o * self.zb_gate[:, vsl]
            self.mixed_scr[par, :, A_WIDTH + g * B_V_DIM:A_WIDTH + (g + 1) * B_V_DIM] = (
                b_out.astype(BF16))

    def out_b(self):
        self.out_ref[0, self.tile_rows, :] += self.gate * jnp.dot(
            self.mixed_scr[self.par, :, A_WIDTH:], self.wout_ref[A_WIDTH:, :],
            preferred_element_type=F32)


def kernel(x, c, positions, norm_w, w_ada, b_ada, w_in, sgu_norm_w, w_s, b_s, q_norm_w, k_norm_w,
           lambda_q1, lambda_k1, lambda_q2, lambda_k2, subln_w, w_out):
    bsz, seq, d = x.shape
    assert d == D_MODEL and seq % (TILES_PER_STEP * SEQ_TILE) == 0 and norm_w.shape[0] == 1
    assert TILES_PER_STEP % 2 == 0
    nt = seq // SEQ_TILE

    mod = pl.pallas_call(
        _adaln_kernel,
        grid=(3,),
        in_specs=[pl.BlockSpec((bsz, d), lambda n: (0, 0)),
                  pl.BlockSpec((None, d, d), lambda n: (0, 0, n)),
                  pl.BlockSpec((1, d), lambda n: (0, n))],
        out_specs=pl.BlockSpec((None, bsz, d), lambda n: (n, 0, 0)),
        out_shape=jax.ShapeDtypeStruct((3, bsz, d), F32),
        name="adaln_mod",
    )(c, w_ada, b_ada)

    n_grp = LANES // HALF
    pos_q = jnp.pad(positions.reshape(bsz, nt, n_grp, SEQ_TILE // n_grp),
                    ((0, 0), (0, 0), (0, SUBLANES - n_grp), (0, LANES - SEQ_TILE // n_grp)))

    inv_freq = ROPE_THETA ** (-jnp.arange(0, B_HEAD_DIM, 2, dtype=F32) / B_HEAD_DIM)
    invf = jnp.tile(inv_freq, LANES // HALF)[None, :]
    lane_head = (np.arange(MXU_DIM) // LANES) * 2 + (np.arange(MXU_DIM) // HALF) % 2
    bd = jnp.asarray(lane_head[:, None] == lane_head[None, :], dtype=BF16)

    qnw = _pair_layout(jnp.tile(q_norm_w.reshape(B_HEAD_DIM), 2 * B_HEADS))[None, :]
    knw = _pair_layout(jnp.tile(k_norm_w.reshape(B_HEAD_DIM), 2 * B_HEADS))[None, :]

    const2 = lambda b, i: (0, 0)
    const3 = lambda b, i: (0, 0, 0)
    step_rows = TILES_PER_STEP * SEQ_TILE
    out = pl.pallas_call(
        _layer_kernel,
        grid=(bsz, nt // TILES_PER_STEP),
        in_specs=[
            pl.BlockSpec((1, step_rows, d), lambda b, i: (b, i, 0)),
            pl.BlockSpec((3, bsz, d), const3),
            pl.BlockSpec((None, TILES_PER_STEP, SUBLANES, LANES),
                         lambda b, i: (b, i, 0, 0)),
            pl.BlockSpec((1, LANES), const2),
            pl.BlockSpec((1, d), const2),
            pl.BlockSpec(memory_space=pl.ANY),
            pl.BlockSpec((A_HEADS, A_HEAD_DIM), const2),
            pl.BlockSpec((A_HEADS, CHUNK, CHUNK), const3),
            pl.BlockSpec((CHUNK, A_HEADS), const2),
            pl.BlockSpec((1, B_WIDTH), const2),
            pl.BlockSpec((1, B_WIDTH), const2),
            pl.BlockSpec((1, B_HEAD_DIM), const2),
            pl.BlockSpec((1, B_HEAD_DIM), const2),
            pl.BlockSpec((1, B_HEAD_DIM), const2),
            pl.BlockSpec((1, B_HEAD_DIM), const2),
            pl.BlockSpec((1, B_V_DIM), const2),
            pl.BlockSpec(memory_space=pl.ANY),
            pl.BlockSpec((MXU_DIM, MXU_DIM), const2),
        ],
        out_specs=pl.BlockSpec((1, step_rows, d), lambda b, i: (b, i, 0)),
        out_shape=jax.ShapeDtypeStruct((bsz, seq, d), F32),
        scratch_shapes=[
            pltpu.VMEM((nt, SEQ_TILE, B_WIDTH), BF16),
            pltpu.VMEM((nt, B_HEADS, V_EXT, SEQ_TILE), BF16),
            pltpu.VMEM((B_HEADS, LANES, 2 * SEQ_TILE), BF16),
            pltpu.VMEM((2, B_HEADS, SUBLANES, 2 * SEQ_TILE), F32),
            pltpu.VMEM((2, 2 * B_HEADS, V_EXT, SEQ_TILE), F32),
            pltpu.VMEM((2, SEQ_TILE, A_WIDTH + B_WIDTH), BF16),
            pltpu.VMEM((B_HEADS, SEQ_TILE, 2 * SEQ_TILE), F32),
            pltpu.VMEM((2, SEQ_TILE, 2 * SEQ_TILE), BF16),
            pltpu.VMEM((2, SUBLANES, 2 * SEQ_TILE), F32),
            pltpu.VMEM((B_HEADS, SUBLANES, 2 * SEQ_TILE), F32),
            pltpu.VMEM((d, IN_COLS), BF16),
            pltpu.VMEM((A_WIDTH + B_WIDTH, d), BF16),
            pltpu.VMEM((WEIGHT_STAGE_SLOTS, WEIGHT_CHUNK_ROWS, IN_COLS), F32),
            pltpu.SemaphoreType.DMA((WEIGHT_STAGE_SLOTS,)),
        ],
        compiler_params=pltpu.CompilerParams(
            dimension_semantics=("arbitrary", "arbitrary"),
            vmem_limit_bytes=VMEM_LIMIT_BYTES),
        name="fused_layer",
    )(x, mod, pos_q, invf, norm_w, w_in,
      sgu_norm_w.reshape(A_HEADS, A_HEAD_DIM), w_s.reshape(A_HEADS, CHUNK, CHUNK),
      jnp.transpose(b_s.reshape(A_HEADS, CHUNK)), qnw, knw,
      lambda_q1, lambda_k1, lambda_q2, lambda_k2, subln_w, w_out, bd)
    return out
```

```python
import math

import numpy as np
import jax
import jax.numpy as jnp
from jax import lax
from jax.experimental import pallas as pl
from jax.experimental.pallas import tpu as pltpu

D_MODEL = 1024
A_WIDTH = 512
B_WIDTH = 512
A_HEADS = 4
A_HEAD_DIM = 128
CHUNK = 128
B_HEADS = 4
B_HEAD_DIM = 64
B_V_DIM = 128
ROPE_THETA = 10000.0
NORM_EPS = 1e-6
SUBLN_EPS = 1e-5
IN_COLS = 3 * A_WIDTH + 4 * B_WIDTH
LAMBDA_INIT = 0.8 - 0.6 * math.exp(-0.3 * 0)
LOG2E = math.log2(math.e)

LANES = 128
SUBLANES = 8
MXU_DIM = 256
SEQ_TILE = 256
TILES_PER_STEP = 4
VMEM_LIMIT_BYTES = 56 * 1024 * 1024
HALF = B_HEAD_DIM // 2
BF16_ROWS = 16
V_EXT = B_V_DIM + BF16_ROWS
WEIGHT_CHUNK_ROWS = 64
WEIGHT_STAGE_SLOTS = 4

F32 = jnp.float32
BF16 = jnp.bfloat16


def _load_weights(w_in_hbm, w_out_hbm, win_scr, wout_scr, stage, sem):
    ch = stage.shape[1]
    grp = lax.broadcasted_iota(jnp.int32, (1, LANES), 1) // HALF
    q0, v0 = 3 * A_WIDTH, 3 * A_WIDTH + 2 * B_WIDTH

    def stream(src_hbm, dst_scr, n_cols, reorder):
        n_chunks = src_hbm.shape[1] // ch

        def chunk_copy(c, slot):
            return pltpu.make_async_copy(src_hbm.at[0, pl.ds(c * ch, ch), :],
                                         stage.at[slot, :, pl.ds(0, n_cols)], sem.at[slot])

        depth = stage.shape[0]
        for c0 in range(depth - 1):
            chunk_copy(c0, c0).start()

        def body(c, carry):
            slot = c % depth
            ahead = c + depth - 1

            @pl.when(ahead < n_chunks)
            def _():
                chunk_copy(ahead, ahead % depth).start()

            chunk_copy(c, slot).wait()
            rows = pl.ds(pl.multiple_of(c * ch, ch), ch)
            for cb in range(n_cols // LANES):
                t = stage[slot, :, cb * LANES:(cb + 1) * LANES]
                if reorder and q0 <= cb * LANES < v0:
                    t = jnp.where(grp == 1, pltpu.roll(t, LANES - HALF, 1),
                                  jnp.where(grp == 2, pltpu.roll(t, HALF, 1), t))
                dst_scr[rows, cb * LANES:(cb + 1) * LANES] = t.astype(BF16)
            return carry

        lax.fori_loop(0, n_chunks, body, 0)

    stream(w_in_hbm, win_scr, IN_COLS, True)
    stream(w_out_hbm, wout_scr, D_MODEL, False)


def _adaln_kernel(c_ref, w_ref, b_ref, mod_ref):
    c = c_ref[...]
    c_act = c * jax.nn.sigmoid(c)
    mod_ref[...] = jnp.dot(c_act.astype(BF16), w_ref[...].astype(BF16),
                           preferred_element_type=F32) + b_ref[...]


def _silu(z):
    return z * jax.nn.sigmoid(z)


def _head_mean_sq(t, bd_ref):
    sq = (t * t).astype(BF16)
    halves = [jnp.dot(sq[:, h * MXU_DIM:(h + 1) * MXU_DIM], bd_ref[...],
                      preferred_element_type=F32) for h in range(B_WIDTH // MXU_DIM)]
    return jnp.concatenate(halves, axis=-1) * (1.0 / B_HEAD_DIM)


def _rope(t, cos, sin_signed):
    cols = []
    for cb in range(B_WIDTH // LANES):
        tc = t[:, cb * LANES:(cb + 1) * LANES]
        cols.append(tc * cos + pltpu.roll(tc, LANES // 2, 1) * sin_signed)
    return jnp.concatenate(cols, axis=-1)


def _rope_tables(pos_quarters, invf, lane):
    n_grp = LANES // HALF
    qrows = SEQ_TILE // n_grp
    grp = lane // HALF
    pos_t = pos_quarters.T
    packed = pos_t[:qrows, n_grp - 1:n_grp]
    for gi in range(n_grp - 2, -1, -1):
        packed = jnp.where(grp == gi, pos_t[:qrows, gi:gi + 1], packed)
    ang = packed * invf
    tables = []
    for packed_tab in (jnp.cos(ang), jnp.sin(ang)):
        parts = []
        for gi in range(n_grp):
            t = jnp.where(grp == gi, packed_tab, 0.0)
            t = t + pltpu.roll(t, 2 * HALF, 1)
            parts.append(t + pltpu.roll(t, HALF, 1))
        tables.append(jnp.concatenate(parts, axis=0))
    return tables


def _norm_weight_lanes(w):
    lo, hi = w[:, :HALF], w[:, HALF:]
    block = jnp.concatenate([lo, lo, hi, hi], axis=1)
    return jnp.concatenate([block] * (B_WIDTH // LANES), axis=1)


def _across_sublanes(x, op):
    for shift in (4, 2, 1):
        x = op(x, pltpu.roll(x, shift, 0))
    return x


def _layer_kernel(x_ref, mod_ref, pos_ref, invf_ref, normw_ref, w_in_hbm, sgu_ref, ws_ref,
                  bst_ref, qnw_ref, knw_ref, lq1_ref, lk1_ref, lq2_ref, lk2_ref, subln_ref,
                  w_out_hbm, bd_ref, out_ref,
                  k_scr, vt_scr, qblk_scr, m_scr, acc_scr, mixed_scr, s_scr, p_scr,
                  alpha_scr, smax_scr, win_ref, wout_ref, stage_scr, stage_sem):
    @pl.when((pl.program_id(0) == 0) & (pl.program_id(1) == 0))
    def _():
        _load_weights(w_in_hbm, w_out_hbm, win_ref, wout_ref, stage_scr, stage_sem)

    refs = dict(locals())
    tiles = [_TilePhases(t, refs) for t in range(TILES_PER_STEP)]
    tiles[0].activations()
    tiles[0].qk_projections()
    tiles[0].before_loop()
    for t, tile in enumerate(tiles):
        nxt = tiles[t + 1] if t + 1 < TILES_PER_STEP else None
        tile.key_loop()
        tile.last_value_product_and_out_a()
        if nxt is not None:
            nxt.activations()
            nxt.qk_projections()
        tile.finalise()
        tile.out_b()
        if nxt is not None:
            nxt.before_loop()


class _TilePhases:
    def __init__(self, t, refs):
        self.__dict__.update(refs)
        self.t = t
        self.par = t % 2
        self.base = pl.program_id(1) * TILES_PER_STEP
        self.i = self.base + t
        self.tile_rows = slice(t * SEQ_TILE, (t + 1) * SEQ_TILE)

    def activations(self):
        x_ref, tile_rows = self.x_ref, self.tile_rows
        b = pl.program_id(0)
        shift = self.mod_ref[0, pl.ds(b, 1), :]
        scale = self.mod_ref[1, pl.ds(b, 1), :]
        self.gate = self.mod_ref[2, pl.ds(b, 1), :]
        sq_lanes = None
        for c0 in range(0, D_MODEL, LANES):
            x_c = x_ref[0, tile_rows, c0:c0 + LANES]
            sq_lanes = x_c * x_c if sq_lanes is None else sq_lanes + x_c * x_c
        sum_sq = jnp.sum(sq_lanes, axis=-1, keepdims=True)
        inv_rms = lax.rsqrt(sum_sq * (1.0 / D_MODEL) + NORM_EPS)
        hb_cols = []
        for c0 in range(0, D_MODEL, MXU_DIM):
            cs = slice(c0, c0 + MXU_DIM)
            h_c = ((x_ref[0, tile_rows, cs] * inv_rms * self.normw_ref[:, cs])
                   * (1.0 + scale[:, cs]) + shift[:, cs])
            hb_cols.append(h_c.astype(BF16))
        self.hb = jnp.concatenate(hb_cols, axis=1)

    def proj(self, c0, width):
        return jnp.dot(self.hb, self.win_ref[:, c0:c0 + width], preferred_element_type=F32)

    def qk_projections(self):
        self.qb = self.proj(3 * A_WIDTH, B_WIDTH)
        self.kb = self.proj(3 * A_WIDTH + B_WIDTH, B_WIDTH)

    def scores(self, g, j):
        tk, tq = SEQ_TILE, SEQ_TILE
        kp = self.k_scr[j, :, g * LANES:(g + 1) * LANES]
        s = jnp.dot(kp, self.qblk_scr[g], preferred_element_type=F32)
        self.s_scr[g] = s
        self.smax_scr[g] = jnp.max(s.reshape(tk // SUBLANES, SUBLANES, 2 * tq), axis=0)

    def softmax(self, g, slot, masked):
        tk, tq = SEQ_TILE, SEQ_TILE
        rows = 2 * SUBLANES
        s_scr, m_scr, par = self.s_scr, self.m_scr, self.par

        def chunk(c):
            sc = s_scr[g, c * rows:(c + 1) * rows, :]
            if masked:
                kv_row = lax.broadcasted_iota(jnp.int32, (rows, 2 * tq), 0) + c * rows
                q_col = lax.broadcasted_iota(jnp.int32, (rows, 2 * tq), 1) % tq
                sc = jnp.where(kv_row <= q_col, sc, -jnp.inf)
            return sc.reshape(2, SUBLANES, 2 * tq)

        if masked:
            col_max = jnp.max(chunk(0), axis=0)
            for c in range(1, tk // rows):
                col_max = jnp.maximum(col_max, jnp.max(chunk(c), axis=0))
        else:
            col_max = self.smax_scr[g]
        m_old = m_scr[par, g]
        m_new = jnp.maximum(m_old, _across_sublanes(col_max, jnp.maximum))
        alpha = jnp.exp2(m_old - m_new)
        m_scr[par, g] = m_new
        self.alpha_scr[slot] = alpha
        for c in range(tk // rows):
            pc = jnp.exp2(chunk(c) - m_new[None])
            self.p_scr[slot, c * rows:(c + 1) * rows, :] = (
                pc.reshape(rows, 2 * tq).astype(BF16))

    def value_update(self, g, j, slot):
        tq = SEQ_TILE
        acc_scr, par = self.acc_scr, self.par
        vt = self.vt_scr[j, g]
        alpha = self.alpha_scr[slot]
        for st in range(2):
            pv = jnp.dot(vt, self.p_scr[slot, :, st * tq:(st + 1) * tq],
                         preferred_element_type=F32)
            a = alpha[:, st * tq:(st + 1) * tq]
            acc = acc_scr[par, 2 * g + st].reshape(V_EXT // SUBLANES, SUBLANES, tq)
            acc_scr[par, 2 * g + st] = (acc * a[None]).reshape(V_EXT, tq) + pv

    def unit(self, g, masked, tile, succ_tile, prev_tile):
        ahead = (g + 2) % B_HEADS
        if g + 2 < B_HEADS:
            self.scores(ahead, tile)
        elif succ_tile is not None:
            self.scores(ahead, succ_tile)
        if prev_tile is not None:
            self.value_update((g - 1) % B_HEADS, prev_tile, (g - 1) % 2)
        self.softmax(g, g % 2, masked)

    def before_loop(self):
        tq, tk, i, par = SEQ_TILE, SEQ_TILE, self.i, self.par
        qb, kb = self.qb, self.kb
        lane = lax.broadcasted_iota(jnp.int32, (1, LANES), 1)
        cos, sin = _rope_tables(self.pos_ref[self.t].astype(F32), self.invf_ref[...], lane)
        sin_signed = jnp.where(lane < LANES // 2, -sin, sin)
        qnw = _norm_weight_lanes(self.qnw_ref[...])
        knw = _norm_weight_lanes(self.knw_ref[...])
        q = qb * lax.rsqrt(_head_mean_sq(qb, self.bd_ref) + NORM_EPS) * qnw
        k = kb * lax.rsqrt(_head_mean_sq(kb, self.bd_ref) + NORM_EPS) * knw
        q = _rope(q, cos, sin_signed) * (B_HEAD_DIM ** -0.5 * LOG2E)
        k = _rope(k, cos, sin_signed)
        self.k_scr[i] = k.astype(BF16)

        qt = q.T
        pair_row = lax.broadcasted_iota(jnp.int32, (LANES, tq), 0)
        is_a = (pair_row // HALF) % 2 == 0
        for g in range(B_HEADS):
            qp = qt[g * LANES:(g + 1) * LANES, :]
            self.qblk_scr[g] = jnp.concatenate(
                [jnp.where(is_a, qp, 0.0), jnp.where(is_a, 0.0, qp)], axis=1).astype(BF16)

        self.m_scr[par] = jnp.full(self.m_scr.shape[1:], -jnp.inf, F32)
        self.acc_scr[par] = jnp.zeros(self.acc_scr.shape[1:], F32)

        vb = self.proj(3 * A_WIDTH + 2 * B_WIDTH, B_WIDTH)
        vbt = vb.T.astype(BF16)
        for g in range(B_HEADS):
            self.vt_scr[i, g, :B_V_DIM, :] = vbt[g * B_V_DIM:(g + 1) * B_V_DIM, :]
            self.vt_scr[i, g, B_V_DIM:, :] = jnp.ones((V_EXT - B_V_DIM, tk), BF16)
        va = self.proj(A_WIDTH, A_WIDTH)
        self.scores(0, i)
        self.scores(1, i)
        ua = self.proj(0, A_WIDTH)
        self.unit(0, True, i, 0, None)
        za = self.proj(2 * A_WIDTH, A_WIDTH)
        self.unit(1, True, i, 0, i)
        zb = self.proj(3 * A_WIDTH + 3 * B_WIDTH, B_WIDTH)
        after_diag = i - 1 if self.t > 0 else 0
        self.unit(2, True, i, after_diag, i)

        row_c = lax.broadcasted_iota(jnp.int32, (CHUNK, CHUNK), 0)
        col_c = lax.broadcasted_iota(jnp.int32, (CHUNK, CHUNK), 1)
        tril = row_c >= col_c
        for hh in range(A_HEADS):
            sl = slice(hh * A_HEAD_DIM, (hh + 1) * A_HEAD_DIM)
            v = va[:, sl]
            vn = (v * lax.rsqrt(jnp.mean(v * v, axis=-1, keepdims=True) + NORM_EPS)
                  * self.sgu_ref[hh:hh + 1, :])
            vnb = vn.astype(BF16)
            ws = jnp.where(tril, self.ws_ref[hh], 0.0).astype(BF16)
            bias = self.bst_ref[:, hh:hh + 1]
            n_chunks = tq // CHUNK
            chunks = jnp.concatenate(
                [vnb[c * CHUNK:(c + 1) * CHUNK, :] for c in range(n_chunks)], axis=1)
            mixed_chunks = jnp.dot(ws, chunks, preferred_element_type=F32)
            mix = jnp.concatenate([mixed_chunks[:, c * A_HEAD_DIM:(c + 1) * A_HEAD_DIM] + bias
                                   for c in range(n_chunks)], axis=0)
            a_out = ua[:, sl] * mix * _silu(za[:, sl])
            self.mixed_scr[par, :, sl] = a_out.astype(BF16)
        self.zb_gate = _silu(zb)
        self.unit(3, True, i, after_diag, i)
        for s in range(self.t):
            tile = i - 1 - s
            succ = tile - 1 if s + 1 < self.t else 0
            self.tile_units(tile, succ, tile + 1)

    def tile_units(self, tile, succ_tile, prev_tile):
        self.unit(0, False, tile, succ_tile, prev_tile)
        for g in range(1, B_HEADS):
            self.unit(g, False, tile, succ_tile, tile)

    def key_loop(self):
        base = self.base

        def pair_body(jj, carry):
            j = 2 * jj
            self.tile_units(j, j + 1, jnp.where(j == 0, base, j - 1))
            self.tile_units(j + 1, j + 2, j)
            return carry

        lax.fori_loop(0, base // 2, pair_body, 0)

    def last_value_product_and_out_a(self):
        last = B_HEADS - 1
        base = self.base
        self.value_update(last, jnp.where(base > 0, base - 1, base), last % 2)
        rows = self.tile_rows
        self.out_ref[0, rows, :] = self.x_ref[0, rows, :] + self.gate * jnp.dot(
            self.mixed_scr[self.par, :, :A_WIDTH], self.wout_ref[:A_WIDTH, :],
            preferred_element_type=F32)

    def finalise(self):
        tq, par, acc_scr = SEQ_TILE, self.par, self.acc_scr
        lam = (jnp.exp(jnp.sum(self.lq1_ref[...] * self.lk1_ref[...], axis=-1, keepdims=True))
               - jnp.exp(jnp.sum(self.lq2_ref[...] * self.lk2_ref[...], axis=-1, keepdims=True))
               + LAMBDA_INIT)
        for g in range(B_HEADS):
            inv_l1 = 1.0 / acc_scr[par, 2 * g, B_V_DIM:B_V_DIM + SUBLANES, :]
            inv_l2 = 1.0 / acc_scr[par, 2 * g + 1, B_V_DIM:B_V_DIM + SUBLANES, :]
            acc1 = acc_scr[par, 2 * g, :B_V_DIM, :].reshape(B_V_DIM // SUBLANES, SUBLANES, tq)
            acc2 = acc_scr[par, 2 * g + 1, :B_V_DIM, :].reshape(
                B_V_DIM // SUBLANES, SUBLANES, tq)
            ot = acc1 * inv_l1[None] - lam * (acc2 * inv_l2[None])
            o = ot.reshape(B_V_DIM, tq).T
            o = (o * lax.rsqrt(jnp.mean(o * o, axis=-1, keepdims=True) + SUBLN_EPS)
                 * self.subln_ref[...])
            o = o * (1.0 - LAMBDA_INIT)
            vsl = slice(g * B_V_DIM, (g + 1) * B_V_DIM)
            b_out = o * self.zb_gate[:, vsl]
            self.mixed_scr[par, :, A_WIDTH + g * B_V_DIM:A_WIDTH + (g + 1) * B_V_DIM] = (
                b_out.astype(BF16))

    def out_b(self):
        self.out_ref[0, self.tile_rows, :] += self.gate * jnp.dot(
            self.mixed_scr[self.par, :, A_WIDTH:], self.wout_ref[A_WIDTH:, :],
            preferred_element_type=F32)


def kernel(x, c, positions, norm_w, w_ada, b_ada, w_in, sgu_norm_w, w_s, b_s, q_norm_w, k_norm_w,
           lambda_q1, lambda_k1, lambda_q2, lambda_k2, subln_w, w_out):
    bsz, seq, d = x.shape
    assert d == D_MODEL and seq % (TILES_PER_STEP * SEQ_TILE) == 0 and norm_w.shape[0] == 1
    assert TILES_PER_STEP % 2 == 0
    nt = seq // SEQ_TILE

    mod = pl.pallas_call(
        _adaln_kernel,
        grid=(3,),
        in_specs=[pl.BlockSpec((bsz, d), lambda n: (0, 0)),
                  pl.BlockSpec((None, d, d), lambda n: (0, 0, n)),
                  pl.BlockSpec((1, d), lambda n: (0, n))],
        out_specs=pl.BlockSpec((None, bsz, d), lambda n: (n, 0, 0)),
        out_shape=jax.ShapeDtypeStruct((3, bsz, d), F32),
        name="adaln_mod",
    )(c, w_ada, b_ada)

    n_grp = LANES // HALF
    pos_q = jnp.pad(positions.reshape(bsz, nt, n_grp, SEQ_TILE // n_grp),
                    ((0, 0), (0, 0), (0, SUBLANES - n_grp), (0, LANES - SEQ_TILE // n_grp)))

    inv_freq = ROPE_THETA ** (-jnp.arange(0, B_HEAD_DIM, 2, dtype=F32) / B_HEAD_DIM)
    invf = jnp.tile(inv_freq, LANES // HALF)[None, :]
    lane_head = (np.arange(MXU_DIM) // LANES) * 2 + (np.arange(MXU_DIM) // HALF) % 2
    bd = jnp.asarray(lane_head[:, None] == lane_head[None, :], dtype=BF16)


    const2 = lambda b, i: (0, 0)
    const3 = lambda b, i: (0, 0, 0)
    step_rows = TILES_PER_STEP * SEQ_TILE
    out = pl.pallas_call(
        _layer_kernel,
        grid=(bsz, nt // TILES_PER_STEP),
        in_specs=[
            pl.BlockSpec((1, step_rows, d), lambda b, i: (b, i, 0)),
            pl.BlockSpec((3, bsz, d), const3),
            pl.BlockSpec((None, TILES_PER_STEP, SUBLANES, LANES),
                         lambda b, i: (b, i, 0, 0)),
            pl.BlockSpec((1, LANES), const2),
            pl.BlockSpec((1, d), const2),
            pl.BlockSpec(memory_space=pl.ANY),
            pl.BlockSpec((A_HEADS, A_HEAD_DIM), const2),
            pl.BlockSpec((A_HEADS, CHUNK, CHUNK), const3),
            pl.BlockSpec((CHUNK, A_HEADS), const2),
            pl.BlockSpec((1, B_HEAD_DIM), const2),
            pl.BlockSpec((1, B_HEAD_DIM), const2),
            pl.BlockSpec((1, B_HEAD_DIM), const2),
            pl.BlockSpec((1, B_HEAD_DIM), const2),
            pl.BlockSpec((1, B_HEAD_DIM), const2),
            pl.BlockSpec((1, B_HEAD_DIM), const2),
            pl.BlockSpec((1, B_V_DIM), const2),
            pl.BlockSpec(memory_space=pl.ANY),
            pl.BlockSpec((MXU_DIM, MXU_DIM), const2),
        ],
        out_specs=pl.BlockSpec((1, step_rows, d), lambda b, i: (b, i, 0)),
        out_shape=jax.ShapeDtypeStruct((bsz, seq, d), F32),
        scratch_shapes=[
            pltpu.VMEM((nt, SEQ_TILE, B_WIDTH), BF16),
            pltpu.VMEM((nt, B_HEADS, V_EXT, SEQ_TILE), BF16),
            pltpu.VMEM((B_HEADS, LANES, 2 * SEQ_TILE), BF16),
            pltpu.VMEM((2, B_HEADS, SUBLANES, 2 * SEQ_TILE), F32),
            pltpu.VMEM((2, 2 * B_HEADS, V_EXT, SEQ_TILE), F32),
            pltpu.VMEM((2, SEQ_TILE, A_WIDTH + B_WIDTH), BF16),
            pltpu.VMEM((B_HEADS, SEQ_TILE, 2 * SEQ_TILE), F32),
            pltpu.VMEM((2, SEQ_TILE, 2 * SEQ_TILE), BF16),
            pltpu.VMEM((2, SUBLANES, 2 * SEQ_TILE), F32),
            pltpu.VMEM((B_HEADS, SUBLANES, 2 * SEQ_TILE), F32),
            pltpu.VMEM((d, IN_COLS), BF16),
            pltpu.VMEM((A_WIDTH + B_WIDTH, d), BF16),
            pltpu.VMEM((WEIGHT_STAGE_SLOTS, WEIGHT_CHUNK_ROWS, IN_COLS), F32),
            pltpu.SemaphoreType.DMA((WEIGHT_STAGE_SLOTS,)),
        ],
        compiler_params=pltpu.CompilerParams(
            dimension_semantics=("arbitrary", "arbitrary"),
            vmem_limit_bytes=VMEM_LIMIT_BYTES),
        name="fused_layer",
    )(x, mod, pos_q, invf, norm_w, w_in,
      sgu_norm_w.reshape(A_HEADS, A_HEAD_DIM), w_s.reshape(A_HEADS, CHUNK, CHUNK),
      jnp.transpose(b_s.reshape(A_HEADS, CHUNK)), q_norm_w, k_norm_w,
      lambda_q1, lambda_k1, lambda_q2, lambda_k2, subln_w, w_out, bd)
    return out
```

```python
import math

import numpy as np
import jax
import jax.numpy as jnp
from jax import lax
from jax.experimental import pallas as pl
from jax.experimental.pallas import tpu as pltpu

D_MODEL = 1024
A_WIDTH = 512
B_WIDTH = 512
A_HEADS = 4
A_HEAD_DIM = 128
CHUNK = 128
B_HEADS = 4
B_HEAD_DIM = 64
B_V_DIM = 128
ROPE_THETA = 10000.0
NORM_EPS = 1e-6
SUBLN_EPS = 1e-5
IN_COLS = 3 * A_WIDTH + 4 * B_WIDTH
LAMBDA_INIT = 0.8 - 0.6 * math.exp(-0.3 * 0)
LOG2E = math.log2(math.e)

LANES = 128
SUBLANES = 8
MXU_DIM = 256
SEQ_TILE = 256
TILES_PER_STEP = 4
VMEM_LIMIT_BYTES = 56 * 1024 * 1024
HALF = B_HEAD_DIM // 2
BF16_ROWS = 16
V_EXT = B_V_DIM + BF16_ROWS
WEIGHT_CHUNK_ROWS = 64
WEIGHT_STAGE_SLOTS = 4

F32 = jnp.float32
BF16 = jnp.bfloat16


def _load_weights(w_in_hbm, w_out_hbm, win_scr, wout_scr, stage, sem):
    ch = stage.shape[1]
    grp = lax.broadcasted_iota(jnp.int32, (1, LANES), 1) // HALF
    q0, v0 = 3 * A_WIDTH, 3 * A_WIDTH + 2 * B_WIDTH

    def stream(src_hbm, dst_scr, n_cols, reorder):
        n_chunks = src_hbm.shape[1] // ch

        def chunk_copy(c, slot):
            return pltpu.make_async_copy(src_hbm.at[0, pl.ds(c * ch, ch), :],
                                         stage.at[slot, :, pl.ds(0, n_cols)], sem.at[slot])

        depth = stage.shape[0]
        for c0 in range(depth - 1):
            chunk_copy(c0, c0).start()

        def body(c, carry):
            slot = c % depth
            ahead = c + depth - 1

            @pl.when(ahead < n_chunks)
            def _():
                chunk_copy(ahead, ahead % depth).start()

            chunk_copy(c, slot).wait()
            rows = pl.ds(pl.multiple_of(c * ch, ch), ch)
            for cb in range(n_cols // LANES):
                t = stage[slot, :, cb * LANES:(cb + 1) * LANES]
                if reorder and q0 <= cb * LANES < v0:
                    t = jnp.where(grp == 1, pltpu.roll(t, LANES - HALF, 1),
                                  jnp.where(grp == 2, pltpu.roll(t, HALF, 1), t))
                dst_scr[rows, cb * LANES:(cb + 1) * LANES] = t.astype(BF16)
            return carry

        lax.fori_loop(0, n_chunks, body, 0)

    stream(w_in_hbm, win_scr, IN_COLS, True)
    stream(w_out_hbm, wout_scr, D_MODEL, False)


def _adaln_kernel(c_ref, w_ref, b_ref, mod_ref):
    c = c_ref[...]
    c_act = c * jax.nn.sigmoid(c)
    mod_ref[...] = jnp.dot(c_act.astype(BF16), w_ref[...].astype(BF16),
                           preferred_element_type=F32) + b_ref[...]


def _silu(z):
    return z * jax.nn.sigmoid(z)


def _head_mean_sq(t, bd_ref):
    sq = (t * t).astype(BF16)
    halves = [jnp.dot(sq[:, h * MXU_DIM:(h + 1) * MXU_DIM], bd_ref[...],
                      preferred_element_type=F32) for h in range(B_WIDTH // MXU_DIM)]
    return jnp.concatenate(halves, axis=-1) * (1.0 / B_HEAD_DIM)


def _rope(t, cos, sin_signed):
    cols = []
    for cb in range(B_WIDTH // LANES):
        tc = t[:, cb * LANES:(cb + 1) * LANES]
        cols.append(tc * cos + pltpu.roll(tc, LANES // 2, 1) * sin_signed)
    return jnp.concatenate(cols, axis=-1)


def _rope_tables(pos_quarters, invf, lane):
    n_grp = LANES // HALF
    qrows = SEQ_TILE // n_grp
    grp = lane // HALF
    pos_t = pos_quarters.T
    packed = pos_t[:qrows, n_grp - 1:n_grp]
    for gi in range(n_grp - 2, -1, -1):
        packed = jnp.where(grp == gi, pos_t[:qrows, gi:gi + 1], packed)
    ang = packed * invf
    tables = []
    for packed_tab in (jnp.cos(ang), jnp.sin(ang)):
        parts = []
        for gi in range(n_grp):
            t = jnp.where(grp == gi, packed_tab, 0.0)
            t = t + pltpu.roll(t, 2 * HALF, 1)
            parts.append(t + pltpu.roll(t, HALF, 1))
        tables.append(jnp.concatenate(parts, axis=0))
    return tables


def _norm_weight_lanes(w):
    lo, hi = w[:, :HALF], w[:, HALF:]
    block = jnp.concatenate([lo, lo, hi, hi], axis=1)
    return jnp.concatenate([block] * (B_WIDTH // LANES), axis=1)


def _across_sublanes(x, op):
    for shift in (4, 2, 1):
        x = op(x, pltpu.roll(x, shift, 0))
    return x


def _layer_kernel(x_ref, mod_ref, pos_ref, invf_ref, normw_ref, w_in_hbm, sgu_ref, ws_ref,
                  bst_ref, qnw_ref, knw_ref, lq1_ref, lk1_ref, lq2_ref, lk2_ref, subln_ref,
                  w_out_hbm, bd_ref, out_ref,
                  k_scr, vt_scr, qblk_scr, m_scr, acc_scr, mixed_scr, s_scr, p_scr,
                  alpha_scr, smax_scr, win_ref, wout_ref, stage_scr, stage_sem, nw_scr):
    @pl.when((pl.program_id(0) == 0) & (pl.program_id(1) == 0))
    def _():
        _load_weights(w_in_hbm, w_out_hbm, win_ref, wout_ref, stage_scr, stage_sem)
        nw_scr[0:1, :] = _norm_weight_lanes(qnw_ref[...])
        nw_scr[1:2, :] = _norm_weight_lanes(knw_ref[...])

    refs = dict(locals())
    tiles = [_TilePhases(t, refs) for t in range(TILES_PER_STEP)]
    tiles[0].activations()
    tiles[0].qk_projections()
    tiles[0].before_loop()
    for t, tile in enumerate(tiles):
        nxt = tiles[t + 1] if t + 1 < TILES_PER_STEP else None
        tile.key_loop()
        tile.last_value_product_and_out_a()
        if nxt is not None:
            nxt.activations()
            nxt.qk_projections()
        tile.finalise()
        tile.out_b()
        if nxt is not None:
            nxt.before_loop()


class _TilePhases:
    def __init__(self, t, refs):
        self.__dict__.update(refs)
        self.t = t
        self.par = t % 2
        self.base = pl.program_id(1) * TILES_PER_STEP
        self.i = self.base + t
        self.tile_rows = slice(t * SEQ_TILE, (t + 1) * SEQ_TILE)

    def activations(self):
        x_ref, tile_rows = self.x_ref, self.tile_rows
        b = pl.program_id(0)
        shift = self.mod_ref[0, pl.ds(b, 1), :]
        scale = self.mod_ref[1, pl.ds(b, 1), :]
        self.gate = self.mod_ref[2, pl.ds(b, 1), :]
        sq_lanes = None
        for c0 in range(0, D_MODEL, LANES):
            x_c = x_ref[0, tile_rows, c0:c0 + LANES]
            sq_lanes = x_c * x_c if sq_lanes is None else sq_lanes + x_c * x_c
        sum_sq = jnp.sum(sq_lanes, axis=-1, keepdims=True)
        inv_rms = lax.rsqrt(sum_sq * (1.0 / D_MODEL) + NORM_EPS)
        hb_cols = []
        for c0 in range(0, D_MODEL, MXU_DIM):
            cs = slice(c0, c0 + MXU_DIM)
            h_c = ((x_ref[0, tile_rows, cs] * inv_rms * self.normw_ref[:, cs])
                   * (1.0 + scale[:, cs]) + shift[:, cs])
            hb_cols.append(h_c.astype(BF16))
        self.hb = jnp.concatenate(hb_cols, axis=1)

    def proj(self, c0, width):
        return jnp.dot(self.hb, self.win_ref[:, c0:c0 + width], preferred_element_type=F32)

    def qk_projections(self):
        self.qb = self.proj(3 * A_WIDTH, B_WIDTH)
        self.kb = self.proj(3 * A_WIDTH + B_WIDTH, B_WIDTH)

    def scores(self, g, j):
        tk, tq = SEQ_TILE, SEQ_TILE
        kp = self.k_scr[j, :, g * LANES:(g + 1) * LANES]
        s = jnp.dot(kp, self.qblk_scr[g], preferred_element_type=F32)
        self.s_scr[g] = s
        self.smax_scr[g] = jnp.max(s.reshape(tk // SUBLANES, SUBLANES, 2 * tq), axis=0)

    def softmax(self, g, slot, masked):
        tk, tq = SEQ_TILE, SEQ_TILE
        rows = 2 * SUBLANES
        s_scr, m_scr, par = self.s_scr, self.m_scr, self.par

        def chunk(c):
            sc = s_scr[g, c * rows:(c + 1) * rows, :]
            if masked:
                kv_row = lax.broadcasted_iota(jnp.int32, (rows, 2 * tq), 0) + c * rows
                q_col = lax.broadcasted_iota(jnp.int32, (rows, 2 * tq), 1) % tq
                sc = jnp.where(kv_row <= q_col, sc, -jnp.inf)
            return sc.reshape(2, SUBLANES, 2 * tq)

        if masked:
            col_max = jnp.max(chunk(0), axis=0)
            for c in range(1, tk // rows):
                col_max = jnp.maximum(col_max, jnp.max(chunk(c), axis=0))
        else:
            col_max = self.smax_scr[g]
        m_old = m_scr[par, g]
        m_new = jnp.maximum(m_old, _across_sublanes(col_max, jnp.maximum))
        alpha = jnp.exp2(m_old - m_new)
        m_scr[par, g] = m_new
        self.alpha_scr[slot] = alpha
        for c in range(tk // rows):
            pc = jnp.exp2(chunk(c) - m_new[None])
            self.p_scr[slot, c * rows:(c + 1) * rows, :] = (
                pc.reshape(rows, 2 * tq).astype(BF16))

    def value_update(self, g, j, slot):
        tq = SEQ_TILE
        acc_scr, par = self.acc_scr, self.par
        vt = self.vt_scr[j, g]
        alpha = self.alpha_scr[slot]
        for st in range(2):
            pv = jnp.dot(vt, self.p_scr[slot, :, st * tq:(st + 1) * tq],
                         preferred_element_type=F32)
            a = alpha[:, st * tq:(st + 1) * tq]
            acc = acc_scr[par, 2 * g + st].reshape(V_EXT // SUBLANES, SUBLANES, tq)
            acc_scr[par, 2 * g + st] = (acc * a[None]).reshape(V_EXT, tq) + pv

    def unit(self, g, masked, tile, succ_tile, prev_tile):
        ahead = (g + 2) % B_HEADS
        if g + 2 < B_HEADS:
            self.scores(ahead, tile)
        elif succ_tile is not None:
            self.scores(ahead, succ_tile)
        if prev_tile is not None:
            self.value_update((g - 1) % B_HEADS, prev_tile, (g - 1) % 2)
        self.softmax(g, g % 2, masked)

    def before_loop(self):
        tq, tk, i, par = SEQ_TILE, SEQ_TILE, self.i, self.par
        qb, kb = self.qb, self.kb
        lane = lax.broadcasted_iota(jnp.int32, (1, LANES), 1)
        cos, sin = _rope_tables(self.pos_ref[self.t].astype(F32), self.invf_ref[...], lane)
        sin_signed = jnp.where(lane < LANES // 2, -sin, sin)
        q = qb * lax.rsqrt(_head_mean_sq(qb, self.bd_ref) + NORM_EPS) * self.nw_scr[0:1, :]
        k = kb * lax.rsqrt(_head_mean_sq(kb, self.bd_ref) + NORM_EPS) * self.nw_scr[1:2, :]
        q = _rope(q, cos, sin_signed) * (B_HEAD_DIM ** -0.5 * LOG2E)
        k = _rope(k, cos, sin_signed)
        self.k_scr[i] = k.astype(BF16)

        qt = q.T
        pair_row = lax.broadcasted_iota(jnp.int32, (LANES, tq), 0)
        is_a = (pair_row // HALF) % 2 == 0
        for g in range(B_HEADS):
            qp = qt[g * LANES:(g + 1) * LANES, :]
            self.qblk_scr[g] = jnp.concatenate(
                [jnp.where(is_a, qp, 0.0), jnp.where(is_a, 0.0, qp)], axis=1).astype(BF16)

        self.m_scr[par] = jnp.full(self.m_scr.shape[1:], -jnp.inf, F32)
        self.acc_scr[par] = jnp.zeros(self.acc_scr.shape[1:], F32)

        vb = self.proj(3 * A_WIDTH + 2 * B_WIDTH, B_WIDTH)
        vbt = vb.T.astype(BF16)
        for g in range(B_HEADS):
            self.vt_scr[i, g, :B_V_DIM, :] = vbt[g * B_V_DIM:(g + 1) * B_V_DIM, :]
            self.vt_scr[i, g, B_V_DIM:, :] = jnp.ones((V_EXT - B_V_DIM, tk), BF16)
        va = self.proj(A_WIDTH, A_WIDTH)
        self.scores(0, i)
        self.scores(1, i)
        ua = self.proj(0, A_WIDTH)
        self.unit(0, True, i, 0, None)
        za = self.proj(2 * A_WIDTH, A_WIDTH)
        self.unit(1, True, i, 0, i)
        zb = self.proj(3 * A_WIDTH + 3 * B_WIDTH, B_WIDTH)
        after_diag = i - 1 if self.t > 0 else 0
        self.unit(2, True, i, after_diag, i)

        row_c = lax.broadcasted_iota(jnp.int32, (CHUNK, CHUNK), 0)
        col_c = lax.broadcasted_iota(jnp.int32, (CHUNK, CHUNK), 1)
        tril = row_c >= col_c
        for hh in range(A_HEADS):
            sl = slice(hh * A_HEAD_DIM, (hh + 1) * A_HEAD_DIM)
            v = va[:, sl]
            vn = (v * lax.rsqrt(jnp.mean(v * v, axis=-1, keepdims=True) + NORM_EPS)
                  * self.sgu_ref[hh:hh + 1, :])
            vnb = vn.astype(BF16)
            ws = jnp.where(tril, self.ws_ref[hh], 0.0).astype(BF16)
            bias = self.bst_ref[:, hh:hh + 1]
            n_chunks = tq // CHUNK
            chunks = jnp.concatenate(
                [vnb[c * CHUNK:(c + 1) * CHUNK, :] for c in range(n_chunks)], axis=1)
            mixed_chunks = jnp.dot(ws, chunks, preferred_element_type=F32)
            mix = jnp.concatenate([mixed_chunks[:, c * A_HEAD_DIM:(c + 1) * A_HEAD_DIM] + bias
                                   for c in range(n_chunks)], axis=0)
            a_out = ua[:, sl] * mix * _silu(za[:, sl])
            self.mixed_scr[par, :, sl] = a_out.astype(BF16)
        self.zb_gate = _silu(zb)
        self.unit(3, True, i, after_diag, i)
        for s in range(self.t):
            tile = i - 1 - s
            succ = tile - 1 if s + 1 < self.t else 0
            self.tile_units(tile, succ, tile + 1)

    def tile_units(self, tile, succ_tile, prev_tile):
        self.unit(0, False, tile, succ_tile, prev_tile)
        for g in range(1, B_HEADS):
            self.unit(g, False, tile, succ_tile, tile)

    def key_loop(self):
        base = self.base

        def pair_body(jj, carry):
            j = 2 * jj
            self.tile_units(j, j + 1, jnp.where(j == 0, base, j - 1))
            self.tile_units(j + 1, j + 2, j)
            return carry

        lax.fori_loop(0, base // 2, pair_body, 0)

    def last_value_product_and_out_a(self):
        last = B_HEADS - 1
        base = self.base
        self.value_update(last, jnp.where(base > 0, base - 1, base), last % 2)
        rows = self.tile_rows
        self.out_ref[0, rows, :] = self.x_ref[0, rows, :] + self.gate * jnp.dot(
            self.mixed_scr[self.par, :, :A_WIDTH], self.wout_ref[:A_WIDTH, :],
            preferred_element_type=F32)

    def finalise(self):
        tq, par, acc_scr = SEQ_TILE, self.par, self.acc_scr
        lam = (jnp.exp(jnp.sum(self.lq1_ref[...] * self.lk1_ref[...], axis=-1, keepdims=True))
               - jnp.exp(jnp.sum(self.lq2_ref[...] * self.lk2_ref[...], axis=-1, keepdims=True))
               + LAMBDA_INIT)
        for g in range(B_HEADS):
            inv_l1 = 1.0 / acc_scr[par, 2 * g, B_V_DIM:B_V_DIM + SUBLANES, :]
            inv_l2 = 1.0 / acc_scr[par, 2 * g + 1, B_V_DIM:B_V_DIM + SUBLANES, :]
            acc1 = acc_scr[par, 2 * g, :B_V_DIM, :].reshape(B_V_DIM // SUBLANES, SUBLANES, tq)
            acc2 = acc_scr[par, 2 * g + 1, :B_V_DIM, :].reshape(
                B_V_DIM // SUBLANES, SUBLANES, tq)
            ot = acc1 * inv_l1[None] - lam * (acc2 * inv_l2[None])
            o = ot.reshape(B_V_DIM, tq).T
            o = (o * lax.rsqrt(jnp.mean(o * o, axis=-1, keepdims=True) + SUBLN_EPS)
                 * self.subln_ref[...])
            o = o * (1.0 - LAMBDA_INIT)
            vsl = slice(g * B_V_DIM, (g + 1) * B_V_DIM)
            b_out = o * self.zb_gate[:, vsl]
            self.mixed_scr[par, :, A_WIDTH + g * B_V_DIM:A_WIDTH + (g + 1) * B_V_DIM] = (
                b_out.astype(BF16))

    def out_b(self):
        self.out_ref[0, self.tile_rows, :] += self.gate * jnp.dot(
            self.mixed_scr[self.par, :, A_WIDTH:], self.wout_ref[A_WIDTH:, :],
            preferred_element_type=F32)


def kernel(x, c, positions, norm_w, w_ada, b_ada, w_in, sgu_norm_w, w_s, b_s, q_norm_w, k_norm_w,
           lambda_q1, lambda_k1, lambda_q2, lambda_k2, subln_w, w_out):
    bsz, seq, d = x.shape
    assert d == D_MODEL and seq % (TILES_PER_STEP * SEQ_TILE) == 0 and norm_w.shape[0] == 1
    assert TILES_PER_STEP % 2 == 0
    nt = seq // SEQ_TILE

    mod = pl.pallas_call(
        _adaln_kernel,
        grid=(3,),
        in_specs=[pl.BlockSpec((bsz, d), lambda n: (0, 0)),
                  pl.BlockSpec((None, d, d), lambda n: (0, 0, n)),
                  pl.BlockSpec((1, d), lambda n: (0, n))],
        out_specs=pl.BlockSpec((None, bsz, d), lambda n: (n, 0, 0)),
        out_shape=jax.ShapeDtypeStruct((3, bsz, d), F32),
        name="adaln_mod",
    )(c, w_ada, b_ada)

    n_grp = LANES // HALF
    pos_q = jnp.pad(positions.reshape(bsz, nt, n_grp, SEQ_TILE // n_grp),
                    ((0, 0), (0, 0), (0, SUBLANES - n_grp), (0, LANES - SEQ_TILE // n_grp)))

    inv_freq = ROPE_THETA ** (-jnp.arange(0, B_HEAD_DIM, 2, dtype=F32) / B_HEAD_DIM)
    invf = jnp.tile(inv_freq, LANES // HALF)[None, :]
    lane_head = (np.arange(MXU_DIM) // LANES) * 2 + (np.arange(MXU_DIM) // HALF) % 2
    bd = jnp.asarray(lane_head[:, None] == lane_head[None, :], dtype=BF16)


    const2 = lambda b, i: (0, 0)
    const3 = lambda b, i: (0, 0, 0)
    step_rows = TILES_PER_STEP * SEQ_TILE
    out = pl.pallas_call(
        _layer_kernel,
        grid=(bsz, nt // TILES_PER_STEP),
        in_specs=[
            pl.BlockSpec((1, step_rows, d), lambda b, i: (b, i, 0)),
            pl.BlockSpec((3, bsz, d), const3),
            pl.BlockSpec((None, TILES_PER_STEP, SUBLANES, LANES),
                         lambda b, i: (b, i, 0, 0)),
            pl.BlockSpec((1, LANES), const2),
            pl.BlockSpec((1, d), const2),
            pl.BlockSpec(memory_space=pl.ANY),
            pl.BlockSpec((A_HEADS, A_HEAD_DIM), const2),
            pl.BlockSpec((A_HEADS, CHUNK, CHUNK), const3),
            pl.BlockSpec((CHUNK, A_HEADS), const2),
            pl.BlockSpec((1, B_HEAD_DIM), const2),
            pl.BlockSpec((1, B_HEAD_DIM), const2),
            pl.BlockSpec((1, B_HEAD_DIM), const2),
            pl.BlockSpec((1, B_HEAD_DIM), const2),
            pl.BlockSpec((1, B_HEAD_DIM), const2),
            pl.BlockSpec((1, B_HEAD_DIM), const2),
            pl.BlockSpec((1, B_V_DIM), const2),
            pl.BlockSpec(memory_space=pl.ANY),
            pl.BlockSpec((MXU_DIM, MXU_DIM), const2),
        ],
        out_specs=pl.BlockSpec((1, step_rows, d), lambda b, i: (b, i, 0)),
        out_shape=jax.ShapeDtypeStruct((bsz, seq, d), F32),
        scratch_shapes=[
            pltpu.VMEM((nt, SEQ_TILE, B_WIDTH), BF16),
            pltpu.VMEM((nt, B_HEADS, V_EXT, SEQ_TILE), BF16),
            pltpu.VMEM((B_HEADS, LANES, 2 * SEQ_TILE), BF16),
            pltpu.VMEM((2, B_HEADS, SUBLANES, 2 * SEQ_TILE), F32),
            pltpu.VMEM((2, 2 * B_HEADS, V_EXT, SEQ_TILE), F32),
            pltpu.VMEM((2, SEQ_TILE, A_WIDTH + B_WIDTH), BF16),
            pltpu.VMEM((B_HEADS, SEQ_TILE, 2 * SEQ_TILE), F32),
            pltpu.VMEM((2, SEQ_TILE, 2 * SEQ_TILE), BF16),
            pltpu.VMEM((2, SUBLANES, 2 * SEQ_TILE), F32),
            pltpu.VMEM((B_HEADS, SUBLANES, 2 * SEQ_TILE), F32),
            pltpu.VMEM((d, IN_COLS), BF16),
            pltpu.VMEM((A_WIDTH + B_WIDTH, d), BF16),
            pltpu.VMEM((WEIGHT_STAGE_SLOTS, WEIGHT_CHUNK_ROWS, IN_COLS), F32),
            pltpu.SemaphoreType.DMA((WEIGHT_STAGE_SLOTS,)),
            pltpu.VMEM((SUBLANES, B_WIDTH), F32),
        ],
        compiler_params=pltpu.CompilerParams(
            dimension_semantics=("arbitrary", "arbitrary"),
            vmem_limit_bytes=VMEM_LIMIT_BYTES),
        name="fused_layer",
    )(x, mod, pos_q, invf, norm_w, w_in,
      sgu_norm_w.reshape(A_HEADS, A_HEAD_DIM), w_s.reshape(A_HEADS, CHUNK, CHUNK),
      jnp.transpose(b_s.reshape(A_HEADS, CHUNK)), q_norm_w, k_norm_w,
      lambda_q1, lambda_k1, lambda_q2, lambda_k2, subln_w, w_out, bd)
    return out
```

```python
import math

import numpy as np
import jax
import jax.numpy as jnp
from jax import lax
from jax.experimental import pallas as pl
from jax.experimental.pallas import tpu as pltpu

D_MODEL = 1024
A_WIDTH = 512
B_WIDTH = 512
A_HEADS = 4
A_HEAD_DIM = 128
CHUNK = 128
B_HEADS = 4
B_HEAD_DIM = 64
B_V_DIM = 128
ROPE_THETA = 10000.0
NORM_EPS = 1e-6
SUBLN_EPS = 1e-5
IN_COLS = 3 * A_WIDTH + 4 * B_WIDTH
LAMBDA_INIT = 0.8 - 0.6 * math.exp(-0.3 * 0)
LOG2E = math.log2(math.e)

LANES = 128
SUBLANES = 8
MXU_DIM = 256
SEQ_TILE = 256
TILES_PER_STEP = 4
VMEM_LIMIT_BYTES = 56 * 1024 * 1024
HALF = B_HEAD_DIM // 2
BF16_ROWS = 16
V_EXT = B_V_DIM + BF16_ROWS
WEIGHT_CHUNK_ROWS = 64
WEIGHT_STAGE_SLOTS = 4

F32 = jnp.float32
BF16 = jnp.bfloat16


def _load_weights(w_in_hbm, w_out_hbm, win_scr, wout_scr, stage, sem):
    ch = stage.shape[1]
    grp = lax.broadcasted_iota(jnp.int32, (1, LANES), 1) // HALF
    q0, v0 = 3 * A_WIDTH, 3 * A_WIDTH + 2 * B_WIDTH

    def stream(src_hbm, dst_scr, n_cols, reorder):
        n_chunks = src_hbm.shape[1] // ch

        def chunk_copy(c, slot):
            return pltpu.make_async_copy(src_hbm.at[0, pl.ds(c * ch, ch), :],
                                         stage.at[slot, :, pl.ds(0, n_cols)], sem.at[slot])

        depth = stage.shape[0]
        for c0 in range(depth - 1):
            chunk_copy(c0, c0).start()

        def body(c, carry):
            slot = c % depth
            ahead = c + depth - 1

            @pl.when(ahead < n_chunks)
            def _():
                chunk_copy(ahead, ahead % depth).start()

            chunk_copy(c, slot).wait()
            rows = pl.ds(pl.multiple_of(c * ch, ch), ch)
            for cb in range(n_cols // LANES):
                t = stage[slot, :, cb * LANES:(cb + 1) * LANES]
                if reorder and q0 <= cb * LANES < v0:
                    t = jnp.where(grp == 1, pltpu.roll(t, LANES - HALF, 1),
                                  jnp.where(grp == 2, pltpu.roll(t, HALF, 1), t))
                dst_scr[rows, cb * LANES:(cb + 1) * LANES] = t.astype(BF16)
            return carry

        lax.fori_loop(0, n_chunks, body, 0)

    stream(w_in_hbm, win_scr, IN_COLS, True)
    stream(w_out_hbm, wout_scr, D_MODEL, False)


def _adaln_kernel(c_ref, w_ref, b_ref, mod_ref):
    c = c_ref[...]
    c_act = c * jax.nn.sigmoid(c)
    mod_ref[...] = jnp.dot(c_act.astype(BF16), w_ref[...].astype(BF16),
                           preferred_element_type=F32) + b_ref[...]


def _silu(z):
    return z * jax.nn.sigmoid(z)


def _head_mean_sq(t, bd_ref):
    sq = (t * t).astype(BF16)
    halves = [jnp.dot(sq[:, h * MXU_DIM:(h + 1) * MXU_DIM], bd_ref[...],
                      preferred_element_type=F32) for h in range(B_WIDTH // MXU_DIM)]
    return jnp.concatenate(halves, axis=-1) * (1.0 / B_HEAD_DIM)


def _rope(t, cos, sin_signed):
    cols = []
    for cb in range(B_WIDTH // LANES):
        tc = t[:, cb * LANES:(cb + 1) * LANES]
        cols.append(tc * cos + pltpu.roll(tc, LANES // 2, 1) * sin_signed)
    return jnp.concatenate(cols, axis=-1)


def _rope_tables(pos_quarters, invf, lane):
    n_grp = LANES // HALF
    qrows = SEQ_TILE // n_grp
    grp = lane // HALF
    pos_t = pos_quarters.T
    packed = pos_t[:qrows, n_grp - 1:n_grp]
    for gi in range(n_grp - 2, -1, -1):
        packed = jnp.where(grp == gi, pos_t[:qrows, gi:gi + 1], packed)
    ang = packed * invf
    tables = []
    for packed_tab in (jnp.cos(ang), jnp.sin(ang)):
        parts = []
        for gi in range(n_grp):
            t = jnp.where(grp == gi, packed_tab, 0.0)
            t = t + pltpu.roll(t, 2 * HALF, 1)
            parts.append(t + pltpu.roll(t, HALF, 1))
        tables.append(jnp.concatenate(parts, axis=0))
    return tables


def _norm_weight_lanes(w):
    lo, hi = w[:, :HALF], w[:, HALF:]
    block = jnp.concatenate([lo, lo, hi, hi], axis=1)
    return jnp.concatenate([block] * (B_WIDTH // LANES), axis=1)


def _across_sublanes(x, op):
    for shift in (4, 2, 1):
        x = op(x, pltpu.roll(x, shift, 0))
    return x


def _layer_kernel(x_ref, mod_ref, pos_ref, invf_ref, normw_ref, w_in_hbm, sgu_ref, ws_ref,
                  bs_ref, qnw_ref, knw_ref, lq1_ref, lk1_ref, lq2_ref, lk2_ref, subln_ref,
                  w_out_hbm, bd_ref, out_ref,
                  k_scr, vt_scr, qblk_scr, m_scr, acc_scr, mixed_scr, s_scr, p_scr,
                  alpha_scr, smax_scr, win_ref, wout_ref, stage_scr, stage_sem, nw_scr,
                  bst_scr):
    @pl.when((pl.program_id(0) == 0) & (pl.program_id(1) == 0))
    def _():
        _load_weights(w_in_hbm, w_out_hbm, win_ref, wout_ref, stage_scr, stage_sem)
        nw_scr[0:1, :] = _norm_weight_lanes(qnw_ref[...])
        nw_scr[1:2, :] = _norm_weight_lanes(knw_ref[...])
        bs_pad = jnp.concatenate(
            [bs_ref[...], jnp.zeros((SUBLANES - A_HEADS, CHUNK), F32)], axis=0)
        bst_scr[:, 0:SUBLANES] = bs_pad.T

    refs = dict(locals())
    tiles = [_TilePhases(t, refs) for t in range(TILES_PER_STEP)]
    tiles[0].activations()
    tiles[0].qk_projections()
    tiles[0].before_loop()
    for t, tile in enumerate(tiles):
        nxt = tiles[t + 1] if t + 1 < TILES_PER_STEP else None
        tile.key_loop()
        tile.last_value_product_and_out_a()
        if nxt is not None:
            nxt.activations()
            nxt.qk_projections()
        tile.finalise()
        tile.out_b()
        if nxt is not None:
            nxt.before_loop()


class _TilePhases:
    def __init__(self, t, refs):
        self.__dict__.update(refs)
        self.t = t
        self.par = t % 2
        self.base = pl.program_id(1) * TILES_PER_STEP
        self.i = self.base + t
        self.tile_rows = slice(t * SEQ_TILE, (t + 1) * SEQ_TILE)

    def activations(self):
        x_ref, tile_rows = self.x_ref, self.tile_rows
        b = pl.program_id(0)
        shift = self.mod_ref[0, pl.ds(b, 1), :]
        scale = self.mod_ref[1, pl.ds(b, 1), :]
        self.gate = self.mod_ref[2, pl.ds(b, 1), :]
        sq_lanes = None
        for c0 in range(0, D_MODEL, LANES):
            x_c = x_ref[0, tile_rows, c0:c0 + LANES]
            sq_lanes = x_c * x_c if sq_lanes is None else sq_lanes + x_c * x_c
        sum_sq = jnp.sum(sq_lanes, axis=-1, keepdims=True)
        inv_rms = lax.rsqrt(sum_sq * (1.0 / D_MODEL) + NORM_EPS)
        hb_cols = []
        for c0 in range(0, D_MODEL, MXU_DIM):
            cs = slice(c0, c0 + MXU_DIM)
            h_c = ((x_ref[0, tile_rows, cs] * inv_rms * self.normw_ref[:, cs])
                   * (1.0 + scale[:, cs]) + shift[:, cs])
            hb_cols.append(h_c.astype(BF16))
        self.hb = jnp.concatenate(hb_cols, axis=1)

    def proj(self, c0, width):
        return jnp.dot(self.hb, self.win_ref[:, c0:c0 + width], preferred_element_type=F32)

    def qk_projections(self):
        self.qb = self.proj(3 * A_WIDTH, B_WIDTH)
        self.kb = self.proj(3 * A_WIDTH + B_WIDTH, B_WIDTH)

    def scores(self, g, j):
        tk, tq = SEQ_TILE, SEQ_TILE
        kp = self.k_scr[j, :, g * LANES:(g + 1) * LANES]
        s = jnp.dot(kp, self.qblk_scr[g], preferred_element_type=F32)
        self.s_scr[g] = s
        self.smax_scr[g] = jnp.max(s.reshape(tk // SUBLANES, SUBLANES, 2 * tq), axis=0)

    def softmax(self, g, slot, masked):
        tk, tq = SEQ_TILE, SEQ_TILE
        rows = 2 * SUBLANES
        s_scr, m_scr, par = self.s_scr, self.m_scr, self.par

        def chunk(c):
            sc = s_scr[g, c * rows:(c + 1) * rows, :]
            if masked:
                kv_row = lax.broadcasted_iota(jnp.int32, (rows, 2 * tq), 0) + c * rows
                q_col = lax.broadcasted_iota(jnp.int32, (rows, 2 * tq), 1) % tq
                sc = jnp.where(kv_row <= q_col, sc, -jnp.inf)
            return sc.reshape(2, SUBLANES, 2 * tq)

        if masked:
            col_max = jnp.max(chunk(0), axis=0)
            for c in range(1, tk // rows):
                col_max = jnp.maximum(col_max, jnp.max(chunk(c), axis=0))
        else:
            col_max = self.smax_scr[g]
        m_old = m_scr[par, g]
        m_new = jnp.maximum(m_old, _across_sublanes(col_max, jnp.maximum))
        alpha = jnp.exp2(m_old - m_new)
        m_scr[par, g] = m_new
        self.alpha_scr[slot] = alpha
        for c in range(tk // rows):
            pc = jnp.exp2(chunk(c) - m_new[None])
            self.p_scr[slot, c * rows:(c + 1) * rows, :] = (
                pc.reshape(rows, 2 * tq).astype(BF16))

    def value_update(self, g, j, slot):
        tq = SEQ_TILE
        acc_scr, par = self.acc_scr, self.par
        vt = self.vt_scr[j, g]
        alpha = self.alpha_scr[slot]
        for st in range(2):
            pv = jnp.dot(vt, self.p_scr[slot, :, st * tq:(st + 1) * tq],
                         preferred_element_type=F32)
            a = alpha[:, st * tq:(st + 1) * tq]
            acc = acc_scr[par, 2 * g + st].reshape(V_EXT // SUBLANES, SUBLANES, tq)
            acc_scr[par, 2 * g + st] = (acc * a[None]).reshape(V_EXT, tq) + pv

    def unit(self, g, masked, tile, succ_tile, prev_tile):
        ahead = (g + 2) % B_HEADS
        if g + 2 < B_HEADS:
            self.scores(ahead, tile)
        elif succ_tile is not None:
            self.scores(ahead, succ_tile)
        if prev_tile is not None:
            self.value_update((g - 1) % B_HEADS, prev_tile, (g - 1) % 2)
        self.softmax(g, g % 2, masked)

    def before_loop(self):
        tq, tk, i, par = SEQ_TILE, SEQ_TILE, self.i, self.par
        qb, kb = self.qb, self.kb
        lane = lax.broadcasted_iota(jnp.int32, (1, LANES), 1)
        cos, sin = _rope_tables(self.pos_ref[self.t].astype(F32), self.invf_ref[...], lane)
        sin_signed = jnp.where(lane < LANES // 2, -sin, sin)
        q = qb * lax.rsqrt(_head_mean_sq(qb, self.bd_ref) + NORM_EPS) * self.nw_scr[0:1, :]
        k = kb * lax.rsqrt(_head_mean_sq(kb, self.bd_ref) + NORM_EPS) * self.nw_scr[1:2, :]
        q = _rope(q, cos, sin_signed) * (B_HEAD_DIM ** -0.5 * LOG2E)
        k = _rope(k, cos, sin_signed)
        self.k_scr[i] = k.astype(BF16)

        qt = q.T
        pair_row = lax.broadcasted_iota(jnp.int32, (LANES, tq), 0)
        is_a = (pair_row // HALF) % 2 == 0
        for g in range(B_HEADS):
            qp = qt[g * LANES:(g + 1) * LANES, :]
            self.qblk_scr[g] = jnp.concatenate(
                [jnp.where(is_a, qp, 0.0), jnp.where(is_a, 0.0, qp)], axis=1).astype(BF16)

        self.m_scr[par] = jnp.full(self.m_scr.shape[1:], -jnp.inf, F32)
        self.acc_scr[par] = jnp.zeros(self.acc_scr.shape[1:], F32)

        vb = self.proj(3 * A_WIDTH + 2 * B_WIDTH, B_WIDTH)
        vbt = vb.T.astype(BF16)
        for g in range(B_HEADS):
            self.vt_scr[i, g, :B_V_DIM, :] = vbt[g * B_V_DIM:(g + 1) * B_V_DIM, :]
            self.vt_scr[i, g, B_V_DIM:, :] = jnp.ones((V_EXT - B_V_DIM, tk), BF16)
        va = self.proj(A_WIDTH, A_WIDTH)
        self.scores(0, i)
        self.scores(1, i)
        ua = self.proj(0, A_WIDTH)
        self.unit(0, True, i, 0, None)
        za = self.proj(2 * A_WIDTH, A_WIDTH)
        self.unit(1, True, i, 0, i)
        zb = self.proj(3 * A_WIDTH + 3 * B_WIDTH, B_WIDTH)
        after_diag = i - 1 if self.t > 0 else 0
        self.unit(2, True, i, after_diag, i)

        row_c = lax.broadcasted_iota(jnp.int32, (CHUNK, CHUNK), 0)
        col_c = lax.broadcasted_iota(jnp.int32, (CHUNK, CHUNK), 1)
        tril = row_c >= col_c
        for hh in range(A_HEADS):
            sl = slice(hh * A_HEAD_DIM, (hh + 1) * A_HEAD_DIM)
            v = va[:, sl]
            vn = (v * lax.rsqrt(jnp.mean(v * v, axis=-1, keepdims=True) + NORM_EPS)
                  * self.sgu_ref[hh:hh + 1, :])
            vnb = vn.astype(BF16)
            ws = jnp.where(tril, self.ws_ref[hh], 0.0).astype(BF16)
            bias = self.bst_scr[:, hh:hh + 1]
            n_chunks = tq // CHUNK
            chunks = jnp.concatenate(
                [vnb[c * CHUNK:(c + 1) * CHUNK, :] for c in range(n_chunks)], axis=1)
            mixed_chunks = jnp.dot(ws, chunks, preferred_element_type=F32)
            mix = jnp.concatenate([mixed_chunks[:, c * A_HEAD_DIM:(c + 1) * A_HEAD_DIM] + bias
                                   for c in range(n_chunks)], axis=0)
            a_out = ua[:, sl] * mix * _silu(za[:, sl])
            self.mixed_scr[par, :, sl] = a_out.astype(BF16)
        self.zb_gate = _silu(zb)
        self.unit(3, True, i, after_diag, i)
        for s in range(self.t):
            tile = i - 1 - s
            succ = tile - 1 if s + 1 < self.t else 0
            self.tile_units(tile, succ, tile + 1)

    def tile_units(self, tile, succ_tile, prev_tile):
        self.unit(0, False, tile, succ_tile, prev_tile)
        for g in range(1, B_HEADS):
            self.unit(g, False, tile, succ_tile, tile)

    def key_loop(self):
        base = self.base

        def pair_body(jj, carry):
            j = 2 * jj
            self.tile_units(j, j + 1, jnp.where(j == 0, base, j - 1))
            self.tile_units(j + 1, j + 2, j)
            return carry

        lax.fori_loop(0, base // 2, pair_body, 0)

    def last_value_product_and_out_a(self):
        last = B_HEADS - 1
        base = self.base
        self.value_update(last, jnp.where(base > 0, base - 1, base), last % 2)
        rows = self.tile_rows
        self.out_ref[0, rows, :] = self.x_ref[0, rows, :] + self.gate * jnp.dot(
            self.mixed_scr[self.par, :, :A_WIDTH], self.wout_ref[:A_WIDTH, :],
            preferred_element_type=F32)

    def finalise(self):
        tq, par, acc_scr = SEQ_TILE, self.par, self.acc_scr
        lam = (jnp.exp(jnp.sum(self.lq1_ref[...] * self.lk1_ref[...], axis=-1, keepdims=True))
               - jnp.exp(jnp.sum(self.lq2_ref[...] * self.lk2_ref[...], axis=-1, keepdims=True))
               + LAMBDA_INIT)
        for g in range(B_HEADS):
            inv_l1 = 1.0 / acc_scr[par, 2 * g, B_V_DIM:B_V_DIM + SUBLANES, :]
            inv_l2 = 1.0 / acc_scr[par, 2 * g + 1, B_V_DIM:B_V_DIM + SUBLANES, :]
            acc1 = acc_scr[par, 2 * g, :B_V_DIM, :].reshape(B_V_DIM // SUBLANES, SUBLANES, tq)
            acc2 = acc_scr[par, 2 * g + 1, :B_V_DIM, :].reshape(
                B_V_DIM // SUBLANES, SUBLANES, tq)
            ot = acc1 * inv_l1[None] - lam * (acc2 * inv_l2[None])
            o = ot.reshape(B_V_DIM, tq).T
            o = (o * lax.rsqrt(jnp.mean(o * o, axis=-1, keepdims=True) + SUBLN_EPS)
                 * self.subln_ref[...])
            o = o * (1.0 - LAMBDA_INIT)
            vsl = slice(g * B_V_DIM, (g + 1) * B_V_DIM)
            b_out = o * self.zb_gate[:, vsl]
            self.mixed_scr[par, :, A_WIDTH + g * B_V_DIM:A_WIDTH + (g + 1) * B_V_DIM] = (
                b_out.astype(BF16))

    def out_b(self):
        self.out_ref[0, self.tile_rows, :] += self.gate * jnp.dot(
            self.mixed_scr[self.par, :, A_WIDTH:], self.wout_ref[A_WIDTH:, :],
            preferred_element_type=F32)


def kernel(x, c, positions, norm_w, w_ada, b_ada, w_in, sgu_norm_w, w_s, b_s, q_norm_w, k_norm_w,
           lambda_q1, lambda_k1, lambda_q2, lambda_k2, subln_w, w_out):
    bsz, seq, d = x.shape
    assert d == D_MODEL and seq % (TILES_PER_STEP * SEQ_TILE) == 0 and norm_w.shape[0] == 1
    assert TILES_PER_STEP % 2 == 0
    nt = seq // SEQ_TILE

    mod = pl.pallas_call(
        _adaln_kernel,
        grid=(3,),
        in_specs=[pl.BlockSpec((bsz, d), lambda n: (0, 0)),
                  pl.BlockSpec((None, d, d), lambda n: (0, 0, n)),
                  pl.BlockSpec((1, d), lambda n: (0, n))],
        out_specs=pl.BlockSpec((None, bsz, d), lambda n: (n, 0, 0)),
        out_shape=jax.ShapeDtypeStruct((3, bsz, d), F32),
        name="adaln_mod",
    )(c, w_ada, b_ada)

    n_grp = LANES // HALF
    pos_q = jnp.pad(positions.reshape(bsz, nt, n_grp, SEQ_TILE // n_grp),
                    ((0, 0), (0, 0), (0, SUBLANES - n_grp), (0, LANES - SEQ_TILE // n_grp)))

    inv_freq = ROPE_THETA ** (-jnp.arange(0, B_HEAD_DIM, 2, dtype=F32) / B_HEAD_DIM)
    invf = jnp.tile(inv_freq, LANES // HALF)[None, :]
    lane_head = (np.arange(MXU_DIM) // LANES) * 2 + (np.arange(MXU_DIM) // HALF) % 2
    bd = jnp.asarray(lane_head[:, None] == lane_head[None, :], dtype=BF16)


    const2 = lambda b, i: (0, 0)
    const3 = lambda b, i: (0, 0, 0)
    step_rows = TILES_PER_STEP * SEQ_TILE
    out = pl.pallas_call(
        _layer_kernel,
        grid=(bsz, nt // TILES_PER_STEP),
        in_specs=[
            pl.BlockSpec((1, step_rows, d), lambda b, i: (b, i, 0)),
            pl.BlockSpec((3, bsz, d), const3),
            pl.BlockSpec((None, TILES_PER_STEP, SUBLANES, LANES),
                         lambda b, i: (b, i, 0, 0)),
            pl.BlockSpec((1, LANES), const2),
            pl.BlockSpec((1, d), const2),
            pl.BlockSpec(memory_space=pl.ANY),
            pl.BlockSpec((A_HEADS, A_HEAD_DIM), const2),
            pl.BlockSpec((A_HEADS, CHUNK, CHUNK), const3),
            pl.BlockSpec((A_HEADS, CHUNK), const2),
            pl.BlockSpec((1, B_HEAD_DIM), const2),
            pl.BlockSpec((1, B_HEAD_DIM), const2),
            pl.BlockSpec((1, B_HEAD_DIM), const2),
            pl.BlockSpec((1, B_HEAD_DIM), const2),
            pl.BlockSpec((1, B_HEAD_DIM), const2),
            pl.BlockSpec((1, B_HEAD_DIM), const2),
            pl.BlockSpec((1, B_V_DIM), const2),
            pl.BlockSpec(memory_space=pl.ANY),
            pl.BlockSpec((MXU_DIM, MXU_DIM), const2),
        ],
        out_specs=pl.BlockSpec((1, step_rows, d), lambda b, i: (b, i, 0)),
        out_shape=jax.ShapeDtypeStruct((bsz, seq, d), F32),
        scratch_shapes=[
            pltpu.VMEM((nt, SEQ_TILE, B_WIDTH), BF16),
            pltpu.VMEM((nt, B_HEADS, V_EXT, SEQ_TILE), BF16),
            pltpu.VMEM((B_HEADS, LANES, 2 * SEQ_TILE), BF16),
            pltpu.VMEM((2, B_HEADS, SUBLANES, 2 * SEQ_TILE), F32),
            pltpu.VMEM((2, 2 * B_HEADS, V_EXT, SEQ_TILE), F32),
            pltpu.VMEM((2, SEQ_TILE, A_WIDTH + B_WIDTH), BF16),
            pltpu.VMEM((B_HEADS, SEQ_TILE, 2 * SEQ_TILE), F32),
            pltpu.VMEM((2, SEQ_TILE, 2 * SEQ_TILE), BF16),
            pltpu.VMEM((2, SUBLANES, 2 * SEQ_TILE), F32),
            pltpu.VMEM((B_HEADS, SUBLANES, 2 * SEQ_TILE), F32),
            pltpu.VMEM((d, IN_COLS), BF16),
            pltpu.VMEM((A_WIDTH + B_WIDTH, d), BF16),
            pltpu.VMEM((WEIGHT_STAGE_SLOTS, WEIGHT_CHUNK_ROWS, IN_COLS), F32),
            pltpu.SemaphoreType.DMA((WEIGHT_STAGE_SLOTS,)),
            pltpu.VMEM((SUBLANES, B_WIDTH), F32),
            pltpu.VMEM((CHUNK, LANES), F32),
        ],
        compiler_params=pltpu.CompilerParams(
            dimension_semantics=("arbitrary", "arbitrary"),
            vmem_limit_bytes=VMEM_LIMIT_BYTES),
        name="fused_layer",
    )(x, mod, pos_q, invf, norm_w, w_in,
      sgu_norm_w.reshape(A_HEADS, A_HEAD_DIM), w_s.reshape(A_HEADS, CHUNK, CHUNK),
      b_s.reshape(A_HEADS, CHUNK), q_norm_w, k_norm_w,
      lambda_q1, lambda_k1, lambda_q2, lambda_k2, subln_w, w_out, bd)
    return out
```

```python
import math

import numpy as np
import jax
import jax.numpy as jnp
from jax import lax
from jax.experimental import pallas as pl
from jax.experimental.pallas import tpu as pltpu

D_MODEL = 1024
A_WIDTH = 512
B_WIDTH = 512
A_HEADS = 4
A_HEAD_DIM = 128
CHUNK = 128
B_HEADS = 4
B_HEAD_DIM = 64
B_V_DIM = 128
ROPE_THETA = 10000.0
NORM_EPS = 1e-6
SUBLN_EPS = 1e-5
IN_COLS = 3 * A_WIDTH + 4 * B_WIDTH
LAMBDA_INIT = 0.8 - 0.6 * math.exp(-0.3 * 0)
LOG2E = math.log2(math.e)

LANES = 128
SUBLANES = 8
MXU_DIM = 256
SEQ_TILE = 256
TILES_PER_STEP = 4
VMEM_LIMIT_BYTES = 56 * 1024 * 1024
HALF = B_HEAD_DIM // 2
BF16_ROWS = 16
V_EXT = B_V_DIM + BF16_ROWS
WEIGHT_CHUNK_ROWS = 64
WEIGHT_STAGE_SLOTS = 4

F32 = jnp.float32
BF16 = jnp.bfloat16


def _load_weights(w_in_hbm, w_out_hbm, win_scr, wout_scr, stage, sem):
    ch = stage.shape[1]
    grp = lax.broadcasted_iota(jnp.int32, (1, LANES), 1) // HALF
    q0, v0 = 3 * A_WIDTH, 3 * A_WIDTH + 2 * B_WIDTH

    def stream(src_hbm, dst_scr, n_cols, reorder):
        n_chunks = src_hbm.shape[1] // ch

        def chunk_copy(c, slot):
            return pltpu.make_async_copy(src_hbm.at[0, pl.ds(c * ch, ch), :],
                                         stage.at[slot, :, pl.ds(0, n_cols)], sem.at[slot])

        depth = stage.shape[0]
        for c0 in range(depth - 1):
            chunk_copy(c0, c0).start()

        def body(c, carry):
            slot = c % depth
            ahead = c + depth - 1

            @pl.when(ahead < n_chunks)
            def _():
                chunk_copy(ahead, ahead % depth).start()

            chunk_copy(c, slot).wait()
            rows = pl.ds(pl.multiple_of(c * ch, ch), ch)
            for cb in range(n_cols // LANES):
                t = stage[slot, :, cb * LANES:(cb + 1) * LANES]
                if reorder and q0 <= cb * LANES < v0:
                    t = jnp.where(grp == 1, pltpu.roll(t, LANES - HALF, 1),
                                  jnp.where(grp == 2, pltpu.roll(t, HALF, 1), t))
                dst_scr[rows, cb * LANES:(cb + 1) * LANES] = t.astype(BF16)
            return carry

        lax.fori_loop(0, n_chunks, body, 0)

    stream(w_in_hbm, win_scr, IN_COLS, True)
    stream(w_out_hbm, wout_scr, D_MODEL, False)


def _adaln_kernel(c_ref, w_ref, b_ref, mod_ref):
    c = c_ref[...]
    c_act = c * jax.nn.sigmoid(c)
    mod_ref[...] = jnp.dot(c_act.astype(BF16), w_ref[...].astype(BF16),
                           preferred_element_type=F32) + b_ref[...]


def _silu(z):
    return z * jax.nn.sigmoid(z)


def _head_mean_sq(t, bd_ref):
    sq = (t * t).astype(BF16)
    halves = [jnp.dot(sq[:, h * MXU_DIM:(h + 1) * MXU_DIM], bd_ref[...],
                      preferred_element_type=F32) for h in range(B_WIDTH // MXU_DIM)]
    return jnp.concatenate(halves, axis=-1) * (1.0 / B_HEAD_DIM)


def _rope(t, cos, sin_signed):
    cols = []
    for cb in range(B_WIDTH // LANES):
        tc = t[:, cb * LANES:(cb + 1) * LANES]
        cols.append(tc * cos + pltpu.roll(tc, LANES // 2, 1) * sin_signed)
    return jnp.concatenate(cols, axis=-1)


def _rope_tables(pos_quarters, invf, lane):
    n_grp = LANES // HALF
    qrows = SEQ_TILE // n_grp
    grp = lane // HALF
    pos_t = pos_quarters.T
    packed = pos_t[:qrows, n_grp - 1:n_grp]
    for gi in range(n_grp - 2, -1, -1):
        packed = jnp.where(grp == gi, pos_t[:qrows, gi:gi + 1], packed)
    ang = packed * invf
    tables = []
    for packed_tab in (jnp.cos(ang), jnp.sin(ang)):
        parts = []
        for gi in range(n_grp):
            t = jnp.where(grp == gi, packed_tab, 0.0)
            t = t + pltpu.roll(t, 2 * HALF, 1)
            parts.append(t + pltpu.roll(t, HALF, 1))
        tables.append(jnp.concatenate(parts, axis=0))
    return tables


def _norm_weight_lanes(w):
    lo, hi = w[:, :HALF], w[:, HALF:]
    block = jnp.concatenate([lo, lo, hi, hi], axis=1)
    return jnp.concatenate([block] * (B_WIDTH // LANES), axis=1)


def _across_sublanes(x, op):
    for shift in (4, 2, 1):
        x = op(x, pltpu.roll(x, shift, 0))
    return x


def _layer_kernel(x_ref, mod_ref, pos_ref, invf_ref, normw_ref, w_in_hbm, sgu_ref, ws_ref,
                  bs_ref, qnw_ref, knw_ref, lq1_ref, lk1_ref, lq2_ref, lk2_ref, subln_ref,
                  w_out_hbm, bd_ref, out_ref,
                  k_scr, vt_scr, qblk_scr, m_scr, acc_scr, mixed_scr, s_scr, p_scr,
                  alpha_scr, smax_scr, win_ref, wout_ref, stage_scr, stage_sem, nw_scr,
                  bst_scr):
    @pl.when((pl.program_id(0) == 0) & (pl.program_id(1) == 0))
    def _():
        _load_weights(w_in_hbm, w_out_hbm, win_ref, wout_ref, stage_scr, stage_sem)
        nw_scr[0:1, :] = _norm_weight_lanes(qnw_ref[...])
        nw_scr[1:2, :] = _norm_weight_lanes(knw_ref[...])
        bs_pad = jnp.concatenate(
            [bs_ref[...], jnp.zeros((SUBLANES - A_HEADS, CHUNK), F32)], axis=0)
        bst_scr[:, 0:SUBLANES] = bs_pad.T

    refs = dict(locals())
    tiles = [_TilePhases(t, refs) for t in range(TILES_PER_STEP)]
    tiles[0].activations()
    tiles[0].qk_projections()
    tiles[0].before_loop()
    for t, tile in enumerate(tiles):
        nxt = tiles[t + 1] if t + 1 < TILES_PER_STEP else None
        tile.key_loop()
        tile.last_value_product_and_out_a()
        if nxt is not None:
            nxt.activations()
            nxt.qk_projections()
        tile.finalise()
        tile.out_b()
        if nxt is not None:
            nxt.before_loop()


class _TilePhases:
    def __init__(self, t, refs):
        self.__dict__.update(refs)
        self.t = t
        self.par = t % 2
        self.base = pl.program_id(1) * TILES_PER_STEP
        self.i = self.base + t
        self.tile_rows = slice(t * SEQ_TILE, (t + 1) * SEQ_TILE)

    def activations(self):
        x_ref, tile_rows = self.x_ref, self.tile_rows
        b = pl.program_id(0)
        shift = self.mod_ref[0, pl.ds(b, 1), :]
        scale = self.mod_ref[1, pl.ds(b, 1), :]
        self.gate = self.mod_ref[2, pl.ds(b, 1), :]
        sq_lanes = None
        for c0 in range(0, D_MODEL, LANES):
            x_c = x_ref[0, tile_rows, c0:c0 + LANES]
            sq_lanes = x_c * x_c if sq_lanes is None else sq_lanes + x_c * x_c
        sum_sq = jnp.sum(sq_lanes, axis=-1, keepdims=True)
        inv_rms = lax.rsqrt(sum_sq * (1.0 / D_MODEL) + NORM_EPS)
        hb_cols = []
        for c0 in range(0, D_MODEL, MXU_DIM):
            cs = slice(c0, c0 + MXU_DIM)
            h_c = ((x_ref[0, tile_rows, cs] * inv_rms * self.normw_ref[:, cs])
                   * (1.0 + scale[:, cs]) + shift[:, cs])
            hb_cols.append(h_c.astype(BF16))
        self.hb = jnp.concatenate(hb_cols, axis=1)

    def proj(self, c0, width):
        return jnp.dot(self.hb, self.win_ref[:, c0:c0 + width], preferred_element_type=F32)

    def qk_projections(self):
        self.qb = self.proj(3 * A_WIDTH, B_WIDTH)
        self.kb = self.proj(3 * A_WIDTH + B_WIDTH, B_WIDTH)

    def scores(self, g, j):
        tk, tq = SEQ_TILE, SEQ_TILE
        kp = self.k_scr[j, :, g * LANES:(g + 1) * LANES]
        s = jnp.dot(kp, self.qblk_scr[g], preferred_element_type=F32)
        self.s_scr[g] = s
        self.smax_scr[g] = jnp.max(s.reshape(tk // SUBLANES, SUBLANES, 2 * tq), axis=0)

    def softmax(self, g, slot, masked):
        tk, tq = SEQ_TILE, SEQ_TILE
        rows = 2 * SUBLANES
        s_scr, m_scr, par = self.s_scr, self.m_scr, self.par

        def chunk(c):
            sc = s_scr[g, c * rows:(c + 1) * rows, :]
            if masked:
                kv_row = lax.broadcasted_iota(jnp.int32, (rows, 2 * tq), 0) + c * rows
                q_col = lax.broadcasted_iota(jnp.int32, (rows, 2 * tq), 1) % tq
                sc = jnp.where(kv_row <= q_col, sc, -jnp.inf)
            return sc.reshape(2, SUBLANES, 2 * tq)

        if masked:
            col_max = jnp.max(chunk(0), axis=0)
            for c in range(1, tk // rows):
                col_max = jnp.maximum(col_max, jnp.max(chunk(c), axis=0))
        else:
            col_max = self.smax_scr[g]
        m_old = m_scr[par, g]
        m_new = jnp.maximum(m_old, _across_sublanes(col_max, jnp.maximum))
        alpha = jnp.exp2(m_old - m_new)
        m_scr[par, g] = m_new
        self.alpha_scr[slot] = alpha
        for c in range(tk // rows):
            pc = jnp.exp2(chunk(c) - m_new[None])
            self.p_scr[slot, c * rows:(c + 1) * rows, :] = pc.reshape(rows, 2 * tq)

    def value_update(self, g, j, slot):
        tq = SEQ_TILE
        acc_scr, par = self.acc_scr, self.par
        vt = self.vt_scr[j, g]
        alpha = self.alpha_scr[slot]
        for st in range(2):
            pv = lax.dot_general(vt, self.p_scr[slot, :, st * tq:(st + 1) * tq],
                                 (((1,), (0,)), ((), ())), preferred_element_type=F32)
            a = alpha[:, st * tq:(st + 1) * tq]
            acc = acc_scr[par, 2 * g + st].reshape(V_EXT // SUBLANES, SUBLANES, tq)
            acc_scr[par, 2 * g + st] = (acc * a[None]).reshape(V_EXT, tq) + pv

    def unit(self, g, masked, tile, succ_tile, prev_tile):
        ahead = (g + 2) % B_HEADS
        if g + 2 < B_HEADS:
            self.scores(ahead, tile)
        elif succ_tile is not None:
            self.scores(ahead, succ_tile)
        if prev_tile is not None:
            self.value_update((g - 1) % B_HEADS, prev_tile, (g - 1) % 2)
        self.softmax(g, g % 2, masked)

    def before_loop(self):
        tq, tk, i, par = SEQ_TILE, SEQ_TILE, self.i, self.par
        qb, kb = self.qb, self.kb
        lane = lax.broadcasted_iota(jnp.int32, (1, LANES), 1)
        cos, sin = _rope_tables(self.pos_ref[self.t].astype(F32), self.invf_ref[...], lane)
        sin_signed = jnp.where(lane < LANES // 2, -sin, sin)
        q = qb * lax.rsqrt(_head_mean_sq(qb, self.bd_ref) + NORM_EPS) * self.nw_scr[0:1, :]
        k = kb * lax.rsqrt(_head_mean_sq(kb, self.bd_ref) + NORM_EPS) * self.nw_scr[1:2, :]
        q = _rope(q, cos, sin_signed) * (B_HEAD_DIM ** -0.5 * LOG2E)
        k = _rope(k, cos, sin_signed)
        self.k_scr[i] = k.astype(BF16)

        qt = q.T
        pair_row = lax.broadcasted_iota(jnp.int32, (LANES, tq), 0)
        is_a = (pair_row // HALF) % 2 == 0
        for g in range(B_HEADS):
            qp = qt[g * LANES:(g + 1) * LANES, :]
            self.qblk_scr[g] = jnp.concatenate(
                [jnp.where(is_a, qp, 0.0), jnp.where(is_a, 0.0, qp)], axis=1).astype(BF16)

        self.m_scr[par] = jnp.full(self.m_scr.shape[1:], -jnp.inf, F32)
        self.acc_scr[par] = jnp.zeros(self.acc_scr.shape[1:], F32)

        vb = self.proj(3 * A_WIDTH + 2 * B_WIDTH, B_WIDTH)
        vbt = vb.T.astype(BF16)
        for g in range(B_HEADS):
            self.vt_scr[i, g, :B_V_DIM, :] = vbt[g * B_V_DIM:(g + 1) * B_V_DIM, :]
            self.vt_scr[i, g, B_V_DIM:, :] = jnp.ones((V_EXT - B_V_DIM, tk), BF16)
        va = self.proj(A_WIDTH, A_WIDTH)
        self.scores(0, i)
        self.scores(1, i)
        ua = self.proj(0, A_WIDTH)
        self.unit(0, True, i, 0, None)
        za = self.proj(2 * A_WIDTH, A_WIDTH)
        self.unit(1, True, i, 0, i)
        zb = self.proj(3 * A_WIDTH + 3 * B_WIDTH, B_WIDTH)
        after_diag = i - 1 if self.t > 0 else 0
        self.unit(2, True, i, after_diag, i)

        row_c = lax.broadcasted_iota(jnp.int32, (CHUNK, CHUNK), 0)
        col_c = lax.broadcasted_iota(jnp.int32, (CHUNK, CHUNK), 1)
        tril = row_c >= col_c
        for hh in range(A_HEADS):
            sl = slice(hh * A_HEAD_DIM, (hh + 1) * A_HEAD_DIM)
            v = va[:, sl]
            vn = (v * lax.rsqrt(jnp.mean(v * v, axis=-1, keepdims=True) + NORM_EPS)
                  * self.sgu_ref[hh:hh + 1, :])
            vnb = vn.astype(BF16)
            ws = jnp.where(tril, self.ws_ref[hh], 0.0).astype(BF16)
            bias = self.bst_scr[:, hh:hh + 1]
            n_chunks = tq // CHUNK
            chunks = jnp.concatenate(
                [vnb[c * CHUNK:(c + 1) * CHUNK, :] for c in range(n_chunks)], axis=1)
            mixed_chunks = jnp.dot(ws, chunks, preferred_element_type=F32)
            mix = jnp.concatenate([mixed_chunks[:, c * A_HEAD_DIM:(c + 1) * A_HEAD_DIM] + bias
                                   for c in range(n_chunks)], axis=0)
            a_out = ua[:, sl] * mix * _silu(za[:, sl])
            self.mixed_scr[par, :, sl] = a_out.astype(BF16)
        self.zb_gate = _silu(zb)
        self.unit(3, True, i, after_diag, i)
        for s in range(self.t):
            tile = i - 1 - s
            succ = tile - 1 if s + 1 < self.t else 0
            self.tile_units(tile, succ, tile + 1)

    def tile_units(self, tile, succ_tile, prev_tile):
        self.unit(0, False, tile, succ_tile, prev_tile)
        for g in range(1, B_HEADS):
            self.unit(g, False, tile, succ_tile, tile)

    def key_loop(self):
        base = self.base

        def pair_body(jj, carry):
            j = 2 * jj
            self.tile_units(j, j + 1, jnp.where(j == 0, base, j - 1))
            self.tile_units(j + 1, j + 2, j)
            return carry

        lax.fori_loop(0, base // 2, pair_body, 0)

    def last_value_product_and_out_a(self):
        last = B_HEADS - 1
        base = self.base
        self.value_update(last, jnp.where(base > 0, base - 1, base), last % 2)
        rows = self.tile_rows
        self.out_ref[0, rows, :] = self.x_ref[0, rows, :] + self.gate * jnp.dot(
            self.mixed_scr[self.par, :, :A_WIDTH], self.wout_ref[:A_WIDTH, :],
            preferred_element_type=F32)

    def finalise(self):
        tq, par, acc_scr = SEQ_TILE, self.par, self.acc_scr
        lam = (jnp.exp(jnp.sum(self.lq1_ref[...] * self.lk1_ref[...], axis=-1, keepdims=True))
               - jnp.exp(jnp.sum(self.lq2_ref[...] * self.lk2_ref[...], axis=-1, keepdims=True))
               + LAMBDA_INIT)
        for g in range(B_HEADS):
            inv_l1 = 1.0 / acc_scr[par, 2 * g, B_V_DIM:B_V_DIM + SUBLANES, :]
            inv_l2 = 1.0 / acc_scr[par, 2 * g + 1, B_V_DIM:B_V_DIM + SUBLANES, :]
            acc1 = acc_scr[par, 2 * g, :B_V_DIM, :].reshape(B_V_DIM // SUBLANES, SUBLANES, tq)
            acc2 = acc_scr[par, 2 * g + 1, :B_V_DIM, :].reshape(
                B_V_DIM // SUBLANES, SUBLANES, tq)
            ot = acc1 * inv_l1[None] - lam * (acc2 * inv_l2[None])
            o = ot.reshape(B_V_DIM, tq).T
            o = (o * lax.rsqrt(jnp.mean(o * o, axis=-1, keepdims=True) + SUBLN_EPS)
                 * self.subln_ref[...])
            o = o * (1.0 - LAMBDA_INIT)
            vsl = slice(g * B_V_DIM, (g + 1) * B_V_DIM)
            b_out = o * self.zb_gate[:, vsl]
            self.mixed_scr[par, :, A_WIDTH + g * B_V_DIM:A_WIDTH + (g + 1) * B_V_DIM] = (
                b_out.astype(BF16))

    def out_b(self):
        self.out_ref[0, self.tile_rows, :] += self.gate * jnp.dot(
            self.mixed_scr[self.par, :, A_WIDTH:], self.wout_ref[A_WIDTH:, :],
            preferred_element_type=F32)


def kernel(x, c, positions, norm_w, w_ada, b_ada, w_in, sgu_norm_w, w_s, b_s, q_norm_w, k_norm_w,
           lambda_q1, lambda_k1, lambda_q2, lambda_k2, subln_w, w_out):
    bsz, seq, d = x.shape
    assert d == D_MODEL and seq % (TILES_PER_STEP * SEQ_TILE) == 0 and norm_w.shape[0] == 1
    assert TILES_PER_STEP % 2 == 0
    nt = seq // SEQ_TILE

    mod = pl.pallas_call(
        _adaln_kernel,
        grid=(3,),
        in_specs=[pl.BlockSpec((bsz, d), lambda n: (0, 0)),
                  pl.BlockSpec((None, d, d), lambda n: (0, 0, n)),
                  pl.BlockSpec((1, d), lambda n: (0, n))],
        out_specs=pl.BlockSpec((None, bsz, d), lambda n: (n, 0, 0)),
        out_shape=jax.ShapeDtypeStruct((3, bsz, d), F32),
        name="adaln_mod",
    )(c, w_ada, b_ada)

    n_grp = LANES // HALF
    pos_q = jnp.pad(positions.reshape(bsz, nt, n_grp, SEQ_TILE // n_grp),
                    ((0, 0), (0, 0), (0, SUBLANES - n_grp), (0, LANES - SEQ_TILE // n_grp)))

    inv_freq = ROPE_THETA ** (-jnp.arange(0, B_HEAD_DIM, 2, dtype=F32) / B_HEAD_DIM)
    invf = jnp.tile(inv_freq, LANES // HALF)[None, :]
    lane_head = (np.arange(MXU_DIM) // LANES) * 2 + (np.arange(MXU_DIM) // HALF) % 2
    bd = jnp.asarray(lane_head[:, None] == lane_head[None, :], dtype=BF16)


    const2 = lambda b, i: (0, 0)
    const3 = lambda b, i: (0, 0, 0)
    step_rows = TILES_PER_STEP * SEQ_TILE
    out = pl.pallas_call(
        _layer_kernel,
        grid=(bsz, nt // TILES_PER_STEP),
        in_specs=[
            pl.BlockSpec((1, step_rows, d), lambda b, i: (b, i, 0)),
            pl.BlockSpec((3, bsz, d), const3),
            pl.BlockSpec((None, TILES_PER_STEP, SUBLANES, LANES),
                         lambda b, i: (b, i, 0, 0)),
            pl.BlockSpec((1, LANES), const2),
            pl.BlockSpec((1, d), const2),
            pl.BlockSpec(memory_space=pl.ANY),
            pl.BlockSpec((A_HEADS, A_HEAD_DIM), const2),
            pl.BlockSpec((A_HEADS, CHUNK, CHUNK), const3),
            pl.BlockSpec((A_HEADS, CHUNK), const2),
            pl.BlockSpec((1, B_HEAD_DIM), const2),
            pl.BlockSpec((1, B_HEAD_DIM), const2),
            pl.BlockSpec((1, B_HEAD_DIM), const2),
            pl.BlockSpec((1, B_HEAD_DIM), const2),
            pl.BlockSpec((1, B_HEAD_DIM), const2),
            pl.BlockSpec((1, B_HEAD_DIM), const2),
            pl.BlockSpec((1, B_V_DIM), const2),
            pl.BlockSpec(memory_space=pl.ANY),
            pl.BlockSpec((MXU_DIM, MXU_DIM), const2),
        ],
        out_specs=pl.BlockSpec((1, step_rows, d), lambda b, i: (b, i, 0)),
        out_shape=jax.ShapeDtypeStruct((bsz, seq, d), F32),
        scratch_shapes=[
            pltpu.VMEM((nt, SEQ_TILE, B_WIDTH), BF16),
            pltpu.VMEM((nt, B_HEADS, V_EXT, SEQ_TILE), BF16),
            pltpu.VMEM((B_HEADS, LANES, 2 * SEQ_TILE), BF16),
            pltpu.VMEM((2, B_HEADS, SUBLANES, 2 * SEQ_TILE), F32),
            pltpu.VMEM((2, 2 * B_HEADS, V_EXT, SEQ_TILE), F32),
            pltpu.VMEM((2, SEQ_TILE, A_WIDTH + B_WIDTH), BF16),
            pltpu.VMEM((B_HEADS, SEQ_TILE, 2 * SEQ_TILE), F32),
            pltpu.VMEM((2, SEQ_TILE, 2 * SEQ_TILE), F32),
            pltpu.VMEM((2, SUBLANES, 2 * SEQ_TILE), F32),
            pltpu.VMEM((B_HEADS, SUBLANES, 2 * SEQ_TILE), F32),
            pltpu.VMEM((d, IN_COLS), BF16),
            pltpu.VMEM((A_WIDTH + B_WIDTH, d), BF16),
            pltpu.VMEM((WEIGHT_STAGE_SLOTS, WEIGHT_CHUNK_ROWS, IN_COLS), F32),
            pltpu.SemaphoreType.DMA((WEIGHT_STAGE_SLOTS,)),
            pltpu.VMEM((SUBLANES, B_WIDTH), F32),
            pltpu.VMEM((CHUNK, LANES), F32),
        ],
        compiler_params=pltpu.CompilerParams(
            dimension_semantics=("arbitrary", "arbitrary"),
            vmem_limit_bytes=VMEM_LIMIT_BYTES),
        name="fused_layer",
    )(x, mod, pos_q, invf, norm_w, w_in,
      sgu_norm_w.reshape(A_HEADS, A_HEAD_DIM), w_s.reshape(A_HEADS, CHUNK, CHUNK),
      b_s.reshape(A_HEADS, CHUNK), q_norm_w, k_norm_w,
      lambda_q1, lambda_k1, lambda_q2, lambda_k2, subln_w, w_out, bd)
    return out
```

```python
import math

import numpy as np
import jax
import jax.numpy as jnp
from jax import lax
from jax.experimental import pallas as pl
from jax.experimental.pallas import tpu as pltpu

D_MODEL = 1024
A_WIDTH = 512
B_WIDTH = 512
A_HEADS = 4
A_HEAD_DIM = 128
CHUNK = 128
B_HEADS = 4
B_HEAD_DIM = 64
B_V_DIM = 128
ROPE_THETA = 10000.0
NORM_EPS = 1e-6
SUBLN_EPS = 1e-5
IN_COLS = 3 * A_WIDTH + 4 * B_WIDTH
LAMBDA_INIT = 0.8 - 0.6 * math.exp(-0.3 * 0)
LOG2E = math.log2(math.e)

LANES = 128
SUBLANES = 8
MXU_DIM = 256
SEQ_TILE = 256
TILES_PER_STEP = 4
VMEM_LIMIT_BYTES = 56 * 1024 * 1024
HALF = B_HEAD_DIM // 2
BF16_ROWS = 16
V_EXT = B_V_DIM + BF16_ROWS
WEIGHT_CHUNK_ROWS = 64
WEIGHT_STAGE_SLOTS = 4

F32 = jnp.float32
BF16 = jnp.bfloat16


def _load_weights(w_in_hbm, w_out_hbm, win_scr, wout_scr, stage, sem):
    ch = stage.shape[1]
    grp = lax.broadcasted_iota(jnp.int32, (1, LANES), 1) // HALF
    q0, v0 = 3 * A_WIDTH, 3 * A_WIDTH + 2 * B_WIDTH

    def stream(src_hbm, dst_scr, n_cols, reorder):
        n_chunks = src_hbm.shape[1] // ch

        def chunk_copy(c, slot):
            return pltpu.make_async_copy(src_hbm.at[0, pl.ds(c * ch, ch), :],
                                         stage.at[slot, :, pl.ds(0, n_cols)], sem.at[slot])

        depth = stage.shape[0]
        for c0 in range(depth - 1):
            chunk_copy(c0, c0).start()

        def body(c, carry):
            slot = c % depth
            ahead = c + depth - 1

            @pl.when(ahead < n_chunks)
            def _():
                chunk_copy(ahead, ahead % depth).start()

            chunk_copy(c, slot).wait()
            rows = pl.ds(pl.multiple_of(c * ch, ch), ch)
            for cb in range(n_cols // LANES):
                t = stage[slot, :, cb * LANES:(cb + 1) * LANES]
                if reorder and q0 <= cb * LANES < v0:
                    t = jnp.where(grp == 1, pltpu.roll(t, LANES - HALF, 1),
                                  jnp.where(grp == 2, pltpu.roll(t, HALF, 1), t))
                dst_scr[rows, cb * LANES:(cb + 1) * LANES] = t.astype(BF16)
            return carry

        lax.fori_loop(0, n_chunks, body, 0)

    stream(w_in_hbm, win_scr, IN_COLS, True)
    stream(w_out_hbm, wout_scr, D_MODEL, False)


def _adaln_kernel(c_ref, w_ref, b_ref, mod_ref):
    c = c_ref[...]
    c_act = c * jax.nn.sigmoid(c)
    mod_ref[...] = jnp.dot(c_act.astype(BF16), w_ref[...].astype(BF16),
                           preferred_element_type=F32) + b_ref[...]


def _silu(z):
    return z * jax.nn.sigmoid(z)


def _head_mean_sq(t, bd_ref):
    sq = (t * t).astype(BF16)
    halves = [jnp.dot(sq[:, h * MXU_DIM:(h + 1) * MXU_DIM], bd_ref[...],
                      preferred_element_type=F32) for h in range(B_WIDTH // MXU_DIM)]
    return jnp.concatenate(halves, axis=-1) * (1.0 / B_HEAD_DIM)


def _rope(t, cos, sin_signed):
    cols = []
    for cb in range(B_WIDTH // LANES):
        tc = t[:, cb * LANES:(cb + 1) * LANES]
        cols.append(tc * cos + pltpu.roll(tc, LANES // 2, 1) * sin_signed)
    return jnp.concatenate(cols, axis=-1)


def _rope_tables(pos_quarters, invf, lane):
    n_grp = LANES // HALF
    qrows = SEQ_TILE // n_grp
    grp = lane // HALF
    pos_t = pos_quarters.T
    packed = pos_t[:qrows, n_grp - 1:n_grp]
    for gi in range(n_grp - 2, -1, -1):
        packed = jnp.where(grp == gi, pos_t[:qrows, gi:gi + 1], packed)
    ang = packed * invf
    tables = []
    for packed_tab in (jnp.cos(ang), jnp.sin(ang)):
        parts = []
        for gi in range(n_grp):
            t = jnp.where(grp == gi, packed_tab, 0.0)
            t = t + pltpu.roll(t, 2 * HALF, 1)
            parts.append(t + pltpu.roll(t, HALF, 1))
        tables.append(jnp.concatenate(parts, axis=0))
    return tables


def _norm_weight_lanes(w):
    lo, hi = w[:, :HALF], w[:, HALF:]
    block = jnp.concatenate([lo, lo, hi, hi], axis=1)
    return jnp.concatenate([block] * (B_WIDTH // LANES), axis=1)


def _across_sublanes(x, op):
    for shift in (4, 2, 1):
        x = op(x, pltpu.roll(x, shift, 0))
    return x


def _layer_kernel(x_ref, mod_ref, pos_ref, invf_ref, normw_ref, w_in_hbm, sgu_ref, ws_ref,
                  bs_ref, qnw_ref, knw_ref, lq1_ref, lk1_ref, lq2_ref, lk2_ref, subln_ref,
                  w_out_hbm, bd_ref, out_ref,
                  k_scr, vt_scr, qblk_scr, m_scr, acc_scr, mixed_scr, s_scr, p_scr,
                  alpha_scr, smax_scr, win_ref, wout_ref, stage_scr, stage_sem, nw_scr,
                  bst_scr):
    @pl.when((pl.program_id(0) == 0) & (pl.program_id(1) == 0))
    def _():
        _load_weights(w_in_hbm, w_out_hbm, win_ref, wout_ref, stage_scr, stage_sem)
        nw_scr[0:1, :] = _norm_weight_lanes(qnw_ref[...])
        nw_scr[1:2, :] = _norm_weight_lanes(knw_ref[...])
        bs_pad = jnp.concatenate(
            [bs_ref[...], jnp.zeros((SUBLANES - A_HEADS, CHUNK), F32)], axis=0)
        bst_scr[:, 0:SUBLANES] = bs_pad.T

    refs = dict(locals())
    tiles = [_TilePhases(t, refs) for t in range(TILES_PER_STEP)]
    tiles[0].activations()
    tiles[0].qk_projections()
    tiles[0].before_loop()
    for t, tile in enumerate(tiles):
        nxt = tiles[t + 1] if t + 1 < TILES_PER_STEP else None
        tile.key_loop()
        tile.last_value_product_and_out_a()
        if nxt is not None:
            nxt.activations()
            nxt.qk_projections()
        tile.finalise()
        tile.out_b()
        if nxt is not None:
            nxt.before_loop()


class _TilePhases:
    def __init__(self, t, refs):
        self.__dict__.update(refs)
        self.t = t
        self.par = t % 2
        self.base = pl.program_id(1) * TILES_PER_STEP
        self.i = self.base + t
        self.tile_rows = slice(t * SEQ_TILE, (t + 1) * SEQ_TILE)

    def activations(self):
        x_ref, tile_rows = self.x_ref, self.tile_rows
        b = pl.program_id(0)
        shift = self.mod_ref[0, pl.ds(b, 1), :]
        scale = self.mod_ref[1, pl.ds(b, 1), :]
        self.gate = self.mod_ref[2, pl.ds(b, 1), :]
        sq_lanes = None
        for c0 in range(0, D_MODEL, LANES):
            x_c = x_ref[0, tile_rows, c0:c0 + LANES]
            sq_lanes = x_c * x_c if sq_lanes is None else sq_lanes + x_c * x_c
        sum_sq = jnp.sum(sq_lanes, axis=-1, keepdims=True)
        inv_rms = lax.rsqrt(sum_sq * (1.0 / D_MODEL) + NORM_EPS)
        hb_cols = []
        for c0 in range(0, D_MODEL, MXU_DIM):
            cs = slice(c0, c0 + MXU_DIM)
            h_c = ((x_ref[0, tile_rows, cs] * inv_rms * self.normw_ref[:, cs])
                   * (1.0 + scale[:, cs]) + shift[:, cs])
            hb_cols.append(h_c.astype(BF16))
        self.hb = jnp.concatenate(hb_cols, axis=1)

    def proj(self, c0, width):
        return jnp.dot(self.hb, self.win_ref[:, c0:c0 + width], preferred_element_type=F32)

    def qk_projections(self):
        self.qb = self.proj(3 * A_WIDTH, B_WIDTH)
        self.kb = self.proj(3 * A_WIDTH + B_WIDTH, B_WIDTH)

    def scores(self, g, j, masked=False):
        tk, tq = SEQ_TILE, SEQ_TILE
        kp = self.k_scr[j, :, g * LANES:(g + 1) * LANES]
        s = jnp.dot(kp, self.qblk_scr[g], preferred_element_type=F32)
        if masked:
            kv_row = lax.broadcasted_iota(jnp.int32, (tk, 2 * tq), 0)
            q_col = lax.broadcasted_iota(jnp.int32, (tk, 2 * tq), 1) % tq
            s = jnp.where(kv_row <= q_col, s, -jnp.inf)
        self.s_scr[g] = s
        self.smax_scr[g] = jnp.max(s.reshape(tk // SUBLANES, SUBLANES, 2 * tq), axis=0)

    def softmax(self, g, slot):
        tk, tq = SEQ_TILE, SEQ_TILE
        rows = 2 * SUBLANES
        s_scr, m_scr, par = self.s_scr, self.m_scr, self.par

        def chunk(c):
            return s_scr[g, c * rows:(c + 1) * rows, :].reshape(2, SUBLANES, 2 * tq)

        m_old = m_scr[par, g]
        m_new = jnp.maximum(m_old, _across_sublanes(self.smax_scr[g], jnp.maximum))
        alpha = jnp.exp2(m_old - m_new)
        m_scr[par, g] = m_new
        self.alpha_scr[slot] = alpha
        for c in range(tk // rows):
            pc = jnp.exp2(chunk(c) - m_new[None])
            self.p_scr[slot, c * rows:(c + 1) * rows, :] = (
                pc.reshape(rows, 2 * tq).astype(BF16))

    def value_update(self, g, j, slot):
        tq = SEQ_TILE
        acc_scr, par = self.acc_scr, self.par
        vt = self.vt_scr[j, g]
        alpha = self.alpha_scr[slot]
        for st in range(2):
            pv = jnp.dot(vt, self.p_scr[slot, :, st * tq:(st + 1) * tq],
                         preferred_element_type=F32)
            a = alpha[:, st * tq:(st + 1) * tq]
            acc = acc_scr[par, 2 * g + st].reshape(V_EXT // SUBLANES, SUBLANES, tq)
            acc_scr[par, 2 * g + st] = (acc * a[None]).reshape(V_EXT, tq) + pv

    def unit(self, g, masked, tile, succ_tile, prev_tile):
        ahead = (g + 2) % B_HEADS
        if g + 2 < B_HEADS:
            self.scores(ahead, tile, masked)
        elif succ_tile is not None:
            self.scores(ahead, succ_tile)
        if prev_tile is not None:
            self.value_update((g - 1) % B_HEADS, prev_tile, (g - 1) % 2)
        self.softmax(g, g % 2)

    def before_loop(self):
        tq, tk, i, par = SEQ_TILE, SEQ_TILE, self.i, self.par
        qb, kb = self.qb, self.kb
        lane = lax.broadcasted_iota(jnp.int32, (1, LANES), 1)
        cos, sin = _rope_tables(self.pos_ref[self.t].astype(F32), self.invf_ref[...], lane)
        sin_signed = jnp.where(lane < LANES // 2, -sin, sin)
        q = qb * lax.rsqrt(_head_mean_sq(qb, self.bd_ref) + NORM_EPS) * self.nw_scr[0:1, :]
        k = kb * lax.rsqrt(_head_mean_sq(kb, self.bd_ref) + NORM_EPS) * self.nw_scr[1:2, :]
        q = _rope(q, cos, sin_signed) * (B_HEAD_DIM ** -0.5 * LOG2E)
        k = _rope(k, cos, sin_signed)
        self.k_scr[i] = k.astype(BF16)

        qt = q.T
        pair_row = lax.broadcasted_iota(jnp.int32, (LANES, tq), 0)
        is_a = (pair_row // HALF) % 2 == 0
        for g in range(B_HEADS):
            qp = qt[g * LANES:(g + 1) * LANES, :]
            self.qblk_scr[g] = jnp.concatenate(
                [jnp.where(is_a, qp, 0.0), jnp.where(is_a, 0.0, qp)], axis=1).astype(BF16)

        self.m_scr[par] = jnp.full(self.m_scr.shape[1:], -jnp.inf, F32)
        self.acc_scr[par] = jnp.zeros(self.acc_scr.shape[1:], F32)

        vb = self.proj(3 * A_WIDTH + 2 * B_WIDTH, B_WIDTH)
        vbt = vb.T.astype(BF16)
        for g in range(B_HEADS):
            self.vt_scr[i, g, :B_V_DIM, :] = vbt[g * B_V_DIM:(g + 1) * B_V_DIM, :]
            self.vt_scr[i, g, B_V_DIM:, :] = jnp.ones((V_EXT - B_V_DIM, tk), BF16)
        va = self.proj(A_WIDTH, A_WIDTH)
        self.scores(0, i, masked=True)
        self.scores(1, i, masked=True)
        ua = self.proj(0, A_WIDTH)
        self.unit(0, True, i, 0, None)
        za = self.proj(2 * A_WIDTH, A_WIDTH)
        self.unit(1, True, i, 0, i)
        zb = self.proj(3 * A_WIDTH + 3 * B_WIDTH, B_WIDTH)
        after_diag = i - 1 if self.t > 0 else 0
        self.unit(2, True, i, after_diag, i)

        row_c = lax.broadcasted_iota(jnp.int32, (CHUNK, CHUNK), 0)
        col_c = lax.broadcasted_iota(jnp.int32, (CHUNK, CHUNK), 1)
        tril = row_c >= col_c
        for hh in range(A_HEADS):
            sl = slice(hh * A_HEAD_DIM, (hh + 1) * A_HEAD_DIM)
            v = va[:, sl]
            vn = (v * lax.rsqrt(jnp.mean(v * v, axis=-1, keepdims=True) + NORM_EPS)
                  * self.sgu_ref[hh:hh + 1, :])
            vnb = vn.astype(BF16)
            ws = jnp.where(tril, self.ws_ref[hh], 0.0).astype(BF16)
            bias = self.bst_scr[:, hh:hh + 1]
            n_chunks = tq // CHUNK
            chunks = jnp.concatenate(
                [vnb[c * CHUNK:(c + 1) * CHUNK, :] for c in range(n_chunks)], axis=1)
            mixed_chunks = jnp.dot(ws, chunks, preferred_element_type=F32)
            mix = jnp.concatenate([mixed_chunks[:, c * A_HEAD_DIM:(c + 1) * A_HEAD_DIM] + bias
                                   for c in range(n_chunks)], axis=0)
            a_out = ua[:, sl] * mix * _silu(za[:, sl])
            self.mixed_scr[par, :, sl] = a_out.astype(BF16)
        self.zb_gate = _silu(zb)
        self.unit(3, True, i, after_diag, i)
        for s in range(self.t):
            tile = i - 1 - s
            succ = tile - 1 if s + 1 < self.t else 0
            self.tile_units(tile, succ, tile + 1)

    def tile_units(self, tile, succ_tile, prev_tile):
        self.unit(0, False, tile, succ_tile, prev_tile)
        for g in range(1, B_HEADS):
            self.unit(g, False, tile, succ_tile, tile)

    def key_loop(self):
        base = self.base

        def pair_body(jj, carry):
            j = 2 * jj
            self.tile_units(j, j + 1, jnp.where(j == 0, base, j - 1))
            self.tile_units(j + 1, j + 2, j)
            return carry

        lax.fori_loop(0, base // 2, pair_body, 0)

    def last_value_product_and_out_a(self):
        last = B_HEADS - 1
        base = self.base
        self.value_update(last, jnp.where(base > 0, base - 1, base), last % 2)
        rows = self.tile_rows
        self.out_ref[0, rows, :] = self.x_ref[0, rows, :] + self.gate * jnp.dot(
            self.mixed_scr[self.par, :, :A_WIDTH], self.wout_ref[:A_WIDTH, :],
            preferred_element_type=F32)

    def finalise(self):
        tq, par, acc_scr = SEQ_TILE, self.par, self.acc_scr
        lam = (jnp.exp(jnp.sum(self.lq1_ref[...] * self.lk1_ref[...], axis=-1, keepdims=True))
               - jnp.exp(jnp.sum(self.lq2_ref[...] * self.lk2_ref[...], axis=-1, keepdims=True))
               + LAMBDA_INIT)
        for g in range(B_HEADS):
            inv_l1 = 1.0 / acc_scr[par, 2 * g, B_V_DIM:B_V_DIM + SUBLANES, :]
            inv_l2 = 1.0 / acc_scr[par, 2 * g + 1, B_V_DIM:B_V_DIM + SUBLANES, :]
            acc1 = acc_scr[par, 2 * g, :B_V_DIM, :].reshape(B_V_DIM // SUBLANES, SUBLANES, tq)
            acc2 = acc_scr[par, 2 * g + 1, :B_V_DIM, :].reshape(
                B_V_DIM // SUBLANES, SUBLANES, tq)
            ot = acc1 * inv_l1[None] - lam * (acc2 * inv_l2[None])
            o = ot.reshape(B_V_DIM, tq).T
            o = (o * lax.rsqrt(jnp.mean(o * o, axis=-1, keepdims=True) + SUBLN_EPS)
                 * self.subln_ref[...])
            o = o * (1.0 - LAMBDA_INIT)
            vsl = slice(g * B_V_DIM, (g + 1) * B_V_DIM)
            b_out = o * self.zb_gate[:, vsl]
            self.mixed_scr[par, :, A_WIDTH + g * B_V_DIM:A_WIDTH + (g + 1) * B_V_DIM] = (
                b_out.astype(BF16))

    def out_b(self):
        self.out_ref[0, self.tile_rows, :] += self.gate * jnp.dot(
            self.mixed_scr[self.par, :, A_WIDTH:], self.wout_ref[A_WIDTH:, :],
            preferred_element_type=F32)


def kernel(x, c, positions, norm_w, w_ada, b_ada, w_in, sgu_norm_w, w_s, b_s, q_norm_w, k_norm_w,
           lambda_q1, lambda_k1, lambda_q2, lambda_k2, subln_w, w_out):
    bsz, seq, d = x.shape
    assert d == D_MODEL and seq % (TILES_PER_STEP * SEQ_TILE) == 0 and norm_w.shape[0] == 1
    assert TILES_PER_STEP % 2 == 0
    nt = seq // SEQ_TILE

    mod = pl.pallas_call(
        _adaln_kernel,
        grid=(3,),
        in_specs=[pl.BlockSpec((bsz, d), lambda n: (0, 0)),
                  pl.BlockSpec((None, d, d), lambda n: (0, 0, n)),
                  pl.BlockSpec((1, d), lambda n: (0, n))],
        out_specs=pl.BlockSpec((None, bsz, d), lambda n: (n, 0, 0)),
        out_shape=jax.ShapeDtypeStruct((3, bsz, d), F32),
        name="adaln_mod",
    )(c, w_ada, b_ada)

    n_grp = LANES // HALF
    pos_q = jnp.pad(positions.reshape(bsz, nt, n_grp, SEQ_TILE // n_grp),
                    ((0, 0), (0, 0), (0, SUBLANES - n_grp), (0, LANES - SEQ_TILE // n_grp)))

    inv_freq = ROPE_THETA ** (-jnp.arange(0, B_HEAD_DIM, 2, dtype=F32) / B_HEAD_DIM)
    invf = jnp.tile(inv_freq, LANES // HALF)[None, :]
    lane_head = (np.arange(MXU_DIM) // LANES) * 2 + (np.arange(MXU_DIM) // HALF) % 2
    bd = jnp.asarray(lane_head[:, None] == lane_head[None, :], dtype=BF16)


    const2 = lambda b, i: (0, 0)
    const3 = lambda b, i: (0, 0, 0)
    step_rows = TILES_PER_STEP * SEQ_TILE
    out = pl.pallas_call(
        _layer_kernel,
        grid=(bsz, nt // TILES_PER_STEP),
        in_specs=[
            pl.BlockSpec((1, step_rows, d), lambda b, i: (b, i, 0)),
            pl.BlockSpec((3, bsz, d), const3),
            pl.BlockSpec((None, TILES_PER_STEP, SUBLANES, LANES),
                         lambda b, i: (b, i, 0, 0)),
            pl.BlockSpec((1, LANES), const2),
            pl.BlockSpec((1, d), const2),
            pl.BlockSpec(memory_space=pl.ANY),
            pl.BlockSpec((A_HEADS, A_HEAD_DIM), const2),
            pl.BlockSpec((A_HEADS, CHUNK, CHUNK), const3),
            pl.BlockSpec((A_HEADS, CHUNK), const2),
            pl.BlockSpec((1, B_HEAD_DIM), const2),
            pl.BlockSpec((1, B_HEAD_DIM), const2),
            pl.BlockSpec((1, B_HEAD_DIM), const2),
            pl.BlockSpec((1, B_HEAD_DIM), const2),
            pl.BlockSpec((1, B_HEAD_DIM), const2),
            pl.BlockSpec((1, B_HEAD_DIM), const2),
            pl.BlockSpec((1, B_V_DIM), const2),
            pl.BlockSpec(memory_space=pl.ANY),
            pl.BlockSpec((MXU_DIM, MXU_DIM), const2),
        ],
        out_specs=pl.BlockSpec((1, step_rows, d), lambda b, i: (b, i, 0)),
        out_shape=jax.ShapeDtypeStruct((bsz, seq, d), F32),
        scratch_shapes=[
            pltpu.VMEM((nt, SEQ_TILE, B_WIDTH), BF16),
            pltpu.VMEM((nt, B_HEADS, V_EXT, SEQ_TILE), BF16),
            pltpu.VMEM((B_HEADS, LANES, 2 * SEQ_TILE), BF16),
            pltpu.VMEM((2, B_HEADS, SUBLANES, 2 * SEQ_TILE), F32),
            pltpu.VMEM((2, 2 * B_HEADS, V_EXT, SEQ_TILE), F32),
            pltpu.VMEM((2, SEQ_TILE, A_WIDTH + B_WIDTH), BF16),
            pltpu.VMEM((B_HEADS, SEQ_TILE, 2 * SEQ_TILE), F32),
            pltpu.VMEM((2, SEQ_TILE, 2 * SEQ_TILE), BF16),
            pltpu.VMEM((2, SUBLANES, 2 * SEQ_TILE), F32),
            pltpu.VMEM((B_HEADS, SUBLANES, 2 * SEQ_TILE), F32),
            pltpu.VMEM((d, IN_COLS), BF16),
            pltpu.VMEM((A_WIDTH + B_WIDTH, d), BF16),
            pltpu.VMEM((WEIGHT_STAGE_SLOTS, WEIGHT_CHUNK_ROWS, IN_COLS), F32),
            pltpu.SemaphoreType.DMA((WEIGHT_STAGE_SLOTS,)),
            pltpu.VMEM((SUBLANES, B_WIDTH), F32),
            pltpu.VMEM((CHUNK, LANES), F32),
        ],
        compiler_params=pltpu.CompilerParams(
            dimension_semantics=("arbitrary", "arbitrary"),
            vmem_limit_bytes=VMEM_LIMIT_BYTES),
        name="fused_layer",
    )(x, mod, pos_q, invf, norm_w, w_in,
      sgu_norm_w.reshape(A_HEADS, A_HEAD_DIM), w_s.reshape(A_HEADS, CHUNK, CHUNK),
      b_s.reshape(A_HEADS, CHUNK), q_norm_w, k_norm_w,
      lambda_q1, lambda_k1, lambda_q2, lambda_k2, subln_w, w_out, bd)
    return out
```

```python
import functools
import math

import numpy as np
import jax
import jax.numpy as jnp
from jax import lax
from jax.experimental import pallas as pl
from jax.experimental.pallas import tpu as pltpu

D_MODEL = 1024
A_WIDTH = 512
B_WIDTH = 512
A_HEADS = 4
A_HEAD_DIM = 128
CHUNK = 128
B_HEADS = 4
B_HEAD_DIM = 64
B_V_DIM = 128
ROPE_THETA = 10000.0
NORM_EPS = 1e-6
SUBLN_EPS = 1e-5
IN_COLS = 3 * A_WIDTH + 4 * B_WIDTH
LAMBDA_INIT = 0.8 - 0.6 * math.exp(-0.3 * 0)
LOG2E = math.log2(math.e)

LANES = 128
SUBLANES = 8
MXU_DIM = 256
SEQ_TILE = 256
TILES_PER_STEP = 4
VMEM_LIMIT_BYTES = 56 * 1024 * 1024
HALF = B_HEAD_DIM // 2
BF16_ROWS = 16
V_EXT = B_V_DIM + BF16_ROWS
ADALN_ROW_PARTS = 4
WEIGHT_CHUNK_ROWS = 64
WEIGHT_STAGE_SLOTS = 4

F32 = jnp.float32
BF16 = jnp.bfloat16


def _load_weights(w_in_hbm, w_out_hbm, win_scr, wout_scr, stage, sem):
    ch = stage.shape[1]
    grp = lax.broadcasted_iota(jnp.int32, (1, LANES), 1) // HALF
    q0, v0 = 3 * A_WIDTH, 3 * A_WIDTH + 2 * B_WIDTH

    def stream(src_hbm, dst_scr, n_cols, reorder):
        n_chunks = src_hbm.shape[1] // ch

        def chunk_copy(c, slot):
            return pltpu.make_async_copy(src_hbm.at[0, pl.ds(c * ch, ch), :],
                                         stage.at[slot, :, pl.ds(0, n_cols)], sem.at[slot])

        depth = stage.shape[0]
        for c0 in range(depth - 1):
            chunk_copy(c0, c0).start()

        def body(c, carry):
            slot = c % depth
            ahead = c + depth - 1

            @pl.when(ahead < n_chunks)
            def _():
                chunk_copy(ahead, ahead % depth).start()

            chunk_copy(c, slot).wait()
            rows = pl.ds(pl.multiple_of(c * ch, ch), ch)
            for cb in range(n_cols // LANES):
                t = stage[slot, :, cb * LANES:(cb + 1) * LANES]
                if reorder and q0 <= cb * LANES < v0:
                    t = jnp.where(grp == 1, pltpu.roll(t, LANES - HALF, 1),
                                  jnp.where(grp == 2, pltpu.roll(t, HALF, 1), t))
                dst_scr[rows, cb * LANES:(cb + 1) * LANES] = t.astype(BF16)
            return carry

        lax.fori_loop(0, n_chunks, body, 0)

    stream(w_in_hbm, win_scr, IN_COLS, True)
    stream(w_out_hbm, wout_scr, D_MODEL, False)


def _adaln_kernel(c_ref, *refs):
    w_refs, b_ref, mod_ref = refs[:ADALN_ROW_PARTS], refs[-2], refs[-1]
    c = c_ref[...]
    c_act = (c * jax.nn.sigmoid(c)).astype(BF16)
    rows = c.shape[1] // ADALN_ROW_PARTS
    acc = b_ref[...]
    for part, w_ref in enumerate(w_refs):
        acc = acc + jnp.dot(c_act[:, part * rows:(part + 1) * rows], w_ref[...].astype(BF16),
                            preferred_element_type=F32)
    mod_ref[...] = acc


def _silu(z):
    return z * jax.nn.sigmoid(z)


def _head_mean_sq(t, bd_ref):
    sq = (t * t).astype(BF16)
    halves = [jnp.dot(sq[:, h * MXU_DIM:(h + 1) * MXU_DIM], bd_ref[...],
                      preferred_element_type=F32) for h in range(B_WIDTH // MXU_DIM)]
    return jnp.concatenate(halves, axis=-1) * (1.0 / B_HEAD_DIM)


def _rope(t, cos, sin_signed):
    cols = []
    for cb in range(B_WIDTH // LANES):
        tc = t[:, cb * LANES:(cb + 1) * LANES]
        cols.append(tc * cos + pltpu.roll(tc, LANES // 2, 1) * sin_signed)
    return jnp.concatenate(cols, axis=-1)


def _rope_tables(pos_quarters, invf, lane):
    n_grp = LANES // HALF
    qrows = SEQ_TILE // n_grp
    grp = lane // HALF
    pos_t = pos_quarters.T
    packed = pos_t[:qrows, n_grp - 1:n_grp]
    for gi in range(n_grp - 2, -1, -1):
        packed = jnp.where(grp == gi, pos_t[:qrows, gi:gi + 1], packed)
    ang = packed * invf
    tables = []
    for packed_tab in (jnp.cos(ang), jnp.sin(ang)):
        parts = []
        for gi in range(n_grp):
            t = jnp.where(grp == gi, packed_tab, 0.0)
            t = t + pltpu.roll(t, 2 * HALF, 1)
            parts.append(t + pltpu.roll(t, HALF, 1))
        tables.append(jnp.concatenate(parts, axis=0))
    return tables


def _norm_weight_lanes(w):
    lo, hi = w[:, :HALF], w[:, HALF:]
    block = jnp.concatenate([lo, lo, hi, hi], axis=1)
    return jnp.concatenate([block] * (B_WIDTH // LANES), axis=1)


def _across_sublanes(x, op):
    for shift in (4, 2, 1):
        x = op(x, pltpu.roll(x, shift, 0))
    return x


def _layer_kernel(x_ref, mod_ref, pos_ref, invf_ref, normw_ref, w_in_hbm, sgu_ref, ws_ref,
                  bs_ref, qnw_ref, knw_ref, lq1_ref, lk1_ref, lq2_ref, lk2_ref, subln_ref,
                  w_out_hbm, bd_ref, out_ref,
                  k_scr, vt_scr, qblk_scr, m_scr, acc_scr, mixed_scr, s_scr, p_scr,
                  alpha_scr, smax_scr, win_ref, wout_ref, stage_scr, stage_sem, nw_scr,
                  bst_scr):
    @pl.when((pl.program_id(0) == 0) & (pl.program_id(1) == 0))
    def _():
        _load_weights(w_in_hbm, w_out_hbm, win_ref, wout_ref, stage_scr, stage_sem)
        nw_scr[0:1, :] = _norm_weight_lanes(qnw_ref[...])
        nw_scr[1:2, :] = _norm_weight_lanes(knw_ref[...])
        bs_pad = jnp.concatenate(
            [bs_ref[...], jnp.zeros((SUBLANES - A_HEADS, CHUNK), F32)], axis=0)
        bst_scr[:, 0:SUBLANES] = bs_pad.T

    refs = dict(locals())
    tiles = [_TilePhases(t, refs) for t in range(TILES_PER_STEP)]
    tiles[0].activations()
    tiles[0].qk_projections()
    tiles[0].before_loop()
    for t, tile in enumerate(tiles):
        nxt = tiles[t + 1] if t + 1 < TILES_PER_STEP else None
        tile.key_loop()
        tile.last_value_product_and_out_a()
        if nxt is not None:
            nxt.activations()
            nxt.qk_projections()
        tile.finalise()
        tile.out_b()
        if nxt is not None:
            nxt.before_loop()


class _TilePhases:
    def __init__(self, t, refs):
        self.__dict__.update(refs)
        self.t = t
        self.par = t % 2
        self.base = pl.program_id(1) * TILES_PER_STEP
        self.i = self.base + t
        self.tile_rows = slice(t * SEQ_TILE, (t + 1) * SEQ_TILE)

    def activations(self):
        x_ref, tile_rows = self.x_ref, self.tile_rows
        b = pl.program_id(0)
        shift = self.mod_ref[0, pl.ds(b, 1), :]
        scale = self.mod_ref[1, pl.ds(b, 1), :]
        self.gate = self.mod_ref[2, pl.ds(b, 1), :]
        sq_lanes = None
        for c0 in range(0, D_MODEL, LANES):
            x_c = x_ref[0, tile_rows, c0:c0 + LANES]
            sq_lanes = x_c * x_c if sq_lanes is None else sq_lanes + x_c * x_c
        sum_sq = jnp.sum(sq_lanes, axis=-1, keepdims=True)
        inv_rms = lax.rsqrt(sum_sq * (1.0 / D_MODEL) + NORM_EPS)
        hb_cols = []
        for c0 in range(0, D_MODEL, MXU_DIM):
            cs = slice(c0, c0 + MXU_DIM)
            h_c = ((x_ref[0, tile_rows, cs] * inv_rms * self.normw_ref[:, cs])
                   * (1.0 + scale[:, cs]) + shift[:, cs])
            hb_cols.append(h_c.astype(BF16))
        self.hb = jnp.concatenate(hb_cols, axis=1)

    def proj(self, c0, width):
        return jnp.dot(self.hb, self.win_ref[:, c0:c0 + width], preferred_element_type=F32)

    def qk_projections(self):
        self.qb = self.proj(3 * A_WIDTH, B_WIDTH)
        self.kb = self.proj(3 * A_WIDTH + B_WIDTH, B_WIDTH)

    def scores(self, g, j):
        tk, tq = SEQ_TILE, SEQ_TILE
        kp = self.k_scr[j, :, g * LANES:(g + 1) * LANES]
        s = jnp.dot(kp, self.qblk_scr[g], preferred_element_type=F32)
        self.s_scr[g] = s
        self.smax_scr[g] = jnp.max(s.reshape(tk // SUBLANES, SUBLANES, 2 * tq), axis=0)

    def softmax(self, g, slot, masked):
        tk, tq = SEQ_TILE, SEQ_TILE
        rows = 2 * SUBLANES
        s_scr, m_scr, par = self.s_scr, self.m_scr, self.par

        def chunk(c):
            sc = s_scr[g, c * rows:(c + 1) * rows, :]
            if masked:
                kv_row = lax.broadcasted_iota(jnp.int32, (rows, 2 * tq), 0) + c * rows
                q_col = lax.broadcasted_iota(jnp.int32, (rows, 2 * tq), 1) % tq
                sc = jnp.where(kv_row <= q_col, sc, -jnp.inf)
            return sc.reshape(2, SUBLANES, 2 * tq)

        if masked:
            col_max = jnp.max(chunk(0), axis=0)
            for c in range(1, tk // rows):
                col_max = jnp.maximum(col_max, jnp.max(chunk(c), axis=0))
        else:
            col_max = self.smax_scr[g]
        m_old = m_scr[par, g]
        m_new = jnp.maximum(m_old, _across_sublanes(col_max, jnp.maximum))
        alpha = jnp.exp2(m_old - m_new)
        m_scr[par, g] = m_new
        self.alpha_scr[slot] = alpha
        for c in range(tk // rows):
            pc = jnp.exp2(chunk(c) - m_new[None])
            self.p_scr[slot, c * rows:(c + 1) * rows, :] = (
                pc.reshape(rows, 2 * tq).astype(BF16))

    def value_update(self, g, j, slot):
        tq = SEQ_TILE
        acc_scr, par = self.acc_scr, self.par
        vt = self.vt_scr[j, g]
        alpha = self.alpha_scr[slot]
        for st in range(2):
            pv = jnp.dot(vt, self.p_scr[slot, :, st * tq:(st + 1) * tq],
                         preferred_element_type=F32)
            a = alpha[:, st * tq:(st + 1) * tq]
            acc = acc_scr[par, 2 * g + st].reshape(V_EXT // SUBLANES, SUBLANES, tq)
            acc_scr[par, 2 * g + st] = (acc * a[None]).reshape(V_EXT, tq) + pv

    def unit(self, g, masked, tile, succ_tile, prev_tile):
        ahead = (g + 2) % B_HEADS
        if g + 2 < B_HEADS:
            self.scores(ahead, tile)
        elif succ_tile is not None:
            self.scores(ahead, succ_tile)
        if prev_tile is not None:
            self.value_update((g - 1) % B_HEADS, prev_tile, (g - 1) % 2)
        self.softmax(g, g % 2, masked)

    def before_loop(self):
        tq, tk, i, par = SEQ_TILE, SEQ_TILE, self.i, self.par
        qb, kb = self.qb, self.kb
        lane = lax.broadcasted_iota(jnp.int32, (1, LANES), 1)
        cos, sin = _rope_tables(self.pos_ref[self.t].astype(F32), self.invf_ref[...], lane)
        sin_signed = jnp.where(lane < LANES // 2, -sin, sin)
        q = qb * lax.rsqrt(_head_mean_sq(qb, self.bd_ref) + NORM_EPS) * self.nw_scr[0:1, :]
        k = kb * lax.rsqrt(_head_mean_sq(kb, self.bd_ref) + NORM_EPS) * self.nw_scr[1:2, :]
        q = _rope(q, cos, sin_signed) * (B_HEAD_DIM ** -0.5 * LOG2E)
        k = _rope(k, cos, sin_signed)
        self.k_scr[i] = k.astype(BF16)

        qt = q.T
        pair_row = lax.broadcasted_iota(jnp.int32, (LANES, tq), 0)
        is_a = (pair_row // HALF) % 2 == 0
        for g in range(B_HEADS):
            qp = qt[g * LANES:(g + 1) * LANES, :]
            self.qblk_scr[g] = jnp.concatenate(
                [jnp.where(is_a, qp, 0.0), jnp.where(is_a, 0.0, qp)], axis=1).astype(BF16)

        self.m_scr[par] = jnp.full(self.m_scr.shape[1:], -jnp.inf, F32)
        self.acc_scr[par] = jnp.zeros(self.acc_scr.shape[1:], F32)

        vb = self.proj(3 * A_WIDTH + 2 * B_WIDTH, B_WIDTH)
        vbt = vb.T.astype(BF16)
        for g in range(B_HEADS):
            self.vt_scr[i, g, :B_V_DIM, :] = vbt[g * B_V_DIM:(g + 1) * B_V_DIM, :]
            self.vt_scr[i, g, B_V_DIM:, :] = jnp.ones((V_EXT - B_V_DIM, tk), BF16)
        va = self.proj(A_WIDTH, A_WIDTH)
        self.scores(0, i)
        self.scores(1, i)
        ua = self.proj(0, A_WIDTH)
        self.unit(0, True, i, 0, None)
        za = self.proj(2 * A_WIDTH, A_WIDTH)
        self.unit(1, True, i, 0, i)
        zb = self.proj(3 * A_WIDTH + 3 * B_WIDTH, B_WIDTH)
        after_diag = i - 1 if self.t > 0 else 0
        self.unit(2, True, i, after_diag, i)

        row_c = lax.broadcasted_iota(jnp.int32, (CHUNK, CHUNK), 0)
        col_c = lax.broadcasted_iota(jnp.int32, (CHUNK, CHUNK), 1)
        tril = row_c >= col_c
        for hh in range(A_HEADS):
            sl = slice(hh * A_HEAD_DIM, (hh + 1) * A_HEAD_DIM)
            v = va[:, sl]
            vn = (v * lax.rsqrt(jnp.mean(v * v, axis=-1, keepdims=True) + NORM_EPS)
                  * self.sgu_ref[hh:hh + 1, :])
            vnb = vn.astype(BF16)
            ws = jnp.where(tril, self.ws_ref[hh], 0.0).astype(BF16)
            bias = self.bst_scr[:, hh:hh + 1]
            n_chunks = tq // CHUNK
            chunks = jnp.concatenate(
                [vnb[c * CHUNK:(c + 1) * CHUNK, :] for c in range(n_chunks)], axis=1)
            mixed_chunks = jnp.dot(ws, chunks, preferred_element_type=F32)
            mix = jnp.concatenate([mixed_chunks[:, c * A_HEAD_DIM:(c + 1) * A_HEAD_DIM] + bias
                                   for c in range(n_chunks)], axis=0)
            a_out = ua[:, sl] * mix * _silu(za[:, sl])
            self.mixed_scr[par, :, sl] = a_out.astype(BF16)
        self.zb_gate = _silu(zb)
        self.unit(3, True, i, after_diag, i)
        for s in range(self.t):
            tile = i - 1 - s
            succ = tile - 1 if s + 1 < self.t else 0
            self.tile_units(tile, succ, tile + 1)

    def tile_units(self, tile, succ_tile, prev_tile):
        self.unit(0, False, tile, succ_tile, prev_tile)
        for g in range(1, B_HEADS):
            self.unit(g, False, tile, succ_tile, tile)

    def key_loop(self):
        base = self.base

        def pair_body(jj, carry):
            j = 2 * jj
            self.tile_units(j, j + 1, jnp.where(j == 0, base, j - 1))
            self.tile_units(j + 1, j + 2, j)
            return carry

        lax.fori_loop(0, base // 2, pair_body, 0)

    def last_value_product_and_out_a(self):
        last = B_HEADS - 1
        base = self.base
        self.value_update(last, jnp.where(base > 0, base - 1, base), last % 2)
        rows = self.tile_rows
        self.out_ref[0, rows, :] = self.x_ref[0, rows, :] + self.gate * jnp.dot(
            self.mixed_scr[self.par, :, :A_WIDTH], self.wout_ref[:A_WIDTH, :],
            preferred_element_type=F32)

    def finalise(self):
        tq, par, acc_scr = SEQ_TILE, self.par, self.acc_scr
        lam = (jnp.exp(jnp.sum(self.lq1_ref[...] * self.lk1_ref[...], axis=-1, keepdims=True))
               - jnp.exp(jnp.sum(self.lq2_ref[...] * self.lk2_ref[...], axis=-1, keepdims=True))
               + LAMBDA_INIT)
        for g in range(B_HEADS):
            inv_l1 = 1.0 / acc_scr[par, 2 * g, B_V_DIM:B_V_DIM + SUBLANES, :]
            inv_l2 = 1.0 / acc_scr[par, 2 * g + 1, B_V_DIM:B_V_DIM + SUBLANES, :]
            acc1 = acc_scr[par, 2 * g, :B_V_DIM, :].reshape(B_V_DIM // SUBLANES, SUBLANES, tq)
            acc2 = acc_scr[par, 2 * g + 1, :B_V_DIM, :].reshape(
                B_V_DIM // SUBLANES, SUBLANES, tq)
            ot = acc1 * inv_l1[None] - lam * (acc2 * inv_l2[None])
            o = ot.reshape(B_V_DIM, tq).T
            o = (o * lax.rsqrt(jnp.mean(o * o, axis=-1, keepdims=True) + SUBLN_EPS)
                 * self.subln_ref[...])
            o = o * (1.0 - LAMBDA_INIT)
            vsl = slice(g * B_V_DIM, (g + 1) * B_V_DIM)
            b_out = o * self.zb_gate[:, vsl]
            self.mixed_scr[par, :, A_WIDTH + g * B_V_DIM:A_WIDTH + (g + 1) * B_V_DIM] = (
                b_out.astype(BF16))

    def out_b(self):
        self.out_ref[0, self.tile_rows, :] += self.gate * jnp.dot(
            self.mixed_scr[self.par, :, A_WIDTH:], self.wout_ref[A_WIDTH:, :],
            preferred_element_type=F32)


def kernel(x, c, positions, norm_w, w_ada, b_ada, w_in, sgu_norm_w, w_s, b_s, q_norm_w, k_norm_w,
           lambda_q1, lambda_k1, lambda_q2, lambda_k2, subln_w, w_out):
    bsz, seq, d = x.shape
    assert d == D_MODEL and seq % (TILES_PER_STEP * SEQ_TILE) == 0 and norm_w.shape[0] == 1
    assert TILES_PER_STEP % 2 == 0
    nt = seq // SEQ_TILE

    mod = pl.pallas_call(
        _adaln_kernel,
        grid=(3,),
        in_specs=([pl.BlockSpec((bsz, d), lambda n: (0, 0))]
                  + [pl.BlockSpec((None, d // ADALN_ROW_PARTS, d),
                                  functools.partial(lambda part, n: (0, part, n), part))
                     for part in range(ADALN_ROW_PARTS)]
                  + [pl.BlockSpec((1, d), lambda n: (0, n))]),
        out_specs=pl.BlockSpec((None, bsz, d), lambda n: (n, 0, 0)),
        out_shape=jax.ShapeDtypeStruct((3, bsz, d), F32),
        name="adaln_mod",
    )(c, *([w_ada] * ADALN_ROW_PARTS), b_ada)

    n_grp = LANES // HALF
    pos_q = jnp.pad(positions.reshape(bsz, nt, n_grp, SEQ_TILE // n_grp),
                    ((0, 0), (0, 0), (0, SUBLANES - n_grp), (0, LANES - SEQ_TILE // n_grp)))

    inv_freq = ROPE_THETA ** (-jnp.arange(0, B_HEAD_DIM, 2, dtype=F32) / B_HEAD_DIM)
    invf = jnp.tile(inv_freq, LANES // HALF)[None, :]
    lane_head = (np.arange(MXU_DIM) // LANES) * 2 + (np.arange(MXU_DIM) // HALF) % 2
    bd = jnp.asarray(lane_head[:, None] == lane_head[None, :], dtype=BF16)


    const2 = lambda b, i: (0, 0)
    const3 = lambda b, i: (0, 0, 0)
    step_rows = TILES_PER_STEP * SEQ_TILE
    out = pl.pallas_call(
        _layer_kernel,
        grid=(bsz, nt // TILES_PER_STEP),
        in_specs=[
            pl.BlockSpec((1, step_rows, d), lambda b, i: (b, i, 0)),
            pl.BlockSpec((3, bsz, d), const3),
            pl.BlockSpec((None, TILES_PER_STEP, SUBLANES, LANES),
                         lambda b, i: (b, i, 0, 0)),
            pl.BlockSpec((1, LANES), const2),
            pl.BlockSpec((1, d), const2),
            pl.BlockSpec(memory_space=pl.ANY),
            pl.BlockSpec((A_HEADS, A_HEAD_DIM), const2),
            pl.BlockSpec((A_HEADS, CHUNK, CHUNK), const3),
            pl.BlockSpec((A_HEADS, CHUNK), const2),
            pl.BlockSpec((1, B_HEAD_DIM), const2),
            pl.BlockSpec((1, B_HEAD_DIM), const2),
            pl.BlockSpec((1, B_HEAD_DIM), const2),
            pl.BlockSpec((1, B_HEAD_DIM), const2),
            pl.BlockSpec((1, B_HEAD_DIM), const2),
            pl.BlockSpec((1, B_HEAD_DIM), const2),
            pl.BlockSpec((1, B_V_DIM), const2),
            pl.BlockSpec(memory_space=pl.ANY),
            pl.BlockSpec((MXU_DIM, MXU_DIM), const2),
        ],
        out_specs=pl.BlockSpec((1, step_rows, d), lambda b, i: (b, i, 0)),
        out_shape=jax.ShapeDtypeStruct((bsz, seq, d), F32),
        scratch_shapes=[
            pltpu.VMEM((nt, SEQ_TILE, B_WIDTH), BF16),
            pltpu.VMEM((nt, B_HEADS, V_EXT, SEQ_TILE), BF16),
            pltpu.VMEM((B_HEADS, LANES, 2 * SEQ_TILE), BF16),
            pltpu.VMEM((2, B_HEADS, SUBLANES, 2 * SEQ_TILE), F32),
            pltpu.VMEM((2, 2 * B_HEADS, V_EXT, SEQ_TILE), F32),
            pltpu.VMEM((2, SEQ_TILE, A_WIDTH + B_WIDTH), BF16),
            pltpu.VMEM((B_HEADS, SEQ_TILE, 2 * SEQ_TILE), F32),
            pltpu.VMEM((2, SEQ_TILE, 2 * SEQ_TILE), BF16),
            pltpu.VMEM((2, SUBLANES, 2 * SEQ_TILE), F32),
            pltpu.VMEM((B_HEADS, SUBLANES, 2 * SEQ_TILE), F32),
            pltpu.VMEM((d, IN_COLS), BF16),
            pltpu.VMEM((A_WIDTH + B_WIDTH, d), BF16),
            pltpu.VMEM((WEIGHT_STAGE_SLOTS, WEIGHT_CHUNK_ROWS, IN_COLS), F32),
            pltpu.SemaphoreType.DMA((WEIGHT_STAGE_SLOTS,)),
            pltpu.VMEM((SUBLANES, B_WIDTH), F32),
            pltpu.VMEM((CHUNK, LANES), F32),
        ],
        compiler_params=pltpu.CompilerParams(
            dimension_semantics=("arbitrary", "arbitrary"),
            vmem_limit_bytes=VMEM_LIMIT_BYTES),
        name="fused_layer",
    )(x, mod, pos_q, invf, norm_w, w_in,
      sgu_norm_w.reshape(A_HEADS, A_HEAD_DIM), w_s.reshape(A_HEADS, CHUNK, CHUNK),
      b_s.reshape(A_HEADS, CHUNK), q_norm_w, k_norm_w,
      lambda_q1, lambda_k1, lambda_q2, lambda_k2, subln_w, w_out, bd)
    return out
```

```python
import math

import numpy as np
import jax
import jax.numpy as jnp
from jax import lax
from jax.experimental import pallas as pl
from jax.experimental.pallas import tpu as pltpu

D_MODEL = 1024
A_WIDTH = 512
B_WIDTH = 512
A_HEADS = 4
A_HEAD_DIM = 128
CHUNK = 128
B_HEADS = 4
B_HEAD_DIM = 64
B_V_DIM = 128
ROPE_THETA = 10000.0
NORM_EPS = 1e-6
SUBLN_EPS = 1e-5
IN_COLS = 3 * A_WIDTH + 4 * B_WIDTH
LAMBDA_INIT = 0.8 - 0.6 * math.exp(-0.3 * 0)
LOG2E = math.log2(math.e)

LANES = 128
SUBLANES = 8
MXU_DIM = 256
SEQ_TILE = 256
TILES_PER_STEP = 4
VMEM_LIMIT_BYTES = 56 * 1024 * 1024
HALF = B_HEAD_DIM // 2
BF16_ROWS = 16
V_EXT = B_V_DIM + BF16_ROWS
WEIGHT_CHUNK_ROWS = 64
WEIGHT_STAGE_SLOTS = 4

F32 = jnp.float32
BF16 = jnp.bfloat16


def _load_weights(w_in_hbm, w_out_hbm, win_scr, wout_scr, stage, sem):
    ch = stage.shape[1]
    grp = lax.broadcasted_iota(jnp.int32, (1, LANES), 1) // HALF
    q0, v0 = 3 * A_WIDTH, 3 * A_WIDTH + 2 * B_WIDTH

    def stream(src_hbm, dst_scr, n_cols, reorder):
        n_chunks = src_hbm.shape[1] // ch

        def chunk_copy(c, slot):
            return pltpu.make_async_copy(src_hbm.at[0, pl.ds(c * ch, ch), :],
                                         stage.at[slot, :, pl.ds(0, n_cols)], sem.at[slot])

        depth = stage.shape[0]
        for c0 in range(depth - 1):
            chunk_copy(c0, c0).start()

        def body(c, carry):
            slot = c % depth
            ahead = c + depth - 1

            @pl.when(ahead < n_chunks)
            def _():
                chunk_copy(ahead, ahead % depth).start()

            chunk_copy(c, slot).wait()
            rows = pl.ds(pl.multiple_of(c * ch, ch), ch)
            for cb in range(n_cols // LANES):
                t = stage[slot, :, cb * LANES:(cb + 1) * LANES]
                if reorder and q0 <= cb * LANES < v0:
                    t = jnp.where(grp == 1, pltpu.roll(t, LANES - HALF, 1),
                                  jnp.where(grp == 2, pltpu.roll(t, HALF, 1), t))
                dst_scr[rows, cb * LANES:(cb + 1) * LANES] = t.astype(BF16)
            return carry

        lax.fori_loop(0, n_chunks, body, 0)

    stream(w_in_hbm, win_scr, IN_COLS, True)
    stream(w_out_hbm, wout_scr, D_MODEL, False)


def _adaln_kernel(c_ref, w_ref, b_ref, mod_ref):
    c = c_ref[...]
    c_act = c * jax.nn.sigmoid(c)
    mod_ref[...] = jnp.dot(c_act.astype(BF16), w_ref[...].astype(BF16),
                           preferred_element_type=F32) + b_ref[...]


def _silu(z):
    return z * jax.nn.sigmoid(z)


def _head_mean_sq(t, bd_ref):
    sq = (t * t).astype(BF16)
    halves = [jnp.dot(sq[:, h * MXU_DIM:(h + 1) * MXU_DIM], bd_ref[...],
                      preferred_element_type=F32) for h in range(B_WIDTH // MXU_DIM)]
    return jnp.concatenate(halves, axis=-1) * (1.0 / B_HEAD_DIM)


def _rope(t, cos, sin_signed):
    cols = []
    for cb in range(B_WIDTH // LANES):
        tc = t[:, cb * LANES:(cb + 1) * LANES]
        cols.append(tc * cos + pltpu.roll(tc, LANES // 2, 1) * sin_signed)
    return jnp.concatenate(cols, axis=-1)


def _rope_tables(pos_quarters, invf, lane):
    n_grp = LANES // HALF
    qrows = SEQ_TILE // n_grp
    grp = lane // HALF
    pos_t = pos_quarters.T
    packed = pos_t[:qrows, n_grp - 1:n_grp]
    for gi in range(n_grp - 2, -1, -1):
        packed = jnp.where(grp == gi, pos_t[:qrows, gi:gi + 1], packed)
    ang = packed * invf
    tables = []
    for packed_tab in (jnp.cos(ang), jnp.sin(ang)):
        parts = []
        for gi in range(n_grp):
            t = jnp.where(grp == gi, packed_tab, 0.0)
            t = t + pltpu.roll(t, 2 * HALF, 1)
            parts.append(t + pltpu.roll(t, HALF, 1))
        tables.append(jnp.concatenate(parts, axis=0))
    return tables


def _norm_weight_lanes(w):
    lo, hi = w[:, :HALF], w[:, HALF:]
    block = jnp.concatenate([lo, lo, hi, hi], axis=1)
    return jnp.concatenate([block] * (B_WIDTH // LANES), axis=1)


def _across_sublanes(x, op):
    for shift in (4, 2, 1):
        x = op(x, pltpu.roll(x, shift, 0))
    return x


def _layer_kernel(x_ref, mod_ref, pos_ref, invf_ref, normw_ref, w_in_hbm, sgu_ref, ws_ref,
                  bs_ref, qnw_ref, knw_ref, lq1_ref, lk1_ref, lq2_ref, lk2_ref, subln_ref,
                  w_out_hbm, bd_ref, out_ref,
                  k_scr, vt_scr, qblk_scr, m_scr, acc_scr, mixed_scr, s_scr, p_scr,
                  alpha_scr, smax_scr, win_ref, wout_ref, stage_scr, stage_sem, nw_scr,
                  bst_scr):
    @pl.when((pl.program_id(0) == 0) & (pl.program_id(1) == 0))
    def _():
        _load_weights(w_in_hbm, w_out_hbm, win_ref, wout_ref, stage_scr, stage_sem)
        nw_scr[0:1, :] = _norm_weight_lanes(qnw_ref[...])
        nw_scr[1:2, :] = _norm_weight_lanes(knw_ref[...])
        bs_pad = jnp.concatenate(
            [bs_ref[...], jnp.zeros((SUBLANES - A_HEADS, CHUNK), F32)], axis=0)
        bst_scr[:, 0:SUBLANES] = bs_pad.T

    refs = dict(locals())
    tiles = [_TilePhases(t, refs) for t in range(TILES_PER_STEP)]
    tiles[0].activations()
    tiles[0].qk_projections()
    tiles[0].before_loop()
    for t, tile in enumerate(tiles):
        nxt = tiles[t + 1] if t + 1 < TILES_PER_STEP else None
        tile.key_loop()
        tile.last_value_product_and_out_a()
        if nxt is not None:
            nxt.activations()
            nxt.qk_projections()
        tile.finalise()
        tile.out_b()
        if nxt is not None:
            nxt.before_loop()


class _TilePhases:
    def __init__(self, t, refs):
        self.__dict__.update(refs)
        self.t = t
        self.par = t % 2
        self.base = pl.program_id(1) * TILES_PER_STEP
        self.i = self.base + t
        self.tile_rows = slice(t * SEQ_TILE, (t + 1) * SEQ_TILE)

    def activations(self):
        x_ref, tile_rows = self.x_ref, self.tile_rows
        b = pl.program_id(0)
        shift = self.mod_ref[0, pl.ds(b, 1), :]
        scale = self.mod_ref[1, pl.ds(b, 1), :]
        self.gate = self.mod_ref[2, pl.ds(b, 1), :]
        sq_lanes = None
        for c0 in range(0, D_MODEL, LANES):
            x_c = x_ref[0, tile_rows, c0:c0 + LANES]
            sq_lanes = x_c * x_c if sq_lanes is None else sq_lanes + x_c * x_c
        sum_sq = jnp.sum(sq_lanes, axis=-1, keepdims=True)
        inv_rms = lax.rsqrt(sum_sq * (1.0 / D_MODEL) + NORM_EPS)
        hb_cols = []
        for c0 in range(0, D_MODEL, MXU_DIM):
            cs = slice(c0, c0 + MXU_DIM)
            h_c = ((x_ref[0, tile_rows, cs] * inv_rms * self.normw_ref[:, cs])
                   * (1.0 + scale[:, cs]) + shift[:, cs])
            hb_cols.append(h_c.astype(BF16))
        self.hb = jnp.concatenate(hb_cols, axis=1)

    def proj(self, c0, width):
        return jnp.dot(self.hb, self.win_ref[:, c0:c0 + width], preferred_element_type=F32)

    def qk_projections(self):
        self.qb = self.proj(3 * A_WIDTH, B_WIDTH)
        self.kb = self.proj(3 * A_WIDTH + B_WIDTH, B_WIDTH)

    def scores(self, g, j, masked=False):
        tk, tq = SEQ_TILE, SEQ_TILE
        kp = self.k_scr[j, :, g * LANES:(g + 1) * LANES]
        s = jnp.dot(kp, self.qblk_scr[g], preferred_element_type=F32)
        if masked:
            kv_row = lax.broadcasted_iota(jnp.int32, (tk, 2 * tq), 0)
            q_col = lax.broadcasted_iota(jnp.int32, (tk, 2 * tq), 1) % tq
            s = jnp.where(kv_row <= q_col, s, -jnp.inf)
        self.s_scr[g] = s
        self.smax_scr[g] = jnp.max(s.reshape(tk // SUBLANES, SUBLANES, 2 * tq), axis=0)

    def softmax(self, g, slot):
        tk, tq = SEQ_TILE, SEQ_TILE
        rows = 2 * SUBLANES
        s_scr, m_scr, par = self.s_scr, self.m_scr, self.par

        def chunk(c):
            return s_scr[g, c * rows:(c + 1) * rows, :].reshape(2, SUBLANES, 2 * tq)

        m_old = m_scr[par, g]
        m_new = jnp.maximum(m_old, _across_sublanes(self.smax_scr[g], jnp.maximum))
        alpha = jnp.exp2(m_old - m_new)
        m_scr[par, g] = m_new
        self.alpha_scr[slot] = alpha
        for c in range(tk // rows):
            pc = jnp.exp2(chunk(c) - m_new[None])
            self.p_scr[slot, c * rows:(c + 1) * rows, :] = (
                pc.reshape(rows, 2 * tq).astype(BF16))

    def value_update(self, g, j, slot):
        tq = SEQ_TILE
        acc_scr, par = self.acc_scr, self.par
        vt = self.vt_scr[j, g]
        alpha = self.alpha_scr[slot]
        for st in range(2):
            pv = jnp.dot(vt, self.p_scr[slot, :, st * tq:(st + 1) * tq],
                         preferred_element_type=F32)
            a = alpha[:, st * tq:(st + 1) * tq]
            acc = acc_scr[par, 2 * g + st].reshape(V_EXT // SUBLANES, SUBLANES, tq)
            acc_scr[par, 2 * g + st] = (acc * a[None]).reshape(V_EXT, tq) + pv

    def unit(self, g, masked, tile, succ_tile, prev_tile):
        ahead = (g + 2) % B_HEADS
        if g + 2 < B_HEADS:
            self.scores(ahead, tile, masked)
        elif succ_tile is not None:
            self.scores(ahead, succ_tile)
        if prev_tile is not None:
            self.value_update((g - 1) % B_HEADS, prev_tile, (g - 1) % 2)
        self.softmax(g, g % 2)

    def before_loop(self):
        tq, tk, i, par = SEQ_TILE, SEQ_TILE, self.i, self.par
        qb, kb = self.qb, self.kb
        lane = lax.broadcasted_iota(jnp.int32, (1, LANES), 1)
        cos, sin = _rope_tables(self.pos_ref[self.t].astype(F32), self.invf_ref[...], lane)
        sin_signed = jnp.where(lane < LANES // 2, -sin, sin)
        q = qb * lax.rsqrt(_head_mean_sq(qb, self.bd_ref) + NORM_EPS) * self.nw_scr[0:1, :]
        k = kb * lax.rsqrt(_head_mean_sq(kb, self.bd_ref) + NORM_EPS) * self.nw_scr[1:2, :]
        q = _rope(q, cos, sin_signed) * (B_HEAD_DIM ** -0.5 * LOG2E)
        k = _rope(k, cos, sin_signed)
        self.k_scr[i] = k.astype(BF16)

        qt = q.T
        pair_row = lax.broadcasted_iota(jnp.int32, (LANES, tq), 0)
        is_a = (pair_row // HALF) % 2 == 0
        for g in range(B_HEADS):
            qp = qt[g * LANES:(g + 1) * LANES, :]
            self.qblk_scr[g] = jnp.concatenate(
                [jnp.where(is_a, qp, 0.0), jnp.where(is_a, 0.0, qp)], axis=1).astype(BF16)

        self.m_scr[par] = jnp.full(self.m_scr.shape[1:], -jnp.inf, F32)
        self.acc_scr[par] = jnp.zeros(self.acc_scr.shape[1:], F32)

        vb = self.proj(3 * A_WIDTH + 2 * B_WIDTH, B_WIDTH)
        vbt = vb.T.astype(BF16)
        for g in range(B_HEADS):
            self.vt_scr[i, g, :B_V_DIM, :] = vbt[g * B_V_DIM:(g + 1) * B_V_DIM, :]
            self.vt_scr[i, g, B_V_DIM:, :] = jnp.ones((V_EXT - B_V_DIM, tk), BF16)
        va = self.proj(A_WIDTH, A_WIDTH)
        self.scores(0, i, masked=True)
        self.scores(1, i, masked=True)
        ua = self.proj(0, A_WIDTH)
        self.unit(0, True, i, 0, None)
        za = self.proj(2 * A_WIDTH, A_WIDTH)
        self.unit(1, True, i, 0, i)
        zb = self.proj(3 * A_WIDTH + 3 * B_WIDTH, B_WIDTH)
        after_diag = i - 1 if self.t > 0 else 0
        self.unit(2, True, i, after_diag, i)

        row_c = lax.broadcasted_iota(jnp.int32, (CHUNK, CHUNK), 0)
        col_c = lax.broadcasted_iota(jnp.int32, (CHUNK, CHUNK), 1)
        tril = row_c >= col_c
        for hh in range(A_HEADS):
            sl = slice(hh * A_HEAD_DIM, (hh + 1) * A_HEAD_DIM)
            v = va[:, sl]
            vn = (v * lax.rsqrt(jnp.mean(v * v, axis=-1, keepdims=True) + NORM_EPS)
                  * self.sgu_ref[hh:hh + 1, :])
            vnb = vn.astype(BF16)
            ws = jnp.where(tril, self.ws_ref[hh], 0.0).astype(BF16)
            bias = self.bst_scr[:, hh:hh + 1]
            n_chunks = tq // CHUNK
            chunks = jnp.concatenate(
                [vnb[c * CHUNK:(c + 1) * CHUNK, :] for c in range(n_chunks)], axis=1)
            mixed_chunks = jnp.dot(ws, chunks, preferred_element_type=F32)
            mix = jnp.concatenate([mixed_chunks[:, c * A_HEAD_DIM:(c + 1) * A_HEAD_DIM] + bias
                                   for c in range(n_chunks)], axis=0)
            a_out = ua[:, sl] * mix * _silu(za[:, sl])
            self.mixed_scr[par, :, sl] = a_out.astype(BF16)
        self.zb_gate = _silu(zb)
        self.unit(3, True, i, after_diag, i)
        for s in range(self.t):
            tile = i - 1 - s
            succ = tile - 1 if s + 1 < self.t else 0
            self.tile_units(tile, succ, tile + 1)

    def tile_units(self, tile, succ_tile, prev_tile):
        self.unit(0, False, tile, succ_tile, prev_tile)
        for g in range(1, B_HEADS):
            self.unit(g, False, tile, succ_tile, tile)

    def key_loop(self):
        base = self.base

        def pair_body(jj, carry):
            j = 2 * jj
            self.tile_units(j, j + 1, jnp.where(j == 0, base, j - 1))
            self.tile_units(j + 1, j + 2, j)
            return carry

        lax.fori_loop(0, base // 2, pair_body, 0)

    def last_value_product_and_out_a(self):
        last = B_HEADS - 1
        base = self.base
        self.value_update(last, jnp.where(base > 0, base - 1, base), last % 2)
        rows = self.tile_rows
        self.out_ref[0, rows, :] = self.x_ref[0, rows, :] + self.gate * jnp.dot(
            self.mixed_scr[self.par, :, :A_WIDTH], self.wout_ref[:A_WIDTH, :],
            preferred_element_type=F32)

    def finalise(self):
        tq, par, acc_scr = SEQ_TILE, self.par, self.acc_scr
        lam = (jnp.exp(jnp.sum(self.lq1_ref[...] * self.lk1_ref[...], axis=-1, keepdims=True))
               - jnp.exp(jnp.sum(self.lq2_ref[...] * self.lk2_ref[...], axis=-1, keepdims=True))
               + LAMBDA_INIT)
        for g in range(B_HEADS):
            inv_l1 = 1.0 / acc_scr[par, 2 * g, B_V_DIM:B_V_DIM + SUBLANES, :]
            inv_l2 = 1.0 / acc_scr[par, 2 * g + 1, B_V_DIM:B_V_DIM + SUBLANES, :]
            acc1 = acc_scr[par, 2 * g, :B_V_DIM, :].reshape(B_V_DIM // SUBLANES, SUBLANES, tq)
            acc2 = acc_scr[par, 2 * g + 1, :B_V_DIM, :].reshape(
                B_V_DIM // SUBLANES, SUBLANES, tq)
            ot = acc1 * inv_l1[None] - lam * (acc2 * inv_l2[None])
            o = ot.reshape(B_V_DIM, tq).T
            o = (o * lax.rsqrt(jnp.mean(o * o, axis=-1, keepdims=True) + SUBLN_EPS)
                 * self.subln_ref[...])
            o = o * (1.0 - LAMBDA_INIT)
            vsl = slice(g * B_V_DIM, (g + 1) * B_V_DIM)
            b_out = o * self.zb_gate[:, vsl]
            self.mixed_scr[par, :, A_WIDTH + g * B_V_DIM:A_WIDTH + (g + 1) * B_V_DIM] = (
                b_out.astype(BF16))

    def out_b(self):
        self.out_ref[0, self.tile_rows, :] += self.gate * jnp.dot(
            self.mixed_scr[self.par, :, A_WIDTH:], self.wout_ref[A_WIDTH:, :],
            preferred_element_type=F32)


def kernel(x, c, positions, norm_w, w_ada, b_ada, w_in, sgu_norm_w, w_s, b_s, q_norm_w, k_norm_w,
           lambda_q1, lambda_k1, lambda_q2, lambda_k2, subln_w, w_out):
    bsz, seq, d = x.shape
    assert d == D_MODEL and seq % (TILES_PER_STEP * SEQ_TILE) == 0 and norm_w.shape[0] == 1
    assert TILES_PER_STEP % 2 == 0
    nt = seq // SEQ_TILE

    mod = pl.pallas_call(
        _adaln_kernel,
        grid=(3,),
        in_specs=[pl.BlockSpec((bsz, d), lambda n: (0, 0)),
                  pl.BlockSpec((None, d, d), lambda n: (0, 0, n)),
                  pl.BlockSpec((1, d), lambda n: (0, n))],
        out_specs=pl.BlockSpec((None, bsz, d), lambda n: (n, 0, 0)),
        out_shape=jax.ShapeDtypeStruct((3, bsz, d), F32),
        name="adaln_mod",
    )(c, w_ada, b_ada)

    n_grp = LANES // HALF
    pos_q = jnp.pad(positions.reshape(bsz, nt, n_grp, SEQ_TILE // n_grp),
                    ((0, 0), (0, 0), (0, SUBLANES - n_grp), (0, LANES - SEQ_TILE // n_grp)))

    inv_freq = ROPE_THETA ** (-jnp.arange(0, B_HEAD_DIM, 2, dtype=F32) / B_HEAD_DIM)
    invf = jnp.tile(inv_freq, LANES // HALF)[None, :]
    lane_head = (np.arange(MXU_DIM) // LANES) * 2 + (np.arange(MXU_DIM) // HALF) % 2
    bd = jnp.asarray(lane_head[:, None] == lane_head[None, :], dtype=BF16)


    const2 = lambda b, i: (0, 0)
    const3 = lambda b, i: (0, 0, 0)
    step_rows = TILES_PER_STEP * SEQ_TILE
    out = pl.pallas_call(
        _layer_kernel,
        grid=(bsz, nt // TILES_PER_STEP),
        in_specs=[
            pl.BlockSpec((1, step_rows, d), lambda b, i: (b, i, 0)),
            pl.BlockSpec((3, bsz, d), const3),
            pl.BlockSpec((None, TILES_PER_STEP, SUBLANES, LANES),
                         lambda b, i: (b, i, 0, 0)),
            pl.BlockSpec((1, LANES), const2),
            pl.BlockSpec((1, d), const2),
            pl.BlockSpec(memory_space=pl.ANY),
            pl.BlockSpec((A_HEADS, A_HEAD_DIM), const2),
            pl.BlockSpec((A_HEADS, CHUNK, CHUNK), const3),
            pl.BlockSpec((A_HEADS, CHUNK), const2),
            pl.BlockSpec((1, B_HEAD_DIM), const2),
            pl.BlockSpec((1, B_HEAD_DIM), const2),
            pl.BlockSpec((1, B_HEAD_DIM), const2),
            pl.BlockSpec((1, B_HEAD_DIM), const2),
            pl.BlockSpec((1, B_HEAD_DIM), const2),
            pl.BlockSpec((1, B_HEAD_DIM), const2),
            pl.BlockSpec((1, B_V_DIM), const2),
            pl.BlockSpec(memory_space=pl.ANY),
            pl.BlockSpec((MXU_DIM, MXU_DIM), const2),
        ],
        out_specs=pl.BlockSpec((1, step_rows, d), lambda b, i: (b, i, 0)),
        out_shape=jax.ShapeDtypeStruct((bsz, seq, d), F32),
        scratch_shapes=[
            pltpu.VMEM((nt, SEQ_TILE, B_WIDTH), BF16),
            pltpu.VMEM((nt, B_HEADS, V_EXT, SEQ_TILE), BF16),
            pltpu.VMEM((B_HEADS, LANES, 2 * SEQ_TILE), BF16),
            pltpu.VMEM((2, B_HEADS, SUBLANES, 2 * SEQ_TILE), F32),
            pltpu.VMEM((2, 2 * B_HEADS, V_EXT, SEQ_TILE), F32),
            pltpu.VMEM((2, SEQ_TILE, A_WIDTH + B_WIDTH), BF16),
            pltpu.VMEM((B_HEADS, SEQ_TILE, 2 * SEQ_TILE), F32),
            pltpu.VMEM((2, SEQ_TILE, 2 * SEQ_TILE), BF16),
            pltpu.VMEM((2, SUBLANES, 2 * SEQ_TILE), F32),
            pltpu.VMEM((B_HEADS, SUBLANES, 2 * SEQ_TILE), F32),
            pltpu.VMEM((d, IN_COLS), BF16),
            pltpu.VMEM((A_WIDTH + B_WIDTH, d), BF16),
            pltpu.VMEM((WEIGHT_STAGE_SLOTS, WEIGHT_CHUNK_ROWS, IN_COLS), F32),
            pltpu.SemaphoreType.DMA((WEIGHT_STAGE_SLOTS,)),
            pltpu.VMEM((SUBLANES, B_WIDTH), F32),
            pltpu.VMEM((CHUNK, LANES), F32),
        ],
        compiler_params=pltpu.CompilerParams(
            dimension_semantics=("arbitrary", "arbitrary"),
            vmem_limit_bytes=VMEM_LIMIT_BYTES),
        name="fused_layer",
    )(x, mod, pos_q, invf, norm_w, w_in,
      sgu_norm_w.reshape(A_HEADS, A_HEAD_DIM), w_s.reshape(A_HEADS, CHUNK, CHUNK),
      b_s.reshape(A_HEADS, CHUNK), q_norm_w, k_norm_w,
      lambda_q1, lambda_k1, lambda_q2, lambda_k2, subln_w, w_out, bd)
    return out
```

```python
import math

import numpy as np
import jax
import jax.numpy as jnp
from jax import lax
from jax.experimental import pallas as pl
from jax.experimental.pallas import tpu as pltpu

D_MODEL = 1024
A_WIDTH = 512
B_WIDTH = 512
A_HEADS = 4
A_HEAD_DIM = 128
CHUNK = 128
B_HEADS = 4
B_HEAD_DIM = 64
B_V_DIM = 128
ROPE_THETA = 10000.0
NORM_EPS = 1e-6
SUBLN_EPS = 1e-5
IN_COLS = 3 * A_WIDTH + 4 * B_WIDTH
LAMBDA_INIT = 0.8 - 0.6 * math.exp(-0.3 * 0)
LOG2E = math.log2(math.e)

LANES = 128
SUBLANES = 8
MXU_DIM = 256
SEQ_TILE = 256
TILES_PER_STEP = 4
VMEM_LIMIT_BYTES = 56 * 1024 * 1024
HALF = B_HEAD_DIM // 2
BF16_ROWS = 16
V_EXT = B_V_DIM + BF16_ROWS
WEIGHT_CHUNK_ROWS = 64
WEIGHT_STAGE_SLOTS = 4

F32 = jnp.float32
BF16 = jnp.bfloat16


def _load_weights(w_in_hbm, w_out_hbm, win_scr, wout_scr, stage, sem):
    ch = stage.shape[1]
    grp = lax.broadcasted_iota(jnp.int32, (1, LANES), 1) // HALF
    q0, v0 = 3 * A_WIDTH, 3 * A_WIDTH + 2 * B_WIDTH

    def stream(src_hbm, dst_scr, n_cols, reorder):
        n_chunks = src_hbm.shape[1] // ch

        def chunk_copy(c, slot):
            return pltpu.make_async_copy(src_hbm.at[0, pl.ds(c * ch, ch), :],
                                         stage.at[slot, :, pl.ds(0, n_cols)], sem.at[slot])

        depth = stage.shape[0]
        for c0 in range(depth - 1):
            chunk_copy(c0, c0).start()

        def body(c, carry):
            slot = c % depth
            ahead = c + depth - 1

            @pl.when(ahead < n_chunks)
            def _():
                chunk_copy(ahead, ahead % depth).start()

            chunk_copy(c, slot).wait()
            rows = pl.ds(pl.multiple_of(c * ch, ch), ch)
            for cb in range(n_cols // LANES):
                t = stage[slot, :, cb * LANES:(cb + 1) * LANES]
                if reorder and q0 <= cb * LANES < v0:
                    t = jnp.where(grp == 1, pltpu.roll(t, LANES - HALF, 1),
                                  jnp.where(grp == 2, pltpu.roll(t, HALF, 1), t))
                dst_scr[rows, cb * LANES:(cb + 1) * LANES] = t.astype(BF16)
            return carry

        lax.fori_loop(0, n_chunks, body, 0)

    stream(w_in_hbm, win_scr, IN_COLS, True)
    stream(w_out_hbm, wout_scr, D_MODEL, False)


def _adaln_kernel(c_ref, w_ref, b_ref, mod_ref):
    c = c_ref[...]
    c_act = c * jax.nn.sigmoid(c)
    mod_ref[...] = jnp.dot(c_act.astype(BF16), w_ref[...].astype(BF16),
                           preferred_element_type=F32) + b_ref[...]


def _silu(z):
    return z * jax.nn.sigmoid(z)


def _head_mean_sq(t, bd_ref):
    sq = (t * t).astype(BF16)
    halves = [jnp.dot(sq[:, h * MXU_DIM:(h + 1) * MXU_DIM], bd_ref[...],
                      preferred_element_type=F32) for h in range(B_WIDTH // MXU_DIM)]
    return jnp.concatenate(halves, axis=-1) * (1.0 / B_HEAD_DIM)


def _rope(t, cos, sin_signed):
    cols = []
    for cb in range(B_WIDTH // LANES):
        tc = t[:, cb * LANES:(cb + 1) * LANES]
        cols.append(tc * cos + pltpu.roll(tc, LANES // 2, 1) * sin_signed)
    return jnp.concatenate(cols, axis=-1)


def _rope_tables(pos_quarters, invf, lane):
    n_grp = LANES // HALF
    qrows = SEQ_TILE // n_grp
    grp = lane // HALF
    pos_t = pos_quarters.T
    packed = pos_t[:qrows, n_grp - 1:n_grp]
    for gi in range(n_grp - 2, -1, -1):
        packed = jnp.where(grp == gi, pos_t[:qrows, gi:gi + 1], packed)
    ang = packed * invf
    tables = []
    for packed_tab in (jnp.cos(ang), jnp.sin(ang)):
        parts = []
        for gi in range(n_grp):
            t = jnp.where(grp == gi, packed_tab, 0.0)
            t = t + pltpu.roll(t, 2 * HALF, 1)
            parts.append(t + pltpu.roll(t, HALF, 1))
        tables.append(jnp.concatenate(parts, axis=0))
    return tables


def _norm_weight_lanes(w):
    lo, hi = w[:, :HALF], w[:, HALF:]
    block = jnp.concatenate([lo, lo, hi, hi], axis=1)
    return jnp.concatenate([block] * (B_WIDTH // LANES), axis=1)


def _across_sublanes(x, op):
    for shift in (4, 2, 1):
        x = op(x, pltpu.roll(x, shift, 0))
    return x


def _layer_kernel(x_ref, mod_ref, pos_ref, invf_ref, normw_ref, w_in_hbm, sgu_ref, ws_ref,
                  bs_ref, qnw_ref, knw_ref, lq1_ref, lk1_ref, lq2_ref, lk2_ref, subln_ref,
                  w_out_hbm, bd_ref, out_ref,
                  k_scr, vt_scr, qblk_scr, m_scr, acc_scr, mixed_scr, s_scr, p_scr,
                  alpha_scr, smax_scr, win_ref, wout_ref, stage_scr, stage_sem, nw_scr,
                  bst_scr):
    @pl.when((pl.program_id(0) == 0) & (pl.program_id(1) == 0))
    def _():
        _load_weights(w_in_hbm, w_out_hbm, win_ref, wout_ref, stage_scr, stage_sem)
        nw_scr[0:1, :] = _norm_weight_lanes(qnw_ref[...])
        nw_scr[1:2, :] = _norm_weight_lanes(knw_ref[...])
        bs_pad = jnp.concatenate(
            [bs_ref[...], jnp.zeros((SUBLANES - A_HEADS, CHUNK), F32)], axis=0)
        bst_scr[:, 0:SUBLANES] = bs_pad.T

    refs = dict(locals())
    tiles = [_TilePhases(t, refs) for t in range(TILES_PER_STEP)]
    tiles[0].activations()
    tiles[0].qk_projections()
    tiles[0].before_loop()
    for t, tile in enumerate(tiles):
        nxt = tiles[t + 1] if t + 1 < TILES_PER_STEP else None
        tile.key_loop()
        tile.last_value_product_and_out_a()
        if nxt is not None:
            nxt.activations()
            nxt.qk_projections()
        tile.finalise()
        tile.out_b()
        if nxt is not None:
            nxt.before_loop()


class _TilePhases:
    def __init__(self, t, refs):
        self.__dict__.update(refs)
        self.t = t
        self.par = t % 2
        self.base = pl.program_id(1) * TILES_PER_STEP
        self.i = self.base + t
        self.tile_rows = slice(t * SEQ_TILE, (t + 1) * SEQ_TILE)

    def activations(self):
        x_ref, tile_rows = self.x_ref, self.tile_rows
        b = pl.program_id(0)
        shift = self.mod_ref[0, pl.ds(b, 1), :]
        scale = self.mod_ref[1, pl.ds(b, 1), :]
        self.gate = self.mod_ref[2, pl.ds(b, 1), :]
        sq_lanes = None
        for c0 in range(0, D_MODEL, LANES):
            x_c = x_ref[0, tile_rows, c0:c0 + LANES]
            sq_lanes = x_c * x_c if sq_lanes is None else sq_lanes + x_c * x_c
        sum_sq = jnp.sum(sq_lanes, axis=-1, keepdims=True)
        inv_rms = lax.rsqrt(sum_sq * (1.0 / D_MODEL) + NORM_EPS)
        hb_cols = []
        for c0 in range(0, D_MODEL, MXU_DIM):
            cs = slice(c0, c0 + MXU_DIM)
            h_c = ((x_ref[0, tile_rows, cs] * inv_rms * self.normw_ref[:, cs])
                   * (1.0 + scale[:, cs]) + shift[:, cs])
            hb_cols.append(h_c.astype(BF16))
        self.hb = jnp.concatenate(hb_cols, axis=1)

    def proj(self, c0, width):
        return jnp.dot(self.hb, self.win_ref[:, c0:c0 + width], preferred_element_type=F32)

    def qk_projections(self):
        self.qb = self.proj(3 * A_WIDTH, B_WIDTH)
        self.kb = self.proj(3 * A_WIDTH + B_WIDTH, B_WIDTH)

    def scores(self, g, j):
        tk, tq = SEQ_TILE, SEQ_TILE
        kp = self.k_scr[j, :, g * LANES:(g + 1) * LANES]
        s = jnp.dot(kp, self.qblk_scr[g], preferred_element_type=F32)
        self.s_scr[g] = s
        self.smax_scr[g] = jnp.max(s.reshape(tk // SUBLANES, SUBLANES, 2 * tq), axis=0)

    def softmax(self, g, slot, masked):
        tk, tq = SEQ_TILE, SEQ_TILE
        rows = 2 * SUBLANES
        s_scr, m_scr, par = self.s_scr, self.m_scr, self.par

        def chunk(c):
            sc = s_scr[g, c * rows:(c + 1) * rows, :]
            if masked:
                kv_row = lax.broadcasted_iota(jnp.int32, (rows, 2 * tq), 0) + c * rows
                q_col = lax.broadcasted_iota(jnp.int32, (rows, 2 * tq), 1) % tq
                sc = jnp.where(kv_row <= q_col, sc, -jnp.inf)
            return sc.reshape(2, SUBLANES, 2 * tq)

        if masked:
            col_max = jnp.max(chunk(0), axis=0)
            for c in range(1, tk // rows):
                col_max = jnp.maximum(col_max, jnp.max(chunk(c), axis=0))
        else:
            col_max = self.smax_scr[g]
        m_old = m_scr[par, g]
        m_new = jnp.maximum(m_old, _across_sublanes(col_max, jnp.maximum))
        alpha = jnp.exp2(m_old - m_new)
        m_scr[par, g] = m_new
        self.alpha_scr[slot] = alpha
        for c in range(tk // rows):
            pc = jnp.exp2(chunk(c) - m_new[None])
            self.p_scr[slot, c * rows:(c + 1) * rows, :] = (
                pc.reshape(rows, 2 * tq).astype(BF16))

    def value_update(self, g, j, slot):
        tq = SEQ_TILE
        acc_scr, par = self.acc_scr, self.par
        vt = self.vt_scr[j, g]
        alpha = self.alpha_scr[slot]
        for st in range(2):
            pv = jnp.dot(vt, self.p_scr[slot, :, st * tq:(st + 1) * tq],
                         preferred_element_type=F32)
            a = alpha[:, st * tq:(st + 1) * tq]
            acc = acc_scr[par, 2 * g + st].reshape(V_EXT // SUBLANES, SUBLANES, tq)
            acc_scr[par, 2 * g + st] = (acc * a[None]).reshape(V_EXT, tq) + pv

    def unit(self, g, masked, tile, succ_tile, prev_tile):
        ahead = (g + 2) % B_HEADS
        if prev_tile is not None:
            self.value_update((g - 1) % B_HEADS, prev_tile, (g - 1) % 2)
        if g + 2 < B_HEADS:
            self.scores(ahead, tile)
        elif succ_tile is not None:
            self.scores(ahead, succ_tile)
        self.softmax(g, g % 2, masked)

    def before_loop(self):
        tq, tk, i, par = SEQ_TILE, SEQ_TILE, self.i, self.par
        qb, kb = self.qb, self.kb
        lane = lax.broadcasted_iota(jnp.int32, (1, LANES), 1)
        cos, sin = _rope_tables(self.pos_ref[self.t].astype(F32), self.invf_ref[...], lane)
        sin_signed = jnp.where(lane < LANES // 2, -sin, sin)
        q = qb * lax.rsqrt(_head_mean_sq(qb, self.bd_ref) + NORM_EPS) * self.nw_scr[0:1, :]
        k = kb * lax.rsqrt(_head_mean_sq(kb, self.bd_ref) + NORM_EPS) * self.nw_scr[1:2, :]
        q = _rope(q, cos, sin_signed) * (B_HEAD_DIM ** -0.5 * LOG2E)
        k = _rope(k, cos, sin_signed)
        self.k_scr[i] = k.astype(BF16)

        qt = q.T
        pair_row = lax.broadcasted_iota(jnp.int32, (LANES, tq), 0)
        is_a = (pair_row // HALF) % 2 == 0
        for g in range(B_HEADS):
            qp = qt[g * LANES:(g + 1) * LANES, :]
            self.qblk_scr[g] = jnp.concatenate(
                [jnp.where(is_a, qp, 0.0), jnp.where(is_a, 0.0, qp)], axis=1).astype(BF16)

        self.m_scr[par] = jnp.full(self.m_scr.shape[1:], -jnp.inf, F32)
        self.acc_scr[par] = jnp.zeros(self.acc_scr.shape[1:], F32)

        vb = self.proj(3 * A_WIDTH + 2 * B_WIDTH, B_WIDTH)
        vbt = vb.T.astype(BF16)
        for g in range(B_HEADS):
            self.vt_scr[i, g, :B_V_DIM, :] = vbt[g * B_V_DIM:(g + 1) * B_V_DIM, :]
            self.vt_scr[i, g, B_V_DIM:, :] = jnp.ones((V_EXT - B_V_DIM, tk), BF16)
        va = self.proj(A_WIDTH, A_WIDTH)
        self.scores(0, i)
        self.scores(1, i)
        ua = self.proj(0, A_WIDTH)
        self.unit(0, True, i, 0, None)
        za = self.proj(2 * A_WIDTH, A_WIDTH)
        self.unit(1, True, i, 0, i)
        zb = self.proj(3 * A_WIDTH + 3 * B_WIDTH, B_WIDTH)
        after_diag = i - 1 if self.t > 0 else 0
        self.unit(2, True, i, after_diag, i)

        row_c = lax.broadcasted_iota(jnp.int32, (CHUNK, CHUNK), 0)
        col_c = lax.broadcasted_iota(jnp.int32, (CHUNK, CHUNK), 1)
        tril = row_c >= col_c
        for hh in range(A_HEADS):
            sl = slice(hh * A_HEAD_DIM, (hh + 1) * A_HEAD_DIM)
            v = va[:, sl]
            vn = (v * lax.rsqrt(jnp.mean(v * v, axis=-1, keepdims=True) + NORM_EPS)
                  * self.sgu_ref[hh:hh + 1, :])
            vnb = vn.astype(BF16)
            ws = jnp.where(tril, self.ws_ref[hh], 0.0).astype(BF16)
            bias = self.bst_scr[:, hh:hh + 1]
            n_chunks = tq // CHUNK
            chunks = jnp.concatenate(
                [vnb[c * CHUNK:(c + 1) * CHUNK, :] for c in range(n_chunks)], axis=1)
            mixed_chunks = jnp.dot(ws, chunks, preferred_element_type=F32)
            mix = jnp.concatenate([mixed_chunks[:, c * A_HEAD_DIM:(c + 1) * A_HEAD_DIM] + bias
                                   for c in range(n_chunks)], axis=0)
            a_out = ua[:, sl] * mix * _silu(za[:, sl])
            self.mixed_scr[par, :, sl] = a_out.astype(BF16)
        self.zb_gate = _silu(zb)
        self.unit(3, True, i, after_diag, i)
        for s in range(self.t):
            tile = i - 1 - s
            succ = tile - 1 if s + 1 < self.t else 0
            self.tile_units(tile, succ, tile + 1)

    def tile_units(self, tile, succ_tile, prev_tile):
        self.unit(0, False, tile, succ_tile, prev_tile)
        for g in range(1, B_HEADS):
            self.unit(g, False, tile, succ_tile, tile)

    def key_loop(self):
        base = self.base

        def pair_body(jj, carry):
            j = 2 * jj
            self.tile_units(j, j + 1, jnp.where(j == 0, base, j - 1))
            self.tile_units(j + 1, j + 2, j)
            return carry

        lax.fori_loop(0, base // 2, pair_body, 0)

    def last_value_product_and_out_a(self):
        last = B_HEADS - 1
        base = self.base
        self.value_update(last, jnp.where(base > 0, base - 1, base), last % 2)
        rows = self.tile_rows
        self.out_ref[0, rows, :] = self.x_ref[0, rows, :] + self.gate * jnp.dot(
            self.mixed_scr[self.par, :, :A_WIDTH], self.wout_ref[:A_WIDTH, :],
            preferred_element_type=F32)

    def finalise(self):
        tq, par, acc_scr = SEQ_TILE, self.par, self.acc_scr
        lam = (jnp.exp(jnp.sum(self.lq1_ref[...] * self.lk1_ref[...], axis=-1, keepdims=True))
               - jnp.exp(jnp.sum(self.lq2_ref[...] * self.lk2_ref[...], axis=-1, keepdims=True))
               + LAMBDA_INIT)
        for g in range(B_HEADS):
            inv_l1 = 1.0 / acc_scr[par, 2 * g, B_V_DIM:B_V_DIM + SUBLANES, :]
            inv_l2 = 1.0 / acc_scr[par, 2 * g + 1, B_V_DIM:B_V_DIM + SUBLANES, :]
            acc1 = acc_scr[par, 2 * g, :B_V_DIM, :].reshape(B_V_DIM // SUBLANES, SUBLANES, tq)
            acc2 = acc_scr[par, 2 * g + 1, :B_V_DIM, :].reshape(
                B_V_DIM // SUBLANES, SUBLANES, tq)
            ot = acc1 * inv_l1[None] - lam * (acc2 * inv_l2[None])
            o = ot.reshape(B_V_DIM, tq).T
            o = (o * lax.rsqrt(jnp.mean(o * o, axis=-1, keepdims=True) + SUBLN_EPS)
                 * self.subln_ref[...])
            o = o * (1.0 - LAMBDA_INIT)
            vsl = slice(g * B_V_DIM, (g + 1) * B_V_DIM)
            b_out = o * self.zb_gate[:, vsl]
            self.mixed_scr[par, :, A_WIDTH + g * B_V_DIM:A_WIDTH + (g + 1) * B_V_DIM] = (
                b_out.astype(BF16))

    def out_b(self):
        self.out_ref[0, self.tile_rows, :] += self.gate * jnp.dot(
            self.mixed_scr[self.par, :, A_WIDTH:], self.wout_ref[A_WIDTH:, :],
            preferred_element_type=F32)


def kernel(x, c, positions, norm_w, w_ada, b_ada, w_in, sgu_norm_w, w_s, b_s, q_norm_w, k_norm_w,
           lambda_q1, lambda_k1, lambda_q2, lambda_k2, subln_w, w_out):
    bsz, seq, d = x.shape
    assert d == D_MODEL and seq % (TILES_PER_STEP * SEQ_TILE) == 0 and norm_w.shape[0] == 1
    assert TILES_PER_STEP % 2 == 0
    nt = seq // SEQ_TILE

    mod = pl.pallas_call(
        _adaln_kernel,
        grid=(3,),
        in_specs=[pl.BlockSpec((bsz, d), lambda n: (0, 0)),
                  pl.BlockSpec((None, d, d), lambda n: (0, 0, n)),
                  pl.BlockSpec((1, d), lambda n: (0, n))],
        out_specs=pl.BlockSpec((None, bsz, d), lambda n: (n, 0, 0)),
        out_shape=jax.ShapeDtypeStruct((3, bsz, d), F32),
        name="adaln_mod",
    )(c, w_ada, b_ada)

    n_grp = LANES // HALF
    pos_q = jnp.pad(positions.reshape(bsz, nt, n_grp, SEQ_TILE // n_grp),
                    ((0, 0), (0, 0), (0, SUBLANES - n_grp), (0, LANES - SEQ_TILE // n_grp)))

    inv_freq = ROPE_THETA ** (-jnp.arange(0, B_HEAD_DIM, 2, dtype=F32) / B_HEAD_DIM)
    invf = jnp.tile(inv_freq, LANES // HALF)[None, :]
    lane_head = (np.arange(MXU_DIM) // LANES) * 2 + (np.arange(MXU_DIM) // HALF) % 2
    bd = jnp.asarray(lane_head[:, None] == lane_head[None, :], dtype=BF16)


    const2 = lambda b, i: (0, 0)
    const3 = lambda b, i: (0, 0, 0)
    step_rows = TILES_PER_STEP * SEQ_TILE
    out = pl.pallas_call(
        _layer_kernel,
        grid=(bsz, nt // TILES_PER_STEP),
        in_specs=[
            pl.BlockSpec((1, step_rows, d), lambda b, i: (b, i, 0)),
            pl.BlockSpec((3, bsz, d), const3),
            pl.BlockSpec((None, TILES_PER_STEP, SUBLANES, LANES),
                         lambda b, i: (b, i, 0, 0)),
            pl.BlockSpec((1, LANES), const2),
            pl.BlockSpec((1, d), const2),
            pl.BlockSpec(memory_space=pl.ANY),
            pl.BlockSpec((A_HEADS, A_HEAD_DIM), const2),
            pl.BlockSpec((A_HEADS, CHUNK, CHUNK), const3),
            pl.BlockSpec((A_HEADS, CHUNK), const2),
            pl.BlockSpec((1, B_HEAD_DIM), const2),
            pl.BlockSpec((1, B_HEAD_DIM), const2),
            pl.BlockSpec((1, B_HEAD_DIM), const2),
            pl.BlockSpec((1, B_HEAD_DIM), const2),
            pl.BlockSpec((1, B_HEAD_DIM), const2),
            pl.BlockSpec((1, B_HEAD_DIM), const2),
            pl.BlockSpec((1, B_V_DIM), const2),
            pl.BlockSpec(memory_space=pl.ANY),
            pl.BlockSpec((MXU_DIM, MXU_DIM), const2),
        ],
        out_specs=pl.BlockSpec((1, step_rows, d), lambda b, i: (b, i, 0)),
        out_shape=jax.ShapeDtypeStruct((bsz, seq, d), F32),
        scratch_shapes=[
            pltpu.VMEM((nt, SEQ_TILE, B_WIDTH), BF16),
            pltpu.VMEM((nt, B_HEADS, V_EXT, SEQ_TILE), BF16),
            pltpu.VMEM((B_HEADS, LANES, 2 * SEQ_TILE), BF16),
            pltpu.VMEM((2, B_HEADS, SUBLANES, 2 * SEQ_TILE), F32),
            pltpu.VMEM((2, 2 * B_HEADS, V_EXT, SEQ_TILE), F32),
            pltpu.VMEM((2, SEQ_TILE, A_WIDTH + B_WIDTH), BF16),
            pltpu.VMEM((B_HEADS, SEQ_TILE, 2 * SEQ_TILE), F32),
            pltpu.VMEM((2, SEQ_TILE, 2 * SEQ_TILE), BF16),
            pltpu.VMEM((2, SUBLANES, 2 * SEQ_TILE), F32),
            pltpu.VMEM((B_HEADS, SUBLANES, 2 * SEQ_TILE), F32),
            pltpu.VMEM((d, IN_COLS), BF16),
            pltpu.VMEM((A_WIDTH + B_WIDTH, d), BF16),
            pltpu.VMEM((WEIGHT_STAGE_SLOTS, WEIGHT_CHUNK_ROWS, IN_COLS), F32),
            pltpu.SemaphoreType.DMA((WEIGHT_STAGE_SLOTS,)),
            pltpu.VMEM((SUBLANES, B_WIDTH), F32),
            pltpu.VMEM((CHUNK, LANES), F32),
        ],
        compiler_params=pltpu.CompilerParams(
            dimension_semantics=("arbitrary", "arbitrary"),
            vmem_limit_bytes=VMEM_LIMIT_BYTES),
        name="fused_layer",
    )(x, mod, pos_q, invf, norm_w, w_in,
      sgu_norm_w.reshape(A_HEADS, A_HEAD_DIM), w_s.reshape(A_HEADS, CHUNK, CHUNK),
      b_s.reshape(A_HEADS, CHUNK), q_norm_w, k_norm_w,
      lambda_q1, lambda_k1, lambda_q2, lambda_k2, subln_w, w_out, bd)
    return out
```

```python
import math

import numpy as np
import jax
import jax.numpy as jnp
from jax import lax
from jax.experimental import pallas as pl
from jax.experimental.pallas import tpu as pltpu

D_MODEL = 1024
A_WIDTH = 512
B_WIDTH = 512
A_HEADS = 4
A_HEAD_DIM = 128
CHUNK = 128
B_HEADS = 4
B_HEAD_DIM = 64
B_V_DIM = 128
ROPE_THETA = 10000.0
NORM_EPS = 1e-6
SUBLN_EPS = 1e-5
IN_COLS = 3 * A_WIDTH + 4 * B_WIDTH
LAMBDA_INIT = 0.8 - 0.6 * math.exp(-0.3 * 0)
LOG2E = math.log2(math.e)

LANES = 128
SUBLANES = 8
MXU_DIM = 256
SEQ_TILE = 256
TILES_PER_STEP = 4
VMEM_LIMIT_BYTES = 56 * 1024 * 1024
HALF = B_HEAD_DIM // 2
BF16_ROWS = 16
V_EXT = B_V_DIM + BF16_ROWS
WEIGHT_CHUNK_ROWS = 64
WEIGHT_STAGE_SLOTS = 4

F32 = jnp.float32
BF16 = jnp.bfloat16


def _load_weights(w_in_hbm, w_out_hbm, win_scr, wout_scr, stage, sem):
    ch = stage.shape[1]
    grp = lax.broadcasted_iota(jnp.int32, (1, LANES), 1) // HALF
    q0, v0 = 3 * A_WIDTH, 3 * A_WIDTH + 2 * B_WIDTH

    def stream(src_hbm, dst_scr, n_cols, reorder):
        n_chunks = src_hbm.shape[1] // ch

        def chunk_copy(c, slot):
            return pltpu.make_async_copy(src_hbm.at[0, pl.ds(c * ch, ch), :],
                                         stage.at[slot, :, pl.ds(0, n_cols)], sem.at[slot])

        depth = stage.shape[0]
        for c0 in range(depth - 1):
            chunk_copy(c0, c0).start()

        def body(c, carry):
            slot = c % depth
            ahead = c + depth - 1

            @pl.when(ahead < n_chunks)
            def _():
                chunk_copy(ahead, ahead % depth).start()

            chunk_copy(c, slot).wait()
            rows = pl.ds(pl.multiple_of(c * ch, ch), ch)
            for cb in range(n_cols // LANES):
                t = stage[slot, :, cb * LANES:(cb + 1) * LANES]
                if reorder and q0 <= cb * LANES < v0:
                    t = jnp.where(grp == 1, pltpu.roll(t, LANES - HALF, 1),
                                  jnp.where(grp == 2, pltpu.roll(t, HALF, 1), t))
                dst_scr[rows, cb * LANES:(cb + 1) * LANES] = t.astype(BF16)
            return carry

        lax.fori_loop(0, n_chunks, body, 0)

    stream(w_in_hbm, win_scr, IN_COLS, True)
    stream(w_out_hbm, wout_scr, D_MODEL, False)


def _adaln_kernel(c_ref, w_ref, b_ref, mod_ref):
    c = c_ref[...]
    c_act = c * jax.nn.sigmoid(c)
    mod_ref[...] = jnp.dot(c_act.astype(BF16), w_ref[...].astype(BF16),
                           preferred_element_type=F32) + b_ref[...]


def _silu(z):
    return z * jax.nn.sigmoid(z)


def _head_mean_sq(t, bd_ref):
    sq = (t * t).astype(BF16)
    halves = [jnp.dot(sq[:, h * MXU_DIM:(h + 1) * MXU_DIM], bd_ref[...],
                      preferred_element_type=F32) for h in range(B_WIDTH // MXU_DIM)]
    return jnp.concatenate(halves, axis=-1) * (1.0 / B_HEAD_DIM)


def _rope(t, cos, sin_signed):
    cols = []
    for cb in range(B_WIDTH // LANES):
        tc = t[:, cb * LANES:(cb + 1) * LANES]
        cols.append(tc * cos + pltpu.roll(tc, LANES // 2, 1) * sin_signed)
    return jnp.concatenate(cols, axis=-1)


def _rope_tables(pos_quarters, invf, lane):
    n_grp = LANES // HALF
    qrows = SEQ_TILE // n_grp
    grp = lane // HALF
    pos_t = pos_quarters.T
    packed = pos_t[:qrows, n_grp - 1:n_grp]
    for gi in range(n_grp - 2, -1, -1):
        packed = jnp.where(grp == gi, pos_t[:qrows, gi:gi + 1], packed)
    ang = packed * invf
    tables = []
    for packed_tab in (jnp.cos(ang), jnp.sin(ang)):
        parts = []
        for gi in range(n_grp):
            t = jnp.where(grp == gi, packed_tab, 0.0)
            t = t + pltpu.roll(t, 2 * HALF, 1)
            parts.append(t + pltpu.roll(t, HALF, 1))
        tables.append(jnp.concatenate(parts, axis=0))
    return tables


def _norm_weight_lanes(w):
    lo, hi = w[:, :HALF], w[:, HALF:]
    block = jnp.concatenate([lo, lo, hi, hi], axis=1)
    return jnp.concatenate([block] * (B_WIDTH // LANES), axis=1)


def _across_sublanes(x, op):
    for shift in (4, 2, 1):
        x = op(x, pltpu.roll(x, shift, 0))
    return x


def _layer_kernel(x_ref, mod_ref, pos_ref, invf_ref, normw_ref, w_in_hbm, sgu_ref, ws_ref,
                  bs_ref, qnw_ref, knw_ref, lq1_ref, lk1_ref, lq2_ref, lk2_ref, subln_ref,
                  w_out_hbm, bd_ref, out_ref,
                  k_scr, vt_scr, qblk_scr, m_scr, acc_scr, mixed_scr, s_scr, p_scr,
                  alpha_scr, smax_scr, win_ref, wout_ref, stage_scr, stage_sem, nw_scr,
                  bst_scr):
    @pl.when((pl.program_id(0) == 0) & (pl.program_id(1) == 0))
    def _():
        _load_weights(w_in_hbm, w_out_hbm, win_ref, wout_ref, stage_scr, stage_sem)
        nw_scr[0:1, :] = _norm_weight_lanes(qnw_ref[...])
        nw_scr[1:2, :] = _norm_weight_lanes(knw_ref[...])
        bs_pad = jnp.concatenate(
            [bs_ref[...], jnp.zeros((SUBLANES - A_HEADS, CHUNK), F32)], axis=0)
        bst_scr[:, 0:SUBLANES] = bs_pad.T

    refs = dict(locals())
    tiles = [_TilePhases(t, refs) for t in range(TILES_PER_STEP)]
    tiles[0].activations()
    tiles[0].qk_projections()
    tiles[0].before_loop()
    for t, tile in enumerate(tiles):
        nxt = tiles[t + 1] if t + 1 < TILES_PER_STEP else None
        tile.key_loop()
        tile.last_value_product_and_out_a()
        if nxt is not None:
            nxt.activations()
            nxt.qk_projections()
        tile.finalise()
        tile.out_b()
        if nxt is not None:
            nxt.before_loop()


class _TilePhases:
    def __init__(self, t, refs):
        self.__dict__.update(refs)
        self.t = t
        self.par = t % 2
        self.base = pl.program_id(1) * TILES_PER_STEP
        self.i = self.base + t
        self.tile_rows = slice(t * SEQ_TILE, (t + 1) * SEQ_TILE)

    def activations(self):
        x_ref, tile_rows = self.x_ref, self.tile_rows
        b = pl.program_id(0)
        shift = self.mod_ref[0, pl.ds(b, 1), :]
        scale = self.mod_ref[1, pl.ds(b, 1), :]
        self.gate = self.mod_ref[2, pl.ds(b, 1), :]
        sq_lanes = None
        for c0 in range(0, D_MODEL, LANES):
            x_c = x_ref[0, tile_rows, c0:c0 + LANES]
            sq_lanes = x_c * x_c if sq_lanes is None else sq_lanes + x_c * x_c
        sum_sq = jnp.sum(sq_lanes, axis=-1, keepdims=True)
        inv_rms = lax.rsqrt(sum_sq * (1.0 / D_MODEL) + NORM_EPS)
        hb_cols = []
        for c0 in range(0, D_MODEL, MXU_DIM):
            cs = slice(c0, c0 + MXU_DIM)
            h_c = ((x_ref[0, tile_rows, cs] * inv_rms * self.normw_ref[:, cs])
                   * (1.0 + scale[:, cs]) + shift[:, cs])
            hb_cols.append(h_c.astype(BF16))
        self.hb = jnp.concatenate(hb_cols, axis=1)

    def proj(self, c0, width):
        return jnp.dot(self.hb, self.win_ref[:, c0:c0 + width], preferred_element_type=F32)

    def qk_projections(self):
        self.qb = self.proj(3 * A_WIDTH, B_WIDTH)
        self.kb = self.proj(3 * A_WIDTH + B_WIDTH, B_WIDTH)

    def scores(self, g, j, masked=False):
        tk, tq = SEQ_TILE, SEQ_TILE
        kp = self.k_scr[j, :, g * LANES:(g + 1) * LANES]
        s = jnp.dot(kp, self.qblk_scr[g], preferred_element_type=F32)
        if masked:
            kv_row = lax.broadcasted_iota(jnp.int32, (tk, 2 * tq), 0)
            q_col = lax.broadcasted_iota(jnp.int32, (tk, 2 * tq), 1) % tq
            s = jnp.where(kv_row <= q_col, s, -jnp.inf)
        self.s_scr[g] = s
        self.smax_scr[g] = jnp.max(s.reshape(tk // SUBLANES, SUBLANES, 2 * tq), axis=0)

    def softmax(self, g, slot):
        tk, tq = SEQ_TILE, SEQ_TILE
        rows = 2 * SUBLANES
        s_scr, m_scr, par = self.s_scr, self.m_scr, self.par

        def chunk(c):
            return s_scr[g, c * rows:(c + 1) * rows, :].reshape(2, SUBLANES, 2 * tq)

        m_old = m_scr[par, g]
        m_new = jnp.maximum(m_old, _across_sublanes(self.smax_scr[g], jnp.maximum))
        alpha = jnp.exp2(m_old - m_new)
        m_scr[par, g] = m_new
        self.alpha_scr[slot] = alpha
        for c in range(tk // rows):
            pc = jnp.exp2(chunk(c) - m_new[None])
            self.p_scr[slot, c * rows:(c + 1) * rows, :] = (
                pc.reshape(rows, 2 * tq).astype(BF16))

    def value_update(self, g, j, slot):
        tq = SEQ_TILE
        acc_scr, par = self.acc_scr, self.par
        vt = self.vt_scr[j, g]
        alpha = self.alpha_scr[slot]
        for st in range(2):
            pv = jnp.dot(vt, self.p_scr[slot, :, st * tq:(st + 1) * tq],
                         preferred_element_type=F32)
            a = alpha[:, st * tq:(st + 1) * tq]
            acc = acc_scr[par, 2 * g + st].reshape(V_EXT // SUBLANES, SUBLANES, tq)
            acc_scr[par, 2 * g + st] = (acc * a[None]).reshape(V_EXT, tq) + pv

    def unit(self, g, masked, tile, succ_tile, prev_tile):
        ahead = (g + 2) % B_HEADS
        if prev_tile is not None:
            self.value_update((g - 1) % B_HEADS, prev_tile, (g - 1) % 2)
        if g + 2 < B_HEADS:
            self.scores(ahead, tile, masked)
        elif succ_tile is not None:
            self.scores(ahead, succ_tile)
        self.softmax(g, g % 2)

    def before_loop(self):
        tq, tk, i, par = SEQ_TILE, SEQ_TILE, self.i, self.par
        qb, kb = self.qb, self.kb
        lane = lax.broadcasted_iota(jnp.int32, (1, LANES), 1)
        cos, sin = _rope_tables(self.pos_ref[self.t].astype(F32), self.invf_ref[...], lane)
        sin_signed = jnp.where(lane < LANES // 2, -sin, sin)
        q = qb * lax.rsqrt(_head_mean_sq(qb, self.bd_ref) + NORM_EPS) * self.nw_scr[0:1, :]
        k = kb * lax.rsqrt(_head_mean_sq(kb, self.bd_ref) + NORM_EPS) * self.nw_scr[1:2, :]
        q = _rope(q, cos, sin_signed) * (B_HEAD_DIM ** -0.5 * LOG2E)
        k = _rope(k, cos, sin_signed)
        self.k_scr[i] = k.astype(BF16)

        qt = q.T
        pair_row = lax.broadcasted_iota(jnp.int32, (LANES, tq), 0)
        is_a = (pair_row // HALF) % 2 == 0
        for g in range(B_HEADS):
            qp = qt[g * LANES:(g + 1) * LANES, :]
            self.qblk_scr[g] = jnp.concatenate(
                [jnp.where(is_a, qp, 0.0), jnp.where(is_a, 0.0, qp)], axis=1).astype(BF16)

        self.m_scr[par] = jnp.full(self.m_scr.shape[1:], -jnp.inf, F32)
        self.acc_scr[par] = jnp.zeros(self.acc_scr.shape[1:], F32)

        vb = self.proj(3 * A_WIDTH + 2 * B_WIDTH, B_WIDTH)
        vbt = vb.T.astype(BF16)
        for g in range(B_HEADS):
            self.vt_scr[i, g, :B_V_DIM, :] = vbt[g * B_V_DIM:(g + 1) * B_V_DIM, :]
            self.vt_scr[i, g, B_V_DIM:, :] = jnp.ones((V_EXT - B_V_DIM, tk), BF16)
        va = self.proj(A_WIDTH, A_WIDTH)
        self.scores(0, i, masked=True)
        self.scores(1, i, masked=True)
        ua = self.proj(0, A_WIDTH)
        self.unit(0, True, i, 0, None)
        za = self.proj(2 * A_WIDTH, A_WIDTH)
        self.unit(1, True, i, 0, i)
        zb = self.proj(3 * A_WIDTH + 3 * B_WIDTH, B_WIDTH)
        after_diag = i - 1 if self.t > 0 else 0
        self.unit(2, True, i, after_diag, i)

        row_c = lax.broadcasted_iota(jnp.int32, (CHUNK, CHUNK), 0)
        col_c = lax.broadcasted_iota(jnp.int32, (CHUNK, CHUNK), 1)
        tril = row_c >= col_c
        for hh in range(A_HEADS):
            sl = slice(hh * A_HEAD_DIM, (hh + 1) * A_HEAD_DIM)
            v = va[:, sl]
            vn = (v * lax.rsqrt(jnp.mean(v * v, axis=-1, keepdims=True) + NORM_EPS)
                  * self.sgu_ref[hh:hh + 1, :])
            vnb = vn.astype(BF16)
            ws = jnp.where(tril, self.ws_ref[hh], 0.0).astype(BF16)
            bias = self.bst_scr[:, hh:hh + 1]
            n_chunks = tq // CHUNK
            chunks = jnp.concatenate(
                [vnb[c * CHUNK:(c + 1) * CHUNK, :] for c in range(n_chunks)], axis=1)
            mixed_chunks = jnp.dot(ws, chunks, preferred_element_type=F32)
            mix = jnp.concatenate([mixed_chunks[:, c * A_HEAD_DIM:(c + 1) * A_HEAD_DIM] + bias
                                   for c in range(n_chunks)], axis=0)
            a_out = ua[:, sl] * mix * _silu(za[:, sl])
            self.mixed_scr[par, :, sl] = a_out.astype(BF16)
        self.zb_gate = _silu(zb)
        self.unit(3, True, i, after_diag, i)
        for s in range(self.t):
            tile = i - 1 - s
            succ = tile - 1 if s + 1 < self.t else 0
            self.tile_units(tile, succ, tile + 1)

    def tile_units(self, tile, succ_tile, prev_tile):
        self.unit(0, False, tile, succ_tile, prev_tile)
        for g in range(1, B_HEADS):
            self.unit(g, False, tile, succ_tile, tile)

    def key_loop(self):
        base = self.base

        def pair_body(jj, carry):
            j = 2 * jj
            self.tile_units(j, j + 1, jnp.where(j == 0, base, j - 1))
            self.tile_units(j + 1, j + 2, j)
            return carry

        lax.fori_loop(0, base // 2, pair_body, 0)

    def last_value_product_and_out_a(self):
        last = B_HEADS - 1
        base = self.base
        self.value_update(last, jnp.where(base > 0, base - 1, base), last % 2)
        rows = self.tile_rows
        self.out_ref[0, rows, :] = self.x_ref[0, rows, :] + self.gate * jnp.dot(
            self.mixed_scr[self.par, :, :A_WIDTH], self.wout_ref[:A_WIDTH, :],
            preferred_element_type=F32)

    def finalise(self):
        tq, par, acc_scr = SEQ_TILE, self.par, self.acc_scr
        lam = (jnp.exp(jnp.sum(self.lq1_ref[...] * self.lk1_ref[...], axis=-1, keepdims=True))
               - jnp.exp(jnp.sum(self.lq2_ref[...] * self.lk2_ref[...], axis=-1, keepdims=True))
               + LAMBDA_INIT)
        for g in range(B_HEADS):
            inv_l1 = 1.0 / acc_scr[par, 2 * g, B_V_DIM:B_V_DIM + SUBLANES, :]
            inv_l2 = 1.0 / acc_scr[par, 2 * g + 1, B_V_DIM:B_V_DIM + SUBLANES, :]
            acc1 = acc_scr[par, 2 * g, :B_V_DIM, :].reshape(B_V_DIM // SUBLANES, SUBLANES, tq)
            acc2 = acc_scr[par, 2 * g + 1, :B_V_DIM, :].reshape(
                B_V_DIM // SUBLANES, SUBLANES, tq)
            ot = acc1 * inv_l1[None] - lam * (acc2 * inv_l2[None])
            o = ot.reshape(B_V_DIM, tq).T
            o = (o * lax.rsqrt(jnp.mean(o * o, axis=-1, keepdims=True) + SUBLN_EPS)
                 * self.subln_ref[...])
            o = o * (1.0 - LAMBDA_INIT)
            vsl = slice(g * B_V_DIM, (g + 1) * B_V_DIM)
            b_out = o * self.zb_gate[:, vsl]
            self.mixed_scr[par, :, A_WIDTH + g * B_V_DIM:A_WIDTH + (g + 1) * B_V_DIM] = (
                b_out.astype(BF16))

    def out_b(self):
        self.out_ref[0, self.tile_rows, :] += self.gate * jnp.dot(
            self.mixed_scr[self.par, :, A_WIDTH:], self.wout_ref[A_WIDTH:, :],
            preferred_element_type=F32)


def kernel(x, c, positions, norm_w, w_ada, b_ada, w_in, sgu_norm_w, w_s, b_s, q_norm_w, k_norm_w,
           lambda_q1, lambda_k1, lambda_q2, lambda_k2, subln_w, w_out):
    bsz, seq, d = x.shape
    assert d == D_MODEL and seq % (TILES_PER_STEP * SEQ_TILE) == 0 and norm_w.shape[0] == 1
    assert TILES_PER_STEP % 2 == 0
    nt = seq // SEQ_TILE

    mod = pl.pallas_call(
        _adaln_kernel,
        grid=(3,),
        in_specs=[pl.BlockSpec((bsz, d), lambda n: (0, 0)),
                  pl.BlockSpec((None, d, d), lambda n: (0, 0, n)),
                  pl.BlockSpec((1, d), lambda n: (0, n))],
        out_specs=pl.BlockSpec((None, bsz, d), lambda n: (n, 0, 0)),
        out_shape=jax.ShapeDtypeStruct((3, bsz, d), F32),
        name="adaln_mod",
    )(c, w_ada, b_ada)

    n_grp = LANES // HALF
    pos_q = jnp.pad(positions.reshape(bsz, nt, n_grp, SEQ_TILE // n_grp),
                    ((0, 0), (0, 0), (0, SUBLANES - n_grp), (0, LANES - SEQ_TILE // n_grp)))

    inv_freq = ROPE_THETA ** (-jnp.arange(0, B_HEAD_DIM, 2, dtype=F32) / B_HEAD_DIM)
    invf = jnp.tile(inv_freq, LANES // HALF)[None, :]
    lane_head = (np.arange(MXU_DIM) // LANES) * 2 + (np.arange(MXU_DIM) // HALF) % 2
    bd = jnp.asarray(lane_head[:, None] == lane_head[None, :], dtype=BF16)


    const2 = lambda b, i: (0, 0)
    const3 = lambda b, i: (0, 0, 0)
    step_rows = TILES_PER_STEP * SEQ_TILE
    out = pl.pallas_call(
        _layer_kernel,
        grid=(bsz, nt // TILES_PER_STEP),
        in_specs=[
            pl.BlockSpec((1, step_rows, d), lambda b, i: (b, i, 0)),
            pl.BlockSpec((3, bsz, d), const3),
            pl.BlockSpec((None, TILES_PER_STEP, SUBLANES, LANES),
                         lambda b, i: (b, i, 0, 0)),
            pl.BlockSpec((1, LANES), const2),
            pl.BlockSpec((1, d), const2),
            pl.BlockSpec(memory_space=pl.ANY),
            pl.BlockSpec((A_HEADS, A_HEAD_DIM), const2),
            pl.BlockSpec((A_HEADS, CHUNK, CHUNK), const3),
            pl.BlockSpec((A_HEADS, CHUNK), const2),
            pl.BlockSpec((1, B_HEAD_DIM), const2),
            pl.BlockSpec((1, B_HEAD_DIM), const2),
            pl.BlockSpec((1, B_HEAD_DIM), const2),
            pl.BlockSpec((1, B_HEAD_DIM), const2),
            pl.BlockSpec((1, B_HEAD_DIM), const2),
            pl.BlockSpec((1, B_HEAD_DIM), const2),
            pl.BlockSpec((1, B_V_DIM), const2),
            pl.BlockSpec(memory_space=pl.ANY),
            pl.BlockSpec((MXU_DIM, MXU_DIM), const2),
        ],
        out_specs=pl.BlockSpec((1, step_rows, d), lambda b, i: (b, i, 0)),
        out_shape=jax.ShapeDtypeStruct((bsz, seq, d), F32),
        scratch_shapes=[
            pltpu.VMEM((nt, SEQ_TILE, B_WIDTH), BF16),
            pltpu.VMEM((nt, B_HEADS, V_EXT, SEQ_TILE), BF16),
            pltpu.VMEM((B_HEADS, LANES, 2 * SEQ_TILE), BF16),
            pltpu.VMEM((2, B_HEADS, SUBLANES, 2 * SEQ_TILE), F32),
            pltpu.VMEM((2, 2 * B_HEADS, V_EXT, SEQ_TILE), F32),
            pltpu.VMEM((2, SEQ_TILE, A_WIDTH + B_WIDTH), BF16),
            pltpu.VMEM((B_HEADS, SEQ_TILE, 2 * SEQ_TILE), F32),
            pltpu.VMEM((2, SEQ_TILE, 2 * SEQ_TILE), BF16),
            pltpu.VMEM((2, SUBLANES, 2 * SEQ_TILE), F32),
            pltpu.VMEM((B_HEADS, SUBLANES, 2 * SEQ_TILE), F32),
            pltpu.VMEM((d, IN_COLS), BF16),
            pltpu.VMEM((A_WIDTH + B_WIDTH, d), BF16),
            pltpu.VMEM((WEIGHT_STAGE_SLOTS, WEIGHT_CHUNK_ROWS, IN_COLS), F32),
            pltpu.SemaphoreType.DMA((WEIGHT_STAGE_SLOTS,)),
            pltpu.VMEM((SUBLANES, B_WIDTH), F32),
            pltpu.VMEM((CHUNK, LANES), F32),
        ],
        compiler_params=pltpu.CompilerParams(
            dimension_semantics=("arbitrary", "arbitrary"),
            vmem_limit_bytes=VMEM_LIMIT_BYTES),
        name="fused_layer",
    )(x, mod, pos_q, invf, norm_w, w_in,
      sgu_norm_w.reshape(A_HEADS, A_HEAD_DIM), w_s.reshape(A_HEADS, CHUNK, CHUNK),
      b_s.reshape(A_HEADS, CHUNK), q_norm_w, k_norm_w,
      lambda_q1, lambda_k1, lambda_q2, lambda_k2, subln_w, w_out, bd)
    return out
```

```python
import math

import numpy as np
import jax
import jax.numpy as jnp
from jax import lax
from jax.experimental import pallas as pl
from jax.experimental.pallas import tpu as pltpu

D_MODEL = 1024
A_WIDTH = 512
B_WIDTH = 512
A_HEADS = 4
A_HEAD_DIM = 128
CHUNK = 128
B_HEADS = 4
B_HEAD_DIM = 64
B_V_DIM = 128
ROPE_THETA = 10000.0
NORM_EPS = 1e-6
SUBLN_EPS = 1e-5
IN_COLS = 3 * A_WIDTH + 4 * B_WIDTH
LAMBDA_INIT = 0.8 - 0.6 * math.exp(-0.3 * 0)
LOG2E = math.log2(math.e)

LANES = 128
SUBLANES = 8
MXU_DIM = 256
SEQ_TILE = 256
TILES_PER_STEP = 4
VMEM_LIMIT_BYTES = 56 * 1024 * 1024
HALF = B_HEAD_DIM // 2
BF16_ROWS = 16
V_EXT = B_V_DIM + BF16_ROWS
WEIGHT_CHUNK_ROWS = 64
WEIGHT_STAGE_SLOTS = 4

F32 = jnp.float32
BF16 = jnp.bfloat16


def _load_weights(w_in_hbm, w_out_hbm, win_scr, wout_scr, stage, sem):
    ch = stage.shape[1]
    grp = lax.broadcasted_iota(jnp.int32, (1, LANES), 1) // HALF
    q0, v0 = 3 * A_WIDTH, 3 * A_WIDTH + 2 * B_WIDTH

    def stream(src_hbm, dst_scr, n_cols, reorder):
        n_chunks = src_hbm.shape[1] // ch

        def chunk_copy(c, slot):
            return pltpu.make_async_copy(src_hbm.at[0, pl.ds(c * ch, ch), :],
                                         stage.at[slot, :, pl.ds(0, n_cols)], sem.at[slot])

        depth = stage.shape[0]
        for c0 in range(depth - 1):
            chunk_copy(c0, c0).start()

        def body(c, carry):
            slot = c % depth
            ahead = c + depth - 1

            @pl.when(ahead < n_chunks)
            def _():
                chunk_copy(ahead, ahead % depth).start()

            chunk_copy(c, slot).wait()
            rows = pl.ds(pl.multiple_of(c * ch, ch), ch)
            for cb in range(n_cols // LANES):
                t = stage[slot, :, cb * LANES:(cb + 1) * LANES]
                if reorder and q0 <= cb * LANES < v0:
                    t = jnp.where(grp == 1, pltpu.roll(t, LANES - HALF, 1),
                                  jnp.where(grp == 2, pltpu.roll(t, HALF, 1), t))
                dst_scr[rows, cb * LANES:(cb + 1) * LANES] = t.astype(BF16)
            return carry

        lax.fori_loop(0, n_chunks, body, 0)

    stream(w_in_hbm, win_scr, IN_COLS, True)
    stream(w_out_hbm, wout_scr, D_MODEL, False)


def _adaln_kernel(c_ref, w_ref, b_ref, mod_ref):
    c = c_ref[...]
    c_act = c * jax.nn.sigmoid(c)
    mod_ref[...] = jnp.dot(c_act.astype(BF16), w_ref[...].astype(BF16),
                           preferred_element_type=F32) + b_ref[...]


def _silu(z):
    return z * jax.nn.sigmoid(z)


def _head_mean_sq(t, bd_ref):
    sq = (t * t).astype(BF16)
    halves = [jnp.dot(sq[:, h * MXU_DIM:(h + 1) * MXU_DIM], bd_ref[...],
                      preferred_element_type=F32) for h in range(B_WIDTH // MXU_DIM)]
    return jnp.concatenate(halves, axis=-1) * (1.0 / B_HEAD_DIM)


def _rope(t, cos, sin_signed):
    cols = []
    for cb in range(B_WIDTH // LANES):
        tc = t[:, cb * LANES:(cb + 1) * LANES]
        cols.append(tc * cos + pltpu.roll(tc, LANES // 2, 1) * sin_signed)
    return jnp.concatenate(cols, axis=-1)


def _rope_tables(pos_quarters, invf, lane):
    n_grp = LANES // HALF
    qrows = SEQ_TILE // n_grp
    grp = lane // HALF
    pos_t = pos_quarters.T
    packed = pos_t[:qrows, n_grp - 1:n_grp]
    for gi in range(n_grp - 2, -1, -1):
        packed = jnp.where(grp == gi, pos_t[:qrows, gi:gi + 1], packed)
    ang = packed * invf
    tables = []
    for packed_tab in (jnp.cos(ang), jnp.sin(ang)):
        parts = []
        for gi in range(n_grp):
            t = jnp.where(grp == gi, packed_tab, 0.0)
            t = t + pltpu.roll(t, 2 * HALF, 1)
            parts.append(t + pltpu.roll(t, HALF, 1))
        tables.append(jnp.concatenate(parts, axis=0))
    return tables


def _norm_weight_lanes(w):
    lo, hi = w[:, :HALF], w[:, HALF:]
    block = jnp.concatenate([lo, lo, hi, hi], axis=1)
    return jnp.concatenate([block] * (B_WIDTH // LANES), axis=1)


def _across_sublanes(x, op):
    for shift in (4, 2, 1):
        x = op(x, pltpu.roll(x, shift, 0))
    return x


def _layer_kernel(x_ref, mod_ref, pos_ref, invf_ref, normw_ref, w_in_hbm, sgu_ref, ws_ref,
                  bs_ref, qnw_ref, knw_ref, lq1_ref, lk1_ref, lq2_ref, lk2_ref, subln_ref,
                  w_out_hbm, bd_ref, out_ref,
                  k_scr, vt_scr, qblk_scr, m_scr, acc_scr, mixed_scr, s_scr, p_scr,
                  alpha_scr, smax_scr, win_ref, wout_ref, stage_scr, stage_sem, nw_scr,
                  bst_scr):
    @pl.when((pl.program_id(0) == 0) & (pl.program_id(1) == 0))
    def _():
        _load_weights(w_in_hbm, w_out_hbm, win_ref, wout_ref, stage_scr, stage_sem)
        nw_scr[0:1, :] = _norm_weight_lanes(qnw_ref[...])
        nw_scr[1:2, :] = _norm_weight_lanes(knw_ref[...])
        bs_pad = jnp.concatenate(
            [bs_ref[...], jnp.zeros((SUBLANES - A_HEADS, CHUNK), F32)], axis=0)
        bst_scr[:, 0:SUBLANES] = bs_pad.T

    refs = dict(locals())
    tiles = [_TilePhases(t, refs) for t in range(TILES_PER_STEP)]
    tiles[0].activations()
    tiles[0].qk_projections()
    tiles[0].before_loop()
    for t, tile in enumerate(tiles):
        nxt = tiles[t + 1] if t + 1 < TILES_PER_STEP else None
        tile.key_loop()
        tile.last_value_product_and_out_a()
        if nxt is not None:
            nxt.activations()
            nxt.qk_projections()
        tile.finalise()
        tile.out_b()
        if nxt is not None:
            nxt.before_loop()


class _TilePhases:
    def __init__(self, t, refs):
        self.__dict__.update(refs)
        self.t = t
        self.par = t % 2
        self.base = pl.program_id(1) * TILES_PER_STEP
        self.i = self.base + t
        self.tile_rows = slice(t * SEQ_TILE, (t + 1) * SEQ_TILE)

    def activations(self):
        x_ref, tile_rows = self.x_ref, self.tile_rows
        b = pl.program_id(0)
        shift = self.mod_ref[0, pl.ds(b, 1), :]
        scale = self.mod_ref[1, pl.ds(b, 1), :]
        self.gate = self.mod_ref[2, pl.ds(b, 1), :]
        sq_lanes = None
        for c0 in range(0, D_MODEL, LANES):
            x_c = x_ref[0, tile_rows, c0:c0 + LANES]
            sq_lanes = x_c * x_c if sq_lanes is None else sq_lanes + x_c * x_c
        sum_sq = jnp.sum(sq_lanes, axis=-1, keepdims=True)
        inv_rms = lax.rsqrt(sum_sq * (1.0 / D_MODEL) + NORM_EPS)
        hb_cols = []
        for c0 in range(0, D_MODEL, MXU_DIM):
            cs = slice(c0, c0 + MXU_DIM)
            h_c = ((x_ref[0, tile_rows, cs] * inv_rms * self.normw_ref[:, cs])
                   * (1.0 + scale[:, cs]) + shift[:, cs])
            hb_cols.append(h_c.astype(BF16))
        self.hb = jnp.concatenate(hb_cols, axis=1)

    def proj(self, c0, width):
        return jnp.dot(self.hb, self.win_ref[:, c0:c0 + width], preferred_element_type=F32)

    def qk_projections(self):
        self.qb = self.proj(3 * A_WIDTH, B_WIDTH)
        self.kb = self.proj(3 * A_WIDTH + B_WIDTH, B_WIDTH)

    def scores(self, g, j, masked=False):
        tk, tq = SEQ_TILE, SEQ_TILE
        kp = self.k_scr[j, :, g * LANES:(g + 1) * LANES]
        s = jnp.dot(kp, self.qblk_scr[g], preferred_element_type=F32)
        if masked:
            kv_row = lax.broadcasted_iota(jnp.int32, (tk, 2 * tq), 0)
            q_col = lax.broadcasted_iota(jnp.int32, (tk, 2 * tq), 1) % tq
            s = jnp.where(kv_row <= q_col, s, -jnp.inf)
        self.s_scr[g] = s
        self.smax_scr[g] = jnp.max(s.reshape(tk // SUBLANES, SUBLANES, 2 * tq), axis=0)

    def softmax(self, g, slot):
        tk, tq = SEQ_TILE, SEQ_TILE
        rows = 2 * SUBLANES
        s_scr, m_scr, par = self.s_scr, self.m_scr, self.par

        m_old = m_scr[par, g]
        m_new = jnp.maximum(m_old, _across_sublanes(self.smax_scr[g], jnp.maximum))
        alpha = jnp.exp2(m_old - m_new)
        m_scr[par, g] = m_new
        self.alpha_scr[slot] = alpha

        def stream_probabilities(st):
            lanes = slice(st * tq, (st + 1) * tq)
            for c in range(tk // rows):
                sc = s_scr[g, c * rows:(c + 1) * rows, lanes].reshape(2, SUBLANES, tq)
                pc = jnp.exp2(sc - m_new[None, :, lanes])
                self.p_scr[slot, c * rows:(c + 1) * rows, lanes] = (
                    pc.reshape(rows, tq).astype(BF16))

        return stream_probabilities

    def value_update(self, g, j, slot, st):
        tq = SEQ_TILE
        acc_scr, par = self.acc_scr, self.par
        vt = self.vt_scr[j, g]
        lanes = slice(st * tq, (st + 1) * tq)
        pv = jnp.dot(vt, self.p_scr[slot, :, lanes], preferred_element_type=F32)
        a = self.alpha_scr[slot][:, lanes]
        acc = acc_scr[par, 2 * g + st].reshape(V_EXT // SUBLANES, SUBLANES, tq)
        acc_scr[par, 2 * g + st] = (acc * a[None]).reshape(V_EXT, tq) + pv

    def unit(self, g, masked, tile, succ_tile, prev_tile):
        ahead = (g + 2) % B_HEADS
        if prev_tile is not None:
            self.value_update((g - 1) % B_HEADS, prev_tile, (g - 1) % 2, 1)
        if g + 2 < B_HEADS:
            self.scores(ahead, tile, masked)
        elif succ_tile is not None:
            self.scores(ahead, succ_tile)
        stream_probabilities = self.softmax(g, g % 2)
        stream_probabilities(0)
        self.value_update(g, tile, g % 2, 0)
        stream_probabilities(1)

    def before_loop(self):
        tq, tk, i, par = SEQ_TILE, SEQ_TILE, self.i, self.par
        qb, kb = self.qb, self.kb
        lane = lax.broadcasted_iota(jnp.int32, (1, LANES), 1)
        cos, sin = _rope_tables(self.pos_ref[self.t].astype(F32), self.invf_ref[...], lane)
        sin_signed = jnp.where(lane < LANES // 2, -sin, sin)
        q = qb * lax.rsqrt(_head_mean_sq(qb, self.bd_ref) + NORM_EPS) * self.nw_scr[0:1, :]
        k = kb * lax.rsqrt(_head_mean_sq(kb, self.bd_ref) + NORM_EPS) * self.nw_scr[1:2, :]
        q = _rope(q, cos, sin_signed) * (B_HEAD_DIM ** -0.5 * LOG2E)
        k = _rope(k, cos, sin_signed)
        self.k_scr[i] = k.astype(BF16)

        qt = q.T
        pair_row = lax.broadcasted_iota(jnp.int32, (LANES, tq), 0)
        is_a = (pair_row // HALF) % 2 == 0
        for g in range(B_HEADS):
            qp = qt[g * LANES:(g + 1) * LANES, :]
            self.qblk_scr[g] = jnp.concatenate(
                [jnp.where(is_a, qp, 0.0), jnp.where(is_a, 0.0, qp)], axis=1).astype(BF16)

        self.m_scr[par] = jnp.full(self.m_scr.shape[1:], -jnp.inf, F32)
        self.acc_scr[par] = jnp.zeros(self.acc_scr.shape[1:], F32)

        vb = self.proj(3 * A_WIDTH + 2 * B_WIDTH, B_WIDTH)
        vbt = vb.T.astype(BF16)
        for g in range(B_HEADS):
            self.vt_scr[i, g, :B_V_DIM, :] = vbt[g * B_V_DIM:(g + 1) * B_V_DIM, :]
            self.vt_scr[i, g, B_V_DIM:, :] = jnp.ones((V_EXT - B_V_DIM, tk), BF16)
        va = self.proj(A_WIDTH, A_WIDTH)
        self.scores(0, i, masked=True)
        self.scores(1, i, masked=True)
        ua = self.proj(0, A_WIDTH)
        self.unit(0, True, i, 0, None)
        za = self.proj(2 * A_WIDTH, A_WIDTH)
        self.unit(1, True, i, 0, i)
        zb = self.proj(3 * A_WIDTH + 3 * B_WIDTH, B_WIDTH)
        after_diag = i - 1 if self.t > 0 else 0
        self.unit(2, True, i, after_diag, i)

        row_c = lax.broadcasted_iota(jnp.int32, (CHUNK, CHUNK), 0)
        col_c = lax.broadcasted_iota(jnp.int32, (CHUNK, CHUNK), 1)
        tril = row_c >= col_c
        for hh in range(A_HEADS):
            sl = slice(hh * A_HEAD_DIM, (hh + 1) * A_HEAD_DIM)
            v = va[:, sl]
            vn = (v * lax.rsqrt(jnp.mean(v * v, axis=-1, keepdims=True) + NORM_EPS)
                  * self.sgu_ref[hh:hh + 1, :])
            vnb = vn.astype(BF16)
            ws = jnp.where(tril, self.ws_ref[hh], 0.0).astype(BF16)
            bias = self.bst_scr[:, hh:hh + 1]
            n_chunks = tq // CHUNK
            chunks = jnp.concatenate(
                [vnb[c * CHUNK:(c + 1) * CHUNK, :] for c in range(n_chunks)], axis=1)
            mixed_chunks = jnp.dot(ws, chunks, preferred_element_type=F32)
            mix = jnp.concatenate([mixed_chunks[:, c * A_HEAD_DIM:(c + 1) * A_HEAD_DIM] + bias
                                   for c in range(n_chunks)], axis=0)
            a_out = ua[:, sl] * mix * _silu(za[:, sl])
            self.mixed_scr[par, :, sl] = a_out.astype(BF16)
        self.zb_gate = _silu(zb)
        self.unit(3, True, i, after_diag, i)
        for s in range(self.t):
            tile = i - 1 - s
            succ = tile - 1 if s + 1 < self.t else 0
            self.tile_units(tile, succ, tile + 1)

    def tile_units(self, tile, succ_tile, prev_tile):
        self.unit(0, False, tile, succ_tile, prev_tile)
        for g in range(1, B_HEADS):
            self.unit(g, False, tile, succ_tile, tile)

    def key_loop(self):
        base = self.base

        def pair_body(jj, carry):
            j = 2 * jj
            self.tile_units(j, j + 1, jnp.where(j == 0, base, j - 1))
            self.tile_units(j + 1, j + 2, j)
            return carry

        lax.fori_loop(0, base // 2, pair_body, 0)

    def last_value_product_and_out_a(self):
        last = B_HEADS - 1
        base = self.base
        self.value_update(last, jnp.where(base > 0, base - 1, base), last % 2, 1)
        rows = self.tile_rows
        self.out_ref[0, rows, :] = self.x_ref[0, rows, :] + self.gate * jnp.dot(
            self.mixed_scr[self.par, :, :A_WIDTH], self.wout_ref[:A_WIDTH, :],
            preferred_element_type=F32)

    def finalise(self):
        tq, par, acc_scr = SEQ_TILE, self.par, self.acc_scr
        lam = (jnp.exp(jnp.sum(self.lq1_ref[...] * self.lk1_ref[...], axis=-1, keepdims=True))
               - jnp.exp(jnp.sum(self.lq2_ref[...] * self.lk2_ref[...], axis=-1, keepdims=True))
               + LAMBDA_INIT)
        for g in range(B_HEADS):
            inv_l1 = 1.0 / acc_scr[par, 2 * g, B_V_DIM:B_V_DIM + SUBLANES, :]
            inv_l2 = 1.0 / acc_scr[par, 2 * g + 1, B_V_DIM:B_V_DIM + SUBLANES, :]
            acc1 = acc_scr[par, 2 * g, :B_V_DIM, :].reshape(B_V_DIM // SUBLANES, SUBLANES, tq)
            acc2 = acc_scr[par, 2 * g + 1, :B_V_DIM, :].reshape(
                B_V_DIM // SUBLANES, SUBLANES, tq)
            ot = acc1 * inv_l1[None] - lam * (acc2 * inv_l2[None])
            o = ot.reshape(B_V_DIM, tq).T
            o = (o * lax.rsqrt(jnp.mean(o * o, axis=-1, keepdims=True) + SUBLN_EPS)
                 * self.subln_ref[...])
            o = o * (1.0 - LAMBDA_INIT)
            vsl = slice(g * B_V_DIM, (g + 1) * B_V_DIM)
            b_out = o * self.zb_gate[:, vsl]
            self.mixed_scr[par, :, A_WIDTH + g * B_V_DIM:A_WIDTH + (g + 1) * B_V_DIM] = (
                b_out.astype(BF16))

    def out_b(self):
        self.out_ref[0, self.tile_rows, :] += self.gate * jnp.dot(
            self.mixed_scr[self.par, :, A_WIDTH:], self.wout_ref[A_WIDTH:, :],
            preferred_element_type=F32)


def kernel(x, c, positions, norm_w, w_ada, b_ada, w_in, sgu_norm_w, w_s, b_s, q_norm_w, k_norm_w,
           lambda_q1, lambda_k1, lambda_q2, lambda_k2, subln_w, w_out):
    bsz, seq, d = x.shape
    assert d == D_MODEL and seq % (TILES_PER_STEP * SEQ_TILE) == 0 and norm_w.shape[0] == 1
    assert TILES_PER_STEP % 2 == 0
    nt = seq // SEQ_TILE

    mod = pl.pallas_call(
        _adaln_kernel,
        grid=(3,),
        in_specs=[pl.BlockSpec((bsz, d), lambda n: (0, 0)),
                  pl.BlockSpec((None, d, d), lambda n: (0, 0, n)),
                  pl.BlockSpec((1, d), lambda n: (0, n))],
        out_specs=pl.BlockSpec((None, bsz, d), lambda n: (n, 0, 0)),
        out_shape=jax.ShapeDtypeStruct((3, bsz, d), F32),
        name="adaln_mod",
    )(c, w_ada, b_ada)

    n_grp = LANES // HALF
    pos_q = jnp.pad(positions.reshape(bsz, nt, n_grp, SEQ_TILE // n_grp),
                    ((0, 0), (0, 0), (0, SUBLANES - n_grp), (0, LANES - SEQ_TILE // n_grp)))

    inv_freq = ROPE_THETA ** (-jnp.arange(0, B_HEAD_DIM, 2, dtype=F32) / B_HEAD_DIM)
    invf = jnp.tile(inv_freq, LANES // HALF)[None, :]
    lane_head = (np.arange(MXU_DIM) // LANES) * 2 + (np.arange(MXU_DIM) // HALF) % 2
    bd = jnp.asarray(lane_head[:, None] == lane_head[None, :], dtype=BF16)


    const2 = lambda b, i: (0, 0)
    const3 = lambda b, i: (0, 0, 0)
    step_rows = TILES_PER_STEP * SEQ_TILE
    out = pl.pallas_call(
        _layer_kernel,
        grid=(bsz, nt // TILES_PER_STEP),
        in_specs=[
            pl.BlockSpec((1, step_rows, d), lambda b, i: (b, i, 0)),
            pl.BlockSpec((3, bsz, d), const3),
            pl.BlockSpec((None, TILES_PER_STEP, SUBLANES, LANES),
                         lambda b, i: (b, i, 0, 0)),
            pl.BlockSpec((1, LANES), const2),
            pl.BlockSpec((1, d), const2),
            pl.BlockSpec(memory_space=pl.ANY),
            pl.BlockSpec((A_HEADS, A_HEAD_DIM), const2),
            pl.BlockSpec((A_HEADS, CHUNK, CHUNK), const3),
            pl.BlockSpec((A_HEADS, CHUNK), const2),
            pl.BlockSpec((1, B_HEAD_DIM), const2),
            pl.BlockSpec((1, B_HEAD_DIM), const2),
            pl.BlockSpec((1, B_HEAD_DIM), const2),
            pl.BlockSpec((1, B_HEAD_DIM), const2),
            pl.BlockSpec((1, B_HEAD_DIM), const2),
            pl.BlockSpec((1, B_HEAD_DIM), const2),
            pl.BlockSpec((1, B_V_DIM), const2),
            pl.BlockSpec(memory_space=pl.ANY),
            pl.BlockSpec((MXU_DIM, MXU_DIM), const2),
        ],
        out_specs=pl.BlockSpec((1, step_rows, d), lambda b, i: (b, i, 0)),
        out_shape=jax.ShapeDtypeStruct((bsz, seq, d), F32),
        scratch_shapes=[
            pltpu.VMEM((nt, SEQ_TILE, B_WIDTH), BF16),
            pltpu.VMEM((nt, B_HEADS, V_EXT, SEQ_TILE), BF16),
            pltpu.VMEM((B_HEADS, LANES, 2 * SEQ_TILE), BF16),
            pltpu.VMEM((2, B_HEADS, SUBLANES, 2 * SEQ_TILE), F32),
            pltpu.VMEM((2, 2 * B_HEADS, V_EXT, SEQ_TILE), F32),
            pltpu.VMEM((2, SEQ_TILE, A_WIDTH + B_WIDTH), BF16),
            pltpu.VMEM((B_HEADS, SEQ_TILE, 2 * SEQ_TILE), F32),
            pltpu.VMEM((2, SEQ_TILE, 2 * SEQ_TILE), BF16),
            pltpu.VMEM((2, SUBLANES, 2 * SEQ_TILE), F32),
            pltpu.VMEM((B_HEADS, SUBLANES, 2 * SEQ_TILE), F32),
            pltpu.VMEM((d, IN_COLS), BF16),
            pltpu.VMEM((A_WIDTH + B_WIDTH, d), BF16),
            pltpu.VMEM((WEIGHT_STAGE_SLOTS, WEIGHT_CHUNK_ROWS, IN_COLS), F32),
            pltpu.SemaphoreType.DMA((WEIGHT_STAGE_SLOTS,)),
            pltpu.VMEM((SUBLANES, B_WIDTH), F32),
            pltpu.VMEM((CHUNK, LANES), F32),
        ],
        compiler_params=pltpu.CompilerParams(
            dimension_semantics=("arbitrary", "arbitrary"),
            vmem_limit_bytes=VMEM_LIMIT_BYTES),
        name="fused_layer",
    )(x, mod, pos_q, invf, norm_w, w_in,
      sgu_norm_w.reshape(A_HEADS, A_HEAD_DIM), w_s.reshape(A_HEADS, CHUNK, CHUNK),
      b_s.reshape(A_HEADS, CHUNK), q_norm_w, k_norm_w,
      lambda_q1, lambda_k1, lambda_q2, lambda_k2, subln_w, w_out, bd)
    return out
```

```python
import math

import numpy as np
import jax
import jax.numpy as jnp
from jax import lax
from jax.experimental import pallas as pl
from jax.experimental.pallas import tpu as pltpu

D_MODEL = 1024
A_WIDTH = 512
B_WIDTH = 512
A_HEADS = 4
A_HEAD_DIM = 128
CHUNK = 128
B_HEADS = 4
B_HEAD_DIM = 64
B_V_DIM = 128
ROPE_THETA = 10000.0
NORM_EPS = 1e-6
SUBLN_EPS = 1e-5
IN_COLS = 3 * A_WIDTH + 4 * B_WIDTH
LAMBDA_INIT = 0.8 - 0.6 * math.exp(-0.3 * 0)
LOG2E = math.log2(math.e)

LANES = 128
SUBLANES = 8
MXU_DIM = 256
SEQ_TILE = 256
TILES_PER_STEP = 4
VMEM_LIMIT_BYTES = 56 * 1024 * 1024
HALF = B_HEAD_DIM // 2
BF16_ROWS = 16
V_EXT = B_V_DIM + BF16_ROWS
WEIGHT_CHUNK_ROWS = 64
WEIGHT_STAGE_SLOTS = 4

F32 = jnp.float32
BF16 = jnp.bfloat16


def _load_weights(w_in_hbm, w_out_hbm, win_scr, wout_scr, stage, sem):
    ch = stage.shape[1]
    grp = lax.broadcasted_iota(jnp.int32, (1, LANES), 1) // HALF
    q0, v0 = 3 * A_WIDTH, 3 * A_WIDTH + 2 * B_WIDTH

    def stream(src_hbm, dst_scr, n_cols, reorder):
        n_chunks = src_hbm.shape[1] // ch

        def chunk_copy(c, slot):
            return pltpu.make_async_copy(src_hbm.at[0, pl.ds(c * ch, ch), :],
                                         stage.at[slot, :, pl.ds(0, n_cols)], sem.at[slot])

        depth = stage.shape[0]
        for c0 in range(depth - 1):
            chunk_copy(c0, c0).start()

        def body(c, carry):
            slot = c % depth
            ahead = c + depth - 1

            @pl.when(ahead < n_chunks)
            def _():
                chunk_copy(ahead, ahead % depth).start()

            chunk_copy(c, slot).wait()
            rows = pl.ds(pl.multiple_of(c * ch, ch), ch)
            for cb in range(n_cols // LANES):
                t = stage[slot, :, cb * LANES:(cb + 1) * LANES]
                if reorder and q0 <= cb * LANES < v0:
                    t = jnp.where(grp == 1, pltpu.roll(t, LANES - HALF, 1),
                                  jnp.where(grp == 2, pltpu.roll(t, HALF, 1), t))
                dst_scr[rows, cb * LANES:(cb + 1) * LANES] = t.astype(BF16)
            return carry

        lax.fori_loop(0, n_chunks, body, 0)

    stream(w_in_hbm, win_scr, IN_COLS, True)
    stream(w_out_hbm, wout_scr, D_MODEL, False)


def _adaln_kernel(c_ref, w_ref, b_ref, mod_ref):
    c = c_ref[...]
    c_act = c * jax.nn.sigmoid(c)
    mod_ref[...] = jnp.dot(c_act.astype(BF16), w_ref[...].astype(BF16),
                           preferred_element_type=F32) + b_ref[...]


def _silu(z):
    return z * jax.nn.sigmoid(z)


def _head_mean_sq(t, bd_ref):
    sq = (t * t).astype(BF16)
    halves = [jnp.dot(sq[:, h * MXU_DIM:(h + 1) * MXU_DIM], bd_ref[...],
                      preferred_element_type=F32) for h in range(B_WIDTH // MXU_DIM)]
    return jnp.concatenate(halves, axis=-1) * (1.0 / B_HEAD_DIM)


def _rope(t, cos, sin_signed):
    cols = []
    for cb in range(B_WIDTH // LANES):
        tc = t[:, cb * LANES:(cb + 1) * LANES]
        cols.append(tc * cos + pltpu.roll(tc, LANES // 2, 1) * sin_signed)
    return jnp.concatenate(cols, axis=-1)


def _rope_tables(pos_quarters, invf, lane):
    n_grp = LANES // HALF
    qrows = SEQ_TILE // n_grp
    grp = lane // HALF
    pos_t = pos_quarters.T
    packed = pos_t[:qrows, n_grp - 1:n_grp]
    for gi in range(n_grp - 2, -1, -1):
        packed = jnp.where(grp == gi, pos_t[:qrows, gi:gi + 1], packed)
    ang = packed * invf
    tables = []
    for packed_tab in (jnp.cos(ang), jnp.sin(ang)):
        parts = []
        for gi in range(n_grp):
            t = jnp.where(grp == gi, packed_tab, 0.0)
            t = t + pltpu.roll(t, 2 * HALF, 1)
            parts.append(t + pltpu.roll(t, HALF, 1))
        tables.append(jnp.concatenate(parts, axis=0))
    return tables


def _norm_weight_lanes(w):
    lo, hi = w[:, :HALF], w[:, HALF:]
    block = jnp.concatenate([lo, lo, hi, hi], axis=1)
    return jnp.concatenate([block] * (B_WIDTH // LANES), axis=1)


def _across_sublanes(x, op):
    for shift in (4, 2, 1):
        x = op(x, pltpu.roll(x, shift, 0))
    return x


def _layer_kernel(x_ref, mod_ref, pos_ref, invf_ref, normw_ref, w_in_hbm, sgu_ref, ws_ref,
                  bs_ref, qnw_ref, knw_ref, lq1_ref, lk1_ref, lq2_ref, lk2_ref, subln_ref,
                  w_out_hbm, bd_ref, out_ref,
                  k_scr, vt_scr, qblk_scr, m_scr, acc_scr, mixed_scr, s_scr, p_scr,
                  alpha_scr, smax_scr, win_ref, wout_ref, stage_scr, stage_sem, nw_scr,
                  bst_scr):
    @pl.when((pl.program_id(0) == 0) & (pl.program_id(1) == 0))
    def _():
        _load_weights(w_in_hbm, w_out_hbm, win_ref, wout_ref, stage_scr, stage_sem)
        nw_scr[0:1, :] = _norm_weight_lanes(qnw_ref[...])
        nw_scr[1:2, :] = _norm_weight_lanes(knw_ref[...])
        bs_pad = jnp.concatenate(
            [bs_ref[...], jnp.zeros((SUBLANES - A_HEADS, CHUNK), F32)], axis=0)
        bst_scr[:, 0:SUBLANES] = bs_pad.T

    refs = dict(locals())
    tiles = [_TilePhases(t, refs) for t in range(TILES_PER_STEP)]
    tiles[0].activations()
    tiles[0].qk_projections()
    tiles[0].before_loop()
    for t, tile in enumerate(tiles):
        nxt = tiles[t + 1] if t + 1 < TILES_PER_STEP else None
        tile.key_loop()
        tile.last_value_product_and_out_a()
        if nxt is not None:
            nxt.activations()
            nxt.qk_projections()
        tile.finalise()
        tile.out_b()
        if nxt is not None:
            nxt.before_loop()


class _TilePhases:
    def __init__(self, t, refs):
        self.__dict__.update(refs)
        self.t = t
        self.par = t % 2
        self.base = pl.program_id(1) * TILES_PER_STEP
        self.i = self.base + t
        self.tile_rows = slice(t * SEQ_TILE, (t + 1) * SEQ_TILE)

    def activations(self):
        x_ref, tile_rows = self.x_ref, self.tile_rows
        b = pl.program_id(0)
        shift = self.mod_ref[0, pl.ds(b, 1), :]
        scale = self.mod_ref[1, pl.ds(b, 1), :]
        self.gate = self.mod_ref[2, pl.ds(b, 1), :]
        sq_lanes = None
        for c0 in range(0, D_MODEL, LANES):
            x_c = x_ref[0, tile_rows, c0:c0 + LANES]
            sq_lanes = x_c * x_c if sq_lanes is None else sq_lanes + x_c * x_c
        sum_sq = jnp.sum(sq_lanes, axis=-1, keepdims=True)
        inv_rms = lax.rsqrt(sum_sq * (1.0 / D_MODEL) + NORM_EPS)
        hb_cols = []
        for c0 in range(0, D_MODEL, MXU_DIM):
            cs = slice(c0, c0 + MXU_DIM)
            h_c = ((x_ref[0, tile_rows, cs] * inv_rms * self.normw_ref[:, cs])
                   * (1.0 + scale[:, cs]) + shift[:, cs])
            hb_cols.append(h_c.astype(BF16))
        self.hb = jnp.concatenate(hb_cols, axis=1)

    def proj(self, c0, width):
        return jnp.dot(self.hb, self.win_ref[:, c0:c0 + width], preferred_element_type=F32)

    def qk_projections(self):
        self.qb = self.proj(3 * A_WIDTH, B_WIDTH)
        self.kb = self.proj(3 * A_WIDTH + B_WIDTH, B_WIDTH)

    def scores(self, g, j, masked=False):
        tk, tq = SEQ_TILE, SEQ_TILE
        kp = self.k_scr[j, :, g * LANES:(g + 1) * LANES]
        for st in range(2):
            lanes = slice(st * tq, (st + 1) * tq)
            s = jnp.dot(kp, self.qblk_scr[g, :, lanes], preferred_element_type=F32)
            if masked:
                kv_row = lax.broadcasted_iota(jnp.int32, (tk, tq), 0)
                q_col = lax.broadcasted_iota(jnp.int32, (tk, tq), 1)
                s = jnp.where(kv_row <= q_col, s, -jnp.inf)
            self.s_scr[g, :, lanes] = s
            self.smax_scr[g, :, lanes] = jnp.max(
                s.reshape(tk // SUBLANES, SUBLANES, tq), axis=0)

    def softmax(self, g, slot):
        tk, tq = SEQ_TILE, SEQ_TILE
        rows = 2 * SUBLANES
        s_scr, m_scr, par = self.s_scr, self.m_scr, self.par

        m_old = m_scr[par, g]
        m_new = jnp.maximum(m_old, _across_sublanes(self.smax_scr[g], jnp.maximum))
        alpha = jnp.exp2(m_old - m_new)
        m_scr[par, g] = m_new
        self.alpha_scr[slot] = alpha

        def stream_probabilities(st):
            lanes = slice(st * tq, (st + 1) * tq)
            for c in range(tk // rows):
                sc = s_scr[g, c * rows:(c + 1) * rows, lanes].reshape(2, SUBLANES, tq)
                pc = jnp.exp2(sc - m_new[None, :, lanes])
                self.p_scr[slot, c * rows:(c + 1) * rows, lanes] = (
                    pc.reshape(rows, tq).astype(BF16))

        return stream_probabilities

    def value_update(self, g, j, slot, st):
        tq = SEQ_TILE
        acc_scr, par = self.acc_scr, self.par
        vt = self.vt_scr[j, g]
        lanes = slice(st * tq, (st + 1) * tq)
        pv = jnp.dot(vt, self.p_scr[slot, :, lanes], preferred_element_type=F32)
        a = self.alpha_scr[slot][:, lanes]
        acc = acc_scr[par, 2 * g + st].reshape(V_EXT // SUBLANES, SUBLANES, tq)
        acc_scr[par, 2 * g + st] = (acc * a[None]).reshape(V_EXT, tq) + pv

    def unit(self, g, masked, tile, succ_tile, prev_tile):
        ahead = (g + 2) % B_HEADS
        if prev_tile is not None:
            self.value_update((g - 1) % B_HEADS, prev_tile, (g - 1) % 2, 1)
        if g + 2 < B_HEADS:
            self.scores(ahead, tile, masked)
        elif succ_tile is not None:
            self.scores(ahead, succ_tile)
        stream_probabilities = self.softmax(g, g % 2)
        stream_probabilities(0)
        self.value_update(g, tile, g % 2, 0)
        stream_probabilities(1)

    def before_loop(self):
        tq, tk, i, par = SEQ_TILE, SEQ_TILE, self.i, self.par
        qb, kb = self.qb, self.kb
        lane = lax.broadcasted_iota(jnp.int32, (1, LANES), 1)
        cos, sin = _rope_tables(self.pos_ref[self.t].astype(F32), self.invf_ref[...], lane)
        sin_signed = jnp.where(lane < LANES // 2, -sin, sin)
        q = qb * lax.rsqrt(_head_mean_sq(qb, self.bd_ref) + NORM_EPS) * self.nw_scr[0:1, :]
        k = kb * lax.rsqrt(_head_mean_sq(kb, self.bd_ref) + NORM_EPS) * self.nw_scr[1:2, :]
        q = _rope(q, cos, sin_signed) * (B_HEAD_DIM ** -0.5 * LOG2E)
        k = _rope(k, cos, sin_signed)
        self.k_scr[i] = k.astype(BF16)

        qt = q.T
        pair_row = lax.broadcasted_iota(jnp.int32, (LANES, tq), 0)
        is_a = (pair_row // HALF) % 2 == 0
        for g in range(B_HEADS):
            qp = qt[g * LANES:(g + 1) * LANES, :]
            self.qblk_scr[g] = jnp.concatenate(
                [jnp.where(is_a, qp, 0.0), jnp.where(is_a, 0.0, qp)], axis=1).astype(BF16)

        self.m_scr[par] = jnp.full(self.m_scr.shape[1:], -jnp.inf, F32)
        self.acc_scr[par] = jnp.zeros(self.acc_scr.shape[1:], F32)

        vb = self.proj(3 * A_WIDTH + 2 * B_WIDTH, B_WIDTH)
        vbt = vb.T.astype(BF16)
        for g in range(B_HEADS):
            self.vt_scr[i, g, :B_V_DIM, :] = vbt[g * B_V_DIM:(g + 1) * B_V_DIM, :]
            self.vt_scr[i, g, B_V_DIM:, :] = jnp.ones((V_EXT - B_V_DIM, tk), BF16)
        va = self.proj(A_WIDTH, A_WIDTH)
        self.scores(0, i, masked=True)
        self.scores(1, i, masked=True)
        ua = self.proj(0, A_WIDTH)
        self.unit(0, True, i, 0, None)
        za = self.proj(2 * A_WIDTH, A_WIDTH)
        self.unit(1, True, i, 0, i)
        zb = self.proj(3 * A_WIDTH + 3 * B_WIDTH, B_WIDTH)
        after_diag = i - 1 if self.t > 0 else 0
        self.unit(2, True, i, after_diag, i)

        row_c = lax.broadcasted_iota(jnp.int32, (CHUNK, CHUNK), 0)
        col_c = lax.broadcasted_iota(jnp.int32, (CHUNK, CHUNK), 1)
        tril = row_c >= col_c
        for hh in range(A_HEADS):
            sl = slice(hh * A_HEAD_DIM, (hh + 1) * A_HEAD_DIM)
            v = va[:, sl]
            vn = (v * lax.rsqrt(jnp.mean(v * v, axis=-1, keepdims=True) + NORM_EPS)
                  * self.sgu_ref[hh:hh + 1, :])
            vnb = vn.astype(BF16)
            ws = jnp.where(tril, self.ws_ref[hh], 0.0).astype(BF16)
            bias = self.bst_scr[:, hh:hh + 1]
            n_chunks = tq // CHUNK
            chunks = jnp.concatenate(
                [vnb[c * CHUNK:(c + 1) * CHUNK, :] for c in range(n_chunks)], axis=1)
            mixed_chunks = jnp.dot(ws, chunks, preferred_element_type=F32)
            mix = jnp.concatenate([mixed_chunks[:, c * A_HEAD_DIM:(c + 1) * A_HEAD_DIM] + bias
                                   for c in range(n_chunks)], axis=0)
            a_out = ua[:, sl] * mix * _silu(za[:, sl])
            self.mixed_scr[par, :, sl] = a_out.astype(BF16)
        self.zb_gate = _silu(zb)
        self.unit(3, True, i, after_diag, i)
        for s in range(self.t):
            tile = i - 1 - s
            succ = tile - 1 if s + 1 < self.t else 0
            self.tile_units(tile, succ, tile + 1)

    def tile_units(self, tile, succ_tile, prev_tile):
        self.unit(0, False, tile, succ_tile, prev_tile)
        for g in range(1, B_HEADS):
            self.unit(g, False, tile, succ_tile, tile)

    def key_loop(self):
        base = self.base

        def pair_body(jj, carry):
            j = 2 * jj
            self.tile_units(j, j + 1, jnp.where(j == 0, base, j - 1))
            self.tile_units(j + 1, j + 2, j)
            return carry

        lax.fori_loop(0, base // 2, pair_body, 0)

    def last_value_product_and_out_a(self):
        last = B_HEADS - 1
        base = self.base
        self.value_update(last, jnp.where(base > 0, base - 1, base), last % 2, 1)
        rows = self.tile_rows
        self.out_ref[0, rows, :] = self.x_ref[0, rows, :] + self.gate * jnp.dot(
            self.mixed_scr[self.par, :, :A_WIDTH], self.wout_ref[:A_WIDTH, :],
            preferred_element_type=F32)

    def finalise(self):
        tq, par, acc_scr = SEQ_TILE, self.par, self.acc_scr
        lam = (jnp.exp(jnp.sum(self.lq1_ref[...] * self.lk1_ref[...], axis=-1, keepdims=True))
               - jnp.exp(jnp.sum(self.lq2_ref[...] * self.lk2_ref[...], axis=-1, keepdims=True))
               + LAMBDA_INIT)
        for g in range(B_HEADS):
            inv_l1 = 1.0 / acc_scr[par, 2 * g, B_V_DIM:B_V_DIM + SUBLANES, :]
            inv_l2 = 1.0 / acc_scr[par, 2 * g + 1, B_V_DIM:B_V_DIM + SUBLANES, :]
            acc1 = acc_scr[par, 2 * g, :B_V_DIM, :].reshape(B_V_DIM // SUBLANES, SUBLANES, tq)
            acc2 = acc_scr[par, 2 * g + 1, :B_V_DIM, :].reshape(
                B_V_DIM // SUBLANES, SUBLANES, tq)
            ot = acc1 * inv_l1[None] - lam * (acc2 * inv_l2[None])
            o = ot.reshape(B_V_DIM, tq).T
            o = (o * lax.rsqrt(jnp.mean(o * o, axis=-1, keepdims=True) + SUBLN_EPS)
                 * self.subln_ref[...])
            o = o * (1.0 - LAMBDA_INIT)
            vsl = slice(g * B_V_DIM, (g + 1) * B_V_DIM)
            b_out = o * self.zb_gate[:, vsl]
            self.mixed_scr[par, :, A_WIDTH + g * B_V_DIM:A_WIDTH + (g + 1) * B_V_DIM] = (
                b_out.astype(BF16))

    def out_b(self):
        self.out_ref[0, self.tile_rows, :] += self.gate * jnp.dot(
            self.mixed_scr[self.par, :, A_WIDTH:], self.wout_ref[A_WIDTH:, :],
            preferred_element_type=F32)


def kernel(x, c, positions, norm_w, w_ada, b_ada, w_in, sgu_norm_w, w_s, b_s, q_norm_w, k_norm_w,
           lambda_q1, lambda_k1, lambda_q2, lambda_k2, subln_w, w_out):
    bsz, seq, d = x.shape
    assert d == D_MODEL and seq % (TILES_PER_STEP * SEQ_TILE) == 0 and norm_w.shape[0] == 1
    assert TILES_PER_STEP % 2 == 0
    nt = seq // SEQ_TILE

    mod = pl.pallas_call(
        _adaln_kernel,
        grid=(3,),
        in_specs=[pl.BlockSpec((bsz, d), lambda n: (0, 0)),
                  pl.BlockSpec((None, d, d), lambda n: (0, 0, n)),
                  pl.BlockSpec((1, d), lambda n: (0, n))],
        out_specs=pl.BlockSpec((None, bsz, d), lambda n: (n, 0, 0)),
        out_shape=jax.ShapeDtypeStruct((3, bsz, d), F32),
        name="adaln_mod",
    )(c, w_ada, b_ada)

    n_grp = LANES // HALF
    pos_q = jnp.pad(positions.reshape(bsz, nt, n_grp, SEQ_TILE // n_grp),
                    ((0, 0), (0, 0), (0, SUBLANES - n_grp), (0, LANES - SEQ_TILE // n_grp)))

    inv_freq = ROPE_THETA ** (-jnp.arange(0, B_HEAD_DIM, 2, dtype=F32) / B_HEAD_DIM)
    invf = jnp.tile(inv_freq, LANES // HALF)[None, :]
    lane_head = (np.arange(MXU_DIM) // LANES) * 2 + (np.arange(MXU_DIM) // HALF) % 2
    bd = jnp.asarray(lane_head[:, None] == lane_head[None, :], dtype=BF16)


    const2 = lambda b, i: (0, 0)
    const3 = lambda b, i: (0, 0, 0)
    step_rows = TILES_PER_STEP * SEQ_TILE
    out = pl.pallas_call(
        _layer_kernel,
        grid=(bsz, nt // TILES_PER_STEP),
        in_specs=[
            pl.BlockSpec((1, step_rows, d), lambda b, i: (b, i, 0)),
            pl.BlockSpec((3, bsz, d), const3),
            pl.BlockSpec((None, TILES_PER_STEP, SUBLANES, LANES),
                         lambda b, i: (b, i, 0, 0)),
            pl.BlockSpec((1, LANES), const2),
            pl.BlockSpec((1, d), const2),
            pl.BlockSpec(memory_space=pl.ANY),
            pl.BlockSpec((A_HEADS, A_HEAD_DIM), const2),
            pl.BlockSpec((A_HEADS, CHUNK, CHUNK), const3),
            pl.BlockSpec((A_HEADS, CHUNK), const2),
            pl.BlockSpec((1, B_HEAD_DIM), const2),
            pl.BlockSpec((1, B_HEAD_DIM), const2),
            pl.BlockSpec((1, B_HEAD_DIM), const2),
            pl.BlockSpec((1, B_HEAD_DIM), const2),
            pl.BlockSpec((1, B_HEAD_DIM), const2),
            pl.BlockSpec((1, B_HEAD_DIM), const2),
            pl.BlockSpec((1, B_V_DIM), const2),
            pl.BlockSpec(memory_space=pl.ANY),
            pl.BlockSpec((MXU_DIM, MXU_DIM), const2),
        ],
        out_specs=pl.BlockSpec((1, step_rows, d), lambda b, i: (b, i, 0)),
        out_shape=jax.ShapeDtypeStruct((bsz, seq, d), F32),
        scratch_shapes=[
            pltpu.VMEM((nt, SEQ_TILE, B_WIDTH), BF16),
            pltpu.VMEM((nt, B_HEADS, V_EXT, SEQ_TILE), BF16),
            pltpu.VMEM((B_HEADS, LANES, 2 * SEQ_TILE), BF16),
            pltpu.VMEM((2, B_HEADS, SUBLANES, 2 * SEQ_TILE), F32),
            pltpu.VMEM((2, 2 * B_HEADS, V_EXT, SEQ_TILE), F32),
            pltpu.VMEM((2, SEQ_TILE, A_WIDTH + B_WIDTH), BF16),
            pltpu.VMEM((B_HEADS, SEQ_TILE, 2 * SEQ_TILE), F32),
            pltpu.VMEM((2, SEQ_TILE, 2 * SEQ_TILE), BF16),
            pltpu.VMEM((2, SUBLANES, 2 * SEQ_TILE), F32),
            pltpu.VMEM((B_HEADS, SUBLANES, 2 * SEQ_TILE), F32),
            pltpu.VMEM((d, IN_COLS), BF16),
            pltpu.VMEM((A_WIDTH + B_WIDTH, d), BF16),
            pltpu.VMEM((WEIGHT_STAGE_SLOTS, WEIGHT_CHUNK_ROWS, IN_COLS), F32),
            pltpu.SemaphoreType.DMA((WEIGHT_STAGE_SLOTS,)),
            pltpu.VMEM((SUBLANES, B_WIDTH), F32),
            pltpu.VMEM((CHUNK, LANES), F32),
        ],
        compiler_params=pltpu.CompilerParams(
            dimension_semantics=("arbitrary", "arbitrary"),
            vmem_limit_bytes=VMEM_LIMIT_BYTES),
        name="fused_layer",
    )(x, mod, pos_q, invf, norm_w, w_in,
      sgu_norm_w.reshape(A_HEADS, A_HEAD_DIM), w_s.reshape(A_HEADS, CHUNK, CHUNK),
      b_s.reshape(A_HEADS, CHUNK), q_norm_w, k_norm_w,
      lambda_q1, lambda_k1, lambda_q2, lambda_k2, subln_w, w_out, bd)
    return out
```
